```python
import jax
import jax.numpy as jnp
from jax import lax
import numpy as np


D_MODEL = 2048
BATCH = 8
SEQ = 8192
DEPTH = 4

GRID_W = 64
CTX_LEN = 256

HEAD_DIM = 128
N_HEADS = D_MODEL // HEAD_DIM
N_KV_HEADS = N_HEADS // 4
GQA_GROUP = N_HEADS // N_KV_HEADS
ATTN_DIM = N_HEADS * HEAD_DIM
KV_DIM = N_KV_HEADS * HEAD_DIM
Q_BLOCK = 128
ATTN_SCALE = HEAD_DIM ** -0.5
ROPE_THETA = 10000.0
AXIS_DIM = HEAD_DIM // 2
ODD_IN = 2 * ATTN_DIM + 2 * KV_DIM

WIDTH_A = D_MODEL
WIDTH_B = D_MODEL
CONV_A_WIDTH = 31
CONV_B_WIDTH = 3
EVEN_IN = 3 * WIDTH_A + 4 * WIDTH_B
EVEN_SPLITS = [WIDTH_A, 2 * WIDTH_A, 3 * WIDTH_A, 3 * WIDTH_A + WIDTH_B,
               3 * WIDTH_A + 2 * WIDTH_B, 3 * WIDTH_A + 3 * WIDTH_B]

N_EVEN = (DEPTH + 1) // 2
N_ODD = DEPTH // 2
DEEPNORM_ALPHA = (2.0 * DEPTH) ** 0.25
DEEPNORM_BETA = (8.0 * DEPTH) ** -0.25
LN_EPS = 1e-5
RMS_EPS = 1e-6

kernel_name = 'hybrid_conv_gqa_prefix_dit_block'


def layer_norm(x, g, b):
    xf = x.astype(jnp.float32)
    mu = jnp.mean(xf, axis=-1, keepdims=True)
    var = jnp.mean(jnp.square(xf - mu), axis=-1, keepdims=True)
    return ((xf - mu) * lax.rsqrt(var + LN_EPS)).astype(x.dtype) * g + b


def rms_norm(x, g):
    xf = x.astype(jnp.float32)
    inv = lax.rsqrt(jnp.mean(jnp.square(xf), axis=-1, keepdims=True) + RMS_EPS)
    return (xf * inv).astype(x.dtype) * g


def adaln(cond, w, b):
    m = jax.nn.silu(cond) @ w + b
    return jnp.split(m, 3, axis=-1)


def depthwise_conv(u, w):
    pad = w.shape[0] // 2
    return lax.conv_general_dilated(u, w[:, None, :].astype(u.dtype), (1,), [(pad, pad)],
                                    dimension_numbers=('NWC', 'WIO', 'NWC'),
                                    feature_group_count=u.shape[-1])


def rope_tables(n_tokens, dtype):
    rows_n = n_tokens // GRID_W
    row = jnp.repeat(jnp.arange(rows_n, dtype=jnp.float32), GRID_W)
    col = jnp.tile(jnp.arange(GRID_W, dtype=jnp.float32), rows_n)
    inv_freq = ROPE_THETA ** (-jnp.arange(0, AXIS_DIM, 2, dtype=jnp.float32) / AXIS_DIM)
    ang_r = row[:, None] * inv_freq[None, :]
    ang_c = col[:, None] * inv_freq[None, :]
    return (jnp.cos(ang_r).astype(dtype), jnp.sin(ang_r).astype(dtype),
            jnp.cos(ang_c).astype(dtype), jnp.sin(ang_c).astype(dtype))


def rotate(xh, cos, sin):
    x1, x2 = jnp.split(xh, 2, axis=-1)
    cos = cos[None, :, None, :]
    sin = sin[None, :, None, :]
    return jnp.concatenate([x1 * cos - x2 * sin, x1 * sin + x2 * cos], axis=-1)


def apply_rope_2d(x, tabs):
    cos_r, sin_r, cos_c, sin_c = tabs
    x_row, x_col = jnp.split(x, 2, axis=-1)
    return jnp.concatenate([rotate(x_row, cos_r, sin_r), rotate(x_col, cos_c, sin_c)], axis=-1)


def gqa(q, k, v):
    b, nq = q.shape[:2]
    qg = q.reshape(b, nq, N_KV_HEADS, GQA_GROUP, HEAD_DIM)
    s = jnp.einsum('bqkgd,bnkd->bkgqn', qg, k, preferred_element_type=jnp.float32) * ATTN_SCALE
    p = jax.nn.softmax(s, axis=-1).astype(v.dtype)
    o = jnp.einsum('bkgqn,bnkd->bqkgd', p, v)
    return o.reshape(b, nq, ATTN_DIM)


def q_gate_proj(h, w_qg, qg):
    q, g = jnp.split(h @ w_qg, 2, axis=-1)
    q = rms_norm(q.reshape(h.shape[0], h.shape[1], N_HEADS, HEAD_DIM), qg)
    return q, g


def kv_proj(h, w_kv, kg):
    k, v = jnp.split(h @ w_kv, 2, axis=-1)
    k = rms_norm(k.reshape(h.shape[0], h.shape[1], N_KV_HEADS, HEAD_DIM), kg)
    v = v.reshape(h.shape[0], h.shape[1], N_KV_HEADS, HEAD_DIM)
    return k, v


def attn_mixer(h_lat, h_ctx, w_in, qg, kg, w_out, tabs, ctx_out):
    w_qg = w_in[:, :2 * ATTN_DIM]
    w_kv = w_in[:, 2 * ATTN_DIM:]
    b, s = h_lat.shape[:2]
    k_c, v_c = kv_proj(h_ctx, w_kv, kg)
    q_l, g_l = q_gate_proj(h_lat, w_qg, qg)
    k_l, v_l = kv_proj(h_lat, w_kv, kg)
    q_l = apply_rope_2d(q_l, tabs)
    k_l = apply_rope_2d(k_l, tabs)
    k_all = jnp.concatenate([k_c, k_l], axis=1)
    v_all = jnp.concatenate([v_c, v_l], axis=1)
    n_blocks = s // Q_BLOCK
    q_blocks = q_l.reshape(b, n_blocks, Q_BLOCK, N_HEADS, HEAD_DIM).swapaxes(0, 1)
    o = lax.map(lambda qb: gqa(qb, k_all, v_all), q_blocks)
    o = o.swapaxes(0, 1).reshape(b, s, ATTN_DIM)
    y_lat = (o * jax.nn.silu(g_l)) @ w_out
    if not ctx_out:
        return y_lat, None
    q_c, g_c = q_gate_proj(h_ctx, w_qg, qg)
    o_c = gqa(q_c, k_c, v_c)
    y_ctx = (o_c * jax.nn.silu(g_c)) @ w_out
    return y_lat, y_ctx


def conv_mixer(h, w_in, ca_w, ca_b, na_g, na_b, cb_w, w_out):
    a_val, a_glu, a_gate, b_x, b_b, b_c, b_gate = jnp.split(h @ w_in, EVEN_SPLITS, axis=-1)
    u = a_val * jax.nn.sigmoid(a_glu)
    u = depthwise_conv(u, ca_w) + ca_b
    u = jax.nn.silu(layer_norm(u, na_g, na_b))
    a_out = u * jax.nn.silu(a_gate)
    v = depthwise_conv(b_c * b_x, cb_w)
    b_out = b_b * v * jax.nn.silu(b_gate)
    return jnp.concatenate([a_out, b_out], axis=-1) @ w_out


def _fwd_setup_inputs(seed: int = 0) -> dict:
    key = jax.random.key(seed)
    ks = jax.random.split(key, 19)

    def nrm(k, shape, s):
        return jax.random.normal(k, shape, jnp.float32) * s

    return {
        'x': nrm(ks[0], (BATCH, SEQ, D_MODEL), 1.0),
        'c': nrm(ks[1], (BATCH, D_MODEL), 1.0),
        'ctx': nrm(ks[2], (BATCH, CTX_LEN, D_MODEL), 1.0),
        'c_ctx': nrm(ks[3], (D_MODEL,), 1.0),
        'w_mod': nrm(ks[4], (DEPTH, D_MODEL, 3 * D_MODEL), D_MODEL ** -0.5),
        'b_mod': nrm(ks[5], (DEPTH, 3 * D_MODEL), 0.01),
        'post_ln_g': 1.0 + nrm(ks[6], (DEPTH, D_MODEL), 0.02),
        'post_ln_b': nrm(ks[7], (DEPTH, D_MODEL), 0.02),
        'w_in_e': nrm(ks[8], (N_EVEN, D_MODEL, EVEN_IN), D_MODEL ** -0.5),
        'conv_a_w': nrm(ks[9], (N_EVEN, CONV_A_WIDTH, WIDTH_A), CONV_A_WIDTH ** -0.5),
        'conv_a_b': nrm(ks[10], (N_EVEN, WIDTH_A), 0.02),
        'norm_a_g': 1.0 + nrm(ks[11], (N_EVEN, WIDTH_A), 0.02),
        'norm_a_b': nrm(ks[12], (N_EVEN, WIDTH_A), 0.02),
        'conv_b_w': nrm(ks[13], (N_EVEN, CONV_B_WIDTH, WIDTH_B), CONV_B_WIDTH ** -0.5),
        'w_out_e': nrm(ks[14], (N_EVEN, WIDTH_A + WIDTH_B, D_MODEL),
                       DEEPNORM_BETA * (WIDTH_A + WIDTH_B) ** -0.5),
        'w_in_o': nrm(ks[15], (N_ODD, D_MODEL, ODD_IN), D_MODEL ** -0.5),
        'q_norm_g': 1.0 + nrm(ks[16], (N_ODD, HEAD_DIM), 0.02),
        'k_norm_g': 1.0 + nrm(ks[17], (N_ODD, HEAD_DIM), 0.02),
        'w_out_o': nrm(ks[18], (N_ODD, ATTN_DIM, D_MODEL), DEEPNORM_BETA * ATTN_DIM ** -0.5),
    }


def _fwd_reference(x, c, ctx, c_ctx, w_mod, b_mod, post_ln_g, post_ln_b, w_in_e, conv_a_w, conv_a_b,
              norm_a_g, norm_a_b, conv_b_w, w_out_e, w_in_o, q_norm_g, k_norm_g, w_out_o):
    tabs = rope_tables(x.shape[1], x.dtype)
    for layer in range(DEPTH):
        last = layer == DEPTH - 1
        is_attn = layer % 2 == 1
        i = layer // 2
        need_ctx = (not last) or is_attn
        shift, scale, gate = adaln(c, w_mod[layer], b_mod[layer])
        h_lat = x * (1 + scale[:, None, :]) + shift[:, None, :]
        if need_ctx:
            shift_c, scale_c, gate_c = adaln(c_ctx, w_mod[layer], b_mod[layer])
            h_ctx = ctx * (1 + scale_c) + shift_c
        if is_attn:
            y_lat, y_ctx = attn_mixer(h_lat, h_ctx, w_in_o[i], q_norm_g[i], k_norm_g[i],
                                      w_out_o[i], tabs, not last)
        else:
            y_lat = conv_mixer(h_lat, w_in_e[i], conv_a_w[i], conv_a_b[i], norm_a_g[i],
                               norm_a_b[i], conv_b_w[i], w_out_e[i])
            if not last:
                y_ctx = conv_mixer(h_ctx, w_in_e[i], conv_a_w[i], conv_a_b[i], norm_a_g[i],
                                   norm_a_b[i], conv_b_w[i], w_out_e[i])
        x = layer_norm(DEEPNORM_ALPHA * x + gate[:, None, :] * y_lat,
                       post_ln_g[layer], post_ln_b[layer])
        if not last:
            ctx = layer_norm(DEEPNORM_ALPHA * ctx + gate_c * y_ctx,
                             post_ln_g[layer], post_ln_b[layer])
    return x


import jax as _jax
import jax.numpy as _jnp

TWIN_FORMAT = 'train_step'
FWD_PARAMS = ['x', 'c', 'ctx', 'c_ctx', 'w_mod', 'b_mod', 'post_ln_g', 'post_ln_b', 'w_in_e', 'conv_a_w', 'conv_a_b', 'norm_a_g', 'norm_a_b', 'conv_b_w', 'w_out_e', 'w_in_o', 'q_norm_g', 'k_norm_g', 'w_out_o']
TWIN_WEIGHTS = ['c_ctx', 'w_mod', 'b_mod', 'post_ln_g', 'post_ln_b', 'w_in_e', 'conv_a_w', 'conv_a_b', 'norm_a_g', 'norm_a_b', 'conv_b_w', 'w_out_e', 'w_in_o', 'q_norm_g', 'k_norm_g', 'w_out_o']
TWIN_DIFF_INPUT = 'x'
TWIN_INPUTS = ['x', 'c', 'ctx', 'c_ctx', 'w_mod', 'b_mod', 'post_ln_g', 'post_ln_b', 'w_in_e', 'conv_a_w', 'conv_a_b', 'norm_a_g', 'norm_a_b', 'conv_b_w', 'w_out_e', 'w_in_o', 'q_norm_g', 'k_norm_g', 'w_out_o', 'loss_target', 'm_c_ctx', 'm_w_mod', 'm_b_mod', 'm_post_ln_g', 'm_post_ln_b', 'm_w_in_e', 'm_conv_a_w', 'm_conv_a_b', 'm_norm_a_g', 'm_norm_a_b', 'm_conv_b_w', 'm_w_out_e', 'm_w_in_o', 'm_q_norm_g', 'm_k_norm_g', 'm_w_out_o', 'v_c_ctx', 'v_w_mod', 'v_b_mod', 'v_post_ln_g', 'v_post_ln_b', 'v_w_in_e', 'v_conv_a_w', 'v_conv_a_b', 'v_norm_a_g', 'v_norm_a_b', 'v_conv_b_w', 'v_w_out_e', 'v_w_in_o', 'v_q_norm_g', 'v_k_norm_g', 'v_w_out_o']
TWIN_OUTPUTS = ['loss', 'grad_x', 'grad_c_ctx', 'grad_w_mod', 'grad_b_mod', 'grad_post_ln_g', 'grad_post_ln_b', 'grad_w_in_e', 'grad_conv_a_w', 'grad_conv_a_b', 'grad_norm_a_g', 'grad_norm_a_b', 'grad_conv_b_w', 'grad_w_out_e', 'grad_w_in_o', 'grad_q_norm_g', 'grad_k_norm_g', 'grad_w_out_o', 'delta_c_ctx', 'delta_w_mod', 'delta_b_mod', 'delta_post_ln_g', 'delta_post_ln_b', 'delta_w_in_e', 'delta_conv_a_w', 'delta_conv_a_b', 'delta_norm_a_g', 'delta_norm_a_b', 'delta_conv_b_w', 'delta_w_out_e', 'delta_w_in_o', 'delta_q_norm_g', 'delta_k_norm_g', 'delta_w_out_o', 'new_m_c_ctx', 'new_m_w_mod', 'new_m_b_mod', 'new_m_post_ln_g', 'new_m_post_ln_b', 'new_m_w_in_e', 'new_m_conv_a_w', 'new_m_conv_a_b', 'new_m_norm_a_g', 'new_m_norm_a_b', 'new_m_conv_b_w', 'new_m_w_out_e', 'new_m_w_in_o', 'new_m_q_norm_g', 'new_m_k_norm_g', 'new_m_w_out_o', 'new_v_c_ctx', 'new_v_w_mod', 'new_v_b_mod', 'new_v_post_ln_g', 'new_v_post_ln_b', 'new_v_w_in_e', 'new_v_conv_a_w', 'new_v_conv_a_b', 'new_v_norm_a_g', 'new_v_norm_a_b', 'new_v_conv_b_w', 'new_v_w_out_e', 'new_v_w_in_o', 'new_v_q_norm_g', 'new_v_k_norm_g', 'new_v_w_out_o']
TWIN_LEAF_KINDS = {'loss': 'loss', 'grad_x': 'grad_x', 'grad_c_ctx': 'grad_w', 'grad_w_mod': 'grad_w', 'grad_b_mod': 'grad_w', 'grad_post_ln_g': 'grad_w', 'grad_post_ln_b': 'grad_w', 'grad_w_in_e': 'grad_w', 'grad_conv_a_w': 'grad_w', 'grad_conv_a_b': 'grad_w', 'grad_norm_a_g': 'grad_w', 'grad_norm_a_b': 'grad_w', 'grad_conv_b_w': 'grad_w', 'grad_w_out_e': 'grad_w', 'grad_w_in_o': 'grad_w', 'grad_q_norm_g': 'grad_w', 'grad_k_norm_g': 'grad_w', 'grad_w_out_o': 'grad_w', 'delta_c_ctx': 'delta_w', 'delta_w_mod': 'delta_w', 'delta_b_mod': 'delta_w', 'delta_post_ln_g': 'delta_w', 'delta_post_ln_b': 'delta_w', 'delta_w_in_e': 'delta_w', 'delta_conv_a_w': 'delta_w', 'delta_conv_a_b': 'delta_w', 'delta_norm_a_g': 'delta_w', 'delta_norm_a_b': 'delta_w', 'delta_conv_b_w': 'delta_w', 'delta_w_out_e': 'delta_w', 'delta_w_in_o': 'delta_w', 'delta_q_norm_g': 'delta_w', 'delta_k_norm_g': 'delta_w', 'delta_w_out_o': 'delta_w', 'new_m_c_ctx': 'new_m', 'new_m_w_mod': 'new_m', 'new_m_b_mod': 'new_m', 'new_m_post_ln_g': 'new_m', 'new_m_post_ln_b': 'new_m', 'new_m_w_in_e': 'new_m', 'new_m_conv_a_w': 'new_m', 'new_m_conv_a_b': 'new_m', 'new_m_norm_a_g': 'new_m', 'new_m_norm_a_b': 'new_m', 'new_m_conv_b_w': 'new_m', 'new_m_w_out_e': 'new_m', 'new_m_w_in_o': 'new_m', 'new_m_q_norm_g': 'new_m', 'new_m_k_norm_g': 'new_m', 'new_m_w_out_o': 'new_m', 'new_v_c_ctx': 'new_v', 'new_v_w_mod': 'new_v', 'new_v_b_mod': 'new_v', 'new_v_post_ln_g': 'new_v', 'new_v_post_ln_b': 'new_v', 'new_v_w_in_e': 'new_v', 'new_v_conv_a_w': 'new_v', 'new_v_conv_a_b': 'new_v', 'new_v_norm_a_g': 'new_v', 'new_v_norm_a_b': 'new_v', 'new_v_conv_b_w': 'new_v', 'new_v_w_out_e': 'new_v', 'new_v_w_in_o': 'new_v', 'new_v_q_norm_g': 'new_v', 'new_v_k_norm_g': 'new_v', 'new_v_w_out_o': 'new_v'}


def _forward(args):
    return _fwd_reference(*[args[k] for k in FWD_PARAMS])


def _output_shape():
    def fwd():
        inp = _fwd_setup_inputs(0)
        return _fwd_reference(*[inp[k] for k in FWD_PARAMS])
    out = _jax.eval_shape(fwd)
    return out.shape, out.dtype

N_MICROBATCH = 1
ADAM_LR = 0.001
ADAM_B1 = 0.9
ADAM_B2 = 0.999
ADAM_EPS = 1e-08
ADAM_WD = 0.01
ADAM_STEP = 10
PER_EXAMPLE_BATCH_AXIS = {'x': 0, 'c': 0, 'ctx': 0, 'loss_target': 0}
SHARED_INPUTS = []
_WEIGHT_DTYPES = {'c_ctx': _jnp.float32, 'w_mod': _jnp.float32, 'b_mod': _jnp.float32, 'post_ln_g': _jnp.float32, 'post_ln_b': _jnp.float32, 'w_in_e': _jnp.float32, 'conv_a_w': _jnp.float32, 'conv_a_b': _jnp.float32, 'norm_a_g': _jnp.float32, 'norm_a_b': _jnp.float32, 'conv_b_w': _jnp.float32, 'w_out_e': _jnp.float32, 'w_in_o': _jnp.float32, 'q_norm_g': _jnp.float32, 'k_norm_g': _jnp.float32, 'w_out_o': _jnp.float32}
MOMENT_SCALE = {'c_ctx': 7.218847e-03, 'w_mod': 1.897920e-02, 'b_mod': 3.244225e-02, 'post_ln_g': 1.605105e+01, 'post_ln_b': 7.084132e-01, 'w_in_e': 1.952993e-02, 'conv_a_w': 6.923938e-03, 'conv_a_b': 9.781988e-03, 'norm_a_g': 8.047848e-03, 'norm_a_b': 7.395518e-03, 'conv_b_w': 2.493628e-02, 'w_out_e': 6.183591e-02, 'w_in_o': 8.967526e-03, 'q_norm_g': 7.653842e-03, 'k_norm_g': 7.608008e-03, 'w_out_o': 2.441451e-02}


def _to_microbatches(a, axis):
    t = _jnp.moveaxis(a, axis, 0)
    t = t.reshape((N_MICROBATCH, t.shape[0] // N_MICROBATCH) + t.shape[1:])
    return _jnp.moveaxis(t, 1, axis + 1)


def setup_inputs(seed: int = 0) -> dict:
    inp = _fwd_setup_inputs(seed)
    key = _jax.random.fold_in(_jax.random.key(seed), 7919)
    shape, _ = _output_shape()
    out = dict(inp)
    out["loss_target"] = _jax.random.normal(_jax.random.fold_in(key, 0), shape, _jnp.float32)
    for i, name in enumerate(TWIN_WEIGHTS):
        w = inp[name].astype(_jnp.float32)
        if MOMENT_SCALE is None:
            s = _jnp.sqrt(_jnp.mean(_jnp.square(w)) + 1e-30)
        else:
            s = MOMENT_SCALE[name]
        km, kv = _jax.random.split(_jax.random.fold_in(key, i + 1))
        out[name] = w
        out["m_" + name] = s * _jax.random.normal(km, w.shape, _jnp.float32)
        out["v_" + name] = (s * s) * _jax.random.uniform(kv, w.shape, _jnp.float32, 0.5, 1.5)
    if N_MICROBATCH > 1:
        for name, axis in PER_EXAMPLE_BATCH_AXIS.items():
            out[name] = _to_microbatches(out[name], axis)
    return {'x': out['x'], 'c': out['c'], 'ctx': out['ctx'], 'c_ctx': out['c_ctx'], 'w_mod': out['w_mod'], 'b_mod': out['b_mod'], 'post_ln_g': out['post_ln_g'], 'post_ln_b': out['post_ln_b'], 'w_in_e': out['w_in_e'], 'conv_a_w': out['conv_a_w'], 'conv_a_b': out['conv_a_b'], 'norm_a_g': out['norm_a_g'], 'norm_a_b': out['norm_a_b'], 'conv_b_w': out['conv_b_w'], 'w_out_e': out['w_out_e'], 'w_in_o': out['w_in_o'], 'q_norm_g': out['q_norm_g'], 'k_norm_g': out['k_norm_g'], 'w_out_o': out['w_out_o'], 'loss_target': out['loss_target'], 'm_c_ctx': out['m_c_ctx'], 'm_w_mod': out['m_w_mod'], 'm_b_mod': out['m_b_mod'], 'm_post_ln_g': out['m_post_ln_g'], 'm_post_ln_b': out['m_post_ln_b'], 'm_w_in_e': out['m_w_in_e'], 'm_conv_a_w': out['m_conv_a_w'], 'm_conv_a_b': out['m_conv_a_b'], 'm_norm_a_g': out['m_norm_a_g'], 'm_norm_a_b': out['m_norm_a_b'], 'm_conv_b_w': out['m_conv_b_w'], 'm_w_out_e': out['m_w_out_e'], 'm_w_in_o': out['m_w_in_o'], 'm_q_norm_g': out['m_q_norm_g'], 'm_k_norm_g': out['m_k_norm_g'], 'm_w_out_o': out['m_w_out_o'], 'v_c_ctx': out['v_c_ctx'], 'v_w_mod': out['v_w_mod'], 'v_b_mod': out['v_b_mod'], 'v_post_ln_g': out['v_post_ln_g'], 'v_post_ln_b': out['v_post_ln_b'], 'v_w_in_e': out['v_w_in_e'], 'v_conv_a_w': out['v_conv_a_w'], 'v_conv_a_b': out['v_conv_a_b'], 'v_norm_a_g': out['v_norm_a_g'], 'v_norm_a_b': out['v_norm_a_b'], 'v_conv_b_w': out['v_conv_b_w'], 'v_w_out_e': out['v_w_out_e'], 'v_w_in_o': out['v_w_in_o'], 'v_q_norm_g': out['v_q_norm_g'], 'v_k_norm_g': out['v_k_norm_g'], 'v_w_out_o': out['v_w_out_o']}


def _loss(weights, diff, rest, loss_target):
    with _jax.named_scope("forward"):
        args = {**rest, TWIN_DIFF_INPUT: diff, **{k: w.astype(_WEIGHT_DTYPES[k]) for k, w in weights.items()}}
        y = _forward(args)
    with _jax.named_scope("loss_head"):
        err = _jnp.square(y.astype(_jnp.float32) - loss_target)
        return 0.5 * _jnp.sum(_jnp.mean(err, axis=-1)) if err.ndim else 0.5 * err


def _adamw(w, g, m, v):
    m = ADAM_B1 * m + (1.0 - ADAM_B1) * g
    v = ADAM_B2 * v + (1.0 - ADAM_B2) * _jnp.square(g)
    m_hat = m / (1.0 - ADAM_B1 ** ADAM_STEP)
    v_hat = v / (1.0 - ADAM_B2 ** ADAM_STEP)
    delta = -ADAM_LR * (m_hat / (_jnp.sqrt(v_hat) + ADAM_EPS) + ADAM_WD * w)
    return delta, m, v


def reference(x, c, ctx, c_ctx, w_mod, b_mod, post_ln_g, post_ln_b, w_in_e, conv_a_w, conv_a_b, norm_a_g, norm_a_b, conv_b_w, w_out_e, w_in_o, q_norm_g, k_norm_g, w_out_o, loss_target, m_c_ctx, m_w_mod, m_b_mod, m_post_ln_g, m_post_ln_b, m_w_in_e, m_conv_a_w, m_conv_a_b, m_norm_a_g, m_norm_a_b, m_conv_b_w, m_w_out_e, m_w_in_o, m_q_norm_g, m_k_norm_g, m_w_out_o, v_c_ctx, v_w_mod, v_b_mod, v_post_ln_g, v_post_ln_b, v_w_in_e, v_conv_a_w, v_conv_a_b, v_norm_a_g, v_norm_a_b, v_conv_b_w, v_w_out_e, v_w_in_o, v_q_norm_g, v_k_norm_g, v_w_out_o):
    given = dict(x=x, c=c, ctx=ctx, c_ctx=c_ctx, w_mod=w_mod, b_mod=b_mod, post_ln_g=post_ln_g, post_ln_b=post_ln_b, w_in_e=w_in_e, conv_a_w=conv_a_w, conv_a_b=conv_a_b, norm_a_g=norm_a_g, norm_a_b=norm_a_b, conv_b_w=conv_b_w, w_out_e=w_out_e, w_in_o=w_in_o, q_norm_g=q_norm_g, k_norm_g=k_norm_g, w_out_o=w_out_o, loss_target=loss_target, m_c_ctx=m_c_ctx, m_w_mod=m_w_mod, m_b_mod=m_b_mod, m_post_ln_g=m_post_ln_g, m_post_ln_b=m_post_ln_b, m_w_in_e=m_w_in_e, m_conv_a_w=m_conv_a_w, m_conv_a_b=m_conv_a_b, m_norm_a_g=m_norm_a_g, m_norm_a_b=m_norm_a_b, m_conv_b_w=m_conv_b_w, m_w_out_e=m_w_out_e, m_w_in_o=m_w_in_o, m_q_norm_g=m_q_norm_g, m_k_norm_g=m_k_norm_g, m_w_out_o=m_w_out_o, v_c_ctx=v_c_ctx, v_w_mod=v_w_mod, v_b_mod=v_b_mod, v_post_ln_g=v_post_ln_g, v_post_ln_b=v_post_ln_b, v_w_in_e=v_w_in_e, v_conv_a_w=v_conv_a_w, v_conv_a_b=v_conv_a_b, v_norm_a_g=v_norm_a_g, v_norm_a_b=v_norm_a_b, v_conv_b_w=v_conv_b_w, v_w_out_e=v_w_out_e, v_w_in_o=v_w_in_o, v_q_norm_g=v_q_norm_g, v_k_norm_g=v_k_norm_g, v_w_out_o=v_w_out_o)
    weights = {n: given[n] for n in TWIN_WEIGHTS}
    shared = {n: given[n] for n in SHARED_INPUTS}
    per_example = {n: given[n] for n in ['x', 'c', 'ctx']}
    grad_fn = _jax.value_and_grad(_loss, argnums=(0, 1))

    def one_microbatch(ex, loss_target):
        ex = dict(ex)
        diff = ex.pop(TWIN_DIFF_INPUT)
        return grad_fn(weights, diff, {**shared, **ex}, loss_target)

    if N_MICROBATCH == 1:
        loss, (grad_w, grad_x) = one_microbatch(per_example, given["loss_target"])
    else:
        def body(carry, xs):
            loss_sum, grad_sum = carry
            l_k, (gw_k, gx_k) = one_microbatch(xs[0], xs[1])
            with _jax.named_scope("update"):
                return (loss_sum + l_k, _jax.tree.map(_jnp.add, grad_sum, gw_k)), gx_k

        init = (_jnp.zeros((), _jnp.float32), _jax.tree.map(_jnp.zeros_like, weights))
        (loss, grad_w), grad_x = _jax.lax.scan(body, init, (per_example, given["loss_target"]))
    with _jax.named_scope("update"):
        delta_w, new_m, new_v = {}, {}, {}
        for n in TWIN_WEIGHTS:
            delta_w[n], new_m[n], new_v[n] = _adamw(weights[n], grad_w[n], given["m_" + n], given["v_" + n])
    return (loss, grad_x, *[grad_w[n] for n in TWIN_WEIGHTS], *[delta_w[n] for n in TWIN_WEIGHTS],
            *[new_m[n] for n in TWIN_WEIGHTS], *[new_v[n] for n in TWIN_WEIGHTS])
```

```python
import functools
import math

import jax
import jax.numpy as jnp
from jax import lax
from jax.experimental import pallas as pl
from jax.experimental.pallas import tpu as pltpu

F32 = jnp.float32
BF16 = jnp.bfloat16

NDEV = 8
GRID_W = 64
HEAD_DIM = 128
GQA_GROUP = 4
ROPE_THETA = 10000.0
LN_EPS = 1e-5
RMS_EPS = 1e-6
ATTN_SCALE = HEAD_DIM ** -0.5
ADAM_LR = 0.001
ADAM_B1 = 0.9
ADAM_B2 = 0.999
ADAM_EPS = 1e-08
ADAM_WD = 0.01
ADAM_STEP = 10
HALO = 16
VMEM_LIMIT = 56 * 1024 * 1024
MESH = pl.DeviceIdType.MESH


def _cp(n_axes):
    return pltpu.CompilerParams(dimension_semantics=("arbitrary",) * n_axes, vmem_limit_bytes=VMEM_LIMIT)


def _div(dim, pref, mult):
    t = min(pref, dim) // mult * mult
    while t >= mult:
        if dim % t == 0:
            return t
        t -= mult
    return dim


def _sigmoid(x):
    return 1.0 / (1.0 + jnp.exp(-x))


def _silu(x):
    return x * _sigmoid(x)


def _dsilu(x):
    s = _sigmoid(x)
    return s * (1.0 + x * (1.0 - s))


def _ln_norm(z):
    mu = jnp.mean(z, axis=-1, keepdims=True)
    zc = z - mu
    var = jnp.mean(zc * zc, axis=-1, keepdims=True)
    rstd = lax.rsqrt(var + LN_EPS)
    return zc * rstd, rstd


def _ln_bwd(dn, n, rstd):
    return rstd * (dn - jnp.mean(dn, axis=-1, keepdims=True) - n * jnp.mean(dn * n, axis=-1, keepdims=True))


def _partner(x):
    lane = lax.broadcasted_iota(jnp.int32, x.shape, 1)
    return jnp.where((lane % 64) < 32, pltpu.roll(x, 96, 1), pltpu.roll(x, 32, 1))


def _mm(a, b, mode, out_dtype, name, tm=768, tn=512, tk=2048, out_nd=1):
    if mode == "nn":
        m, kdim = a.shape
        nd, _, ns = b.shape
        tm, tn, tk = _div(m, tm, 8), _div(ns, tn, 128), _div(kdim, tk, 128)
        nbs = ns // tn
        grid = (m // tm, nd * nbs, kdim // tk)
        a_spec = pl.BlockSpec((tm, tk), lambda i, j, k: (i, k))
        b_spec = pl.BlockSpec((None, tk, tn), lambda i, j, k: (j // nbs, k, j % nbs))
        o_spec = pl.BlockSpec((tm, tn), lambda i, j, k: (i, j))
        out_shape = (m, nd * ns)
        dims = (((1,), (0,)), ((), ()))
    elif mode == "nt":
        m, _ = a.shape
        nd, ko, ns = b.shape
        tm, tn, tk = _div(m, tm, 8), _div(ko, tn, 128), _div(ns, tk, 128)
        kbs = ns // tk
        grid = (m // tm, ko // tn, nd * kbs)
        a_spec = pl.BlockSpec((tm, tk), lambda i, j, k: (i, k))
        b_spec = pl.BlockSpec((None, tn, tk), lambda i, j, k: (k // kbs, j, k % kbs))
        o_spec = pl.BlockSpec((tm, tn), lambda i, j, k: (i, j))
        out_shape = (m, ko)
        dims = (((1,), (1,)), ((), ()))
    else:
        m, kdim = a.shape
        n = b.shape[1]
        ns = n // out_nd
        tm, tn, tk = _div(kdim, tm, 128), _div(ns, tn, 128), _div(m, tk, 16)
        nbs = ns // tn
        grid = (kdim // tm, out_nd * nbs, m // tk)
        a_spec = pl.BlockSpec((tk, tm), lambda i, j, k: (k, i))
        b_spec = pl.BlockSpec((tk, tn), lambda i, j, k: (k, j))
        o_spec = pl.BlockSpec((None, tm, tn), lambda i, j, k: (j // nbs, i, j % nbs))
        out_shape = (out_nd, kdim, ns)
        dims = (((0,), (0,)), ((), ()))
    nk = grid[2]

    def body(a_ref, b_ref, o_ref, acc_ref):
        k = pl.program_id(2)

        @pl.when(k == 0)
        def _():
            acc_ref[...] = jnp.zeros_like(acc_ref)

        acc_ref[...] += lax.dot_general(a_ref[...].astype(BF16), b_ref[...].astype(BF16), dims,
                                        preferred_element_type=F32)

        @pl.when(k == nk - 1)
        def _():
            o_ref[...] = acc_ref[...].astype(o_ref.dtype)

    return pl.pallas_call(
        body, name=name, grid=grid, in_specs=[a_spec, b_spec], out_specs=o_spec,
        out_shape=jax.ShapeDtypeStruct(out_shape, out_dtype),
        scratch_shapes=[pltpu.VMEM((tm, tn), F32)], compiler_params=_cp(3),
    )(a, b)


def _ew(fn, out_dtype, name, *xs):
    rows, cols = xs[0].shape
    tr = rows if rows <= 64 else _div(rows, 256, 16)

    def body(*refs):
        refs[-1][...] = fn(*[r[...] for r in refs[:-1]]).astype(out_dtype)

    spec = pl.BlockSpec((tr, cols), lambda i: (i, 0))
    return pl.pallas_call(
        body, name=name, grid=(rows // tr,), in_specs=[spec] * len(xs), out_specs=spec,
        out_shape=jax.ShapeDtypeStruct((rows, cols), out_dtype), compiler_params=_cp(1),
    )(*xs)


def _sum_lead(x, name):
    n, rows, cols = x.shape
    tr = _div(rows, 64, 8)

    def body(x_ref, o_ref):
        acc = x_ref[0]
        for s in range(1, n):
            acc = acc + x_ref[s]
        o_ref[...] = acc

    return pl.pallas_call(
        body, name=name, grid=(rows // tr,),
        in_specs=[pl.BlockSpec((n, tr, cols), lambda i: (0, i, 0))],
        out_specs=pl.BlockSpec((tr, cols), lambda i: (i, 0)),
        out_shape=jax.ShapeDtypeStruct((rows, cols), F32), compiler_params=_cp(1),
    )(x)


def _adam_math(w, g, m, v):
    m = ADAM_B1 * m + (1.0 - ADAM_B1) * g
    v = ADAM_B2 * v + (1.0 - ADAM_B2) * (g * g)
    m_hat = m / (1.0 - ADAM_B1 ** ADAM_STEP)
    v_hat = v / (1.0 - ADAM_B2 ** ADAM_STEP)
    delta = -ADAM_LR * (m_hat / (jnp.sqrt(v_hat) + ADAM_EPS) + ADAM_WD * w)
    return delta, m, v


def _adamw(w, g, m, v, name):
    shape = w.shape
    cols = shape[-1]
    w2, g2, m2, v2 = [t.reshape(-1, cols) for t in (w, g, m, v)]
    rows = w2.shape[0]
    tr = rows if rows <= 512 else _div(rows, 256, 8)

    def body(w_ref, g_ref, m_ref, v_ref, d_ref, nm_ref, nv_ref):
        d, nm, nv = _adam_math(w_ref[...], g_ref[...], m_ref[...], v_ref[...])
        d_ref[...] = d
        nm_ref[...] = nm
        nv_ref[...] = nv

    spec = pl.BlockSpec((tr, cols), lambda i: (i, 0))
    outs = pl.pallas_call(
        body, name=name, grid=(rows // tr,), in_specs=[spec] * 4, out_specs=[spec] * 3,
        out_shape=[jax.ShapeDtypeStruct((rows, cols), F32)] * 3, compiler_params=_cp(1),
    )(w2, g2, m2, v2)
    return [o.reshape(shape) for o in outs]


def _reduce_adamw(parts, w, m, v, name):
    shape = w.shape
    n, rows, cols = parts.shape
    w2, m2, v2 = [t.reshape(rows, cols) for t in (w, m, v)]
    tr = _div(rows, 128, 16)

    def body(p_ref, w_ref, m_ref, v_ref, g_ref, d_ref, nm_ref, nv_ref):
        g = p_ref[0].astype(F32)
        for s in range(1, n):
            g = g + p_ref[s].astype(F32)
        d, nm, nv = _adam_math(w_ref[...], g, m_ref[...], v_ref[...])
        g_ref[...] = g
        d_ref[...] = d
        nm_ref[...] = nm
        nv_ref[...] = nv

    spec = pl.BlockSpec((tr, cols), lambda i: (i, 0))
    outs = pl.pallas_call(
        body, name=name, grid=(rows // tr,),
        in_specs=[pl.BlockSpec((n, tr, cols), lambda i: (0, i, 0))] + [spec] * 3, out_specs=[spec] * 4,
        out_shape=[jax.ShapeDtypeStruct((rows, cols), F32)] * 4, compiler_params=_cp(1),
    )(parts, w2, m2, v2)
    return [o.reshape(shape) for o in outs]


def _all_gather(x, name):
    rows, cols = x.shape

    def body(x_ref, out_ref, send_sems, recv_sems, local_sem):
        mx, my, mc = lax.axis_index("x"), lax.axis_index("y"), lax.axis_index("c")
        me, sibling = (mx, my, mc), (mx, my, 1 - mc)
        chips = [(1 - mx, my), (mx, 1 - my), (1 - mx, 1 - my)]

        def slab(px, py, pc):
            return out_ref.at[4 * px + 2 * py + pc]

        def copy(k, block, to, src=None):
            return pltpu.make_async_remote_copy(
                src_ref=slab(*block) if src is None else src, dst_ref=slab(*block),
                send_sem=send_sems.at[k], recv_sem=recv_sems.at[k], device_id=to, device_id_type=MESH)

        mine = pltpu.make_async_copy(x_ref, slab(*me), local_sem)
        mine.start()
        first = [copy(0, me, sibling, src=x_ref)]
        first += [copy(1 + j, me, (*chip, mc), src=x_ref) for j, chip in enumerate(chips)]
        for cp in first:
            cp.start()
        passed = [copy(4 + j, (*chip, mc), sibling) for j, chip in enumerate(chips)]
        for j, chip in enumerate(chips):
            copy(1 + j, (*chip, mc), me).wait_recv()
            passed[j].start()
        copy(0, sibling, me).wait_recv()
        for j, chip in enumerate(chips):
            copy(4 + j, (*chip, 1 - mc), me).wait_recv()
        for cp in first + passed:
            cp.wait_send()
        mine.wait()

    return pl.pallas_call(
        body, name=name, out_shape=jax.ShapeDtypeStruct((NDEV, rows, cols), x.dtype),
        in_specs=[pl.BlockSpec(memory_space=pltpu.HBM)], out_specs=pl.BlockSpec(memory_space=pltpu.HBM),
        scratch_shapes=[pltpu.SemaphoreType.DMA((7,)), pltpu.SemaphoreType.DMA((7,)), pltpu.SemaphoreType.DMA],
    )(x)


def _all_to_all(x, name):
    def body(x_ref, out_ref, send_sems, recv_sems, local_sem):
        mx, my, mc = lax.axis_index("x"), lax.axis_index("y"), lax.axis_index("c")
        me = 4 * mx + 2 * my + mc
        mine = pltpu.make_async_copy(x_ref.at[me], out_ref.at[me], local_sem)
        mine.start()
        copies = []
        for k in range(1, NDEV):
            px = mx ^ ((k >> 2) & 1)
            py = my ^ ((k >> 1) & 1)
            pc = mc ^ (k & 1)
            peer = 4 * px + 2 * py + pc
            cp = pltpu.make_async_remote_copy(
                src_ref=x_ref.at[peer], dst_ref=out_ref.at[me], send_sem=send_sems.at[k - 1],
                recv_sem=recv_sems.at[k - 1], device_id=(px, py, pc), device_id_type=MESH)
            cp.start()
            copies.append(cp)
        for cp in copies:
            cp.wait()
        mine.wait()

    return pl.pallas_call(
        body, name=name, out_shape=jax.ShapeDtypeStruct(x.shape, x.dtype),
        in_specs=[pl.BlockSpec(memory_space=pltpu.HBM)], out_specs=pl.BlockSpec(memory_space=pltpu.HBM),
        scratch_shapes=[pltpu.SemaphoreType.DMA((7,)), pltpu.SemaphoreType.DMA((7,)), pltpu.SemaphoreType.DMA],
    )(x)


def _row_tile(t_ctx, pref=256):
    return _div(t_ctx, pref, 8)


def _mod_spec(d, nctx):
    return pl.BlockSpec((None, 3, d), lambda i: (jnp.where(i >= nctx, 1, 0), 0, 0))


def _modulate(xb, modv, t_ctx, name):
    r, d = xb.shape
    tm = _row_tile(t_ctx)
    nctx = t_ctx // tm

    def body(x_ref, mv_ref, h_ref):
        h_ref[...] = (x_ref[...] * (1.0 + mv_ref[1:2, :]) + mv_ref[0:1, :]).astype(BF16)

    row = pl.BlockSpec((tm, d), lambda i: (i, 0))
    return pl.pallas_call(
        body, name=name, grid=(r // tm,), in_specs=[row, _mod_spec(d, nctx)], out_specs=row,
        out_shape=jax.ShapeDtypeStruct((r, d), BF16), compiler_params=_cp(1),
    )(xb, modv)


def _postln_fwd(xb, y, modv, lnp, alpha, t_ctx, name):
    r, d = xb.shape
    tm = _row_tile(t_ctx)
    nctx = t_ctx // tm

    def body(x_ref, y_ref, mv_ref, ln_ref, o_ref):
        n, _ = _ln_norm(alpha * x_ref[...] + mv_ref[2:3, :] * y_ref[...])
        o_ref[...] = n * ln_ref[0:1, :] + ln_ref[1:2, :]

    row = pl.BlockSpec((tm, d), lambda i: (i, 0))
    return pl.pallas_call(
        body, name=name, grid=(r // tm,),
        in_specs=[row, row, _mod_spec(d, nctx), pl.BlockSpec((2, d), lambda i: (0, 0))], out_specs=row,
        out_shape=jax.ShapeDtypeStruct((r, d), F32), compiler_params=_cp(1),
    )(xb, y, modv, lnp)


def _postln_bwd(dxn, xb, y, modv, lnp, alpha, t_ctx, name):
    r, d = xb.shape
    tm = _row_tile(t_ctx)
    nctx = t_ctx // tm

    def body(dxn_ref, x_ref, y_ref, mv_ref, ln_ref, dy_ref, dxa_ref, s_ref):
        i = pl.program_id(0)

        @pl.when(i == 0)
        def _():
            s_ref[...] = jnp.zeros_like(s_ref)

        yv = y_ref[...]
        gate = mv_ref[2:3, :]
        n, rstd = _ln_norm(alpha * x_ref[...] + gate * yv)
        dxn_v = dxn_ref[...]
        dz = _ln_bwd(dxn_v * ln_ref[0:1, :], n, rstd)
        dy_ref[...] = (gate * dz).astype(BF16)
        dxa_ref[...] = alpha * dz
        s_ref[0:1, :] += jnp.sum(dxn_v * n, axis=0, keepdims=True)
        s_ref[1:2, :] += jnp.sum(dxn_v, axis=0, keepdims=True)
        dgate = jnp.sum(dz * yv, axis=0, keepdims=True)
        is_ctx = i < nctx
        s_ref[2:3, :] += jnp.where(is_ctx, dgate, 0.0)
        s_ref[3:4, :] += jnp.where(is_ctx, 0.0, dgate)

    row = pl.BlockSpec((tm, d), lambda i: (i, 0))
    return pl.pallas_call(
        body, name=name, grid=(r // tm,),
        in_specs=[row, row, row, _mod_spec(d, nctx), pl.BlockSpec((2, d), lambda i: (0, 0))],
        out_specs=[row, row, pl.BlockSpec((8, d), lambda i: (0, 0))],
        out_shape=[jax.ShapeDtypeStruct((r, d), BF16), jax.ShapeDtypeStruct((r, d), F32),
                   jax.ShapeDtypeStruct((8, d), F32)],
        compiler_params=_cp(1),
    )(dxn, xb, y, modv, lnp)


def _mod_bwd(dh, xb, dxa, modv, t_ctx, name):
    r, d = xb.shape
    tm = _row_tile(t_ctx)
    nctx = t_ctx // tm

    def body(dh_ref, x_ref, dxa_ref, mv_ref, dx_ref, s_ref):
        i = pl.program_id(0)

        @pl.when(i == 0)
        def _():
            s_ref[...] = jnp.zeros_like(s_ref)

        dhv = dh_ref[...]
        dx_ref[...] = dxa_ref[...] + dhv * (1.0 + mv_ref[1:2, :])
        dshift = jnp.sum(dhv, axis=0, keepdims=True)
        dscale = jnp.sum(dhv * x_ref[...], axis=0, keepdims=True)
        is_ctx = i < nctx
        s_ref[0:1, :] += jnp.where(is_ctx, dshift, 0.0)
        s_ref[1:2, :] += jnp.where(is_ctx, dscale, 0.0)
        s_ref[2:3, :] += jnp.where(is_ctx, 0.0, dshift)
        s_ref[3:4, :] += jnp.where(is_ctx, 0.0, dscale)

    row = pl.BlockSpec((tm, d), lambda i: (i, 0))
    return pl.pallas_call(
        body, name=name, grid=(r // tm,), in_specs=[row, row, row, _mod_spec(d, nctx)],
        out_specs=[row, pl.BlockSpec((8, d), lambda i: (0, 0))],
        out_shape=[jax.ShapeDtypeStruct((r, d), F32), jax.ShapeDtypeStruct((8, d), F32)],
        compiler_params=_cp(1),
    )(dh, xb, dxa, modv)


def _loss_grad(xb, target, t_ctx, name):
    r, d = xb.shape
    tm = _row_tile(t_ctx)
    nctx = t_ctx // tm

    def body(x_ref, t_ref, l_ref, dx_ref):
        i = pl.program_id(0)

        @pl.when(i == 0)
        def _():
            l_ref[...] = jnp.zeros_like(l_ref)

        @pl.when(i < nctx)
        def _():
            dx_ref[...] = jnp.zeros_like(dx_ref)

        @pl.when(i >= nctx)
        def _():
            e = x_ref[...] - t_ref[...]
            dx_ref[...] = e / d
            l_ref[...] += 0.5 * jnp.sum(jnp.sum(e * e, axis=1, keepdims=True), axis=0, keepdims=True) / d

    row = pl.BlockSpec((tm, d), lambda i: (i, 0))
    return pl.pallas_call(
        body, name=name, grid=(r // tm,),
        in_specs=[row, pl.BlockSpec((tm, d), lambda i: (jnp.maximum(i - nctx, 0), 0))],
        out_specs=[pl.BlockSpec((8, 128), lambda i: (0, 0)), row],
        out_shape=[jax.ShapeDtypeStruct((8, 128), F32), jax.ShapeDtypeStruct((r, d), F32)],
        compiler_params=_cp(1),
    )(xb, target)


def _conv_specs(r, w, tm, cblk):
    hb = tm // HALO
    last = r // HALO - 1
    main = pl.BlockSpec((tm, w), lambda i: (i, cblk))
    top = pl.BlockSpec((HALO, w), lambda i: (jnp.maximum(i * hb - 1, 0), cblk))
    bot = pl.BlockSpec((HALO, w), lambda i: (jnp.minimum((i + 1) * hb, last), cblk))
    return [main, top, bot]


def _fill_pad(pad_ref, main, top, bot, top_ok, bot_ok, tm):
    pad_ref[0:HALO, :] = jnp.where(top_ok, top, 0.0)
    pad_ref[HALO:HALO + tm, :] = main
    pad_ref[HALO + tm:2 * HALO + tm, :] = jnp.where(bot_ok, bot, 0.0)


def _edges(i, nctx, nr):
    top_ok = jnp.logical_and(i != 0, i != nctx)
    bot_ok = jnp.logical_and(i != nctx - 1, i != nr - 1)
    return top_ok, bot_ok


def _conv_fwd(p, caw, cab, nag, nab, cbw, t_ctx, name):
    r = p.shape[0]
    w = p.shape[1] // 7
    ka, kb = caw.shape[0], cbw.shape[0]
    tm = _row_tile(t_ctx, 128)
    nctx, nr = t_ctx // tm, r // tm

    def body(av, avt, avb, ag, agt, agb, agate, bx, bxt, bxb, bb, bc, bct, bcb, bgate,
             caw_ref, cab_ref, nag_ref, nab_ref, cbw_ref, cat_ref, u1_ref, pad_a, pad_b):
        top_ok, bot_ok = _edges(pl.program_id(0), nctx, nr)
        _fill_pad(pad_a, av[...] * _sigmoid(ag[...]), avt[...] * _sigmoid(agt[...]),
                  avb[...] * _sigmoid(agb[...]), top_ok, bot_ok, tm)
        u1 = jnp.zeros((tm, w), F32) + cab_ref[...]
        for k in range(ka):
            u1 = u1 + caw_ref[k:k + 1, :] * pad_a[pl.ds(HALO - ka // 2 + k, tm), :]
        u1_ref[...] = u1
        n, _ = _ln_norm(u1)
        a_out = _silu(n * nag_ref[...] + nab_ref[...]) * _silu(agate[...])
        cat_ref[:, 0:w] = a_out.astype(BF16)
        _fill_pad(pad_b, bc[...] * bx[...], bct[...] * bxt[...], bcb[...] * bxb[...], top_ok, bot_ok, tm)
        v = jnp.zeros((tm, w), F32)
        for k in range(kb):
            v = v + cbw_ref[k:k + 1, :] * pad_b[pl.ds(HALO - kb // 2 + k, tm), :]
        cat_ref[:, w:2 * w] = (bb[...] * v * _silu(bgate[...])).astype(BF16)

    def main(cblk):
        return pl.BlockSpec((tm, w), lambda i: (i, cblk))

    def whole(a):
        return pl.BlockSpec(a.shape, lambda i: (0, 0))

    in_specs = (_conv_specs(r, w, tm, 0) + _conv_specs(r, w, tm, 1) + [main(2)] + _conv_specs(r, w, tm, 3)
                + [main(4)] + _conv_specs(r, w, tm, 5) + [main(6)]
                + [whole(caw), whole(cab), whole(nag), whole(nab), whole(cbw)])
    return pl.pallas_call(
        body, name=name, grid=(nr,), in_specs=in_specs,
        out_specs=[pl.BlockSpec((tm, 2 * w), lambda i: (i, 0)), pl.BlockSpec((tm, w), lambda i: (i, 0))],
        out_shape=[jax.ShapeDtypeStruct((r, 2 * w), BF16), jax.ShapeDtypeStruct((r, w), F32)],
        scratch_shapes=[pltpu.VMEM((tm + 2 * HALO, w), F32), pltpu.VMEM((tm + 2 * HALO, w), F32)],
        compiler_params=_cp(1),
    )(*([p] * 15), caw, cab, nag, nab, cbw)


def _conv_bwd1(dcat, p, u1, nag, nab, t_ctx, name):
    r, w = u1.shape
    tm = _row_tile(t_ctx, 128)

    def body(da_ref, agate_ref, u1_ref, nag_ref, nab_ref, du1_ref, dag_ref, s_ref):
        @pl.when(pl.program_id(0) == 0)
        def _():
            s_ref[...] = jnp.zeros_like(s_ref)

        n, rstd = _ln_norm(u1_ref[...])
        g = nag_ref[...]
        u2 = n * g + nab_ref[...]
        da = da_ref[...]
        ag = agate_ref[...]
        dag_ref[...] = (da * _silu(u2) * _dsilu(ag)).astype(BF16)
        du2 = da * _silu(ag) * _dsilu(u2)
        du1 = _ln_bwd(du2 * g, n, rstd)
        du1_ref[...] = du1
        s_ref[0:1, :] += jnp.sum(du2 * n, axis=0, keepdims=True)
        s_ref[1:2, :] += jnp.sum(du2, axis=0, keepdims=True)
        s_ref[2:3, :] += jnp.sum(du1, axis=0, keepdims=True)

    def win(cblk):
        return pl.BlockSpec((tm, w), lambda i: (i, cblk))

    one = pl.BlockSpec((1, w), lambda i: (0, 0))
    return pl.pallas_call(
        body, name=name, grid=(r // tm,), in_specs=[win(0), win(2), win(0), one, one],
        out_specs=[win(0), win(0), pl.BlockSpec((8, w), lambda i: (0, 0))],
        out_shape=[jax.ShapeDtypeStruct((r, w), F32), jax.ShapeDtypeStruct((r, w), BF16),
                   jax.ShapeDtypeStruct((8, w), F32)],
        compiler_params=_cp(1),
    )(dcat, p, u1, nag, nab)


def _conv_bwd2(du1, dcat, p, dag, caw, cbw, t_ctx, name):
    r, w = du1.shape
    ka, kb = caw.shape[0], cbw.shape[0]
    tm = _row_tile(t_ctx, 128)
    nctx, nr = t_ctx // tm, r // tm

    def body(du, dut, dub, av, avt, avb, ag, agt, agb, db, dbt, dbb, bx, bxt, bxb, bb, bbt, bbb,
             bc, bct, bcb, bg, bgt, bgb, dag_ref, caw_ref, cbw_ref, dp_ref, dcaw_ref, dcbw_ref,
             pad_u, pad_d, pad_w, pad_v):
        top_ok, bot_ok = _edges(pl.program_id(0), nctx, nr)

        @pl.when(pl.program_id(0) == 0)
        def _():
            dcaw_ref[...] = jnp.zeros_like(dcaw_ref)
            dcbw_ref[...] = jnp.zeros_like(dcbw_ref)

        sg = _sigmoid(ag[...])
        _fill_pad(pad_u, av[...] * sg, avt[...] * _sigmoid(agt[...]), avb[...] * _sigmoid(agb[...]),
                  top_ok, bot_ok, tm)
        du_m = du[...]
        _fill_pad(pad_d, du_m, dut[...], dub[...], top_ok, bot_ok, tm)
        du0 = jnp.zeros((tm, w), F32)
        for k in range(ka):
            du0 = du0 + caw_ref[k:k + 1, :] * pad_d[pl.ds(HALO + ka // 2 - k, tm), :]
            dcaw_ref[k:k + 1, :] += jnp.sum(du_m * pad_u[pl.ds(HALO - ka // 2 + k, tm), :], axis=0, keepdims=True)
        dp_ref[:, 0:w] = (du0 * sg).astype(BF16)
        dp_ref[:, w:2 * w] = (du0 * av[...] * sg * (1.0 - sg)).astype(BF16)
        dp_ref[:, 2 * w:3 * w] = dag_ref[...]

        _fill_pad(pad_w, bc[...] * bx[...], bct[...] * bxt[...], bcb[...] * bxb[...], top_ok, bot_ok, tm)
        v = jnp.zeros((tm, w), F32)
        for k in range(kb):
            v = v + cbw_ref[k:k + 1, :] * pad_w[pl.ds(HALO - kb // 2 + k, tm), :]
        db_m, bb_m, bg_m = db[...], bb[...], bg[...]
        sbg = _silu(bg_m)
        dv_m = db_m * bb_m * sbg
        _fill_pad(pad_v, dv_m, dbt[...] * bbt[...] * _silu(bgt[...]), dbb[...] * bbb[...] * _silu(bgb[...]),
                  top_ok, bot_ok, tm)
        dw0 = jnp.zeros((tm, w), F32)
        for k in range(kb):
            dw0 = dw0 + cbw_ref[k:k + 1, :] * pad_v[pl.ds(HALO + kb // 2 - k, tm), :]
            dcbw_ref[k:k + 1, :] += jnp.sum(dv_m * pad_w[pl.ds(HALO - kb // 2 + k, tm), :], axis=0, keepdims=True)
        dp_ref[:, 3 * w:4 * w] = (dw0 * bc[...]).astype(BF16)
        dp_ref[:, 4 * w:5 * w] = (db_m * v * sbg).astype(BF16)
        dp_ref[:, 5 * w:6 * w] = (dw0 * bx[...]).astype(BF16)
        dp_ref[:, 6 * w:7 * w] = (db_m * bb_m * v * _dsilu(bg_m)).astype(BF16)

    def whole(a):
        return pl.BlockSpec(a.shape, lambda i: (0, 0))

    in_specs = (_conv_specs(r, w, tm, 0) + _conv_specs(r, w, tm, 0) + _conv_specs(r, w, tm, 1)
                + _conv_specs(r, w, tm, 1) + _conv_specs(r, w, tm, 3) + _conv_specs(r, w, tm, 4)
                + _conv_specs(r, w, tm, 5) + _conv_specs(r, w, tm, 6)
                + [pl.BlockSpec((tm, w), lambda i: (i, 0)), whole(caw), whole(cbw)])
    pad = pltpu.VMEM((tm + 2 * HALO, w), F32)
    return pl.pallas_call(
        body, name=name, grid=(nr,), in_specs=in_specs,
        out_specs=[pl.BlockSpec((tm, 7 * w), lambda i: (i, 0)), pl.BlockSpec((32, w), lambda i: (0, 0)),
                   pl.BlockSpec((8, w), lambda i: (0, 0))],
        out_shape=[jax.ShapeDtypeStruct((r, 7 * w), BF16), jax.ShapeDtypeStruct((32, w), F32),
                   jax.ShapeDtypeStruct((8, w), F32)],
        scratch_shapes=[pad, pad, pad, pad], compiler_params=_cp(1),
    )(*([du1] * 3), *([p] * 6), *([dcat] * 3), *([p] * 12), dag, caw, cbw)


def _rms(xh):
    r = lax.rsqrt(jnp.mean(xh * xh, axis=-1, keepdims=True) + RMS_EPS)
    return xh * r, r


def _qk_fwd(p, cos, sin, qg, kg, att, kv, t_ctx, name):
    r = p.shape[0]
    tm = _row_tile(t_ctx)

    def body(q_ref, k_ref, v_ref, cos_ref, sin_ref, qg_ref, kg_ref, qr_ref, kr_ref, vb_ref):
        cs, sn = cos_ref[...], sin_ref[...]
        for src, g_ref, dst, nh in ((q_ref, qg_ref, qr_ref, att // HEAD_DIM), (k_ref, kg_ref, kr_ref, kv // HEAD_DIM)):
            for h in range(nh):
                cols = slice(h * HEAD_DIM, (h + 1) * HEAD_DIM)
                n, _ = _rms(src[:, cols])
                n = n * g_ref[...]
                dst[:, cols] = (n * cs + _partner(n) * sn).astype(BF16)
        vb_ref[...] = v_ref[...].astype(BF16)

    def rows(width, cblk):
        return pl.BlockSpec((tm, width), lambda i: (i, cblk))

    one = pl.BlockSpec((1, HEAD_DIM), lambda i: (0, 0))
    return pl.pallas_call(
        body, name=name, grid=(r // tm,),
        in_specs=[rows(att, 0), rows(kv, 2 * att // kv), rows(kv, 2 * att // kv + 1),
                  rows(HEAD_DIM, 0), rows(HEAD_DIM, 0), one, one],
        out_specs=[rows(att, 0), rows(kv, 0), rows(kv, 0)],
        out_shape=[jax.ShapeDtypeStruct((r, att), BF16), jax.ShapeDtypeStruct((r, kv), BF16),
                   jax.ShapeDtypeStruct((r, kv), BF16)],
        compiler_params=_cp(1),
    )(p, p, p, cos, sin, qg, kg)


def _qk_bwd(p, dqr, dkr, cos, sin, qg, kg, att, kv, t_ctx, name):
    r = p.shape[0]
    tm = _row_tile(t_ctx)

    def body(q_ref, k_ref, dqr_ref, dkr_ref, cos_ref, sin_ref, qg_ref, kg_ref, dq_ref, dk_ref, s_ref):
        @pl.when(pl.program_id(0) == 0)
        def _():
            s_ref[...] = jnp.zeros_like(s_ref)

        cs, sn = cos_ref[...], sin_ref[...]
        for row, (src, dsrc, g_ref, dst, nh) in enumerate((
                (q_ref, dqr_ref, qg_ref, dq_ref, att // HEAD_DIM), (k_ref, dkr_ref, kg_ref, dk_ref, kv // HEAD_DIM))):
            dg = jnp.zeros((1, HEAD_DIM), F32)
            for h in range(nh):
                cols = slice(h * HEAD_DIM, (h + 1) * HEAD_DIM)
                n0, rr = _rms(src[:, cols])
                d = dsrc[:, cols]
                dng = d * cs + _partner(d * sn)
                dg = dg + jnp.sum(dng * n0, axis=0, keepdims=True)
                dn0 = dng * g_ref[...]
                dst[:, cols] = (rr * (dn0 - n0 * jnp.mean(dn0 * n0, axis=-1, keepdims=True))).astype(BF16)
            s_ref[row:row + 1, :] += dg

    def rows(width, cblk):
        return pl.BlockSpec((tm, width), lambda i: (i, cblk))

    one = pl.BlockSpec((1, HEAD_DIM), lambda i: (0, 0))
    return pl.pallas_call(
        body, name=name, grid=(r // tm,),
        in_specs=[rows(att, 0), rows(kv, 2 * att // kv), rows(att, 0), rows(kv, 0),
                  rows(HEAD_DIM, 0), rows(HEAD_DIM, 0), one, one],
        out_specs=[rows(att, 0), rows(kv, 0), pl.BlockSpec((8, HEAD_DIM), lambda i: (0, 0))],
        out_shape=[jax.ShapeDtypeStruct((r, att), BF16), jax.ShapeDtypeStruct((r, kv), BF16),
                   jax.ShapeDtypeStruct((8, HEAD_DIM), F32)],
        compiler_params=_cp(1),
    )(p, p, dqr, dkr, cos, sin, qg, kg)


def _stack_heads(x, tq):
    return jnp.concatenate([x[:, g * HEAD_DIM:(g + 1) * HEAD_DIM] for g in range(GQA_GROUP)], axis=0)


def _attn_tiles(t_ctx, s_lat):
    tq = _div(t_ctx, 256, 8)
    tkl = _div(s_lat, 512, 8)
    return tq, tkl


def _flash_fwd(qr, kr, vb, p, att, t_ctx, name):
    r = qr.shape[0]
    s_lat = r - t_ctx
    gw = GQA_GROUP * HEAD_DIM
    nkv = att // gw
    tq, tkl = _attn_tiles(t_ctx, s_lat)
    nq, nq_ctx, n_lat = r // tq, t_ctx // tq, s_lat // tkl

    def body(q_ref, k_ref, v_ref, g_ref, o_ref, og_ref, lse_ref, m_s, l_s, acc_s):
        qi = pl.program_id(1)
        q4 = _stack_heads(q_ref[...], tq)
        m_s[...] = jnp.full_like(m_s, -1e30)
        l_s[...] = jnp.zeros_like(l_s)
        acc_s[...] = jnp.zeros_like(acc_s)

        def step(off, size):
            kb = k_ref[pl.ds(off, size), :]
            s = lax.dot_general(q4, kb, (((1,), (1,)), ((), ())), preferred_element_type=F32) * ATTN_SCALE
            m_old = m_s[...]
            m_new = jnp.maximum(m_old, jnp.max(s, axis=-1, keepdims=True))
            pe = jnp.exp(s - m_new)
            a = jnp.exp(m_old - m_new)
            l_s[...] = a * l_s[...] + jnp.sum(pe, axis=-1, keepdims=True)
            acc_s[...] = a * acc_s[...] + jnp.dot(pe.astype(BF16), v_ref[pl.ds(off, size), :],
                                                  preferred_element_type=F32)
            m_s[...] = m_new

        step(0, t_ctx)

        def lat(n, carry):
            step(pl.multiple_of(t_ctx + n * tkl, math.gcd(t_ctx, tkl)), tkl)
            return carry

        lax.fori_loop(0, jnp.where(qi < nq_ctx, 0, n_lat), lat, 0)
        o4 = acc_s[...] / l_s[...]
        lse_ref[...] = m_s[...] + jnp.log(l_s[...])
        sg = _silu(g_ref[...])
        for g in range(GQA_GROUP):
            cols = slice(g * HEAD_DIM, (g + 1) * HEAD_DIM)
            og = o4[g * tq:(g + 1) * tq, :]
            o_ref[:, cols] = og
            og_ref[:, cols] = (og * sg[:, cols]).astype(BF16)

    qspec = pl.BlockSpec((tq, gw), lambda h, i: (i, h))
    kspec = pl.BlockSpec((r, HEAD_DIM), lambda h, i: (0, h))
    return pl.pallas_call(
        body, name=name, grid=(nkv, nq),
        in_specs=[qspec, kspec, kspec, pl.BlockSpec((tq, gw), lambda h, i: (i, att // gw + h))],
        out_specs=[qspec, qspec, pl.BlockSpec((None, None, GQA_GROUP * tq, 1), lambda h, i: (h, i, 0, 0))],
        out_shape=[jax.ShapeDtypeStruct((r, att), F32), jax.ShapeDtypeStruct((r, att), BF16),
                   jax.ShapeDtypeStruct((nkv, nq, GQA_GROUP * tq, 1), F32)],
        scratch_shapes=[pltpu.VMEM((GQA_GROUP * tq, 1), F32), pltpu.VMEM((GQA_GROUP * tq, 1), F32),
                        pltpu.VMEM((GQA_GROUP * tq, HEAD_DIM), F32)],
        compiler_params=_cp(2),
    )(qr, kr, vb, p)


def _flash_bwd(qr, kr, vb, p, o, dog, lse, att, kv, t_ctx, name):
    r = qr.shape[0]
    s_lat = r - t_ctx
    gw = GQA_GROUP * HEAD_DIM
    nkv = att // gw
    tq, tkl = _attn_tiles(t_ctx, s_lat)
    nq, nq_ctx, n_lat = r // tq, t_ctx // tq, s_lat // tkl

    def body(q_ref, k_ref, v_ref, g_ref, o_ref, dog_ref, lse_ref, dq_ref, dgate_ref, dk_ref, dv_ref, dq_s):
        qi = pl.program_id(1)

        @pl.when(qi == 0)
        def _():
            dk_ref[...] = jnp.zeros_like(dk_ref)
            dv_ref[...] = jnp.zeros_like(dv_ref)

        gate, ov, dogv = g_ref[...], o_ref[...], dog_ref[...]
        dgate_ref[...] = (dogv * ov * _dsilu(gate)).astype(BF16)
        do = dogv * _silu(gate)
        do4 = _stack_heads(do, tq)
        delta = jnp.sum(do4 * _stack_heads(ov, tq), axis=-1, keepdims=True)
        do4 = do4.astype(BF16)
        q4 = _stack_heads(q_ref[...], tq)
        lse_v = lse_ref[...]
        dq_s[...] = jnp.zeros_like(dq_s)

        def step(off, size):
            kb = k_ref[pl.ds(off, size), :]
            vv = v_ref[pl.ds(off, size), :]
            s = lax.dot_general(q4, kb, (((1,), (1,)), ((), ())), preferred_element_type=F32) * ATTN_SCALE
            pe = jnp.exp(s - lse_v)
            dp = lax.dot_general(do4, vv, (((1,), (1,)), ((), ())), preferred_element_type=F32)
            ds = (pe * (dp - delta) * ATTN_SCALE).astype(BF16)
            dv_ref[pl.ds(off, size), :] += lax.dot_general(pe.astype(BF16), do4, (((0,), (0,)), ((), ())),
                                                           preferred_element_type=F32)
            dk_ref[pl.ds(off, size), :] += lax.dot_general(ds, q4, (((0,), (0,)), ((), ())),
                                                           preferred_element_type=F32)
            dq_s[...] += jnp.dot(ds, kb, preferred_element_type=F32)

        step(0, t_ctx)

        def lat(n, carry):
            step(pl.multiple_of(t_ctx + n * tkl, math.gcd(t_ctx, tkl)), tkl)
            return carry

        lax.fori_loop(0, jnp.where(qi < nq_ctx, 0, n_lat), lat, 0)
        for g in range(GQA_GROUP):
            dq_ref[:, g * HEAD_DIM:(g + 1) * HEAD_DIM] = dq_s[g * tq:(g + 1) * tq, :]

    qspec = pl.BlockSpec((tq, gw), lambda h, i: (i, h))
    kspec = pl.BlockSpec((r, HEAD_DIM), lambda h, i: (0, h))
    return pl.pallas_call(
        body, name=name, grid=(nkv, nq),
        in_specs=[qspec, kspec, kspec, pl.BlockSpec((tq, gw), lambda h, i: (i, att // gw + h)), qspec, qspec,
                  pl.BlockSpec((None, None, GQA_GROUP * tq, 1), lambda h, i: (h, i, 0, 0))],
        out_specs=[qspec, qspec, kspec, kspec],
        out_shape=[jax.ShapeDtypeStruct((r, att), F32), jax.ShapeDtypeStruct((r, att), BF16),
                   jax.ShapeDtypeStruct((r, kv), F32), jax.ShapeDtypeStruct((r, kv), F32)],
        scratch_shapes=[pltpu.VMEM((GQA_GROUP * tq, HEAD_DIM), F32)],
        compiler_params=_cp(2),
    )(qr, kr, vb, p, o, dog, lse)


def _rope_tables(t_ctx, s_lat):
    rows_n = s_lat // GRID_W
    row = jnp.repeat(jnp.arange(rows_n, dtype=F32), GRID_W)
    col = jnp.tile(jnp.arange(GRID_W, dtype=F32), rows_n)
    axis_dim = HEAD_DIM // 2
    inv_freq = ROPE_THETA ** (-jnp.arange(0, axis_dim, 2, dtype=F32) / axis_dim)
    ang_r = row[:, None] * inv_freq[None, :]
    ang_c = col[:, None] * inv_freq[None, :]
    cos = jnp.concatenate([jnp.cos(ang_r), jnp.cos(ang_r), jnp.cos(ang_c), jnp.cos(ang_c)], axis=1)
    sin = jnp.concatenate([-jnp.sin(ang_r), jnp.sin(ang_r), -jnp.sin(ang_c), jnp.sin(ang_c)], axis=1)
    cos = jnp.concatenate([jnp.ones((t_ctx, HEAD_DIM), F32), cos], axis=0)
    sin = jnp.concatenate([jnp.zeros((t_ctx, HEAD_DIM), F32), sin], axis=0)
    return cos, sin


def _pad_rows(a, rows):
    return jnp.pad(a, ((0, rows - a.shape[0]), (0, 0)))


def kernel(x, c, ctx, c_ctx, w_mod, b_mod, post_ln_g, post_ln_b, w_in_e, conv_a_w, conv_a_b, norm_a_g, norm_a_b, conv_b_w, w_out_e, w_in_o, q_norm_g, k_norm_g, w_out_o, loss_target, m_c_ctx, m_w_mod, m_b_mod, m_post_ln_g, m_post_ln_b, m_w_in_e, m_conv_a_w, m_conv_a_b, m_norm_a_g, m_norm_a_b, m_conv_b_w, m_w_out_e, m_w_in_o, m_q_norm_g, m_k_norm_g, m_w_out_o, v_c_ctx, v_w_mod, v_b_mod, v_post_ln_g, v_post_ln_b, v_w_in_e, v_conv_a_w, v_conv_a_b, v_norm_a_g, v_norm_a_b, v_conv_b_w, v_w_out_e, v_w_in_o, v_q_norm_g, v_k_norm_g, v_w_out_o):
    depth, d, mcols = w_mod.shape
    s_lat, t_ctx = x.shape[1], ctx.shape[1]
    n_even, n_odd = w_in_e.shape[0], w_in_o.shape[0]
    ka, kb = conv_a_w.shape[1], conv_b_w.shape[1]
    wch = conv_a_w.shape[2] * NDEV
    att = w_out_o.shape[1] * NDEV
    kv = (w_in_o.shape[2] * NDEV - 2 * att) // 2
    alpha = (2.0 * depth) ** 0.25
    me = 4 * lax.axis_index("x") + 2 * lax.axis_index("y") + lax.axis_index("c")

    c_all = _all_gather(_pad_rows(c, 8), "ag_c")[:, 0, :]
    c16 = jnp.concatenate([c_all, _pad_rows(c_ctx[None, :], 8)], axis=0)
    sc16 = _ew(_silu, BF16, "silu_c", c16)
    m_part = _mm(sc16, w_mod, "nn", F32, "mod_fwd", tm=16, tn=mcols, tk=d)
    m_all = _all_gather(m_part, "ag_mod")
    m_full = m_all.reshape(NDEV, 16, depth, mcols).transpose(2, 1, 0, 3).reshape(depth * 16, 3 * d)
    b16 = jnp.broadcast_to(b_mod[:, None, :], (depth, 16, 3 * d)).reshape(depth * 16, 3 * d)
    m_full = _ew(lambda a, b: a + b, F32, "mod_bias", m_full, b16).reshape(depth, 16, 3, d)
    modv = [jnp.stack([m_full[l, 8], lax.dynamic_index_in_dim(m_full[l], me, 0, keepdims=False)])
            for l in range(depth)]

    cw = jnp.concatenate([_pad_rows(conv_a_w[i], 32) for i in range(n_even)]
                         + [_pad_rows(conv_b_w[i], 8) for i in range(n_even)], axis=0)
    cw_all = _all_gather(cw, "ag_convw").transpose(1, 0, 2).reshape(cw.shape[0], wch)
    caw = [cw_all[32 * i:32 * i + ka] for i in range(n_even)]
    cbw = [cw_all[32 * n_even + 8 * i:32 * n_even + 8 * i + kb] for i in range(n_even)]

    def gathered(wsh, name):
        return _all_gather(_ew(lambda t: t, BF16, name + "_bf16", wsh), name)

    w_in_e_g = [gathered(w_in_e[i], f"ag_w_in_e{i}") for i in range(n_even)]
    w_out_e_g = [gathered(w_out_e[i], f"ag_w_out_e{i}").reshape(1, 2 * wch, d) for i in range(n_even)]
    w_in_o_g = [gathered(w_in_o[i], f"ag_w_in_o{i}") for i in range(n_odd)]
    w_out_o_g = [gathered(w_out_o[i], f"ag_w_out_o{i}").reshape(1, att, d) for i in range(n_odd)]

    cos, sin = _rope_tables(t_ctx, s_lat)
    lnp = [jnp.stack([post_ln_g[l], post_ln_b[l]]) for l in range(depth)]

    xb = jnp.concatenate([ctx[0], x[0]], axis=0)
    saved = []
    for l in range(depth):
        i = l // 2
        h = _modulate(xb, modv[l], t_ctx, f"modulate{l}")
        if l % 2 == 0:
            p = _mm(h, w_in_e_g[i], "nn", F32, f"in_proj{l}", tn=256)
            cat, u1 = _conv_fwd(p, caw[i], conv_a_b[i][None], norm_a_g[i][None], norm_a_b[i][None], cbw[i],
                                t_ctx, f"conv_fwd{l}")
            y = _mm(cat, w_out_e_g[i], "nn", F32, f"out_proj{l}")
            saved.append((xb, h, p, cat, u1, y))
        else:
            p = _mm(h, w_in_o_g[i], "nn", F32, f"in_proj{l}", tn=640)
            qr, kr, vb = _qk_fwd(p, cos, sin, q_norm_g[i][None], k_norm_g[i][None], att, kv, t_ctx, f"qk_fwd{l}")
            o, og, lse = _flash_fwd(qr, kr, vb, p, att, t_ctx, f"flash_fwd{l}")
            y = _mm(og, w_out_o_g[i], "nn", F32, f"out_proj{l}")
            saved.append((xb, h, p, qr, kr, vb, o, og, lse, y))
        xb = _postln_fwd(xb, y, modv[l], lnp[l], alpha, t_ctx, f"postln_fwd{l}")

    loss_blk, dxb = _loss_grad(xb, loss_target[0], t_ctx, "loss")
    loss = lax.psum(loss_blk[0, 0], ("x", "y", "c"))

    small = {}
    dmod = [None] * depth
    big = {}
    for l in reversed(range(depth)):
        i = l // 2
        sv = saved[l]
        x_in, h, p, y = sv[0], sv[1], sv[2], sv[-1]
        dy, dxa, s_ln = _postln_bwd(dxb, x_in, y, modv[l], lnp[l], alpha, t_ctx, f"postln_bwd{l}")
        small[f"ln{l}"] = s_ln
        if l % 2 == 0:
            cat, u1 = sv[3], sv[4]
            dcat = _mm(dy, w_out_e_g[i], "nt", F32, f"d_cat{l}", tk=d)
            big[f"w_out_e{i}"] = _mm(cat, dy, "tn", BF16, f"d_w_out{l}", tn=d, tk=768, out_nd=1).reshape(
                NDEV, 2 * wch // NDEV, d)
            du1, dag, s_c1 = _conv_bwd1(dcat, p, u1, norm_a_g[i][None], norm_a_b[i][None], t_ctx, f"conv_bwd1_{l}")
            dp, dcaw, dcbw = _conv_bwd2(du1, dcat, p, dag, caw[i], cbw[i], t_ctx, f"conv_bwd2_{l}")
            small[f"c1_{i}"], small[f"caw{i}"], small[f"cbw{i}"] = s_c1, dcaw, dcbw
            w_in_g, wname = w_in_e_g[i], f"w_in_e{i}"
        else:
            qr, kr, vb, o, og, lse = sv[3:9]
            dog = _mm(dy, w_out_o_g[i], "nt", F32, f"d_og{l}", tk=d)
            big[f"w_out_o{i}"] = _mm(og, dy, "tn", BF16, f"d_w_out{l}", tn=d, tk=768, out_nd=1).reshape(
                NDEV, att // NDEV, d)
            dqr, dgate, dkr, dvr = _flash_bwd(qr, kr, vb, p, o, dog, lse, att, kv, t_ctx, f"flash_bwd{l}")
            dq, dk, s_qk = _qk_bwd(p, dqr, dkr, cos, sin, q_norm_g[i][None], k_norm_g[i][None], att, kv, t_ctx,
                                   f"qk_bwd{l}")
            small[f"qk{i}"] = jnp.pad(s_qk, ((0, 0), (0, d - HEAD_DIM)))
            dp = jnp.concatenate([dq, dgate, dk, dvr.astype(BF16)], axis=1)
            w_in_g, wname = w_in_o_g[i], f"w_in_o{i}"
        dh = _mm(dp, w_in_g, "nt", F32, f"d_h{l}", tn=d, tk=w_in_g.shape[2])
        big[wname] = _mm(h, dp, "tn", BF16, f"d_w_in{l}", tm=d, tn=_div(w_in_g.shape[2], 512, 128), tk=768,
                         out_nd=NDEV)
        dxb, s_mod = _mod_bwd(dh, x_in, dxa, modv[l], t_ctx, f"mod_bwd{l}")
        dmod[l] = jnp.stack([jnp.stack([s_mod[0], s_mod[1], s_ln[2]]), jnp.stack([s_mod[2], s_mod[3], s_ln[3]])])
    grad_x = dxb[t_ctx:][None]

    dm_loc = jnp.stack(dmod).reshape(depth * 2, 3 * d)
    dm_all = _all_gather(dm_loc, "ag_dmod").reshape(NDEV, depth, 2, 3 * d)
    dm_ctx = _sum_lead(dm_all[:, :, 0, :], "sum_dmod_ctx")
    dm16 = jnp.concatenate([dm_all[:, :, 1, :].transpose(1, 0, 2), dm_ctx[:, None, :],
                            jnp.zeros((depth, 7, 3 * d), F32)], axis=1)
    g_b_mod = _sum_lead(dm16.transpose(1, 0, 2), "sum_b_mod")
    dm16_me = lax.dynamic_slice_in_dim(dm16.reshape(depth, 16, NDEV, mcols), me, 1, axis=2)
    dm16_me = dm16_me.reshape(depth, 16, mcols).transpose(1, 0, 2).reshape(16, depth * mcols)
    g_w_mod = _mm(sc16, dm16_me, "tn", F32, "mod_bwd_w", tm=d, tn=mcols, tk=16, out_nd=depth)
    dsc16 = _mm(dm16_me, w_mod, "nt", F32, "mod_bwd_c", tm=16, tn=d, tk=mcols)
    small["c_ctx"] = dsc16[8:16]

    names = sorted(small)
    offs, rows = {}, 0
    for nme in names:
        offs[nme] = rows
        rows += small[nme].shape[0]
    sm_all = _all_gather(jnp.concatenate([small[nme] for nme in names], axis=0), "ag_small")
    sm = _sum_lead(sm_all, "sum_small")

    def part(nme, lo, hi):
        return sm[offs[nme] + lo:offs[nme] + hi]

    g_post_ln_g = jnp.concatenate([part(f"ln{l}", 0, 1) for l in range(depth)], axis=0)
    g_post_ln_b = jnp.concatenate([part(f"ln{l}", 1, 2) for l in range(depth)], axis=0)
    g_norm_a_g = jnp.concatenate([part(f"c1_{i}", 0, 1) for i in range(n_even)], axis=0)
    g_norm_a_b = jnp.concatenate([part(f"c1_{i}", 1, 2) for i in range(n_even)], axis=0)
    g_conv_a_b = jnp.concatenate([part(f"c1_{i}", 2, 3) for i in range(n_even)], axis=0)
    g_q_norm_g = jnp.concatenate([part(f"qk{i}", 0, 1)[:, :HEAD_DIM] for i in range(n_odd)], axis=0)
    g_k_norm_g = jnp.concatenate([part(f"qk{i}", 1, 2)[:, :HEAD_DIM] for i in range(n_odd)], axis=0)
    wsh = wch // NDEV
    g_conv_a_w = jnp.stack([lax.dynamic_slice_in_dim(part(f"caw{i}", 0, ka), me * wsh, wsh, axis=1)
                            for i in range(n_even)])
    g_conv_b_w = jnp.stack([lax.dynamic_slice_in_dim(part(f"cbw{i}", 0, kb), me * wsh, wsh, axis=1)
                            for i in range(n_even)])
    g_c_ctx = _ew(lambda a, b: a * _dsilu(b), F32, "d_c_ctx", part("c_ctx", 0, 8), _pad_rows(c_ctx[None, :], 8))[0]

    def exchange(prefix, n, w, m, v):
        outs = []
        for i in range(n):
            recv = _all_to_all(big[f"{prefix}{i}"], f"a2a_{prefix}{i}")
            outs.append(_reduce_adamw(recv, w[i], m[i], v[i], f"adamw_{prefix}{i}"))
        return [jnp.stack([o[j] for o in outs]) for j in range(4)]

    r_w_in_e = exchange("w_in_e", n_even, w_in_e, m_w_in_e, v_w_in_e)
    r_w_out_e = exchange("w_out_e", n_even, w_out_e, m_w_out_e, v_w_out_e)
    r_w_in_o = exchange("w_in_o", n_odd, w_in_o, m_w_in_o, v_w_in_o)
    r_w_out_o = exchange("w_out_o", n_odd, w_out_o, m_w_out_o, v_w_out_o)

    grads = {
        "c_ctx": g_c_ctx, "w_mod": g_w_mod, "b_mod": g_b_mod, "post_ln_g": g_post_ln_g, "post_ln_b": g_post_ln_b,
        "conv_a_w": g_conv_a_w, "conv_a_b": g_conv_a_b, "norm_a_g": g_norm_a_g, "norm_a_b": g_norm_a_b,
        "conv_b_w": g_conv_b_w, "q_norm_g": g_q_norm_g, "k_norm_g": g_k_norm_g,
    }
    state = {
        "c_ctx": (c_ctx, m_c_ctx, v_c_ctx), "w_mod": (w_mod, m_w_mod, v_w_mod), "b_mod": (b_mod, m_b_mod, v_b_mod),
        "post_ln_g": (post_ln_g, m_post_ln_g, v_post_ln_g), "post_ln_b": (post_ln_b, m_post_ln_b, v_post_ln_b),
        "conv_a_w": (conv_a_w, m_conv_a_w, v_conv_a_w), "conv_a_b": (conv_a_b, m_conv_a_b, v_conv_a_b),
        "norm_a_g": (norm_a_g, m_norm_a_g, v_norm_a_g), "norm_a_b": (norm_a_b, m_norm_a_b, v_norm_a_b),
        "conv_b_w": (conv_b_w, m_conv_b_w, v_conv_b_w), "q_norm_g": (q_norm_g, m_q_norm_g, v_q_norm_g),
        "k_norm_g": (k_norm_g, m_k_norm_g, v_k_norm_g),
    }
    res = {"w_in_e": r_w_in_e, "w_out_e": r_w_out_e, "w_in_o": r_w_in_o, "w_out_o": r_w_out_o}
    for nme, g in grads.items():
        w, m, v = state[nme]
        res[nme] = [g] + _adamw(w, g, m, v, f"adamw_{nme}")
    order = ["c_ctx", "w_mod", "b_mod", "post_ln_g", "post_ln_b", "w_in_e", "conv_a_w", "conv_a_b", "norm_a_g",
             "norm_a_b", "conv_b_w", "w_out_e", "w_in_o", "q_norm_g", "k_norm_g", "w_out_o"]
    return (loss, grad_x, *[res[nme][0] for nme in order], *[res[nme][1] for nme in order],
            *[res[nme][2] for nme in order], *[res[nme][3] for nme in order])
```

```python
import functools
import math

import jax
import jax.numpy as jnp
from jax import lax
from jax.experimental import pallas as pl
from jax.experimental.pallas import tpu as pltpu

F32 = jnp.float32
BF16 = jnp.bfloat16

NDEV = 8
GRID_W = 64
HEAD_DIM = 128
GQA_GROUP = 4
ROPE_THETA = 10000.0
LN_EPS = 1e-5
RMS_EPS = 1e-6
ATTN_SCALE = HEAD_DIM ** -0.5
ADAM_LR = 0.001
ADAM_B1 = 0.9
ADAM_B2 = 0.999
ADAM_EPS = 1e-08
ADAM_WD = 0.01
ADAM_STEP = 10
HALO = 16
VMEM_LIMIT = 56 * 1024 * 1024
MESH = pl.DeviceIdType.MESH


def _cp(n_axes):
    return pltpu.CompilerParams(dimension_semantics=("arbitrary",) * n_axes, vmem_limit_bytes=VMEM_LIMIT)


def _div(dim, pref, mult):
    t = min(pref, dim) // mult * mult
    while t >= mult:
        if dim % t == 0:
            return t
        t -= mult
    return dim


def _sigmoid(x):
    return 1.0 / (1.0 + jnp.exp(-x))


def _silu(x):
    return x * _sigmoid(x)


def _dsilu(x):
    s = _sigmoid(x)
    return s * (1.0 + x * (1.0 - s))


def _ln_norm(z):
    mu = jnp.mean(z, axis=-1, keepdims=True)
    zc = z - mu
    var = jnp.mean(zc * zc, axis=-1, keepdims=True)
    rstd = lax.rsqrt(var + LN_EPS)
    return zc * rstd, rstd


def _ln_bwd(dn, n, rstd):
    return rstd * (dn - jnp.mean(dn, axis=-1, keepdims=True) - n * jnp.mean(dn * n, axis=-1, keepdims=True))


def _partner(x):
    lane = lax.broadcasted_iota(jnp.int32, x.shape, 1)
    return jnp.where((lane % 64) < 32, pltpu.roll(x, 96, 1), pltpu.roll(x, 32, 1))


def _mm(a, b, mode, out_dtype, name, tm=768, tn=2048, tk=2048, out_nd=1):
    if mode == "nn":
        m, kdim = a.shape
        nd, _, ns = b.shape
        tm, tn, tk = _div(m, tm, 8), _div(ns, tn, 128), _div(kdim, tk, 128)
        nbs = ns // tn
        grid = (m // tm, nd * nbs, kdim // tk)
        a_spec = pl.BlockSpec((tm, tk), lambda i, j, k: (i, k))
        b_spec = pl.BlockSpec((None, tk, tn), lambda i, j, k: (j // nbs, k, j % nbs))
        o_spec = pl.BlockSpec((tm, tn), lambda i, j, k: (i, j))
        out_shape = (m, nd * ns)
        dims = (((1,), (0,)), ((), ()))
    elif mode == "nt":
        m, _ = a.shape
        nd, ko, ns = b.shape
        tm, tn, tk = _div(m, tm, 8), _div(ko, tn, 128), _div(ns, tk, 128)
        kbs = ns // tk
        grid = (m // tm, ko // tn, nd * kbs)
        a_spec = pl.BlockSpec((tm, tk), lambda i, j, k: (i, k))
        b_spec = pl.BlockSpec((None, tn, tk), lambda i, j, k: (k // kbs, j, k % kbs))
        o_spec = pl.BlockSpec((tm, tn), lambda i, j, k: (i, j))
        out_shape = (m, ko)
        dims = (((1,), (1,)), ((), ()))
    else:
        m, kdim = a.shape
        n = b.shape[1]
        ns = n // out_nd
        tm, tn, tk = _div(kdim, tm, 128), _div(ns, tn, 128), _div(m, tk, 16)
        nbs = ns // tn
        grid = (kdim // tm, out_nd * nbs, m // tk)
        a_spec = pl.BlockSpec((tk, tm), lambda i, j, k: (k, i))
        b_spec = pl.BlockSpec((tk, tn), lambda i, j, k: (k, j))
        o_spec = pl.BlockSpec((None, tm, tn), lambda i, j, k: (j // nbs, i, j % nbs))
        out_shape = (out_nd, kdim, ns)
        dims = (((0,), (0,)), ((), ()))
    nk = grid[2]

    def prod(a_ref, b_ref):
        return lax.dot_general(a_ref[...].astype(BF16), b_ref[...].astype(BF16), dims, preferred_element_type=F32)

    def body_one(a_ref, b_ref, o_ref):
        o_ref[...] = prod(a_ref, b_ref).astype(o_ref.dtype)

    def body_acc(a_ref, b_ref, o_ref, acc_ref):
        k = pl.program_id(2)

        @pl.when(k == 0)
        def _():
            acc_ref[...] = jnp.zeros_like(acc_ref)

        acc_ref[...] += prod(a_ref, b_ref)

        @pl.when(k == nk - 1)
        def _():
            o_ref[...] = acc_ref[...].astype(o_ref.dtype)

    return pl.pallas_call(
        body_one if nk == 1 else body_acc, name=name, grid=grid, in_specs=[a_spec, b_spec], out_specs=o_spec,
        out_shape=jax.ShapeDtypeStruct(out_shape, out_dtype),
        scratch_shapes=[] if nk == 1 else [pltpu.VMEM((tm, tn), F32)], compiler_params=_cp(3),
    )(a, b)


def _ew(fn, out_dtype, name, *xs):
    rows, cols = xs[0].shape
    tr = rows if rows <= 64 else _div(rows, 256, 16)

    def body(*refs):
        refs[-1][...] = fn(*[r[...] for r in refs[:-1]]).astype(out_dtype)

    spec = pl.BlockSpec((tr, cols), lambda i: (i, 0))
    return pl.pallas_call(
        body, name=name, grid=(rows // tr,), in_specs=[spec] * len(xs), out_specs=spec,
        out_shape=jax.ShapeDtypeStruct((rows, cols), out_dtype), compiler_params=_cp(1),
    )(*xs)


def _sum_lead(x, name):
    n, rows, cols = x.shape
    tr = _div(rows, 64, 8)

    def body(x_ref, o_ref):
        acc = x_ref[0]
        for s in range(1, n):
            acc = acc + x_ref[s]
        o_ref[...] = acc

    return pl.pallas_call(
        body, name=name, grid=(rows // tr,),
        in_specs=[pl.BlockSpec((n, tr, cols), lambda i: (0, i, 0))],
        out_specs=pl.BlockSpec((tr, cols), lambda i: (i, 0)),
        out_shape=jax.ShapeDtypeStruct((rows, cols), F32), compiler_params=_cp(1),
    )(x)


def _adam_math(w, g, m, v):
    m = ADAM_B1 * m + (1.0 - ADAM_B1) * g
    v = ADAM_B2 * v + (1.0 - ADAM_B2) * (g * g)
    m_hat = m / (1.0 - ADAM_B1 ** ADAM_STEP)
    v_hat = v / (1.0 - ADAM_B2 ** ADAM_STEP)
    delta = -ADAM_LR * (m_hat / (jnp.sqrt(v_hat) + ADAM_EPS) + ADAM_WD * w)
    return delta, m, v


def _adamw(w, g, m, v, name):
    shape = w.shape
    cols = shape[-1]
    w2, g2, m2, v2 = [t.reshape(-1, cols) for t in (w, g, m, v)]
    rows = w2.shape[0]
    tr = rows if rows <= 512 else _div(rows, 256, 8)

    def body(w_ref, g_ref, m_ref, v_ref, d_ref, nm_ref, nv_ref):
        d, nm, nv = _adam_math(w_ref[...], g_ref[...], m_ref[...], v_ref[...])
        d_ref[...] = d
        nm_ref[...] = nm
        nv_ref[...] = nv

    spec = pl.BlockSpec((tr, cols), lambda i: (i, 0))
    outs = pl.pallas_call(
        body, name=name, grid=(rows // tr,), in_specs=[spec] * 4, out_specs=[spec] * 3,
        out_shape=[jax.ShapeDtypeStruct((rows, cols), F32)] * 3, compiler_params=_cp(1),
    )(w2, g2, m2, v2)
    return [o.reshape(shape) for o in outs]


def _reduce_adamw(parts, w, m, v, name):
    shape = w.shape
    n, rows, cols = parts.shape
    w2, m2, v2 = [t.reshape(rows, cols) for t in (w, m, v)]
    tr = _div(rows, 128, 16)

    def body(p_ref, w_ref, m_ref, v_ref, g_ref, d_ref, nm_ref, nv_ref):
        g = p_ref[0].astype(F32)
        for s in range(1, n):
            g = g + p_ref[s].astype(F32)
        d, nm, nv = _adam_math(w_ref[...], g, m_ref[...], v_ref[...])
        g_ref[...] = g
        d_ref[...] = d
        nm_ref[...] = nm
        nv_ref[...] = nv

    spec = pl.BlockSpec((tr, cols), lambda i: (i, 0))
    outs = pl.pallas_call(
        body, name=name, grid=(rows // tr,),
        in_specs=[pl.BlockSpec((n, tr, cols), lambda i: (0, i, 0))] + [spec] * 3, out_specs=[spec] * 4,
        out_shape=[jax.ShapeDtypeStruct((rows, cols), F32)] * 4, compiler_params=_cp(1),
    )(parts, w2, m2, v2)
    return [o.reshape(shape) for o in outs]


def _all_gather(x, name):
    rows, cols = x.shape

    def body(x_ref, out_ref, send_sems, recv_sems, local_sem):
        mx, my, mc = lax.axis_index("x"), lax.axis_index("y"), lax.axis_index("c")
        me, sibling = (mx, my, mc), (mx, my, 1 - mc)
        chips = [(1 - mx, my), (mx, 1 - my), (1 - mx, 1 - my)]

        def slab(px, py, pc):
            return out_ref.at[4 * px + 2 * py + pc]

        def copy(k, block, to, src=None):
            return pltpu.make_async_remote_copy(
                src_ref=slab(*block) if src is None else src, dst_ref=slab(*block),
                send_sem=send_sems.at[k], recv_sem=recv_sems.at[k], device_id=to, device_id_type=MESH)

        mine = pltpu.make_async_copy(x_ref, slab(*me), local_sem)
        mine.start()
        first = [copy(0, me, sibling, src=x_ref)]
        first += [copy(1 + j, me, (*chip, mc), src=x_ref) for j, chip in enumerate(chips)]
        for cp in first:
            cp.start()
        passed = [copy(4 + j, (*chip, mc), sibling) for j, chip in enumerate(chips)]
        for j, chip in enumerate(chips):
            copy(1 + j, (*chip, mc), me).wait_recv()
            passed[j].start()
        copy(0, sibling, me).wait_recv()
        for j, chip in enumerate(chips):
            copy(4 + j, (*chip, 1 - mc), me).wait_recv()
        for cp in first + passed:
            cp.wait_send()
        mine.wait()

    return pl.pallas_call(
        body, name=name, out_shape=jax.ShapeDtypeStruct((NDEV, rows, cols), x.dtype),
        in_specs=[pl.BlockSpec(memory_space=pltpu.HBM)], out_specs=pl.BlockSpec(memory_space=pltpu.HBM),
        scratch_shapes=[pltpu.SemaphoreType.DMA((7,)), pltpu.SemaphoreType.DMA((7,)), pltpu.SemaphoreType.DMA],
    )(x)


def _all_to_all(x, name):
    def body(x_ref, out_ref, send_sems, recv_sems, local_sem):
        mx, my, mc = lax.axis_index("x"), lax.axis_index("y"), lax.axis_index("c")
        me = 4 * mx + 2 * my + mc
        mine = pltpu.make_async_copy(x_ref.at[me], out_ref.at[me], local_sem)
        mine.start()
        copies = []
        for k in range(1, NDEV):
            px = mx ^ ((k >> 2) & 1)
            py = my ^ ((k >> 1) & 1)
            pc = mc ^ (k & 1)
            peer = 4 * px + 2 * py + pc
            cp = pltpu.make_async_remote_copy(
                src_ref=x_ref.at[peer], dst_ref=out_ref.at[me], send_sem=send_sems.at[k - 1],
                recv_sem=recv_sems.at[k - 1], device_id=(px, py, pc), device_id_type=MESH)
            cp.start()
            copies.append(cp)
        for cp in copies:
            cp.wait()
        mine.wait()

    return pl.pallas_call(
        body, name=name, out_shape=jax.ShapeDtypeStruct(x.shape, x.dtype),
        in_specs=[pl.BlockSpec(memory_space=pltpu.HBM)], out_specs=pl.BlockSpec(memory_space=pltpu.HBM),
        scratch_shapes=[pltpu.SemaphoreType.DMA((7,)), pltpu.SemaphoreType.DMA((7,)), pltpu.SemaphoreType.DMA],
    )(x)


def _row_tile(t_ctx, pref=256):
    return _div(t_ctx, pref, 8)


def _mod_spec(d, nctx):
    return pl.BlockSpec((None, 3, d), lambda i: (jnp.where(i >= nctx, 1, 0), 0, 0))


def _modulate(xb, modv, t_ctx, name):
    r, d = xb.shape
    tm = _row_tile(t_ctx)
    nctx = t_ctx // tm

    def body(x_ref, mv_ref, h_ref):
        h_ref[...] = (x_ref[...] * (1.0 + mv_ref[1:2, :]) + mv_ref[0:1, :]).astype(BF16)

    row = pl.BlockSpec((tm, d), lambda i: (i, 0))
    return pl.pallas_call(
        body, name=name, grid=(r // tm,), in_specs=[row, _mod_spec(d, nctx)], out_specs=row,
        out_shape=jax.ShapeDtypeStruct((r, d), BF16), compiler_params=_cp(1),
    )(xb, modv)


def _postln_fwd(xb, y, modv, lnp, alpha, t_ctx, name):
    r, d = xb.shape
    tm = _row_tile(t_ctx)
    nctx = t_ctx // tm

    def body(x_ref, y_ref, mv_ref, ln_ref, o_ref):
        n, _ = _ln_norm(alpha * x_ref[...] + mv_ref[2:3, :] * y_ref[...])
        o_ref[...] = n * ln_ref[0:1, :] + ln_ref[1:2, :]

    row = pl.BlockSpec((tm, d), lambda i: (i, 0))
    return pl.pallas_call(
        body, name=name, grid=(r // tm,),
        in_specs=[row, row, _mod_spec(d, nctx), pl.BlockSpec((2, d), lambda i: (0, 0))], out_specs=row,
        out_shape=jax.ShapeDtypeStruct((r, d), F32), compiler_params=_cp(1),
    )(xb, y, modv, lnp)


def _postln_bwd(dxn, xb, y, modv, lnp, alpha, t_ctx, name):
    r, d = xb.shape
    tm = _row_tile(t_ctx)
    nctx = t_ctx // tm

    def body(dxn_ref, x_ref, y_ref, mv_ref, ln_ref, dy_ref, dxa_ref, s_ref):
        i = pl.program_id(0)

        @pl.when(i == 0)
        def _():
            s_ref[...] = jnp.zeros_like(s_ref)

        yv = y_ref[...]
        gate = mv_ref[2:3, :]
        n, rstd = _ln_norm(alpha * x_ref[...] + gate * yv)
        dxn_v = dxn_ref[...]
        dz = _ln_bwd(dxn_v * ln_ref[0:1, :], n, rstd)
        dy_ref[...] = (gate * dz).astype(BF16)
        dxa_ref[...] = alpha * dz
        s_ref[0:1, :] += jnp.sum(dxn_v * n, axis=0, keepdims=True)
        s_ref[1:2, :] += jnp.sum(dxn_v, axis=0, keepdims=True)
        dgate = jnp.sum(dz * yv, axis=0, keepdims=True)
        is_ctx = i < nctx
        s_ref[2:3, :] += jnp.where(is_ctx, dgate, 0.0)
        s_ref[3:4, :] += jnp.where(is_ctx, 0.0, dgate)

    row = pl.BlockSpec((tm, d), lambda i: (i, 0))
    return pl.pallas_call(
        body, name=name, grid=(r // tm,),
        in_specs=[row, row, row, _mod_spec(d, nctx), pl.BlockSpec((2, d), lambda i: (0, 0))],
        out_specs=[row, row, pl.BlockSpec((8, d), lambda i: (0, 0))],
        out_shape=[jax.ShapeDtypeStruct((r, d), BF16), jax.ShapeDtypeStruct((r, d), F32),
                   jax.ShapeDtypeStruct((8, d), F32)],
        compiler_params=_cp(1),
    )(dxn, xb, y, modv, lnp)


def _mod_bwd(dh, xb, dxa, modv, t_ctx, name):
    r, d = xb.shape
    tm = _row_tile(t_ctx)
    nctx = t_ctx // tm

    def body(dh_ref, x_ref, dxa_ref, mv_ref, dx_ref, s_ref):
        i = pl.program_id(0)

        @pl.when(i == 0)
        def _():
            s_ref[...] = jnp.zeros_like(s_ref)

        dhv = dh_ref[...]
        dx_ref[...] = dxa_ref[...] + dhv * (1.0 + mv_ref[1:2, :])
        dshift = jnp.sum(dhv, axis=0, keepdims=True)
        dscale = jnp.sum(dhv * x_ref[...], axis=0, keepdims=True)
        is_ctx = i < nctx
        s_ref[0:1, :] += jnp.where(is_ctx, dshift, 0.0)
        s_ref[1:2, :] += jnp.where(is_ctx, dscale, 0.0)
        s_ref[2:3, :] += jnp.where(is_ctx, 0.0, dshift)
        s_ref[3:4, :] += jnp.where(is_ctx, 0.0, dscale)

    row = pl.BlockSpec((tm, d), lambda i: (i, 0))
    return pl.pallas_call(
        body, name=name, grid=(r // tm,), in_specs=[row, row, row, _mod_spec(d, nctx)],
        out_specs=[row, pl.BlockSpec((8, d), lambda i: (0, 0))],
        out_shape=[jax.ShapeDtypeStruct((r, d), F32), jax.ShapeDtypeStruct((8, d), F32)],
        compiler_params=_cp(1),
    )(dh, xb, dxa, modv)


def _loss_grad(xb, target, t_ctx, name):
    r, d = xb.shape
    tm = _row_tile(t_ctx)
    nctx = t_ctx // tm

    def body(x_ref, t_ref, l_ref, dx_ref):
        i = pl.program_id(0)

        @pl.when(i == 0)
        def _():
            l_ref[...] = jnp.zeros_like(l_ref)

        @pl.when(i < nctx)
        def _():
            dx_ref[...] = jnp.zeros_like(dx_ref)

        @pl.when(i >= nctx)
        def _():
            e = x_ref[...] - t_ref[...]
            dx_ref[...] = e / d
            l_ref[...] += 0.5 * jnp.sum(jnp.sum(e * e, axis=1, keepdims=True), axis=0, keepdims=True) / d

    row = pl.BlockSpec((tm, d), lambda i: (i, 0))
    return pl.pallas_call(
        body, name=name, grid=(r // tm,),
        in_specs=[row, pl.BlockSpec((tm, d), lambda i: (jnp.maximum(i - nctx, 0), 0))],
        out_specs=[pl.BlockSpec((8, 128), lambda i: (0, 0)), row],
        out_shape=[jax.ShapeDtypeStruct((8, 128), F32), jax.ShapeDtypeStruct((r, d), F32)],
        compiler_params=_cp(1),
    )(xb, target)


def _conv_specs(r, w, tm, cblk):
    hb = tm // HALO
    last = r // HALO - 1
    main = pl.BlockSpec((tm, w), lambda i: (i, cblk))
    top = pl.BlockSpec((HALO, w), lambda i: (jnp.maximum(i * hb - 1, 0), cblk))
    bot = pl.BlockSpec((HALO, w), lambda i: (jnp.minimum((i + 1) * hb, last), cblk))
    return [main, top, bot]


def _fill_pad(pad_ref, main, top, bot, top_ok, bot_ok, tm):
    pad_ref[0:HALO, :] = jnp.where(top_ok, top, 0.0)
    pad_ref[HALO:HALO + tm, :] = main
    pad_ref[HALO + tm:2 * HALO + tm, :] = jnp.where(bot_ok, bot, 0.0)


def _edges(i, nctx, nr):
    top_ok = jnp.logical_and(i != 0, i != nctx)
    bot_ok = jnp.logical_and(i != nctx - 1, i != nr - 1)
    return top_ok, bot_ok


def _conv_fwd(p, caw, cab, nag, nab, cbw, t_ctx, name):
    r = p.shape[0]
    w = p.shape[1] // 7
    ka, kb = caw.shape[0], cbw.shape[0]
    tm = _row_tile(t_ctx, 128)
    nctx, nr = t_ctx // tm, r // tm

    def body(av, avt, avb, ag, agt, agb, agate, bx, bxt, bxb, bb, bc, bct, bcb, bgate,
             caw_ref, cab_ref, nag_ref, nab_ref, cbw_ref, cat_ref, u1_ref, pad_a, pad_b):
        top_ok, bot_ok = _edges(pl.program_id(0), nctx, nr)
        _fill_pad(pad_a, av[...] * _sigmoid(ag[...]), avt[...] * _sigmoid(agt[...]),
                  avb[...] * _sigmoid(agb[...]), top_ok, bot_ok, tm)
        u1 = jnp.zeros((tm, w), F32) + cab_ref[...]
        for k in range(ka):
            u1 = u1 + caw_ref[k:k + 1, :] * pad_a[pl.ds(HALO - ka // 2 + k, tm), :]
        u1_ref[...] = u1
        n, _ = _ln_norm(u1)
        a_out = _silu(n * nag_ref[...] + nab_ref[...]) * _silu(agate[...])
        cat_ref[:, 0:w] = a_out.astype(BF16)
        _fill_pad(pad_b, bc[...] * bx[...], bct[...] * bxt[...], bcb[...] * bxb[...], top_ok, bot_ok, tm)
        v = jnp.zeros((tm, w), F32)
        for k in range(kb):
            v = v + cbw_ref[k:k + 1, :] * pad_b[pl.ds(HALO - kb // 2 + k, tm), :]
        cat_ref[:, w:2 * w] = (bb[...] * v * _silu(bgate[...])).astype(BF16)

    def main(cblk):
        return pl.BlockSpec((tm, w), lambda i: (i, cblk))

    def whole(a):
        return pl.BlockSpec(a.shape, lambda i: (0, 0))

    in_specs = (_conv_specs(r, w, tm, 0) + _conv_specs(r, w, tm, 1) + [main(2)] + _conv_specs(r, w, tm, 3)
                + [main(4)] + _conv_specs(r, w, tm, 5) + [main(6)]
                + [whole(caw), whole(cab), whole(nag), whole(nab), whole(cbw)])
    return pl.pallas_call(
        body, name=name, grid=(nr,), in_specs=in_specs,
        out_specs=[pl.BlockSpec((tm, 2 * w), lambda i: (i, 0)), pl.BlockSpec((tm, w), lambda i: (i, 0))],
        out_shape=[jax.ShapeDtypeStruct((r, 2 * w), BF16), jax.ShapeDtypeStruct((r, w), F32)],
        scratch_shapes=[pltpu.VMEM((tm + 2 * HALO, w), F32), pltpu.VMEM((tm + 2 * HALO, w), F32)],
        compiler_params=_cp(1),
    )(*([p] * 15), caw, cab, nag, nab, cbw)


def _conv_bwd1(dcat, p, u1, nag, nab, t_ctx, name):
    r, w = u1.shape
    tm = _row_tile(t_ctx, 128)

    def body(da_ref, agate_ref, u1_ref, nag_ref, nab_ref, du1_ref, dag_ref, s_ref):
        @pl.when(pl.program_id(0) == 0)
        def _():
            s_ref[...] = jnp.zeros_like(s_ref)

        n, rstd = _ln_norm(u1_ref[...])
        g = nag_ref[...]
        u2 = n * g + nab_ref[...]
        da = da_ref[...]
        ag = agate_ref[...]
        dag_ref[...] = (da * _silu(u2) * _dsilu(ag)).astype(BF16)
        du2 = da * _silu(ag) * _dsilu(u2)
        du1 = _ln_bwd(du2 * g, n, rstd)
        du1_ref[...] = du1
        s_ref[0:1, :] += jnp.sum(du2 * n, axis=0, keepdims=True)
        s_ref[1:2, :] += jnp.sum(du2, axis=0, keepdims=True)
        s_ref[2:3, :] += jnp.sum(du1, axis=0, keepdims=True)

    def win(cblk):
        return pl.BlockSpec((tm, w), lambda i: (i, cblk))

    one = pl.BlockSpec((1, w), lambda i: (0, 0))
    return pl.pallas_call(
        body, name=name, grid=(r // tm,), in_specs=[win(0), win(2), win(0), one, one],
        out_specs=[win(0), win(0), pl.BlockSpec((8, w), lambda i: (0, 0))],
        out_shape=[jax.ShapeDtypeStruct((r, w), F32), jax.ShapeDtypeStruct((r, w), BF16),
                   jax.ShapeDtypeStruct((8, w), F32)],
        compiler_params=_cp(1),
    )(dcat, p, u1, nag, nab)


def _conv_bwd2(du1, dcat, p, dag, caw, cbw, t_ctx, name):
    r, w = du1.shape
    ka, kb = caw.shape[0], cbw.shape[0]
    tm = _row_tile(t_ctx, 128)
    nctx, nr = t_ctx // tm, r // tm

    def body(du, dut, dub, av, avt, avb, ag, agt, agb, db, dbt, dbb, bx, bxt, bxb, bb, bbt, bbb,
             bc, bct, bcb, bg, bgt, bgb, dag_ref, caw_ref, cbw_ref, dp_ref, dcaw_ref, dcbw_ref,
             pad_u, pad_d, pad_w, pad_v):
        top_ok, bot_ok = _edges(pl.program_id(0), nctx, nr)

        @pl.when(pl.program_id(0) == 0)
        def _():
            dcaw_ref[...] = jnp.zeros_like(dcaw_ref)
            dcbw_ref[...] = jnp.zeros_like(dcbw_ref)

        sg = _sigmoid(ag[...])
        _fill_pad(pad_u, av[...] * sg, avt[...] * _sigmoid(agt[...]), avb[...] * _sigmoid(agb[...]),
                  top_ok, bot_ok, tm)
        du_m = du[...]
        _fill_pad(pad_d, du_m, dut[...], dub[...], top_ok, bot_ok, tm)
        du0 = jnp.zeros((tm, w), F32)
        for k in range(ka):
            du0 = du0 + caw_ref[k:k + 1, :] * pad_d[pl.ds(HALO + ka // 2 - k, tm), :]
            dcaw_ref[k:k + 1, :] += jnp.sum(du_m * pad_u[pl.ds(HALO - ka // 2 + k, tm), :], axis=0, keepdims=True)
        dp_ref[:, 0:w] = (du0 * sg).astype(BF16)
        dp_ref[:, w:2 * w] = (du0 * av[...] * sg * (1.0 - sg)).astype(BF16)
        dp_ref[:, 2 * w:3 * w] = dag_ref[...]

        _fill_pad(pad_w, bc[...] * bx[...], bct[...] * bxt[...], bcb[...] * bxb[...], top_ok, bot_ok, tm)
        v = jnp.zeros((tm, w), F32)
        for k in range(kb):
            v = v + cbw_ref[k:k + 1, :] * pad_w[pl.ds(HALO - kb // 2 + k, tm), :]
        db_m, bb_m, bg_m = db[...], bb[...], bg[...]
        sbg = _silu(bg_m)
        dv_m = db_m * bb_m * sbg
        _fill_pad(pad_v, dv_m, dbt[...] * bbt[...] * _silu(bgt[...]), dbb[...] * bbb[...] * _silu(bgb[...]),
                  top_ok, bot_ok, tm)
        dw0 = jnp.zeros((tm, w), F32)
        for k in range(kb):
            dw0 = dw0 + cbw_ref[k:k + 1, :] * pad_v[pl.ds(HALO + kb // 2 - k, tm), :]
            dcbw_ref[k:k + 1, :] += jnp.sum(dv_m * pad_w[pl.ds(HALO - kb // 2 + k, tm), :], axis=0, keepdims=True)
        dp_ref[:, 3 * w:4 * w] = (dw0 * bc[...]).astype(BF16)
        dp_ref[:, 4 * w:5 * w] = (db_m * v * sbg).astype(BF16)
        dp_ref[:, 5 * w:6 * w] = (dw0 * bx[...]).astype(BF16)
        dp_ref[:, 6 * w:7 * w] = (db_m * bb_m * v * _dsilu(bg_m)).astype(BF16)

    def whole(a):
        return pl.BlockSpec(a.shape, lambda i: (0, 0))

    in_specs = (_conv_specs(r, w, tm, 0) + _conv_specs(r, w, tm, 0) + _conv_specs(r, w, tm, 1)
                + _conv_specs(r, w, tm, 1) + _conv_specs(r, w, tm, 3) + _conv_specs(r, w, tm, 4)
                + _conv_specs(r, w, tm, 5) + _conv_specs(r, w, tm, 6)
                + [pl.BlockSpec((tm, w), lambda i: (i, 0)), whole(caw), whole(cbw)])
    pad = pltpu.VMEM((tm + 2 * HALO, w), F32)
    return pl.pallas_call(
        body, name=name, grid=(nr,), in_specs=in_specs,
        out_specs=[pl.BlockSpec((tm, 7 * w), lambda i: (i, 0)), pl.BlockSpec((32, w), lambda i: (0, 0)),
                   pl.BlockSpec((8, w), lambda i: (0, 0))],
        out_shape=[jax.ShapeDtypeStruct((r, 7 * w), BF16), jax.ShapeDtypeStruct((32, w), F32),
                   jax.ShapeDtypeStruct((8, w), F32)],
        scratch_shapes=[pad, pad, pad, pad], compiler_params=_cp(1),
    )(*([du1] * 3), *([p] * 6), *([dcat] * 3), *([p] * 12), dag, caw, cbw)


def _rms(xh):
    r = lax.rsqrt(jnp.mean(xh * xh, axis=-1, keepdims=True) + RMS_EPS)
    return xh * r, r


def _qk_fwd(p, cos, sin, qg, kg, att, kv, t_ctx, name):
    r = p.shape[0]
    tm = _row_tile(t_ctx)

    def body(q_ref, k_ref, v_ref, cos_ref, sin_ref, qg_ref, kg_ref, qr_ref, kr_ref, vb_ref):
        cs, sn = cos_ref[...], sin_ref[...]
        for src, g_ref, dst, nh in ((q_ref, qg_ref, qr_ref, att // HEAD_DIM), (k_ref, kg_ref, kr_ref, kv // HEAD_DIM)):
            for h in range(nh):
                cols = slice(h * HEAD_DIM, (h + 1) * HEAD_DIM)
                n, _ = _rms(src[:, cols])
                n = n * g_ref[...]
                dst[:, cols] = (n * cs + _partner(n) * sn).astype(BF16)
        vb_ref[...] = v_ref[...].astype(BF16)

    def rows(width, cblk):
        return pl.BlockSpec((tm, width), lambda i: (i, cblk))

    one = pl.BlockSpec((1, HEAD_DIM), lambda i: (0, 0))
    return pl.pallas_call(
        body, name=name, grid=(r // tm,),
        in_specs=[rows(att, 0), rows(kv, 2 * att // kv), rows(kv, 2 * att // kv + 1),
                  rows(HEAD_DIM, 0), rows(HEAD_DIM, 0), one, one],
        out_specs=[rows(att, 0), rows(kv, 0), rows(kv, 0)],
        out_shape=[jax.ShapeDtypeStruct((r, att), BF16), jax.ShapeDtypeStruct((r, kv), BF16),
                   jax.ShapeDtypeStruct((r, kv), BF16)],
        compiler_params=_cp(1),
    )(p, p, p, cos, sin, qg, kg)


def _qk_bwd(p, dqr, dkr, cos, sin, qg, kg, att, kv, t_ctx, name):
    r = p.shape[0]
    tm = _row_tile(t_ctx)

    def body(q_ref, k_ref, dqr_ref, dkr_ref, cos_ref, sin_ref, qg_ref, kg_ref, dq_ref, dk_ref, s_ref):
        @pl.when(pl.program_id(0) == 0)
        def _():
            s_ref[...] = jnp.zeros_like(s_ref)

        cs, sn = cos_ref[...], sin_ref[...]
        for row, (src, dsrc, g_ref, dst, nh) in enumerate((
                (q_ref, dqr_ref, qg_ref, dq_ref, att // HEAD_DIM), (k_ref, dkr_ref, kg_ref, dk_ref, kv // HEAD_DIM))):
            dg = jnp.zeros((1, HEAD_DIM), F32)
            for h in range(nh):
                cols = slice(h * HEAD_DIM, (h + 1) * HEAD_DIM)
                n0, rr = _rms(src[:, cols])
                d = dsrc[:, cols]
                dng = d * cs + _partner(d * sn)
                dg = dg + jnp.sum(dng * n0, axis=0, keepdims=True)
                dn0 = dng * g_ref[...]
                dst[:, cols] = (rr * (dn0 - n0 * jnp.mean(dn0 * n0, axis=-1, keepdims=True))).astype(BF16)
            s_ref[row:row + 1, :] += dg

    def rows(width, cblk):
        return pl.BlockSpec((tm, width), lambda i: (i, cblk))

    one = pl.BlockSpec((1, HEAD_DIM), lambda i: (0, 0))
    return pl.pallas_call(
        body, name=name, grid=(r // tm,),
        in_specs=[rows(att, 0), rows(kv, 2 * att // kv), rows(att, 0), rows(kv, 0),
                  rows(HEAD_DIM, 0), rows(HEAD_DIM, 0), one, one],
        out_specs=[rows(att, 0), rows(kv, 0), pl.BlockSpec((8, HEAD_DIM), lambda i: (0, 0))],
        out_shape=[jax.ShapeDtypeStruct((r, att), BF16), jax.ShapeDtypeStruct((r, kv), BF16),
                   jax.ShapeDtypeStruct((8, HEAD_DIM), F32)],
        compiler_params=_cp(1),
    )(p, p, dqr, dkr, cos, sin, qg, kg)


def _stack_heads(x, tq):
    return jnp.concatenate([x[:, g * HEAD_DIM:(g + 1) * HEAD_DIM] for g in range(GQA_GROUP)], axis=0)


def _attn_tiles(t_ctx, s_lat, tkl_pref):
    return _div(t_ctx, 256, 8), _div(s_lat, tkl_pref, 8)


NT_DIMS = (((1,), (1,)), ((), ()))
TN_DIMS = (((0,), (0,)), ((), ()))
ATTN_SCALE_LOG2 = ATTN_SCALE * math.log2(math.e)


def _flash_fwd(qr, kr, vb, p, att, t_ctx, name):
    r = qr.shape[0]
    s_lat = r - t_ctx
    gw = GQA_GROUP * HEAD_DIM
    nkv = att // gw
    tq, tkl = _attn_tiles(t_ctx, s_lat, 1024)
    nq, nq_ctx, n_lat = r // tq, t_ctx // tq, s_lat // tkl
    nl = GQA_GROUP * tq

    def body(q_ref, k_ref, v_ref, g_ref, o_ref, og_ref, lse_ref, m_s, l_s, acc_s):
        qi = pl.program_id(1)
        q4 = _stack_heads(q_ref[...], tq)
        m_s[...] = jnp.full_like(m_s, -1e30)
        l_s[...] = jnp.zeros_like(l_s)
        acc_s[...] = jnp.zeros_like(acc_s)

        def step(off, size):
            kb = k_ref[pl.ds(off, size), :]
            st = lax.dot_general(kb, q4, NT_DIMS, preferred_element_type=F32) * ATTN_SCALE_LOG2
            m_old = m_s[...]
            m_new = jnp.maximum(m_old, jnp.max(st, axis=0, keepdims=True))
            pe = jnp.exp2(st - m_new)
            a = jnp.exp2(m_old - m_new)
            l_s[...] = a * l_s[...] + jnp.sum(pe, axis=0, keepdims=True)
            acc_s[...] = a * acc_s[...] + lax.dot_general(v_ref[pl.ds(off, size), :], pe.astype(BF16), TN_DIMS,
                                                          preferred_element_type=F32)
            m_s[...] = m_new

        step(0, t_ctx)

        def lat(n, carry):
            step(pl.multiple_of(t_ctx + n * tkl, math.gcd(t_ctx, tkl)), tkl)
            return carry

        lax.fori_loop(0, jnp.where(qi < nq_ctx, 0, n_lat), lat, 0)
        o4 = (acc_s[...] / l_s[...]).T
        lse_ref[...] = m_s[...] + jnp.log2(l_s[...])
        sg = _silu(g_ref[...])
        for g in range(GQA_GROUP):
            cols = slice(g * HEAD_DIM, (g + 1) * HEAD_DIM)
            og = o4[g * tq:(g + 1) * tq, :]
            o_ref[:, cols] = og
            og_ref[:, cols] = (og * sg[:, cols]).astype(BF16)

    qspec = pl.BlockSpec((tq, gw), lambda h, i: (i, h))
    kspec = pl.BlockSpec((r, HEAD_DIM), lambda h, i: (0, h))
    return pl.pallas_call(
        body, name=name, grid=(nkv, nq),
        in_specs=[qspec, kspec, kspec, pl.BlockSpec((tq, gw), lambda h, i: (i, att // gw + h))],
        out_specs=[qspec, qspec, pl.BlockSpec((None, None, 1, nl), lambda h, i: (h, i, 0, 0))],
        out_shape=[jax.ShapeDtypeStruct((r, att), F32), jax.ShapeDtypeStruct((r, att), BF16),
                   jax.ShapeDtypeStruct((nkv, nq, 1, nl), F32)],
        scratch_shapes=[pltpu.VMEM((1, nl), F32), pltpu.VMEM((1, nl), F32), pltpu.VMEM((HEAD_DIM, nl), F32)],
        compiler_params=_cp(2),
    )(qr, kr, vb, p)


def _flash_bwd(qr, kr, vb, p, o, dog, lse, att, kv, t_ctx, name):
    r = qr.shape[0]
    s_lat = r - t_ctx
    gw = GQA_GROUP * HEAD_DIM
    nkv = att // gw
    tq, tkl = _attn_tiles(t_ctx, s_lat, 512)
    nq, nq_ctx, n_lat = r // tq, t_ctx // tq, s_lat // tkl
    nl = GQA_GROUP * tq

    def body(q_ref, k_ref, v_ref, g_ref, o_ref, dog_ref, lse_ref, dq_ref, dgate_ref, dk_ref, dv_ref, dq_s):
        qi = pl.program_id(1)

        @pl.when(qi == 0)
        def _():
            dk_ref[...] = jnp.zeros_like(dk_ref)
            dv_ref[...] = jnp.zeros_like(dv_ref)

        gate, ov, dogv = g_ref[...], o_ref[...], dog_ref[...]
        dgate_ref[...] = (dogv * ov * _dsilu(gate)).astype(BF16)
        do = dogv * _silu(gate)
        do4 = _stack_heads(do, tq)
        delta = jnp.sum((do4 * _stack_heads(ov, tq)).T, axis=0, keepdims=True)
        do4 = do4.astype(BF16)
        q4 = _stack_heads(q_ref[...], tq)
        lse_v = lse_ref[...]
        dq_s[...] = jnp.zeros_like(dq_s)

        def step(off, size):
            kb = k_ref[pl.ds(off, size), :]
            vv = v_ref[pl.ds(off, size), :]
            st = lax.dot_general(kb, q4, NT_DIMS, preferred_element_type=F32) * ATTN_SCALE_LOG2
            pe = jnp.exp2(st - lse_v)
            dp = lax.dot_general(vv, do4, NT_DIMS, preferred_element_type=F32)
            ds = (pe * (dp - delta) * ATTN_SCALE).astype(BF16)
            dv_ref[pl.ds(off, size), :] += jnp.dot(pe.astype(BF16), do4, preferred_element_type=F32)
            dk_ref[pl.ds(off, size), :] += jnp.dot(ds, q4, preferred_element_type=F32)
            dq_s[...] += lax.dot_general(kb, ds, TN_DIMS, preferred_element_type=F32)

        step(0, t_ctx)

        def lat(n, carry):
            step(pl.multiple_of(t_ctx + n * tkl, math.gcd(t_ctx, tkl)), tkl)
            return carry

        lax.fori_loop(0, jnp.where(qi < nq_ctx, 0, n_lat), lat, 0)
        dq4 = dq_s[...].T
        for g in range(GQA_GROUP):
            dq_ref[:, g * HEAD_DIM:(g + 1) * HEAD_DIM] = dq4[g * tq:(g + 1) * tq, :]

    qspec = pl.BlockSpec((tq, gw), lambda h, i: (i, h))
    kspec = pl.BlockSpec((r, HEAD_DIM), lambda h, i: (0, h))
    return pl.pallas_call(
        body, name=name, grid=(nkv, nq),
        in_specs=[qspec, kspec, kspec, pl.BlockSpec((tq, gw), lambda h, i: (i, att // gw + h)), qspec, qspec,
                  pl.BlockSpec((None, None, 1, nl), lambda h, i: (h, i, 0, 0))],
        out_specs=[qspec, qspec, kspec, kspec],
        out_shape=[jax.ShapeDtypeStruct((r, att), F32), jax.ShapeDtypeStruct((r, att), BF16),
                   jax.ShapeDtypeStruct((r, kv), F32), jax.ShapeDtypeStruct((r, kv), F32)],
        scratch_shapes=[pltpu.VMEM((HEAD_DIM, nl), F32)],
        compiler_params=_cp(2),
    )(qr, kr, vb, p, o, dog, lse)


def _rope_tables(t_ctx, s_lat):
    rows_n = s_lat // GRID_W
    row = jnp.repeat(jnp.arange(rows_n, dtype=F32), GRID_W)
    col = jnp.tile(jnp.arange(GRID_W, dtype=F32), rows_n)
    axis_dim = HEAD_DIM // 2
    inv_freq = ROPE_THETA ** (-jnp.arange(0, axis_dim, 2, dtype=F32) / axis_dim)
    ang_r = row[:, None] * inv_freq[None, :]
    ang_c = col[:, None] * inv_freq[None, :]
    cos = jnp.concatenate([jnp.cos(ang_r), jnp.cos(ang_r), jnp.cos(ang_c), jnp.cos(ang_c)], axis=1)
    sin = jnp.concatenate([-jnp.sin(ang_r), jnp.sin(ang_r), -jnp.sin(ang_c), jnp.sin(ang_c)], axis=1)
    cos = jnp.concatenate([jnp.ones((t_ctx, HEAD_DIM), F32), cos], axis=0)
    sin = jnp.concatenate([jnp.zeros((t_ctx, HEAD_DIM), F32), sin], axis=0)
    return cos, sin


def _pad_rows(a, rows):
    return jnp.pad(a, ((0, rows - a.shape[0]), (0, 0)))


def kernel(x, c, ctx, c_ctx, w_mod, b_mod, post_ln_g, post_ln_b, w_in_e, conv_a_w, conv_a_b, norm_a_g, norm_a_b, conv_b_w, w_out_e, w_in_o, q_norm_g, k_norm_g, w_out_o, loss_target, m_c_ctx, m_w_mod, m_b_mod, m_post_ln_g, m_post_ln_b, m_w_in_e, m_conv_a_w, m_conv_a_b, m_norm_a_g, m_norm_a_b, m_conv_b_w, m_w_out_e, m_w_in_o, m_q_norm_g, m_k_norm_g, m_w_out_o, v_c_ctx, v_w_mod, v_b_mod, v_post_ln_g, v_post_ln_b, v_w_in_e, v_conv_a_w, v_conv_a_b, v_norm_a_g, v_norm_a_b, v_conv_b_w, v_w_out_e, v_w_in_o, v_q_norm_g, v_k_norm_g, v_w_out_o):
    depth, d, mcols = w_mod.shape
    s_lat, t_ctx = x.shape[1], ctx.shape[1]
    n_even, n_odd = w_in_e.shape[0], w_in_o.shape[0]
    ka, kb = conv_a_w.shape[1], conv_b_w.shape[1]
    wch = conv_a_w.shape[2] * NDEV
    att = w_out_o.shape[1] * NDEV
    kv = (w_in_o.shape[2] * NDEV - 2 * att) // 2
    alpha = (2.0 * depth) ** 0.25
    me = 4 * lax.axis_index("x") + 2 * lax.axis_index("y") + lax.axis_index("c")

    c_all = _all_gather(_pad_rows(c, 8), "ag_c")[:, 0, :]
    c16 = jnp.concatenate([c_all, _pad_rows(c_ctx[None, :], 8)], axis=0)
    sc16 = _ew(_silu, BF16, "silu_c", c16)
    m_part = _mm(sc16, w_mod, "nn", F32, "mod_fwd", tm=16, tn=mcols, tk=d)
    m_all = _all_gather(m_part, "ag_mod")
    m_full = m_all.reshape(NDEV, 16, depth, mcols).transpose(2, 1, 0, 3).reshape(depth * 16, 3 * d)
    b16 = jnp.broadcast_to(b_mod[:, None, :], (depth, 16, 3 * d)).reshape(depth * 16, 3 * d)
    m_full = _ew(lambda a, b: a + b, F32, "mod_bias", m_full, b16).reshape(depth, 16, 3, d)
    modv = [jnp.stack([m_full[l, 8], lax.dynamic_index_in_dim(m_full[l], me, 0, keepdims=False)])
            for l in range(depth)]

    cw = jnp.concatenate([_pad_rows(conv_a_w[i], 32) for i in range(n_even)]
                         + [_pad_rows(conv_b_w[i], 8) for i in range(n_even)], axis=0)
    cw_all = _all_gather(cw, "ag_convw").transpose(1, 0, 2).reshape(cw.shape[0], wch)
    caw = [cw_all[32 * i:32 * i + ka] for i in range(n_even)]
    cbw = [cw_all[32 * n_even + 8 * i:32 * n_even + 8 * i + kb] for i in range(n_even)]

    def gathered(wsh, name):
        return _all_gather(_ew(lambda t: t, BF16, name + "_bf16", wsh), name)

    w_in_e_g = [gathered(w_in_e[i], f"ag_w_in_e{i}") for i in range(n_even)]
    w_out_e_g = [gathered(w_out_e[i], f"ag_w_out_e{i}").reshape(1, 2 * wch, d) for i in range(n_even)]
    w_in_o_g = [gathered(w_in_o[i], f"ag_w_in_o{i}") for i in range(n_odd)]
    w_out_o_g = [gathered(w_out_o[i], f"ag_w_out_o{i}").reshape(1, att, d) for i in range(n_odd)]

    cos, sin = _rope_tables(t_ctx, s_lat)
    lnp = [jnp.stack([post_ln_g[l], post_ln_b[l]]) for l in range(depth)]

    xb = jnp.concatenate([ctx[0], x[0]], axis=0)
    saved = []
    for l in range(depth):
        i = l // 2
        h = _modulate(xb, modv[l], t_ctx, f"modulate{l}")
        if l % 2 == 0:
            p = _mm(h, w_in_e_g[i], "nn", F32, f"in_proj{l}")
            cat, u1 = _conv_fwd(p, caw[i], conv_a_b[i][None], norm_a_g[i][None], norm_a_b[i][None], cbw[i],
                                t_ctx, f"conv_fwd{l}")
            y = _mm(cat, w_out_e_g[i], "nn", F32, f"out_proj{l}")
            saved.append((xb, h, p, cat, u1, y))
        else:
            p = _mm(h, w_in_o_g[i], "nn", F32, f"in_proj{l}")
            qr, kr, vb = _qk_fwd(p, cos, sin, q_norm_g[i][None], k_norm_g[i][None], att, kv, t_ctx, f"qk_fwd{l}")
            o, og, lse = _flash_fwd(qr, kr, vb, p, att, t_ctx, f"flash_fwd{l}")
            y = _mm(og, w_out_o_g[i], "nn", F32, f"out_proj{l}")
            saved.append((xb, h, p, qr, kr, vb, o, og, lse, y))
        xb = _postln_fwd(xb, y, modv[l], lnp[l], alpha, t_ctx, f"postln_fwd{l}")

    loss_blk, dxb = _loss_grad(xb, loss_target[0], t_ctx, "loss")
    loss = lax.psum(loss_blk[0, 0], ("x", "y", "c"))

    small = {}
    dmod = [None] * depth
    big = {}
    for l in reversed(range(depth)):
        i = l // 2
        sv = saved[l]
        x_in, h, p, y = sv[0], sv[1], sv[2], sv[-1]
        dy, dxa, s_ln = _postln_bwd(dxb, x_in, y, modv[l], lnp[l], alpha, t_ctx, f"postln_bwd{l}")
        small[f"ln{l}"] = s_ln
        if l % 2 == 0:
            cat, u1 = sv[3], sv[4]
            dcat = _mm(dy, w_out_e_g[i], "nt", F32, f"d_cat{l}")
            big[f"w_out_e{i}"] = _mm(cat, dy, "tn", BF16, f"d_w_out{l}", tm=1024, tk=768).reshape(
                NDEV, 2 * wch // NDEV, d)
            du1, dag, s_c1 = _conv_bwd1(dcat, p, u1, norm_a_g[i][None], norm_a_b[i][None], t_ctx, f"conv_bwd1_{l}")
            dp, dcaw, dcbw = _conv_bwd2(du1, dcat, p, dag, caw[i], cbw[i], t_ctx, f"conv_bwd2_{l}")
            small[f"c1_{i}"], small[f"caw{i}"], small[f"cbw{i}"] = s_c1, dcaw, dcbw
            w_in_g, wname = w_in_e_g[i], f"w_in_e{i}"
        else:
            qr, kr, vb, o, og, lse = sv[3:9]
            dog = _mm(dy, w_out_o_g[i], "nt", F32, f"d_og{l}")
            big[f"w_out_o{i}"] = _mm(og, dy, "tn", BF16, f"d_w_out{l}", tm=1024, tk=768).reshape(
                NDEV, att // NDEV, d)
            dqr, dgate, dkr, dvr = _flash_bwd(qr, kr, vb, p, o, dog, lse, att, kv, t_ctx, f"flash_bwd{l}")
            dq, dk, s_qk = _qk_bwd(p, dqr, dkr, cos, sin, q_norm_g[i][None], k_norm_g[i][None], att, kv, t_ctx,
                                   f"qk_bwd{l}")
            small[f"qk{i}"] = jnp.pad(s_qk, ((0, 0), (0, d - HEAD_DIM)))
            dp = jnp.concatenate([dq, dgate, dk, dvr.astype(BF16)], axis=1)
            w_in_g, wname = w_in_o_g[i], f"w_in_o{i}"
        dh = _mm(dp, w_in_g, "nt", F32, f"d_h{l}")
        big[wname] = _mm(h, dp, "tn", BF16, f"d_w_in{l}", tm=d, tk=768,
                         out_nd=NDEV)
        dxb, s_mod = _mod_bwd(dh, x_in, dxa, modv[l], t_ctx, f"mod_bwd{l}")
        dmod[l] = jnp.stack([jnp.stack([s_mod[0], s_mod[1], s_ln[2]]), jnp.stack([s_mod[2], s_mod[3], s_ln[3]])])
    grad_x = dxb[t_ctx:][None]

    dm_loc = jnp.stack(dmod).reshape(depth * 2, 3 * d)
    dm_all = _all_gather(dm_loc, "ag_dmod").reshape(NDEV, depth, 2, 3 * d)
    dm_ctx = _sum_lead(dm_all[:, :, 0, :], "sum_dmod_ctx")
    dm16 = jnp.concatenate([dm_all[:, :, 1, :].transpose(1, 0, 2), dm_ctx[:, None, :],
                            jnp.zeros((depth, 7, 3 * d), F32)], axis=1)
    g_b_mod = _sum_lead(dm16.transpose(1, 0, 2), "sum_b_mod")
    dm16_me = lax.dynamic_slice_in_dim(dm16.reshape(depth, 16, NDEV, mcols), me, 1, axis=2)
    dm16_me = dm16_me.reshape(depth, 16, mcols).transpose(1, 0, 2).reshape(16, depth * mcols)
    g_w_mod = _mm(sc16, dm16_me, "tn", F32, "mod_bwd_w", tm=d, tn=mcols, tk=16, out_nd=depth)
    dsc16 = _mm(dm16_me, w_mod, "nt", F32, "mod_bwd_c", tm=16, tn=d, tk=mcols)
    small["c_ctx"] = dsc16[8:16]

    names = sorted(small)
    offs, rows = {}, 0
    for nme in names:
        offs[nme] = rows
        rows += small[nme].shape[0]
    sm_all = _all_gather(jnp.concatenate([small[nme] for nme in names], axis=0), "ag_small")
    sm = _sum_lead(sm_all, "sum_small")

    def part(nme, lo, hi):
        return sm[offs[nme] + lo:offs[nme] + hi]

    g_post_ln_g = jnp.concatenate([part(f"ln{l}", 0, 1) for l in range(depth)], axis=0)
    g_post_ln_b = jnp.concatenate([part(f"ln{l}", 1, 2) for l in range(depth)], axis=0)
    g_norm_a_g = jnp.concatenate([part(f"c1_{i}", 0, 1) for i in range(n_even)], axis=0)
    g_norm_a_b = jnp.concatenate([part(f"c1_{i}", 1, 2) for i in range(n_even)], axis=0)
    g_conv_a_b = jnp.concatenate([part(f"c1_{i}", 2, 3) for i in range(n_even)], axis=0)
    g_q_norm_g = jnp.concatenate([part(f"qk{i}", 0, 1)[:, :HEAD_DIM] for i in range(n_odd)], axis=0)
    g_k_norm_g = jnp.concatenate([part(f"qk{i}", 1, 2)[:, :HEAD_DIM] for i in range(n_odd)], axis=0)
    wsh = wch // NDEV
    g_conv_a_w = jnp.stack([lax.dynamic_slice_in_dim(part(f"caw{i}", 0, ka), me * wsh, wsh, axis=1)
                            for i in range(n_even)])
    g_conv_b_w = jnp.stack([lax.dynamic_slice_in_dim(part(f"cbw{i}", 0, kb), me * wsh, wsh, axis=1)
                            for i in range(n_even)])
    g_c_ctx = _ew(lambda a, b: a * _dsilu(b), F32, "d_c_ctx", part("c_ctx", 0, 8), _pad_rows(c_ctx[None, :], 8))[0]

    def exchange(prefix, n, w, m, v):
        outs = []
        for i in range(n):
            recv = _all_to_all(big[f"{prefix}{i}"], f"a2a_{prefix}{i}")
            outs.append(_reduce_adamw(recv, w[i], m[i], v[i], f"adamw_{prefix}{i}"))
        return [jnp.stack([o[j] for o in outs]) for j in range(4)]

    r_w_in_e = exchange("w_in_e", n_even, w_in_e, m_w_in_e, v_w_in_e)
    r_w_out_e = exchange("w_out_e", n_even, w_out_e, m_w_out_e, v_w_out_e)
    r_w_in_o = exchange("w_in_o", n_odd, w_in_o, m_w_in_o, v_w_in_o)
    r_w_out_o = exchange("w_out_o", n_odd, w_out_o, m_w_out_o, v_w_out_o)

    grads = {
        "c_ctx": g_c_ctx, "w_mod": g_w_mod, "b_mod": g_b_mod, "post_ln_g": g_post_ln_g, "post_ln_b": g_post_ln_b,
        "conv_a_w": g_conv_a_w, "conv_a_b": g_conv_a_b, "norm_a_g": g_norm_a_g, "norm_a_b": g_norm_a_b,
        "conv_b_w": g_conv_b_w, "q_norm_g": g_q_norm_g, "k_norm_g": g_k_norm_g,
    }
    state = {
        "c_ctx": (c_ctx, m_c_ctx, v_c_ctx), "w_mod": (w_mod, m_w_mod, v_w_mod), "b_mod": (b_mod, m_b_mod, v_b_mod),
        "post_ln_g": (post_ln_g, m_post_ln_g, v_post_ln_g), "post_ln_b": (post_ln_b, m_post_ln_b, v_post_ln_b),
        "conv_a_w": (conv_a_w, m_conv_a_w, v_conv_a_w), "conv_a_b": (conv_a_b, m_conv_a_b, v_conv_a_b),
        "norm_a_g": (norm_a_g, m_norm_a_g, v_norm_a_g), "norm_a_b": (norm_a_b, m_norm_a_b, v_norm_a_b),
        "conv_b_w": (conv_b_w, m_conv_b_w, v_conv_b_w), "q_norm_g": (q_norm_g, m_q_norm_g, v_q_norm_g),
        "k_norm_g": (k_norm_g, m_k_norm_g, v_k_norm_g),
    }
    res = {"w_in_e": r_w_in_e, "w_out_e": r_w_out_e, "w_in_o": r_w_in_o, "w_out_o": r_w_out_o}
    for nme, g in grads.items():
        w, m, v = state[nme]
        res[nme] = [g] + _adamw(w, g, m, v, f"adamw_{nme}")
    order = ["c_ctx", "w_mod", "b_mod", "post_ln_g", "post_ln_b", "w_in_e", "conv_a_w", "conv_a_b", "norm_a_g",
             "norm_a_b", "conv_b_w", "w_out_e", "w_in_o", "q_norm_g", "k_norm_g", "w_out_o"]
    return (loss, grad_x, *[res[nme][0] for nme in order], *[res[nme][1] for nme in order],
            *[res[nme][2] for nme in order], *[res[nme][3] for nme in order])
```

```python
import functools
import math

import jax
import jax.numpy as jnp
from jax import lax
from jax.experimental import pallas as pl
from jax.experimental.pallas import tpu as pltpu

F32 = jnp.float32
BF16 = jnp.bfloat16

NDEV = 8
GRID_W = 64
HEAD_DIM = 128
GQA_GROUP = 4
ROPE_THETA = 10000.0
LN_EPS = 1e-5
RMS_EPS = 1e-6
ATTN_SCALE = HEAD_DIM ** -0.5
ADAM_LR = 0.001
ADAM_B1 = 0.9
ADAM_B2 = 0.999
ADAM_EPS = 1e-08
ADAM_WD = 0.01
ADAM_STEP = 10
HALO = 16
LANES = 128
VMEM_LIMIT = 56 * 1024 * 1024
MESH = pl.DeviceIdType.MESH


def _cp(n_axes):
    return pltpu.CompilerParams(dimension_semantics=("arbitrary",) * n_axes, vmem_limit_bytes=VMEM_LIMIT)


def _div(dim, pref, mult):
    t = min(pref, dim) // mult * mult
    while t >= mult:
        if dim % t == 0:
            return t
        t -= mult
    return dim


def _sigmoid(x):
    return 1.0 / (1.0 + jnp.exp(-x))


def _silu(x):
    return x * _sigmoid(x)


def _dsilu(x):
    s = _sigmoid(x)
    return s * (1.0 + x * (1.0 - s))


def _ln_norm(z):
    mu = jnp.mean(z, axis=-1, keepdims=True)
    zc = z - mu
    var = jnp.mean(zc * zc, axis=-1, keepdims=True)
    rstd = lax.rsqrt(var + LN_EPS)
    return zc * rstd, rstd


def _ln_bwd(dn, n, rstd):
    return rstd * (dn - jnp.mean(dn, axis=-1, keepdims=True) - n * jnp.mean(dn * n, axis=-1, keepdims=True))


def _partner(x):
    lane = lax.broadcasted_iota(jnp.int32, x.shape, 1)
    return jnp.where((lane % 64) < 32, pltpu.roll(x, 96, 1), pltpu.roll(x, 32, 1))


def _mm(a, b, mode, out_dtype, name, tm=768, tn=2048, tk=2048, out_nd=1):
    if mode == "nn":
        m, kdim = a.shape
        nd, _, ns = b.shape
        tm, tn, tk = _div(m, tm, 8), _div(ns, tn, 128), _div(kdim, tk, 128)
        nbs = ns // tn
        grid = (m // tm, nd * nbs, kdim // tk)
        a_spec = pl.BlockSpec((tm, tk), lambda i, j, k: (i, k))
        b_spec = pl.BlockSpec((None, tk, tn), lambda i, j, k: (j // nbs, k, j % nbs))
        o_spec = pl.BlockSpec((tm, tn), lambda i, j, k: (i, j))
        out_shape = (m, nd * ns)
        dims = (((1,), (0,)), ((), ()))
    elif mode == "nt":
        m, _ = a.shape
        nd, ko, ns = b.shape
        tm, tn, tk = _div(m, tm, 8), _div(ko, tn, 128), _div(ns, tk, 128)
        kbs = ns // tk
        grid = (m // tm, ko // tn, nd * kbs)
        a_spec = pl.BlockSpec((tm, tk), lambda i, j, k: (i, k))
        b_spec = pl.BlockSpec((None, tn, tk), lambda i, j, k: (k // kbs, j, k % kbs))
        o_spec = pl.BlockSpec((tm, tn), lambda i, j, k: (i, j))
        out_shape = (m, ko)
        dims = (((1,), (1,)), ((), ()))
    else:
        m, kdim = a.shape
        n = b.shape[1]
        ns = n // out_nd
        tm, tn, tk = _div(kdim, tm, 128), _div(ns, tn, 128), _div(m, tk, 16)
        nbs = ns // tn
        grid = (kdim // tm, out_nd * nbs, m // tk)
        a_spec = pl.BlockSpec((tk, tm), lambda i, j, k: (k, i))
        b_spec = pl.BlockSpec((tk, tn), lambda i, j, k: (k, j))
        o_spec = pl.BlockSpec((None, tm, tn), lambda i, j, k: (j // nbs, i, j % nbs))
        out_shape = (out_nd, kdim, ns)
        dims = (((0,), (0,)), ((), ()))
    nk = grid[2]

    def prod(a_ref, b_ref):
        return lax.dot_general(a_ref[...].astype(BF16), b_ref[...].astype(BF16), dims, preferred_element_type=F32)

    def body_one(a_ref, b_ref, o_ref):
        o_ref[...] = prod(a_ref, b_ref).astype(o_ref.dtype)

    def body_acc(a_ref, b_ref, o_ref, acc_ref):
        k = pl.program_id(2)

        @pl.when(k == 0)
        def _():
            acc_ref[...] = jnp.zeros_like(acc_ref)

        acc_ref[...] += prod(a_ref, b_ref)

        @pl.when(k == nk - 1)
        def _():
            o_ref[...] = acc_ref[...].astype(o_ref.dtype)

    return pl.pallas_call(
        body_one if nk == 1 else body_acc, name=name, grid=grid, in_specs=[a_spec, b_spec], out_specs=o_spec,
        out_shape=jax.ShapeDtypeStruct(out_shape, out_dtype),
        scratch_shapes=[] if nk == 1 else [pltpu.VMEM((tm, tn), F32)], compiler_params=_cp(3),
    )(a, b)


def _ew(fn, out_dtype, name, *xs):
    rows, cols = xs[0].shape
    tr = rows if rows <= 64 else _div(rows, 256, 16)

    def body(*refs):
        refs[-1][...] = fn(*[r[...] for r in refs[:-1]]).astype(out_dtype)

    spec = pl.BlockSpec((tr, cols), lambda i: (i, 0))
    return pl.pallas_call(
        body, name=name, grid=(rows // tr,), in_specs=[spec] * len(xs), out_specs=spec,
        out_shape=jax.ShapeDtypeStruct((rows, cols), out_dtype), compiler_params=_cp(1),
    )(*xs)


def _sum_lead(x, name):
    n, rows, cols = x.shape
    tr = _div(rows, 64, 8)

    def body(x_ref, o_ref):
        acc = x_ref[0]
        for s in range(1, n):
            acc = acc + x_ref[s]
        o_ref[...] = acc

    return pl.pallas_call(
        body, name=name, grid=(rows // tr,),
        in_specs=[pl.BlockSpec((n, tr, cols), lambda i: (0, i, 0))],
        out_specs=pl.BlockSpec((tr, cols), lambda i: (i, 0)),
        out_shape=jax.ShapeDtypeStruct((rows, cols), F32), compiler_params=_cp(1),
    )(x)


def _adam_math(w, g, m, v):
    m = ADAM_B1 * m + (1.0 - ADAM_B1) * g
    v = ADAM_B2 * v + (1.0 - ADAM_B2) * (g * g)
    m_hat = m / (1.0 - ADAM_B1 ** ADAM_STEP)
    v_hat = v / (1.0 - ADAM_B2 ** ADAM_STEP)
    delta = -ADAM_LR * (m_hat / (jnp.sqrt(v_hat) + ADAM_EPS) + ADAM_WD * w)
    return delta, m, v


def _adamw(w, g, m, v, name):
    shape = w.shape
    cols = shape[-1]
    w2, g2, m2, v2 = [t.reshape(-1, cols) for t in (w, g, m, v)]
    rows = w2.shape[0]
    tr = rows if rows <= 512 else _div(rows, 256, 8)

    def body(w_ref, g_ref, m_ref, v_ref, d_ref, nm_ref, nv_ref):
        d, nm, nv = _adam_math(w_ref[...], g_ref[...], m_ref[...], v_ref[...])
        d_ref[...] = d
        nm_ref[...] = nm
        nv_ref[...] = nv

    spec = pl.BlockSpec((tr, cols), lambda i: (i, 0))
    outs = pl.pallas_call(
        body, name=name, grid=(rows // tr,), in_specs=[spec] * 4, out_specs=[spec] * 3,
        out_shape=[jax.ShapeDtypeStruct((rows, cols), F32)] * 3, compiler_params=_cp(1),
    )(w2, g2, m2, v2)
    return [o.reshape(shape) for o in outs]


def _reduce_adamw(parts, w, m, v, name):
    shape = w.shape
    n, rows, cols = parts.shape
    w2, m2, v2 = [t.reshape(rows, cols) for t in (w, m, v)]
    tr = _div(rows, 128, 16)

    def body(p_ref, w_ref, m_ref, v_ref, g_ref, d_ref, nm_ref, nv_ref):
        g = p_ref[0].astype(F32)
        for s in range(1, n):
            g = g + p_ref[s].astype(F32)
        d, nm, nv = _adam_math(w_ref[...], g, m_ref[...], v_ref[...])
        g_ref[...] = g
        d_ref[...] = d
        nm_ref[...] = nm
        nv_ref[...] = nv

    spec = pl.BlockSpec((tr, cols), lambda i: (i, 0))
    outs = pl.pallas_call(
        body, name=name, grid=(rows // tr,),
        in_specs=[pl.BlockSpec((n, tr, cols), lambda i: (0, i, 0))] + [spec] * 3, out_specs=[spec] * 4,
        out_shape=[jax.ShapeDtypeStruct((rows, cols), F32)] * 4, compiler_params=_cp(1),
    )(parts, w2, m2, v2)
    return [o.reshape(shape) for o in outs]


def _all_gather(x, name):
    rows, cols = x.shape

    def body(x_ref, out_ref, send_sems, recv_sems, local_sem):
        mx, my, mc = lax.axis_index("x"), lax.axis_index("y"), lax.axis_index("c")
        me, sibling = (mx, my, mc), (mx, my, 1 - mc)
        chips = [(1 - mx, my), (mx, 1 - my), (1 - mx, 1 - my)]

        def slab(px, py, pc):
            return out_ref.at[4 * px + 2 * py + pc]

        def copy(k, block, to, src=None):
            return pltpu.make_async_remote_copy(
                src_ref=slab(*block) if src is None else src, dst_ref=slab(*block),
                send_sem=send_sems.at[k], recv_sem=recv_sems.at[k], device_id=to, device_id_type=MESH)

        mine = pltpu.make_async_copy(x_ref, slab(*me), local_sem)
        mine.start()
        first = [copy(0, me, sibling, src=x_ref)]
        first += [copy(1 + j, me, (*chip, mc), src=x_ref) for j, chip in enumerate(chips)]
        for cp in first:
            cp.start()
        passed = [copy(4 + j, (*chip, mc), sibling) for j, chip in enumerate(chips)]
        for j, chip in enumerate(chips):
            copy(1 + j, (*chip, mc), me).wait_recv()
            passed[j].start()
        copy(0, sibling, me).wait_recv()
        for j, chip in enumerate(chips):
            copy(4 + j, (*chip, 1 - mc), me).wait_recv()
        for cp in first + passed:
            cp.wait_send()
        mine.wait()

    return pl.pallas_call(
        body, name=name, out_shape=jax.ShapeDtypeStruct((NDEV, rows, cols), x.dtype),
        in_specs=[pl.BlockSpec(memory_space=pltpu.HBM)], out_specs=pl.BlockSpec(memory_space=pltpu.HBM),
        scratch_shapes=[pltpu.SemaphoreType.DMA((7,)), pltpu.SemaphoreType.DMA((7,)), pltpu.SemaphoreType.DMA],
    )(x)


def _all_to_all(x, name):
    def body(x_ref, out_ref, send_sems, recv_sems, local_sem):
        mx, my, mc = lax.axis_index("x"), lax.axis_index("y"), lax.axis_index("c")
        me = 4 * mx + 2 * my + mc
        mine = pltpu.make_async_copy(x_ref.at[me], out_ref.at[me], local_sem)
        mine.start()
        copies = []
        for k in range(1, NDEV):
            px = mx ^ ((k >> 2) & 1)
            py = my ^ ((k >> 1) & 1)
            pc = mc ^ (k & 1)
            peer = 4 * px + 2 * py + pc
            cp = pltpu.make_async_remote_copy(
                src_ref=x_ref.at[peer], dst_ref=out_ref.at[me], send_sem=send_sems.at[k - 1],
                recv_sem=recv_sems.at[k - 1], device_id=(px, py, pc), device_id_type=MESH)
            cp.start()
            copies.append(cp)
        for cp in copies:
            cp.wait()
        mine.wait()

    return pl.pallas_call(
        body, name=name, out_shape=jax.ShapeDtypeStruct(x.shape, x.dtype),
        in_specs=[pl.BlockSpec(memory_space=pltpu.HBM)], out_specs=pl.BlockSpec(memory_space=pltpu.HBM),
        scratch_shapes=[pltpu.SemaphoreType.DMA((7,)), pltpu.SemaphoreType.DMA((7,)), pltpu.SemaphoreType.DMA],
    )(x)


def _row_tile(t_ctx, pref=256):
    return _div(t_ctx, pref, 8)


def _mod_spec(d, nctx):
    return pl.BlockSpec((None, 3, d), lambda i: (jnp.where(i >= nctx, 1, 0), 0, 0))


def _modulate(xb, modv, t_ctx, name):
    r, d = xb.shape
    tm = _row_tile(t_ctx)
    nctx = t_ctx // tm

    def body(x_ref, mv_ref, h_ref):
        h_ref[...] = (x_ref[...] * (1.0 + mv_ref[1:2, :]) + mv_ref[0:1, :]).astype(BF16)

    row = pl.BlockSpec((tm, d), lambda i: (i, 0))
    return pl.pallas_call(
        body, name=name, grid=(r // tm,), in_specs=[row, _mod_spec(d, nctx)], out_specs=row,
        out_shape=jax.ShapeDtypeStruct((r, d), BF16), compiler_params=_cp(1),
    )(xb, modv)


def _postln_fwd(xb, y, modv, lnp, alpha, t_ctx, name):
    r, d = xb.shape
    tm = _row_tile(t_ctx)
    nctx = t_ctx // tm

    def body(x_ref, y_ref, mv_ref, ln_ref, o_ref):
        n, _ = _ln_norm(alpha * x_ref[...] + mv_ref[2:3, :] * y_ref[...])
        o_ref[...] = n * ln_ref[0:1, :] + ln_ref[1:2, :]

    row = pl.BlockSpec((tm, d), lambda i: (i, 0))
    return pl.pallas_call(
        body, name=name, grid=(r // tm,),
        in_specs=[row, row, _mod_spec(d, nctx), pl.BlockSpec((2, d), lambda i: (0, 0))], out_specs=row,
        out_shape=jax.ShapeDtypeStruct((r, d), F32), compiler_params=_cp(1),
    )(xb, y, modv, lnp)


def _postln_bwd(dxn, xb, y, modv, lnp, alpha, t_ctx, name):
    r, d = xb.shape
    tm = _row_tile(t_ctx)
    nctx = t_ctx // tm

    def body(dxn_ref, x_ref, y_ref, mv_ref, ln_ref, dy_ref, dxa_ref, s_ref):
        i = pl.program_id(0)

        @pl.when(i == 0)
        def _():
            s_ref[...] = jnp.zeros_like(s_ref)

        yv = y_ref[...]
        gate = mv_ref[2:3, :]
        n, rstd = _ln_norm(alpha * x_ref[...] + gate * yv)
        dxn_v = dxn_ref[...]
        dz = _ln_bwd(dxn_v * ln_ref[0:1, :], n, rstd)
        dy_ref[...] = (gate * dz).astype(BF16)
        dxa_ref[...] = alpha * dz
        s_ref[0:1, :] += jnp.sum(dxn_v * n, axis=0, keepdims=True)
        s_ref[1:2, :] += jnp.sum(dxn_v, axis=0, keepdims=True)
        dgate = jnp.sum(dz * yv, axis=0, keepdims=True)
        is_ctx = i < nctx
        s_ref[2:3, :] += jnp.where(is_ctx, dgate, 0.0)
        s_ref[3:4, :] += jnp.where(is_ctx, 0.0, dgate)

    row = pl.BlockSpec((tm, d), lambda i: (i, 0))
    return pl.pallas_call(
        body, name=name, grid=(r // tm,),
        in_specs=[row, row, row, _mod_spec(d, nctx), pl.BlockSpec((2, d), lambda i: (0, 0))],
        out_specs=[row, row, pl.BlockSpec((8, d), lambda i: (0, 0))],
        out_shape=[jax.ShapeDtypeStruct((r, d), BF16), jax.ShapeDtypeStruct((r, d), F32),
                   jax.ShapeDtypeStruct((8, d), F32)],
        compiler_params=_cp(1),
    )(dxn, xb, y, modv, lnp)


def _mod_bwd(dh, xb, dxa, modv, t_ctx, name):
    r, d = xb.shape
    tm = _row_tile(t_ctx)
    nctx = t_ctx // tm

    def body(dh_ref, x_ref, dxa_ref, mv_ref, dx_ref, s_ref):
        i = pl.program_id(0)

        @pl.when(i == 0)
        def _():
            s_ref[...] = jnp.zeros_like(s_ref)

        dhv = dh_ref[...]
        dx_ref[...] = dxa_ref[...] + dhv * (1.0 + mv_ref[1:2, :])
        dshift = jnp.sum(dhv, axis=0, keepdims=True)
        dscale = jnp.sum(dhv * x_ref[...], axis=0, keepdims=True)
        is_ctx = i < nctx
        s_ref[0:1, :] += jnp.where(is_ctx, dshift, 0.0)
        s_ref[1:2, :] += jnp.where(is_ctx, dscale, 0.0)
        s_ref[2:3, :] += jnp.where(is_ctx, 0.0, dshift)
        s_ref[3:4, :] += jnp.where(is_ctx, 0.0, dscale)

    row = pl.BlockSpec((tm, d), lambda i: (i, 0))
    return pl.pallas_call(
        body, name=name, grid=(r // tm,), in_specs=[row, row, row, _mod_spec(d, nctx)],
        out_specs=[row, pl.BlockSpec((8, d), lambda i: (0, 0))],
        out_shape=[jax.ShapeDtypeStruct((r, d), F32), jax.ShapeDtypeStruct((8, d), F32)],
        compiler_params=_cp(1),
    )(dh, xb, dxa, modv)


def _loss_grad(xb, target, t_ctx, name):
    r, d = xb.shape
    tm = _row_tile(t_ctx)
    nctx = t_ctx // tm

    def body(x_ref, t_ref, l_ref, dx_ref):
        i = pl.program_id(0)

        @pl.when(i == 0)
        def _():
            l_ref[...] = jnp.zeros_like(l_ref)

        @pl.when(i < nctx)
        def _():
            dx_ref[...] = jnp.zeros_like(dx_ref)

        @pl.when(i >= nctx)
        def _():
            e = x_ref[...] - t_ref[...]
            dx_ref[...] = e / d
            l_ref[...] += 0.5 * jnp.sum(jnp.sum(e * e, axis=1, keepdims=True), axis=0, keepdims=True) / d

    row = pl.BlockSpec((tm, d), lambda i: (i, 0))
    return pl.pallas_call(
        body, name=name, grid=(r // tm,),
        in_specs=[row, pl.BlockSpec((tm, d), lambda i: (jnp.maximum(i - nctx, 0), 0))],
        out_specs=[pl.BlockSpec((8, 128), lambda i: (0, 0)), row],
        out_shape=[jax.ShapeDtypeStruct((8, 128), F32), jax.ShapeDtypeStruct((r, d), F32)],
        compiler_params=_cp(1),
    )(xb, target)


def _conv_specs(r, w, tm, cblk):
    hb = tm // HALO
    last = r // HALO - 1
    main = pl.BlockSpec((tm, w), lambda i: (i, cblk))
    top = pl.BlockSpec((HALO, w), lambda i: (jnp.maximum(i * hb - 1, 0), cblk))
    bot = pl.BlockSpec((HALO, w), lambda i: (jnp.minimum((i + 1) * hb, last), cblk))
    return [main, top, bot]


def _fill_pad(pad_ref, main, top, bot, top_ok, bot_ok, tm):
    pad_ref[0:HALO, :] = jnp.where(top_ok, top, 0.0)
    pad_ref[HALO:HALO + tm, :] = main
    pad_ref[HALO + tm:2 * HALO + tm, :] = jnp.where(bot_ok, bot, 0.0)


def _edges(i, nctx, nr):
    top_ok = jnp.logical_and(i != 0, i != nctx)
    bot_ok = jnp.logical_and(i != nctx - 1, i != nr - 1)
    return top_ok, bot_ok


def _rot_fill(rot_ref, pad_ref, n):
    rot_ref[0, 0:n, :] = pad_ref[0:n, :]
    for b in range(1, 8):
        rot_ref[b, 0:n, :] = pad_ref[pl.ds(b, n), :]


def _tap(rot_ref, off, tm):
    return rot_ref[off % 8, pl.ds(off - off % 8, tm), :]


def _rows8(x):
    return jnp.sum(x.reshape(x.shape[0] // 8, 8, x.shape[1]), axis=0)


def _conv_fwd(p, caw, cab, nag, nab, cbw, t_ctx, name):
    r = p.shape[0]
    w = p.shape[1] // 7
    ka, kb = caw.shape[0], cbw.shape[0]
    tm = _row_tile(t_ctx, 128)
    nctx, nr = t_ctx // tm, r // tm
    n = tm + 2 * HALO - 8

    def body(av, avt, avb, ag, agt, agb, agate, bx, bxt, bxb, bb, bc, bct, bcb, bgate,
             caw_ref, cab_ref, nag_ref, nab_ref, cbw_ref, cat_ref, u1_ref, pad, rot, v_s):
        top_ok, bot_ok = _edges(pl.program_id(0), nctx, nr)

        def chunk(c, carry):
            cols = pl.ds(pl.multiple_of(c * LANES, LANES), LANES)
            _fill_pad(pad, av[:, cols] * _sigmoid(ag[:, cols]), avt[:, cols] * _sigmoid(agt[:, cols]),
                      avb[:, cols] * _sigmoid(agb[:, cols]), top_ok, bot_ok, tm)
            _rot_fill(rot, pad, n)
            u1 = jnp.zeros((tm, LANES), F32) + cab_ref[:, cols]
            for k in range(ka):
                u1 = u1 + caw_ref[k:k + 1, cols] * _tap(rot, HALO - ka // 2 + k, tm)
            u1_ref[:, cols] = u1
            _fill_pad(pad, bc[:, cols] * bx[:, cols], bct[:, cols] * bxt[:, cols], bcb[:, cols] * bxb[:, cols],
                      top_ok, bot_ok, tm)
            v = jnp.zeros((tm, LANES), F32)
            for k in range(kb):
                v = v + cbw_ref[k:k + 1, cols] * pad[pl.ds(HALO - kb // 2 + k, tm), :]
            v_s[:, cols] = v
            return carry

        lax.fori_loop(0, w // LANES, chunk, 0)
        nrm, _ = _ln_norm(u1_ref[...])
        a_out = _silu(nrm * nag_ref[...] + nab_ref[...]) * _silu(agate[...])
        cat_ref[:, 0:w] = a_out.astype(BF16)
        cat_ref[:, w:2 * w] = (bb[...] * v_s[...] * _silu(bgate[...])).astype(BF16)

    def main(cblk):
        return pl.BlockSpec((tm, w), lambda i: (i, cblk))

    def whole(a):
        return pl.BlockSpec(a.shape, lambda i: (0, 0))

    in_specs = (_conv_specs(r, w, tm, 0) + _conv_specs(r, w, tm, 1) + [main(2)] + _conv_specs(r, w, tm, 3)
                + [main(4)] + _conv_specs(r, w, tm, 5) + [main(6)]
                + [whole(caw), whole(cab), whole(nag), whole(nab), whole(cbw)])
    return pl.pallas_call(
        body, name=name, grid=(nr,), in_specs=in_specs,
        out_specs=[pl.BlockSpec((tm, 2 * w), lambda i: (i, 0)), pl.BlockSpec((tm, w), lambda i: (i, 0))],
        out_shape=[jax.ShapeDtypeStruct((r, 2 * w), BF16), jax.ShapeDtypeStruct((r, w), F32)],
        scratch_shapes=[pltpu.VMEM((tm + 2 * HALO, LANES), F32), pltpu.VMEM((8, n, LANES), F32),
                        pltpu.VMEM((tm, w), F32)],
        compiler_params=_cp(1),
    )(*([p] * 15), caw, cab, nag, nab, cbw)


def _conv_bwd1(dcat, p, u1, nag, nab, t_ctx, name):
    r, w = u1.shape
    tm = _row_tile(t_ctx, 128)

    def body(da_ref, agate_ref, u1_ref, nag_ref, nab_ref, du1_ref, dag_ref, s_ref):
        @pl.when(pl.program_id(0) == 0)
        def _():
            s_ref[...] = jnp.zeros_like(s_ref)

        n, rstd = _ln_norm(u1_ref[...])
        g = nag_ref[...]
        u2 = n * g + nab_ref[...]
        da = da_ref[...]
        ag = agate_ref[...]
        dag_ref[...] = (da * _silu(u2) * _dsilu(ag)).astype(BF16)
        du2 = da * _silu(ag) * _dsilu(u2)
        du1 = _ln_bwd(du2 * g, n, rstd)
        du1_ref[...] = du1
        s_ref[0:1, :] += jnp.sum(du2 * n, axis=0, keepdims=True)
        s_ref[1:2, :] += jnp.sum(du2, axis=0, keepdims=True)
        s_ref[2:3, :] += jnp.sum(du1, axis=0, keepdims=True)

    def win(cblk):
        return pl.BlockSpec((tm, w), lambda i: (i, cblk))

    one = pl.BlockSpec((1, w), lambda i: (0, 0))
    return pl.pallas_call(
        body, name=name, grid=(r // tm,), in_specs=[win(0), win(2), win(0), one, one],
        out_specs=[win(0), win(0), pl.BlockSpec((8, w), lambda i: (0, 0))],
        out_shape=[jax.ShapeDtypeStruct((r, w), F32), jax.ShapeDtypeStruct((r, w), BF16),
                   jax.ShapeDtypeStruct((8, w), F32)],
        compiler_params=_cp(1),
    )(dcat, p, u1, nag, nab)


def _conv_bwd2(du1, dcat, p, dag, caw, cbw, t_ctx, name):
    r, w = du1.shape
    ka, kb = caw.shape[0], cbw.shape[0]
    tm = _row_tile(t_ctx, 128)
    nctx, nr = t_ctx // tm, r // tm
    n = tm + 2 * HALO - 8

    def body(du, dut, dub, av, avt, avb, ag, agt, agb, db, dbt, dbb, bx, bxt, bxb, bb, bbt, bbb,
             bc, bct, bcb, bg, bgt, bgb, dag_ref, caw_ref, cbw_ref, dp_ref, dcaw_ref, dcbw_ref,
             pad, pad2, rot_u, rot_d, acc_a, acc_b):
        i = pl.program_id(0)
        top_ok, bot_ok = _edges(i, nctx, nr)

        @pl.when(i == 0)
        def _():
            acc_a[...] = jnp.zeros_like(acc_a)
            acc_b[...] = jnp.zeros_like(acc_b)

        def chunk(c, carry):
            c0 = pl.multiple_of(c * LANES, LANES)
            cols = pl.ds(c0, LANES)

            def seg(s):
                return pl.ds(pl.multiple_of(s * w + c0, LANES), LANES)

            sg = _sigmoid(ag[:, cols])
            av_m = av[:, cols]
            _fill_pad(pad, av_m * sg, avt[:, cols] * _sigmoid(agt[:, cols]), avb[:, cols] * _sigmoid(agb[:, cols]),
                      top_ok, bot_ok, tm)
            _rot_fill(rot_u, pad, n)
            du_m = du[:, cols]
            _fill_pad(pad, du_m, dut[:, cols], dub[:, cols], top_ok, bot_ok, tm)
            _rot_fill(rot_d, pad, n)
            du0 = jnp.zeros((tm, LANES), F32)
            for k in range(ka):
                du0 = du0 + caw_ref[k:k + 1, cols] * _tap(rot_d, HALO + ka // 2 - k, tm)
                acc_a[k, :, cols] += _rows8(du_m * _tap(rot_u, HALO - ka // 2 + k, tm))
            dp_ref[:, seg(0)] = (du0 * sg).astype(BF16)
            dp_ref[:, seg(1)] = (du0 * av_m * sg * (1.0 - sg)).astype(BF16)
            dp_ref[:, seg(2)] = dag_ref[:, cols]

            bc_m, bx_m = bc[:, cols], bx[:, cols]
            _fill_pad(pad, bc_m * bx_m, bct[:, cols] * bxt[:, cols], bcb[:, cols] * bxb[:, cols],
                      top_ok, bot_ok, tm)
            v = jnp.zeros((tm, LANES), F32)
            for k in range(kb):
                v = v + cbw_ref[k:k + 1, cols] * pad[pl.ds(HALO - kb // 2 + k, tm), :]
            db_m, bb_m, bg_m = db[:, cols], bb[:, cols], bg[:, cols]
            sbg = _silu(bg_m)
            dv_m = db_m * bb_m * sbg
            _fill_pad(pad2, dv_m, dbt[:, cols] * bbt[:, cols] * _silu(bgt[:, cols]),
                      dbb[:, cols] * bbb[:, cols] * _silu(bgb[:, cols]), top_ok, bot_ok, tm)
            dw0 = jnp.zeros((tm, LANES), F32)
            for k in range(kb):
                dw0 = dw0 + cbw_ref[k:k + 1, cols] * pad2[pl.ds(HALO + kb // 2 - k, tm), :]
                acc_b[k, :, cols] += _rows8(dv_m * pad[pl.ds(HALO - kb // 2 + k, tm), :])
            dp_ref[:, seg(3)] = (dw0 * bc_m).astype(BF16)
            dp_ref[:, seg(4)] = (db_m * v * sbg).astype(BF16)
            dp_ref[:, seg(5)] = (dw0 * bx_m).astype(BF16)
            dp_ref[:, seg(6)] = (db_m * bb_m * v * _dsilu(bg_m)).astype(BF16)
            return carry

        lax.fori_loop(0, w // LANES, chunk, 0)

        @pl.when(i == nr - 1)
        def _():
            dcaw_ref[...] = jnp.zeros_like(dcaw_ref)
            dcbw_ref[...] = jnp.zeros_like(dcbw_ref)
            for k in range(ka):
                dcaw_ref[k:k + 1, :] = jnp.sum(acc_a[k], axis=0, keepdims=True)
            for k in range(kb):
                dcbw_ref[k:k + 1, :] = jnp.sum(acc_b[k], axis=0, keepdims=True)

    def whole(a):
        return pl.BlockSpec(a.shape, lambda i: (0, 0))

    in_specs = (_conv_specs(r, w, tm, 0) + _conv_specs(r, w, tm, 0) + _conv_specs(r, w, tm, 1)
                + _conv_specs(r, w, tm, 1) + _conv_specs(r, w, tm, 3) + _conv_specs(r, w, tm, 4)
                + _conv_specs(r, w, tm, 5) + _conv_specs(r, w, tm, 6)
                + [pl.BlockSpec((tm, w), lambda i: (i, 0)), whole(caw), whole(cbw)])
    pad_t = pltpu.VMEM((tm + 2 * HALO, LANES), F32)
    rot_t = pltpu.VMEM((8, n, LANES), F32)
    return pl.pallas_call(
        body, name=name, grid=(nr,), in_specs=in_specs,
        out_specs=[pl.BlockSpec((tm, 7 * w), lambda i: (i, 0)), pl.BlockSpec((32, w), lambda i: (0, 0)),
                   pl.BlockSpec((8, w), lambda i: (0, 0))],
        out_shape=[jax.ShapeDtypeStruct((r, 7 * w), BF16), jax.ShapeDtypeStruct((32, w), F32),
                   jax.ShapeDtypeStruct((8, w), F32)],
        scratch_shapes=[pad_t, pad_t, rot_t, rot_t, pltpu.VMEM((32, 8, w), F32), pltpu.VMEM((8, 8, w), F32)],
        compiler_params=_cp(1),
    )(*([du1] * 3), *([p] * 6), *([dcat] * 3), *([p] * 12), dag, caw, cbw)


def _rms(xh):
    r = lax.rsqrt(jnp.mean(xh * xh, axis=-1, keepdims=True) + RMS_EPS)
    return xh * r, r


def _qk_fwd(p, cos, sin, qg, kg, att, kv, t_ctx, name):
    r = p.shape[0]
    tm = _row_tile(t_ctx)

    def body(q_ref, k_ref, v_ref, cos_ref, sin_ref, qg_ref, kg_ref, qr_ref, kr_ref, vb_ref):
        cs, sn = cos_ref[...], sin_ref[...]
        for src, g_ref, dst, nh in ((q_ref, qg_ref, qr_ref, att // HEAD_DIM), (k_ref, kg_ref, kr_ref, kv // HEAD_DIM)):
            for h in range(nh):
                cols = slice(h * HEAD_DIM, (h + 1) * HEAD_DIM)
                n, _ = _rms(src[:, cols])
                n = n * g_ref[...]
                dst[:, cols] = (n * cs + _partner(n) * sn).astype(BF16)
        vb_ref[...] = v_ref[...].astype(BF16)

    def rows(width, cblk):
        return pl.BlockSpec((tm, width), lambda i: (i, cblk))

    one = pl.BlockSpec((1, HEAD_DIM), lambda i: (0, 0))
    return pl.pallas_call(
        body, name=name, grid=(r // tm,),
        in_specs=[rows(att, 0), rows(kv, 2 * att // kv), rows(kv, 2 * att // kv + 1),
                  rows(HEAD_DIM, 0), rows(HEAD_DIM, 0), one, one],
        out_specs=[rows(att, 0), rows(kv, 0), rows(kv, 0)],
        out_shape=[jax.ShapeDtypeStruct((r, att), BF16), jax.ShapeDtypeStruct((r, kv), BF16),
                   jax.ShapeDtypeStruct((r, kv), BF16)],
        compiler_params=_cp(1),
    )(p, p, p, cos, sin, qg, kg)


def _qk_bwd(p, dqr, dkr, cos, sin, qg, kg, att, kv, t_ctx, name):
    r = p.shape[0]
    tm = _row_tile(t_ctx)

    def body(q_ref, k_ref, dqr_ref, dkr_ref, cos_ref, sin_ref, qg_ref, kg_ref, dq_ref, dk_ref, s_ref):
        @pl.when(pl.program_id(0) == 0)
        def _():
            s_ref[...] = jnp.zeros_like(s_ref)

        cs, sn = cos_ref[...], sin_ref[...]
        for row, (src, dsrc, g_ref, dst, nh) in enumerate((
                (q_ref, dqr_ref, qg_ref, dq_ref, att // HEAD_DIM), (k_ref, dkr_ref, kg_ref, dk_ref, kv // HEAD_DIM))):
            dg = jnp.zeros((1, HEAD_DIM), F32)
            for h in range(nh):
                cols = slice(h * HEAD_DIM, (h + 1) * HEAD_DIM)
                n0, rr = _rms(src[:, cols])
                d = dsrc[:, cols]
                dng = d * cs + _partner(d * sn)
                dg = dg + jnp.sum(dng * n0, axis=0, keepdims=True)
                dn0 = dng * g_ref[...]
                dst[:, cols] = (rr * (dn0 - n0 * jnp.mean(dn0 * n0, axis=-1, keepdims=True))).astype(BF16)
            s_ref[row:row + 1, :] += dg

    def rows(width, cblk):
        return pl.BlockSpec((tm, width), lambda i: (i, cblk))

    one = pl.BlockSpec((1, HEAD_DIM), lambda i: (0, 0))
    return pl.pallas_call(
        body, name=name, grid=(r // tm,),
        in_specs=[rows(att, 0), rows(kv, 2 * att // kv), rows(att, 0), rows(kv, 0),
                  rows(HEAD_DIM, 0), rows(HEAD_DIM, 0), one, one],
        out_specs=[rows(att, 0), rows(kv, 0), pl.BlockSpec((8, HEAD_DIM), lambda i: (0, 0))],
        out_shape=[jax.ShapeDtypeStruct((r, att), BF16), jax.ShapeDtypeStruct((r, kv), BF16),
                   jax.ShapeDtypeStruct((8, HEAD_DIM), F32)],
        compiler_params=_cp(1),
    )(p, p, dqr, dkr, cos, sin, qg, kg)


def _stack_heads(x, tq):
    return jnp.concatenate([x[:, g * HEAD_DIM:(g + 1) * HEAD_DIM] for g in range(GQA_GROUP)], axis=0)


def _attn_tiles(t_ctx, s_lat, tkl_pref):
    return _div(t_ctx, 256, 8), _div(s_lat, tkl_pref, 8)


NT_DIMS = (((1,), (1,)), ((), ()))
TN_DIMS = (((0,), (0,)), ((), ()))
ATTN_SCALE_LOG2 = ATTN_SCALE * math.log2(math.e)


def _flash_fwd(qr, kr, vb, p, att, t_ctx, name):
    r = qr.shape[0]
    s_lat = r - t_ctx
    gw = GQA_GROUP * HEAD_DIM
    nkv = att // gw
    tq, tkl = _attn_tiles(t_ctx, s_lat, 1024)
    nq, nq_ctx, n_lat = r // tq, t_ctx // tq, s_lat // tkl
    nl = GQA_GROUP * tq

    def body(q_ref, k_ref, v_ref, g_ref, o_ref, og_ref, lse_ref, m_s, l_s, acc_s):
        qi = pl.program_id(1)
        q4 = _stack_heads(q_ref[...], tq)
        m_s[...] = jnp.full_like(m_s, -1e30)
        l_s[...] = jnp.zeros_like(l_s)
        acc_s[...] = jnp.zeros_like(acc_s)

        def step(off, size):
            kb = k_ref[pl.ds(off, size), :]
            st = lax.dot_general(kb, q4, NT_DIMS, preferred_element_type=F32) * ATTN_SCALE_LOG2
            m_old = m_s[...]
            m_new = jnp.maximum(m_old, jnp.max(st, axis=0, keepdims=True))
            pe = jnp.exp2(st - m_new)
            a = jnp.exp2(m_old - m_new)
            l_s[...] = a * l_s[...] + jnp.sum(pe, axis=0, keepdims=True)
            acc_s[...] = a * acc_s[...] + lax.dot_general(v_ref[pl.ds(off, size), :], pe.astype(BF16), TN_DIMS,
                                                          preferred_element_type=F32)
            m_s[...] = m_new

        step(0, t_ctx)

        def lat(n, carry):
            step(pl.multiple_of(t_ctx + n * tkl, math.gcd(t_ctx, tkl)), tkl)
            return carry

        lax.fori_loop(0, jnp.where(qi < nq_ctx, 0, n_lat), lat, 0)
        o4 = (acc_s[...] / l_s[...]).T
        lse_ref[...] = m_s[...] + jnp.log2(l_s[...])
        sg = _silu(g_ref[...])
        for g in range(GQA_GROUP):
            cols = slice(g * HEAD_DIM, (g + 1) * HEAD_DIM)
            og = o4[g * tq:(g + 1) * tq, :]
            o_ref[:, cols] = og
            og_ref[:, cols] = (og * sg[:, cols]).astype(BF16)

    qspec = pl.BlockSpec((tq, gw), lambda h, i: (i, h))
    kspec = pl.BlockSpec((r, HEAD_DIM), lambda h, i: (0, h))
    return pl.pallas_call(
        body, name=name, grid=(nkv, nq),
        in_specs=[qspec, kspec, kspec, pl.BlockSpec((tq, gw), lambda h, i: (i, att // gw + h))],
        out_specs=[qspec, qspec, pl.BlockSpec((None, None, 1, nl), lambda h, i: (h, i, 0, 0))],
        out_shape=[jax.ShapeDtypeStruct((r, att), F32), jax.ShapeDtypeStruct((r, att), BF16),
                   jax.ShapeDtypeStruct((nkv, nq, 1, nl), F32)],
        scratch_shapes=[pltpu.VMEM((1, nl), F32), pltpu.VMEM((1, nl), F32), pltpu.VMEM((HEAD_DIM, nl), F32)],
        compiler_params=_cp(2),
    )(qr, kr, vb, p)


def _flash_bwd(qr, kr, vb, p, o, dog, lse, att, kv, t_ctx, name):
    r = qr.shape[0]
    s_lat = r - t_ctx
    gw = GQA_GROUP * HEAD_DIM
    nkv = att // gw
    tq, tkl = _attn_tiles(t_ctx, s_lat, 1024)
    nq, nq_ctx, n_lat = r // tq, t_ctx // tq, s_lat // tkl
    nl = GQA_GROUP * tq

    def body(q_ref, k_ref, v_ref, g_ref, o_ref, dog_ref, lse_ref, dq_ref, dgate_ref, dk_ref, dv_ref, dq_s):
        qi = pl.program_id(1)

        @pl.when(qi == 0)
        def _():
            dk_ref[...] = jnp.zeros_like(dk_ref)
            dv_ref[...] = jnp.zeros_like(dv_ref)

        gate, ov, dogv = g_ref[...], o_ref[...], dog_ref[...]
        dgate_ref[...] = (dogv * ov * _dsilu(gate)).astype(BF16)
        do = dogv * _silu(gate)
        do4 = _stack_heads(do, tq)
        delta = jnp.sum((do4 * _stack_heads(ov, tq)).T, axis=0, keepdims=True)
        do4 = do4.astype(BF16)
        q4 = _stack_heads(q_ref[...], tq)
        lse_v = lse_ref[...]
        dq_s[...] = jnp.zeros_like(dq_s)

        def step(off, size):
            kb = k_ref[pl.ds(off, size), :]
            vv = v_ref[pl.ds(off, size), :]
            st = lax.dot_general(kb, q4, NT_DIMS, preferred_element_type=F32) * ATTN_SCALE_LOG2
            pe = jnp.exp2(st - lse_v)
            dp = lax.dot_general(vv, do4, NT_DIMS, preferred_element_type=F32)
            ds = (pe * (dp - delta) * ATTN_SCALE).astype(BF16)
            dv_ref[pl.ds(off, size), :] += jnp.dot(pe.astype(BF16), do4, preferred_element_type=F32)
            dk_ref[pl.ds(off, size), :] += jnp.dot(ds, q4, preferred_element_type=F32)
            dq_s[...] += lax.dot_general(kb, ds, TN_DIMS, preferred_element_type=F32)

        step(0, t_ctx)

        def lat(n, carry):
            step(pl.multiple_of(t_ctx + n * tkl, math.gcd(t_ctx, tkl)), tkl)
            return carry

        lax.fori_loop(0, jnp.where(qi < nq_ctx, 0, n_lat), lat, 0)
        dq4 = dq_s[...].T
        for g in range(GQA_GROUP):
            dq_ref[:, g * HEAD_DIM:(g + 1) * HEAD_DIM] = dq4[g * tq:(g + 1) * tq, :]

    qspec = pl.BlockSpec((tq, gw), lambda h, i: (i, h))
    kspec = pl.BlockSpec((r, HEAD_DIM), lambda h, i: (0, h))
    return pl.pallas_call(
        body, name=name, grid=(nkv, nq),
        in_specs=[qspec, kspec, kspec, pl.BlockSpec((tq, gw), lambda h, i: (i, att // gw + h)), qspec, qspec,
                  pl.BlockSpec((None, None, 1, nl), lambda h, i: (h, i, 0, 0))],
        out_specs=[qspec, qspec, kspec, kspec],
        out_shape=[jax.ShapeDtypeStruct((r, att), F32), jax.ShapeDtypeStruct((r, att), BF16),
                   jax.ShapeDtypeStruct((r, kv), F32), jax.ShapeDtypeStruct((r, kv), F32)],
        scratch_shapes=[pltpu.VMEM((HEAD_DIM, nl), F32)],
        compiler_params=_cp(2),
    )(qr, kr, vb, p, o, dog, lse)


def _rope_tables(t_ctx, s_lat):
    rows_n = s_lat // GRID_W
    row = jnp.repeat(jnp.arange(rows_n, dtype=F32), GRID_W)
    col = jnp.tile(jnp.arange(GRID_W, dtype=F32), rows_n)
    axis_dim = HEAD_DIM // 2
    inv_freq = ROPE_THETA ** (-jnp.arange(0, axis_dim, 2, dtype=F32) / axis_dim)
    ang_r = row[:, None] * inv_freq[None, :]
    ang_c = col[:, None] * inv_freq[None, :]
    cos = jnp.concatenate([jnp.cos(ang_r), jnp.cos(ang_r), jnp.cos(ang_c), jnp.cos(ang_c)], axis=1)
    sin = jnp.concatenate([-jnp.sin(ang_r), jnp.sin(ang_r), -jnp.sin(ang_c), jnp.sin(ang_c)], axis=1)
    cos = jnp.concatenate([jnp.ones((t_ctx, HEAD_DIM), F32), cos], axis=0)
    sin = jnp.concatenate([jnp.zeros((t_ctx, HEAD_DIM), F32), sin], axis=0)
    return cos, sin


def _pad_rows(a, rows):
    return jnp.pad(a, ((0, rows - a.shape[0]), (0, 0)))


def kernel(x, c, ctx, c_ctx, w_mod, b_mod, post_ln_g, post_ln_b, w_in_e, conv_a_w, conv_a_b, norm_a_g, norm_a_b, conv_b_w, w_out_e, w_in_o, q_norm_g, k_norm_g, w_out_o, loss_target, m_c_ctx, m_w_mod, m_b_mod, m_post_ln_g, m_post_ln_b, m_w_in_e, m_conv_a_w, m_conv_a_b, m_norm_a_g, m_norm_a_b, m_conv_b_w, m_w_out_e, m_w_in_o, m_q_norm_g, m_k_norm_g, m_w_out_o, v_c_ctx, v_w_mod, v_b_mod, v_post_ln_g, v_post_ln_b, v_w_in_e, v_conv_a_w, v_conv_a_b, v_norm_a_g, v_norm_a_b, v_conv_b_w, v_w_out_e, v_w_in_o, v_q_norm_g, v_k_norm_g, v_w_out_o):
    depth, d, mcols = w_mod.shape
    s_lat, t_ctx = x.shape[1], ctx.shape[1]
    n_even, n_odd = w_in_e.shape[0], w_in_o.shape[0]
    ka, kb = conv_a_w.shape[1], conv_b_w.shape[1]
    wch = conv_a_w.shape[2] * NDEV
    att = w_out_o.shape[1] * NDEV
    kv = (w_in_o.shape[2] * NDEV - 2 * att) // 2
    alpha = (2.0 * depth) ** 0.25
    me = 4 * lax.axis_index("x") + 2 * lax.axis_index("y") + lax.axis_index("c")

    c_all = _all_gather(_pad_rows(c, 8), "ag_c")[:, 0, :]
    c16 = jnp.concatenate([c_all, _pad_rows(c_ctx[None, :], 8)], axis=0)
    sc16 = _ew(_silu, BF16, "silu_c", c16)
    m_part = _mm(sc16, w_mod, "nn", F32, "mod_fwd", tm=16, tn=mcols, tk=d)
    m_all = _all_gather(m_part, "ag_mod")
    m_full = m_all.reshape(NDEV, 16, depth, mcols).transpose(2, 1, 0, 3).reshape(depth * 16, 3 * d)
    b16 = jnp.broadcast_to(b_mod[:, None, :], (depth, 16, 3 * d)).reshape(depth * 16, 3 * d)
    m_full = _ew(lambda a, b: a + b, F32, "mod_bias", m_full, b16).reshape(depth, 16, 3, d)
    modv = [jnp.stack([m_full[l, 8], lax.dynamic_index_in_dim(m_full[l], me, 0, keepdims=False)])
            for l in range(depth)]

    cw = jnp.concatenate([_pad_rows(conv_a_w[i], 32) for i in range(n_even)]
                         + [_pad_rows(conv_b_w[i], 8) for i in range(n_even)], axis=0)
    cw_all = _all_gather(cw, "ag_convw").transpose(1, 0, 2).reshape(cw.shape[0], wch)
    caw = [cw_all[32 * i:32 * i + ka] for i in range(n_even)]
    cbw = [cw_all[32 * n_even + 8 * i:32 * n_even + 8 * i + kb] for i in range(n_even)]

    def gathered(wsh, name):
        return _all_gather(_ew(lambda t: t, BF16, name + "_bf16", wsh), name)

    w_in_e_g = [gathered(w_in_e[i], f"ag_w_in_e{i}") for i in range(n_even)]
    w_out_e_g = [gathered(w_out_e[i], f"ag_w_out_e{i}").reshape(1, 2 * wch, d) for i in range(n_even)]
    w_in_o_g = [gathered(w_in_o[i], f"ag_w_in_o{i}") for i in range(n_odd)]
    w_out_o_g = [gathered(w_out_o[i], f"ag_w_out_o{i}").reshape(1, att, d) for i in range(n_odd)]

    cos, sin = _rope_tables(t_ctx, s_lat)
    lnp = [jnp.stack([post_ln_g[l], post_ln_b[l]]) for l in range(depth)]

    xb = jnp.concatenate([ctx[0], x[0]], axis=0)
    saved = []
    for l in range(depth):
        i = l // 2
        h = _modulate(xb, modv[l], t_ctx, f"modulate{l}")
        if l % 2 == 0:
            p = _mm(h, w_in_e_g[i], "nn", F32, f"in_proj{l}")
            cat, u1 = _conv_fwd(p, caw[i], conv_a_b[i][None], norm_a_g[i][None], norm_a_b[i][None], cbw[i],
                                t_ctx, f"conv_fwd{l}")
            y = _mm(cat, w_out_e_g[i], "nn", F32, f"out_proj{l}")
            saved.append((xb, h, p, cat, u1, y))
        else:
            p = _mm(h, w_in_o_g[i], "nn", F32, f"in_proj{l}")
            qr, kr, vb = _qk_fwd(p, cos, sin, q_norm_g[i][None], k_norm_g[i][None], att, kv, t_ctx, f"qk_fwd{l}")
            o, og, lse = _flash_fwd(qr, kr, vb, p, att, t_ctx, f"flash_fwd{l}")
            y = _mm(og, w_out_o_g[i], "nn", F32, f"out_proj{l}")
            saved.append((xb, h, p, qr, kr, vb, o, og, lse, y))
        xb = _postln_fwd(xb, y, modv[l], lnp[l], alpha, t_ctx, f"postln_fwd{l}")

    loss_blk, dxb = _loss_grad(xb, loss_target[0], t_ctx, "loss")
    loss = lax.psum(loss_blk[0, 0], ("x", "y", "c"))

    small = {}
    dmod = [None] * depth
    big = {}
    for l in reversed(range(depth)):
        i = l // 2
        sv = saved[l]
        x_in, h, p, y = sv[0], sv[1], sv[2], sv[-1]
        dy, dxa, s_ln = _postln_bwd(dxb, x_in, y, modv[l], lnp[l], alpha, t_ctx, f"postln_bwd{l}")
        small[f"ln{l}"] = s_ln
        if l % 2 == 0:
            cat, u1 = sv[3], sv[4]
            dcat = _mm(dy, w_out_e_g[i], "nt", F32, f"d_cat{l}")
            big[f"w_out_e{i}"] = _mm(cat, dy, "tn", BF16, f"d_w_out{l}", tm=1024, tk=768).reshape(
                NDEV, 2 * wch // NDEV, d)
            du1, dag, s_c1 = _conv_bwd1(dcat, p, u1, norm_a_g[i][None], norm_a_b[i][None], t_ctx, f"conv_bwd1_{l}")
            dp, dcaw, dcbw = _conv_bwd2(du1, dcat, p, dag, caw[i], cbw[i], t_ctx, f"conv_bwd2_{l}")
            small[f"c1_{i}"], small[f"caw{i}"], small[f"cbw{i}"] = s_c1, dcaw, dcbw
            w_in_g, wname = w_in_e_g[i], f"w_in_e{i}"
        else:
            qr, kr, vb, o, og, lse = sv[3:9]
            dog = _mm(dy, w_out_o_g[i], "nt", F32, f"d_og{l}")
            big[f"w_out_o{i}"] = _mm(og, dy, "tn", BF16, f"d_w_out{l}", tm=1024, tk=768).reshape(
                NDEV, att // NDEV, d)
            dqr, dgate, dkr, dvr = _flash_bwd(qr, kr, vb, p, o, dog, lse, att, kv, t_ctx, f"flash_bwd{l}")
            dq, dk, s_qk = _qk_bwd(p, dqr, dkr, cos, sin, q_norm_g[i][None], k_norm_g[i][None], att, kv, t_ctx,
                                   f"qk_bwd{l}")
            small[f"qk{i}"] = jnp.pad(s_qk, ((0, 0), (0, d - HEAD_DIM)))
            dp = jnp.concatenate([dq, dgate, dk, dvr.astype(BF16)], axis=1)
            w_in_g, wname = w_in_o_g[i], f"w_in_o{i}"
        dh = _mm(dp, w_in_g, "nt", F32, f"d_h{l}")
        big[wname] = _mm(h, dp, "tn", BF16, f"d_w_in{l}", tm=d, tk=768,
                         out_nd=NDEV)
        dxb, s_mod = _mod_bwd(dh, x_in, dxa, modv[l], t_ctx, f"mod_bwd{l}")
        dmod[l] = jnp.stack([jnp.stack([s_mod[0], s_mod[1], s_ln[2]]), jnp.stack([s_mod[2], s_mod[3], s_ln[3]])])
    grad_x = dxb[t_ctx:][None]

    dm_loc = jnp.stack(dmod).reshape(depth * 2, 3 * d)
    dm_all = _all_gather(dm_loc, "ag_dmod").reshape(NDEV, depth, 2, 3 * d)
    dm_ctx = _sum_lead(dm_all[:, :, 0, :], "sum_dmod_ctx")
    dm16 = jnp.concatenate([dm_all[:, :, 1, :].transpose(1, 0, 2), dm_ctx[:, None, :],
                            jnp.zeros((depth, 7, 3 * d), F32)], axis=1)
    g_b_mod = _sum_lead(dm16.transpose(1, 0, 2), "sum_b_mod")
    dm16_me = lax.dynamic_slice_in_dim(dm16.reshape(depth, 16, NDEV, mcols), me, 1, axis=2)
    dm16_me = dm16_me.reshape(depth, 16, mcols).transpose(1, 0, 2).reshape(16, depth * mcols)
    g_w_mod = _mm(sc16, dm16_me, "tn", F32, "mod_bwd_w", tm=d, tn=mcols, tk=16, out_nd=depth)
    dsc16 = _mm(dm16_me, w_mod, "nt", F32, "mod_bwd_c", tm=16, tn=d, tk=mcols)
    small["c_ctx"] = dsc16[8:16]

    names = sorted(small)
    offs, rows = {}, 0
    for nme in names:
        offs[nme] = rows
        rows += small[nme].shape[0]
    sm_all = _all_gather(jnp.concatenate([small[nme] for nme in names], axis=0), "ag_small")
    sm = _sum_lead(sm_all, "sum_small")

    def part(nme, lo, hi):
        return sm[offs[nme] + lo:offs[nme] + hi]

    g_post_ln_g = jnp.concatenate([part(f"ln{l}", 0, 1) for l in range(depth)], axis=0)
    g_post_ln_b = jnp.concatenate([part(f"ln{l}", 1, 2) for l in range(depth)], axis=0)
    g_norm_a_g = jnp.concatenate([part(f"c1_{i}", 0, 1) for i in range(n_even)], axis=0)
    g_norm_a_b = jnp.concatenate([part(f"c1_{i}", 1, 2) for i in range(n_even)], axis=0)
    g_conv_a_b = jnp.concatenate([part(f"c1_{i}", 2, 3) for i in range(n_even)], axis=0)
    g_q_norm_g = jnp.concatenate([part(f"qk{i}", 0, 1)[:, :HEAD_DIM] for i in range(n_odd)], axis=0)
    g_k_norm_g = jnp.concatenate([part(f"qk{i}", 1, 2)[:, :HEAD_DIM] for i in range(n_odd)], axis=0)
    wsh = wch // NDEV
    g_conv_a_w = jnp.stack([lax.dynamic_slice_in_dim(part(f"caw{i}", 0, ka), me * wsh, wsh, axis=1)
                            for i in range(n_even)])
    g_conv_b_w = jnp.stack([lax.dynamic_slice_in_dim(part(f"cbw{i}", 0, kb), me * wsh, wsh, axis=1)
                            for i in range(n_even)])
    g_c_ctx = _ew(lambda a, b: a * _dsilu(b), F32, "d_c_ctx", part("c_ctx", 0, 8), _pad_rows(c_ctx[None, :], 8))[0]

    def exchange(prefix, n, w, m, v):
        outs = []
        for i in range(n):
            recv = _all_to_all(big[f"{prefix}{i}"], f"a2a_{prefix}{i}")
            outs.append(_reduce_adamw(recv, w[i], m[i], v[i], f"adamw_{prefix}{i}"))
        return [jnp.stack([o[j] for o in outs]) for j in range(4)]

    r_w_in_e = exchange("w_in_e", n_even, w_in_e, m_w_in_e, v_w_in_e)
    r_w_out_e = exchange("w_out_e", n_even, w_out_e, m_w_out_e, v_w_out_e)
    r_w_in_o = exchange("w_in_o", n_odd, w_in_o, m_w_in_o, v_w_in_o)
    r_w_out_o = exchange("w_out_o", n_odd, w_out_o, m_w_out_o, v_w_out_o)

    grads = {
        "c_ctx": g_c_ctx, "w_mod": g_w_mod, "b_mod": g_b_mod, "post_ln_g": g_post_ln_g, "post_ln_b": g_post_ln_b,
        "conv_a_w": g_conv_a_w, "conv_a_b": g_conv_a_b, "norm_a_g": g_norm_a_g, "norm_a_b": g_norm_a_b,
        "conv_b_w": g_conv_b_w, "q_norm_g": g_q_norm_g, "k_norm_g": g_k_norm_g,
    }
    state = {
        "c_ctx": (c_ctx, m_c_ctx, v_c_ctx), "w_mod": (w_mod, m_w_mod, v_w_mod), "b_mod": (b_mod, m_b_mod, v_b_mod),
        "post_ln_g": (post_ln_g, m_post_ln_g, v_post_ln_g), "post_ln_b": (post_ln_b, m_post_ln_b, v_post_ln_b),
        "conv_a_w": (conv_a_w, m_conv_a_w, v_conv_a_w), "conv_a_b": (conv_a_b, m_conv_a_b, v_conv_a_b),
        "norm_a_g": (norm_a_g, m_norm_a_g, v_norm_a_g), "norm_a_b": (norm_a_b, m_norm_a_b, v_norm_a_b),
        "conv_b_w": (conv_b_w, m_conv_b_w, v_conv_b_w), "q_norm_g": (q_norm_g, m_q_norm_g, v_q_norm_g),
        "k_norm_g": (k_norm_g, m_k_norm_g, v_k_norm_g),
    }
    res = {"w_in_e": r_w_in_e, "w_out_e": r_w_out_e, "w_in_o": r_w_in_o, "w_out_o": r_w_out_o}
    for nme, g in grads.items():
        w, m, v = state[nme]
        res[nme] = [g] + _adamw(w, g, m, v, f"adamw_{nme}")
    order = ["c_ctx", "w_mod", "b_mod", "post_ln_g", "post_ln_b", "w_in_e", "conv_a_w", "conv_a_b", "norm_a_g",
             "norm_a_b", "conv_b_w", "w_out_e", "w_in_o", "q_norm_g", "k_norm_g", "w_out_o"]
    return (loss, grad_x, *[res[nme][0] for nme in order], *[res[nme][1] for nme in order],
            *[res[nme][2] for nme in order], *[res[nme][3] for nme in order])
```

```python
import functools
import math

import jax
import jax.numpy as jnp
from jax import lax
from jax.experimental import pallas as pl
from jax.experimental.pallas import tpu as pltpu

F32 = jnp.float32
BF16 = jnp.bfloat16

NDEV = 8
GRID_W = 64
HEAD_DIM = 128
GQA_GROUP = 4
ROPE_THETA = 10000.0
LN_EPS = 1e-5
RMS_EPS = 1e-6
ATTN_SCALE = HEAD_DIM ** -0.5
ADAM_LR = 0.001
ADAM_B1 = 0.9
ADAM_B2 = 0.999
ADAM_EPS = 1e-08
ADAM_WD = 0.01
ADAM_STEP = 10
HALO = 16
LANES = 128
VMEM_LIMIT = 56 * 1024 * 1024
MESH = pl.DeviceIdType.MESH


def _cp(n_axes):
    return pltpu.CompilerParams(dimension_semantics=("arbitrary",) * n_axes, vmem_limit_bytes=VMEM_LIMIT)


def _div(dim, pref, mult):
    t = min(pref, dim) // mult * mult
    while t >= mult:
        if dim % t == 0:
            return t
        t -= mult
    return dim


def _sigmoid(x):
    return 1.0 / (1.0 + jnp.exp(-x))


def _silu(x):
    return x * _sigmoid(x)


def _dsilu(x):
    s = _sigmoid(x)
    return s * (1.0 + x * (1.0 - s))


def _ln_norm(z):
    mu = jnp.mean(z, axis=-1, keepdims=True)
    zc = z - mu
    var = jnp.mean(zc * zc, axis=-1, keepdims=True)
    rstd = lax.rsqrt(var + LN_EPS)
    return zc * rstd, rstd


def _ln_bwd(dn, n, rstd):
    return rstd * (dn - jnp.mean(dn, axis=-1, keepdims=True) - n * jnp.mean(dn * n, axis=-1, keepdims=True))


def _partner(x):
    lane = lax.broadcasted_iota(jnp.int32, x.shape, 1)
    return jnp.where((lane % 64) < 32, pltpu.roll(x, 96, 1), pltpu.roll(x, 32, 1))


N_PEERS = NDEV - 1
HBM_SPEC = pl.BlockSpec(memory_space=pltpu.HBM)


def _side_plan(jobs):
    xs = [x for _, x in jobs]
    out_shape = [jax.ShapeDtypeStruct((NDEV,) + x.shape[-2:], x.dtype) for x in xs]
    sems = [pltpu.SemaphoreType.DMA((2 * N_PEERS + 1,)) for _ in jobs]
    return xs, [HBM_SPEC] * len(jobs), out_shape, [HBM_SPEC] * len(jobs), sems


def _side_copies(kind, x_ref, out_ref, sems):
    mx, my, mc = lax.axis_index("x"), lax.axis_index("y"), lax.axis_index("c")
    me = 4 * mx + 2 * my + mc
    own = x_ref.at[me] if kind == "exchange" else x_ref
    copies = [pltpu.make_async_copy(own, out_ref.at[me], sems.at[2 * N_PEERS])]
    for k in range(1, NDEV):
        px, py, pc = mx ^ ((k >> 2) & 1), my ^ ((k >> 1) & 1), mc ^ (k & 1)
        src = x_ref.at[4 * px + 2 * py + pc] if kind == "exchange" else x_ref
        copies.append(pltpu.make_async_remote_copy(
            src_ref=src, dst_ref=out_ref.at[me], send_sem=sems.at[k - 1], recv_sem=sems.at[N_PEERS + k - 1],
            device_id=(px, py, pc), device_id_type=MESH))
    return copies


def _side_run(jobs, in_refs, out_refs, sem_refs, first, last):
    if not jobs:
        return (lambda: None), (lambda: None)

    def start():
        @pl.when(first)
        def _():
            for (kind, _), x_ref, o_ref, sems in zip(jobs, in_refs, out_refs, sem_refs):
                for cp in _side_copies(kind, x_ref, o_ref, sems):
                    cp.start()

    def finish():
        @pl.when(last)
        def _():
            for (kind, _), x_ref, o_ref, sems in zip(jobs, in_refs, out_refs, sem_refs):
                for cp in _side_copies(kind, x_ref, o_ref, sems):
                    cp.wait()

    return start, finish


def _grid_ends(grid):
    first = last = None
    for ax, n in enumerate(grid):
        i = pl.program_id(ax)
        f, l = i == 0, i == n - 1
        first = f if first is None else jnp.logical_and(first, f)
        last = l if last is None else jnp.logical_and(last, l)
    return first, last


def _mm(a, b, mode, out_dtype, name, tm=768, tn=2048, tk=2048, out_nd=1, side=()):
    if mode == "nn":
        m, kdim = a.shape
        nd, _, ns = b.shape
        tm, tn, tk = _div(m, tm, 8), _div(ns, tn, 128), _div(kdim, tk, 128)
        nbs = ns // tn
        grid = (m // tm, nd * nbs, kdim // tk)
        a_spec = pl.BlockSpec((tm, tk), lambda i, j, k: (i, k))
        b_spec = pl.BlockSpec((None, tk, tn), lambda i, j, k: (j // nbs, k, j % nbs))
        o_spec = pl.BlockSpec((tm, tn), lambda i, j, k: (i, j))
        out_shape = (m, nd * ns)
        dims = (((1,), (0,)), ((), ()))
    elif mode == "nt":
        m, _ = a.shape
        nd, ko, ns = b.shape
        tm, tn, tk = _div(m, tm, 8), _div(ko, tn, 128), _div(ns, tk, 128)
        kbs = ns // tk
        grid = (m // tm, ko // tn, nd * kbs)
        a_spec = pl.BlockSpec((tm, tk), lambda i, j, k: (i, k))
        b_spec = pl.BlockSpec((None, tn, tk), lambda i, j, k: (k // kbs, j, k % kbs))
        o_spec = pl.BlockSpec((tm, tn), lambda i, j, k: (i, j))
        out_shape = (m, ko)
        dims = (((1,), (1,)), ((), ()))
    else:
        m, kdim = a.shape
        n = b.shape[1]
        ns = n // out_nd
        tm, tn, tk = _div(kdim, tm, 128), _div(ns, tn, 128), _div(m, tk, 16)
        nbs = ns // tn
        grid = (kdim // tm, out_nd * nbs, m // tk)
        a_spec = pl.BlockSpec((tk, tm), lambda i, j, k: (k, i))
        b_spec = pl.BlockSpec((tk, tn), lambda i, j, k: (k, j))
        o_spec = pl.BlockSpec((None, tm, tn), lambda i, j, k: (j // nbs, i, j % nbs))
        out_shape = (out_nd, kdim, ns)
        dims = (((0,), (0,)), ((), ()))
    nk = grid[2]
    nj = len(side)
    s_ins, s_in_specs, s_out_shape, s_out_specs, s_sems = _side_plan(side)

    def body(a_ref, b_ref, *rest):
        s_in, o_ref, s_out = rest[:nj], rest[nj], rest[nj + 1:2 * nj + 1]
        scratch = rest[2 * nj + 1:]
        acc_ref, s_sem = (None, scratch) if nk == 1 else (scratch[0], scratch[1:])
        start, finish = _side_run(side, s_in, s_out, s_sem, *_grid_ends(grid))
        start()
        part = lax.dot_general(a_ref[...].astype(BF16), b_ref[...].astype(BF16), dims, preferred_element_type=F32)
        if nk == 1:
            o_ref[...] = part.astype(o_ref.dtype)
        else:
            k = pl.program_id(2)

            @pl.when(k == 0)
            def _():
                acc_ref[...] = jnp.zeros_like(acc_ref)

            acc_ref[...] += part

            @pl.when(k == nk - 1)
            def _():
                o_ref[...] = acc_ref[...].astype(o_ref.dtype)
        finish()

    outs = pl.pallas_call(
        body, name=name, grid=grid, in_specs=[a_spec, b_spec] + s_in_specs, out_specs=[o_spec] + s_out_specs,
        out_shape=[jax.ShapeDtypeStruct(out_shape, out_dtype)] + s_out_shape,
        scratch_shapes=([] if nk == 1 else [pltpu.VMEM((tm, tn), F32)]) + s_sems, compiler_params=_cp(3),
    )(a, b, *s_ins)
    return outs if side else outs[0]


def _ew(fn, out_dtype, name, *xs):
    rows, cols = xs[0].shape
    tr = rows if rows <= 64 else _div(rows, 256, 16)

    def body(*refs):
        refs[-1][...] = fn(*[r[...] for r in refs[:-1]]).astype(out_dtype)

    spec = pl.BlockSpec((tr, cols), lambda i: (i, 0))
    return pl.pallas_call(
        body, name=name, grid=(rows // tr,), in_specs=[spec] * len(xs), out_specs=spec,
        out_shape=jax.ShapeDtypeStruct((rows, cols), out_dtype), compiler_params=_cp(1),
    )(*xs)


def _sum_lead(x, name):
    n, rows, cols = x.shape
    tr = _div(rows, 64, 8)

    def body(x_ref, o_ref):
        acc = x_ref[0]
        for s in range(1, n):
            acc = acc + x_ref[s]
        o_ref[...] = acc

    return pl.pallas_call(
        body, name=name, grid=(rows // tr,),
        in_specs=[pl.BlockSpec((n, tr, cols), lambda i: (0, i, 0))],
        out_specs=pl.BlockSpec((tr, cols), lambda i: (i, 0)),
        out_shape=jax.ShapeDtypeStruct((rows, cols), F32), compiler_params=_cp(1),
    )(x)


def _adam_math(w, g, m, v):
    m = ADAM_B1 * m + (1.0 - ADAM_B1) * g
    v = ADAM_B2 * v + (1.0 - ADAM_B2) * (g * g)
    m_hat = m / (1.0 - ADAM_B1 ** ADAM_STEP)
    v_hat = v / (1.0 - ADAM_B2 ** ADAM_STEP)
    delta = -ADAM_LR * (m_hat / (jnp.sqrt(v_hat) + ADAM_EPS) + ADAM_WD * w)
    return delta, m, v


def _adamw(w, g, m, v, name):
    shape = w.shape
    cols = shape[-1]
    w2, g2, m2, v2 = [t.reshape(-1, cols) for t in (w, g, m, v)]
    rows = w2.shape[0]
    tr = rows if rows <= 512 else _div(rows, 256, 8)

    def body(w_ref, g_ref, m_ref, v_ref, d_ref, nm_ref, nv_ref):
        d, nm, nv = _adam_math(w_ref[...], g_ref[...], m_ref[...], v_ref[...])
        d_ref[...] = d
        nm_ref[...] = nm
        nv_ref[...] = nv

    spec = pl.BlockSpec((tr, cols), lambda i: (i, 0))
    outs = pl.pallas_call(
        body, name=name, grid=(rows // tr,), in_specs=[spec] * 4, out_specs=[spec] * 3,
        out_shape=[jax.ShapeDtypeStruct((rows, cols), F32)] * 3, compiler_params=_cp(1),
    )(w2, g2, m2, v2)
    return [o.reshape(shape) for o in outs]


def _reduce_adamw(parts, w, m, v, name):
    shape = w.shape
    n, rows, cols = parts.shape
    w2, m2, v2 = [t.reshape(rows, cols) for t in (w, m, v)]
    tr = _div(rows, 128, 16)

    def body(p_ref, w_ref, m_ref, v_ref, g_ref, d_ref, nm_ref, nv_ref):
        g = p_ref[0].astype(F32)
        for s in range(1, n):
            g = g + p_ref[s].astype(F32)
        d, nm, nv = _adam_math(w_ref[...], g, m_ref[...], v_ref[...])
        g_ref[...] = g
        d_ref[...] = d
        nm_ref[...] = nm
        nv_ref[...] = nv

    spec = pl.BlockSpec((tr, cols), lambda i: (i, 0))
    outs = pl.pallas_call(
        body, name=name, grid=(rows // tr,),
        in_specs=[pl.BlockSpec((n, tr, cols), lambda i: (0, i, 0))] + [spec] * 3, out_specs=[spec] * 4,
        out_shape=[jax.ShapeDtypeStruct((rows, cols), F32)] * 4, compiler_params=_cp(1),
    )(parts, w2, m2, v2)
    return [o.reshape(shape) for o in outs]


def _all_gather(x, name):
    rows, cols = x.shape

    def body(x_ref, out_ref, send_sems, recv_sems, local_sem):
        mx, my, mc = lax.axis_index("x"), lax.axis_index("y"), lax.axis_index("c")
        me, sibling = (mx, my, mc), (mx, my, 1 - mc)
        chips = [(1 - mx, my), (mx, 1 - my), (1 - mx, 1 - my)]

        def slab(px, py, pc):
            return out_ref.at[4 * px + 2 * py + pc]

        def copy(k, block, to, src=None):
            return pltpu.make_async_remote_copy(
                src_ref=slab(*block) if src is None else src, dst_ref=slab(*block),
                send_sem=send_sems.at[k], recv_sem=recv_sems.at[k], device_id=to, device_id_type=MESH)

        mine = pltpu.make_async_copy(x_ref, slab(*me), local_sem)
        mine.start()
        first = [copy(0, me, sibling, src=x_ref)]
        first += [copy(1 + j, me, (*chip, mc), src=x_ref) for j, chip in enumerate(chips)]
        for cp in first:
            cp.start()
        passed = [copy(4 + j, (*chip, mc), sibling) for j, chip in enumerate(chips)]
        for j, chip in enumerate(chips):
            copy(1 + j, (*chip, mc), me).wait_recv()
            passed[j].start()
        copy(0, sibling, me).wait_recv()
        for j, chip in enumerate(chips):
            copy(4 + j, (*chip, 1 - mc), me).wait_recv()
        for cp in first + passed:
            cp.wait_send()
        mine.wait()

    return pl.pallas_call(
        body, name=name, out_shape=jax.ShapeDtypeStruct((NDEV, rows, cols), x.dtype),
        in_specs=[pl.BlockSpec(memory_space=pltpu.HBM)], out_specs=pl.BlockSpec(memory_space=pltpu.HBM),
        scratch_shapes=[pltpu.SemaphoreType.DMA((7,)), pltpu.SemaphoreType.DMA((7,)), pltpu.SemaphoreType.DMA],
    )(x)


def _row_tile(t_ctx, pref=256):
    return _div(t_ctx, pref, 8)


def _mod_spec(d, nctx):
    return pl.BlockSpec((None, 3, d), lambda i: (jnp.where(i >= nctx, 1, 0), 0, 0))


def _modulate(xb, modv, t_ctx, name):
    r, d = xb.shape
    tm = _row_tile(t_ctx)
    nctx = t_ctx // tm

    def body(x_ref, mv_ref, h_ref):
        h_ref[...] = (x_ref[...] * (1.0 + mv_ref[1:2, :]) + mv_ref[0:1, :]).astype(BF16)

    row = pl.BlockSpec((tm, d), lambda i: (i, 0))
    return pl.pallas_call(
        body, name=name, grid=(r // tm,), in_specs=[row, _mod_spec(d, nctx)], out_specs=row,
        out_shape=jax.ShapeDtypeStruct((r, d), BF16), compiler_params=_cp(1),
    )(xb, modv)


def _postln_fwd(xb, y, modv, lnp, alpha, t_ctx, name):
    r, d = xb.shape
    tm = _row_tile(t_ctx)
    nctx = t_ctx // tm

    def body(x_ref, y_ref, mv_ref, ln_ref, o_ref):
        n, _ = _ln_norm(alpha * x_ref[...] + mv_ref[2:3, :] * y_ref[...])
        o_ref[...] = n * ln_ref[0:1, :] + ln_ref[1:2, :]

    row = pl.BlockSpec((tm, d), lambda i: (i, 0))
    return pl.pallas_call(
        body, name=name, grid=(r // tm,),
        in_specs=[row, row, _mod_spec(d, nctx), pl.BlockSpec((2, d), lambda i: (0, 0))], out_specs=row,
        out_shape=jax.ShapeDtypeStruct((r, d), F32), compiler_params=_cp(1),
    )(xb, y, modv, lnp)


def _postln_bwd(dxn, xb, y, modv, lnp, alpha, t_ctx, name):
    r, d = xb.shape
    tm = _row_tile(t_ctx)
    nctx = t_ctx // tm

    def body(dxn_ref, x_ref, y_ref, mv_ref, ln_ref, dy_ref, dxa_ref, s_ref):
        i = pl.program_id(0)

        @pl.when(i == 0)
        def _():
            s_ref[...] = jnp.zeros_like(s_ref)

        yv = y_ref[...]
        gate = mv_ref[2:3, :]
        n, rstd = _ln_norm(alpha * x_ref[...] + gate * yv)
        dxn_v = dxn_ref[...]
        dz = _ln_bwd(dxn_v * ln_ref[0:1, :], n, rstd)
        dy_ref[...] = (gate * dz).astype(BF16)
        dxa_ref[...] = alpha * dz
        s_ref[0:1, :] += jnp.sum(dxn_v * n, axis=0, keepdims=True)
        s_ref[1:2, :] += jnp.sum(dxn_v, axis=0, keepdims=True)
        dgate = jnp.sum(dz * yv, axis=0, keepdims=True)
        is_ctx = i < nctx
        s_ref[2:3, :] += jnp.where(is_ctx, dgate, 0.0)
        s_ref[3:4, :] += jnp.where(is_ctx, 0.0, dgate)

    row = pl.BlockSpec((tm, d), lambda i: (i, 0))
    return pl.pallas_call(
        body, name=name, grid=(r // tm,),
        in_specs=[row, row, row, _mod_spec(d, nctx), pl.BlockSpec((2, d), lambda i: (0, 0))],
        out_specs=[row, row, pl.BlockSpec((8, d), lambda i: (0, 0))],
        out_shape=[jax.ShapeDtypeStruct((r, d), BF16), jax.ShapeDtypeStruct((r, d), F32),
                   jax.ShapeDtypeStruct((8, d), F32)],
        compiler_params=_cp(1),
    )(dxn, xb, y, modv, lnp)


def _mod_bwd(dh, xb, dxa, modv, t_ctx, name):
    r, d = xb.shape
    tm = _row_tile(t_ctx)
    nctx = t_ctx // tm

    def body(dh_ref, x_ref, dxa_ref, mv_ref, dx_ref, s_ref):
        i = pl.program_id(0)

        @pl.when(i == 0)
        def _():
            s_ref[...] = jnp.zeros_like(s_ref)

        dhv = dh_ref[...]
        dx_ref[...] = dxa_ref[...] + dhv * (1.0 + mv_ref[1:2, :])
        dshift = jnp.sum(dhv, axis=0, keepdims=True)
        dscale = jnp.sum(dhv * x_ref[...], axis=0, keepdims=True)
        is_ctx = i < nctx
        s_ref[0:1, :] += jnp.where(is_ctx, dshift, 0.0)
        s_ref[1:2, :] += jnp.where(is_ctx, dscale, 0.0)
        s_ref[2:3, :] += jnp.where(is_ctx, 0.0, dshift)
        s_ref[3:4, :] += jnp.where(is_ctx, 0.0, dscale)

    row = pl.BlockSpec((tm, d), lambda i: (i, 0))
    return pl.pallas_call(
        body, name=name, grid=(r // tm,), in_specs=[row, row, row, _mod_spec(d, nctx)],
        out_specs=[row, pl.BlockSpec((8, d), lambda i: (0, 0))],
        out_shape=[jax.ShapeDtypeStruct((r, d), F32), jax.ShapeDtypeStruct((8, d), F32)],
        compiler_params=_cp(1),
    )(dh, xb, dxa, modv)


def _loss_grad(xb, target, t_ctx, name):
    r, d = xb.shape
    tm = _row_tile(t_ctx)
    nctx = t_ctx // tm

    def body(x_ref, t_ref, l_ref, dx_ref):
        i = pl.program_id(0)

        @pl.when(i == 0)
        def _():
            l_ref[...] = jnp.zeros_like(l_ref)

        @pl.when(i < nctx)
        def _():
            dx_ref[...] = jnp.zeros_like(dx_ref)

        @pl.when(i >= nctx)
        def _():
            e = x_ref[...] - t_ref[...]
            dx_ref[...] = e / d
            l_ref[...] += 0.5 * jnp.sum(jnp.sum(e * e, axis=1, keepdims=True), axis=0, keepdims=True) / d

    row = pl.BlockSpec((tm, d), lambda i: (i, 0))
    return pl.pallas_call(
        body, name=name, grid=(r // tm,),
        in_specs=[row, pl.BlockSpec((tm, d), lambda i: (jnp.maximum(i - nctx, 0), 0))],
        out_specs=[pl.BlockSpec((8, 128), lambda i: (0, 0)), row],
        out_shape=[jax.ShapeDtypeStruct((8, 128), F32), jax.ShapeDtypeStruct((r, d), F32)],
        compiler_params=_cp(1),
    )(xb, target)


def _conv_specs(r, w, tm, cblk):
    hb = tm // HALO
    last = r // HALO - 1
    main = pl.BlockSpec((tm, w), lambda i: (i, cblk))
    top = pl.BlockSpec((HALO, w), lambda i: (jnp.maximum(i * hb - 1, 0), cblk))
    bot = pl.BlockSpec((HALO, w), lambda i: (jnp.minimum((i + 1) * hb, last), cblk))
    return [main, top, bot]


def _fill_pad(pad_ref, main, top, bot, top_ok, bot_ok, tm):
    pad_ref[0:HALO, :] = jnp.where(top_ok, top, 0.0)
    pad_ref[HALO:HALO + tm, :] = main
    pad_ref[HALO + tm:2 * HALO + tm, :] = jnp.where(bot_ok, bot, 0.0)


def _edges(i, nctx, nr):
    top_ok = jnp.logical_and(i != 0, i != nctx)
    bot_ok = jnp.logical_and(i != nctx - 1, i != nr - 1)
    return top_ok, bot_ok


def _rot_fill(rot_ref, pad_ref, n):
    rot_ref[0, 0:n, :] = pad_ref[0:n, :]
    for b in range(1, 8):
        rot_ref[b, 0:n, :] = pad_ref[pl.ds(b, n), :]


def _tap(rot_ref, off, tm):
    return rot_ref[off % 8, pl.ds(off - off % 8, tm), :]


def _rows8(x):
    return jnp.sum(x.reshape(x.shape[0] // 8, 8, x.shape[1]), axis=0)


def _conv_fwd(p, caw, cab, nag, nab, cbw, t_ctx, name):
    r = p.shape[0]
    w = p.shape[1] // 7
    ka, kb = caw.shape[0], cbw.shape[0]
    tm = _row_tile(t_ctx, 128)
    nctx, nr = t_ctx // tm, r // tm
    n = tm + 2 * HALO - 8

    def body(av, avt, avb, ag, agt, agb, agate, bx, bxt, bxb, bb, bc, bct, bcb, bgate,
             caw_ref, cab_ref, nag_ref, nab_ref, cbw_ref, cat_ref, u1_ref, pad, rot, v_s):
        top_ok, bot_ok = _edges(pl.program_id(0), nctx, nr)

        def chunk(c, carry):
            cols = pl.ds(pl.multiple_of(c * LANES, LANES), LANES)
            _fill_pad(pad, av[:, cols] * _sigmoid(ag[:, cols]), avt[:, cols] * _sigmoid(agt[:, cols]),
                      avb[:, cols] * _sigmoid(agb[:, cols]), top_ok, bot_ok, tm)
            _rot_fill(rot, pad, n)
            u1 = jnp.zeros((tm, LANES), F32) + cab_ref[:, cols]
            for k in range(ka):
                u1 = u1 + caw_ref[k:k + 1, cols] * _tap(rot, HALO - ka // 2 + k, tm)
            u1_ref[:, cols] = u1
            _fill_pad(pad, bc[:, cols] * bx[:, cols], bct[:, cols] * bxt[:, cols], bcb[:, cols] * bxb[:, cols],
                      top_ok, bot_ok, tm)
            v = jnp.zeros((tm, LANES), F32)
            for k in range(kb):
                v = v + cbw_ref[k:k + 1, cols] * pad[pl.ds(HALO - kb // 2 + k, tm), :]
            v_s[:, cols] = v
            return carry

        lax.fori_loop(0, w // LANES, chunk, 0)
        nrm, _ = _ln_norm(u1_ref[...])
        a_out = _silu(nrm * nag_ref[...] + nab_ref[...]) * _silu(agate[...])
        cat_ref[:, 0:w] = a_out.astype(BF16)
        cat_ref[:, w:2 * w] = (bb[...] * v_s[...] * _silu(bgate[...])).astype(BF16)

    def main(cblk):
        return pl.BlockSpec((tm, w), lambda i: (i, cblk))

    def whole(a):
        return pl.BlockSpec(a.shape, lambda i: (0, 0))

    in_specs = (_conv_specs(r, w, tm, 0) + _conv_specs(r, w, tm, 1) + [main(2)] + _conv_specs(r, w, tm, 3)
                + [main(4)] + _conv_specs(r, w, tm, 5) + [main(6)]
                + [whole(caw), whole(cab), whole(nag), whole(nab), whole(cbw)])
    return pl.pallas_call(
        body, name=name, grid=(nr,), in_specs=in_specs,
        out_specs=[pl.BlockSpec((tm, 2 * w), lambda i: (i, 0)), pl.BlockSpec((tm, w), lambda i: (i, 0))],
        out_shape=[jax.ShapeDtypeStruct((r, 2 * w), BF16), jax.ShapeDtypeStruct((r, w), F32)],
        scratch_shapes=[pltpu.VMEM((tm + 2 * HALO, LANES), F32), pltpu.VMEM((8, n, LANES), F32),
                        pltpu.VMEM((tm, w), F32)],
        compiler_params=_cp(1),
    )(*([p] * 15), caw, cab, nag, nab, cbw)


def _conv_bwd1(dcat, p, u1, nag, nab, t_ctx, name):
    r, w = u1.shape
    tm = _row_tile(t_ctx, 128)

    def body(da_ref, agate_ref, u1_ref, nag_ref, nab_ref, du1_ref, dag_ref, s_ref):
        @pl.when(pl.program_id(0) == 0)
        def _():
            s_ref[...] = jnp.zeros_like(s_ref)

        n, rstd = _ln_norm(u1_ref[...])
        g = nag_ref[...]
        u2 = n * g + nab_ref[...]
        da = da_ref[...]
        ag = agate_ref[...]
        dag_ref[...] = (da * _silu(u2) * _dsilu(ag)).astype(BF16)
        du2 = da * _silu(ag) * _dsilu(u2)
        du1 = _ln_bwd(du2 * g, n, rstd)
        du1_ref[...] = du1
        s_ref[0:1, :] += jnp.sum(du2 * n, axis=0, keepdims=True)
        s_ref[1:2, :] += jnp.sum(du2, axis=0, keepdims=True)
        s_ref[2:3, :] += jnp.sum(du1, axis=0, keepdims=True)

    def win(cblk):
        return pl.BlockSpec((tm, w), lambda i: (i, cblk))

    one = pl.BlockSpec((1, w), lambda i: (0, 0))
    return pl.pallas_call(
        body, name=name, grid=(r // tm,), in_specs=[win(0), win(2), win(0), one, one],
        out_specs=[win(0), win(0), pl.BlockSpec((8, w), lambda i: (0, 0))],
        out_shape=[jax.ShapeDtypeStruct((r, w), F32), jax.ShapeDtypeStruct((r, w), BF16),
                   jax.ShapeDtypeStruct((8, w), F32)],
        compiler_params=_cp(1),
    )(dcat, p, u1, nag, nab)


def _conv_bwd2(du1, dcat, p, dag, caw, cbw, t_ctx, name):
    r, w = du1.shape
    ka, kb = caw.shape[0], cbw.shape[0]
    tm = _row_tile(t_ctx, 128)
    nctx, nr = t_ctx // tm, r // tm
    n = tm + 2 * HALO - 8

    def body(du, dut, dub, av, avt, avb, ag, agt, agb, db, dbt, dbb, bx, bxt, bxb, bb, bbt, bbb,
             bc, bct, bcb, bg, bgt, bgb, dag_ref, caw_ref, cbw_ref, dp_ref, dcaw_ref, dcbw_ref,
             pad, pad2, rot_u, rot_d, acc_a, acc_b):
        i = pl.program_id(0)
        top_ok, bot_ok = _edges(i, nctx, nr)

        @pl.when(i == 0)
        def _():
            acc_a[...] = jnp.zeros_like(acc_a)
            acc_b[...] = jnp.zeros_like(acc_b)

        def chunk(c, carry):
            c0 = pl.multiple_of(c * LANES, LANES)
            cols = pl.ds(c0, LANES)

            def seg(s):
                return pl.ds(pl.multiple_of(s * w + c0, LANES), LANES)

            sg = _sigmoid(ag[:, cols])
            av_m = av[:, cols]
            _fill_pad(pad, av_m * sg, avt[:, cols] * _sigmoid(agt[:, cols]), avb[:, cols] * _sigmoid(agb[:, cols]),
                      top_ok, bot_ok, tm)
            _rot_fill(rot_u, pad, n)
            du_m = du[:, cols]
            _fill_pad(pad, du_m, dut[:, cols], dub[:, cols], top_ok, bot_ok, tm)
            _rot_fill(rot_d, pad, n)
            du0 = jnp.zeros((tm, LANES), F32)
            for k in range(ka):
                du0 = du0 + caw_ref[k:k + 1, cols] * _tap(rot_d, HALO + ka // 2 - k, tm)
                acc_a[k, :, cols] += _rows8(du_m * _tap(rot_u, HALO - ka // 2 + k, tm))
            dp_ref[:, seg(0)] = (du0 * sg).astype(BF16)
            dp_ref[:, seg(1)] = (du0 * av_m * sg * (1.0 - sg)).astype(BF16)
            dp_ref[:, seg(2)] = dag_ref[:, cols]

            bc_m, bx_m = bc[:, cols], bx[:, cols]
            _fill_pad(pad, bc_m * bx_m, bct[:, cols] * bxt[:, cols], bcb[:, cols] * bxb[:, cols],
                      top_ok, bot_ok, tm)
            v = jnp.zeros((tm, LANES), F32)
            for k in range(kb):
                v = v + cbw_ref[k:k + 1, cols] * pad[pl.ds(HALO - kb // 2 + k, tm), :]
            db_m, bb_m, bg_m = db[:, cols], bb[:, cols], bg[:, cols]
            sbg = _silu(bg_m)
            dv_m = db_m * bb_m * sbg
            _fill_pad(pad2, dv_m, dbt[:, cols] * bbt[:, cols] * _silu(bgt[:, cols]),
                      dbb[:, cols] * bbb[:, cols] * _silu(bgb[:, cols]), top_ok, bot_ok, tm)
            dw0 = jnp.zeros((tm, LANES), F32)
            for k in range(kb):
                dw0 = dw0 + cbw_ref[k:k + 1, cols] * pad2[pl.ds(HALO + kb // 2 - k, tm), :]
                acc_b[k, :, cols] += _rows8(dv_m * pad[pl.ds(HALO - kb // 2 + k, tm), :])
            dp_ref[:, seg(3)] = (dw0 * bc_m).astype(BF16)
            dp_ref[:, seg(4)] = (db_m * v * sbg).astype(BF16)
            dp_ref[:, seg(5)] = (dw0 * bx_m).astype(BF16)
            dp_ref[:, seg(6)] = (db_m * bb_m * v * _dsilu(bg_m)).astype(BF16)
            return carry

        lax.fori_loop(0, w // LANES, chunk, 0)

        @pl.when(i == nr - 1)
        def _():
            dcaw_ref[...] = jnp.zeros_like(dcaw_ref)
            dcbw_ref[...] = jnp.zeros_like(dcbw_ref)
            for k in range(ka):
                dcaw_ref[k:k + 1, :] = jnp.sum(acc_a[k], axis=0, keepdims=True)
            for k in range(kb):
                dcbw_ref[k:k + 1, :] = jnp.sum(acc_b[k], axis=0, keepdims=True)

    def whole(a):
        return pl.BlockSpec(a.shape, lambda i: (0, 0))

    in_specs = (_conv_specs(r, w, tm, 0) + _conv_specs(r, w, tm, 0) + _conv_specs(r, w, tm, 1)
                + _conv_specs(r, w, tm, 1) + _conv_specs(r, w, tm, 3) + _conv_specs(r, w, tm, 4)
                + _conv_specs(r, w, tm, 5) + _conv_specs(r, w, tm, 6)
                + [pl.BlockSpec((tm, w), lambda i: (i, 0)), whole(caw), whole(cbw)])
    pad_t = pltpu.VMEM((tm + 2 * HALO, LANES), F32)
    rot_t = pltpu.VMEM((8, n, LANES), F32)
    return pl.pallas_call(
        body, name=name, grid=(nr,), in_specs=in_specs,
        out_specs=[pl.BlockSpec((tm, 7 * w), lambda i: (i, 0)), pl.BlockSpec((32, w), lambda i: (0, 0)),
                   pl.BlockSpec((8, w), lambda i: (0, 0))],
        out_shape=[jax.ShapeDtypeStruct((r, 7 * w), BF16), jax.ShapeDtypeStruct((32, w), F32),
                   jax.ShapeDtypeStruct((8, w), F32)],
        scratch_shapes=[pad_t, pad_t, rot_t, rot_t, pltpu.VMEM((32, 8, w), F32), pltpu.VMEM((8, 8, w), F32)],
        compiler_params=_cp(1),
    )(*([du1] * 3), *([p] * 6), *([dcat] * 3), *([p] * 12), dag, caw, cbw)


def _rms(xh):
    r = lax.rsqrt(jnp.mean(xh * xh, axis=-1, keepdims=True) + RMS_EPS)
    return xh * r, r


def _qk_fwd(p, cos, sin, qg, kg, att, kv, t_ctx, name):
    r = p.shape[0]
    tm = _row_tile(t_ctx)

    def body(q_ref, k_ref, v_ref, cos_ref, sin_ref, qg_ref, kg_ref, qr_ref, kr_ref, vb_ref):
        cs, sn = cos_ref[...], sin_ref[...]
        for src, g_ref, dst, nh in ((q_ref, qg_ref, qr_ref, att // HEAD_DIM), (k_ref, kg_ref, kr_ref, kv // HEAD_DIM)):
            for h in range(nh):
                cols = slice(h * HEAD_DIM, (h + 1) * HEAD_DIM)
                n, _ = _rms(src[:, cols])
                n = n * g_ref[...]
                dst[:, cols] = (n * cs + _partner(n) * sn).astype(BF16)
        vb_ref[...] = v_ref[...].astype(BF16)

    def rows(width, cblk):
        return pl.BlockSpec((tm, width), lambda i: (i, cblk))

    one = pl.BlockSpec((1, HEAD_DIM), lambda i: (0, 0))
    return pl.pallas_call(
        body, name=name, grid=(r // tm,),
        in_specs=[rows(att, 0), rows(kv, 2 * att // kv), rows(kv, 2 * att // kv + 1),
                  rows(HEAD_DIM, 0), rows(HEAD_DIM, 0), one, one],
        out_specs=[rows(att, 0), rows(kv, 0), rows(kv, 0)],
        out_shape=[jax.ShapeDtypeStruct((r, att), BF16), jax.ShapeDtypeStruct((r, kv), BF16),
                   jax.ShapeDtypeStruct((r, kv), BF16)],
        compiler_params=_cp(1),
    )(p, p, p, cos, sin, qg, kg)


def _qk_bwd(p, dqr, dkr, cos, sin, qg, kg, att, kv, t_ctx, name):
    r = p.shape[0]
    tm = _row_tile(t_ctx)

    def body(q_ref, k_ref, dqr_ref, dkr_ref, cos_ref, sin_ref, qg_ref, kg_ref, dq_ref, dk_ref, s_ref):
        @pl.when(pl.program_id(0) == 0)
        def _():
            s_ref[...] = jnp.zeros_like(s_ref)

        cs, sn = cos_ref[...], sin_ref[...]
        for row, (src, dsrc, g_ref, dst, nh) in enumerate((
                (q_ref, dqr_ref, qg_ref, dq_ref, att // HEAD_DIM), (k_ref, dkr_ref, kg_ref, dk_ref, kv // HEAD_DIM))):
            dg = jnp.zeros((1, HEAD_DIM), F32)
            for h in range(nh):
                cols = slice(h * HEAD_DIM, (h + 1) * HEAD_DIM)
                n0, rr = _rms(src[:, cols])
                d = dsrc[:, cols]
                dng = d * cs + _partner(d * sn)
                dg = dg + jnp.sum(dng * n0, axis=0, keepdims=True)
                dn0 = dng * g_ref[...]
                dst[:, cols] = (rr * (dn0 - n0 * jnp.mean(dn0 * n0, axis=-1, keepdims=True))).astype(BF16)
            s_ref[row:row + 1, :] += dg

    def rows(width, cblk):
        return pl.BlockSpec((tm, width), lambda i: (i, cblk))

    one = pl.BlockSpec((1, HEAD_DIM), lambda i: (0, 0))
    return pl.pallas_call(
        body, name=name, grid=(r // tm,),
        in_specs=[rows(att, 0), rows(kv, 2 * att // kv), rows(att, 0), rows(kv, 0),
                  rows(HEAD_DIM, 0), rows(HEAD_DIM, 0), one, one],
        out_specs=[rows(att, 0), rows(kv, 0), pl.BlockSpec((8, HEAD_DIM), lambda i: (0, 0))],
        out_shape=[jax.ShapeDtypeStruct((r, att), BF16), jax.ShapeDtypeStruct((r, kv), BF16),
                   jax.ShapeDtypeStruct((8, HEAD_DIM), F32)],
        compiler_params=_cp(1),
    )(p, p, dqr, dkr, cos, sin, qg, kg)


def _stack_heads(x, tq):
    return jnp.concatenate([x[:, g * HEAD_DIM:(g + 1) * HEAD_DIM] for g in range(GQA_GROUP)], axis=0)


def _attn_tiles(t_ctx, s_lat, tkl_pref):
    return _div(t_ctx, 256, 8), _div(s_lat, tkl_pref, 8)


NT_DIMS = (((1,), (1,)), ((), ()))
TN_DIMS = (((0,), (0,)), ((), ()))
ATTN_SCALE_LOG2 = ATTN_SCALE * math.log2(math.e)


def _flash_fwd(qr, kr, vb, p, att, t_ctx, name, side=()):
    r = qr.shape[0]
    s_lat = r - t_ctx
    gw = GQA_GROUP * HEAD_DIM
    nkv = att // gw
    tq = _div(t_ctx, 256, 8)
    tkl = _div(s_lat // 2, 1024, 8)
    nq, nq_ctx, n_lat = r // tq, t_ctx // tq, s_lat // tkl
    nl = GQA_GROUP * tq
    align = math.gcd(t_ctx, tkl)
    nj = len(side)
    s_ins, s_in_specs, s_out_shape, s_out_specs, s_sems = _side_plan(side)

    def body(q_ref, k_ref, v_ref, g_ref, *rest):
        s_in, (o_ref, og_ref, lse_ref), s_out = rest[:nj], rest[nj:nj + 3], rest[nj + 3:2 * nj + 3]
        (m_s, l_s, acc_s, st_a, st_b), s_sem = rest[2 * nj + 3:2 * nj + 8], rest[2 * nj + 8:]
        start, finish = _side_run(side, s_in, s_out, s_sem, *_grid_ends((nkv, nq)))
        start()
        qi = pl.program_id(1)
        q4 = _stack_heads(q_ref[...], tq)
        m_s[...] = jnp.full_like(m_s, -1e30)
        l_s[...] = jnp.zeros_like(l_s)
        acc_s[...] = jnp.zeros_like(acc_s)

        def scores(off, size):
            return lax.dot_general(k_ref[pl.ds(off, size), :], q4, NT_DIMS,
                                   preferred_element_type=F32) * ATTN_SCALE_LOG2

        def update(st, off, size):
            m_old = m_s[...]
            m_new = jnp.maximum(m_old, jnp.max(st, axis=0, keepdims=True))
            pe = jnp.exp2(st - m_new)
            a = jnp.exp2(m_old - m_new)
            l_s[...] = a * l_s[...] + jnp.sum(pe, axis=0, keepdims=True)
            acc_s[...] = a * acc_s[...] + lax.dot_general(v_ref[pl.ds(off, size), :], pe.astype(BF16), TN_DIMS,
                                                          preferred_element_type=F32)
            m_s[...] = m_new

        def lat_off(n):
            return pl.multiple_of(t_ctx + jnp.minimum(n, n_lat - 1) * tkl, align)

        is_lat = qi >= nq_ctx

        @pl.when(is_lat)
        def _():
            st_a[...] = scores(lat_off(0), tkl)

        update(scores(0, t_ctx), 0, t_ctx)

        def pair(j, carry):
            n = 2 * j
            st_b[...] = scores(lat_off(n + 1), tkl)
            update(st_a[...], lat_off(n), tkl)
            st_a[...] = scores(lat_off(n + 2), tkl)
            update(st_b[...], lat_off(n + 1), tkl)
            return carry

        lax.fori_loop(0, jnp.where(is_lat, n_lat // 2, 0), pair, 0)
        o4 = (acc_s[...] / l_s[...]).T
        lse_ref[...] = m_s[...] + jnp.log2(l_s[...])
        sg = _silu(g_ref[...])
        for g in range(GQA_GROUP):
            cols = slice(g * HEAD_DIM, (g + 1) * HEAD_DIM)
            og = o4[g * tq:(g + 1) * tq, :]
            o_ref[:, cols] = og
            og_ref[:, cols] = (og * sg[:, cols]).astype(BF16)
        finish()

    qspec = pl.BlockSpec((tq, gw), lambda h, i: (i, h))
    kspec = pl.BlockSpec((r, HEAD_DIM), lambda h, i: (0, h))
    return pl.pallas_call(
        body, name=name, grid=(nkv, nq),
        in_specs=[qspec, kspec, kspec, pl.BlockSpec((tq, gw), lambda h, i: (i, att // gw + h))] + s_in_specs,
        out_specs=[qspec, qspec, pl.BlockSpec((None, None, 1, nl), lambda h, i: (h, i, 0, 0))] + s_out_specs,
        out_shape=[jax.ShapeDtypeStruct((r, att), F32), jax.ShapeDtypeStruct((r, att), BF16),
                   jax.ShapeDtypeStruct((nkv, nq, 1, nl), F32)] + s_out_shape,
        scratch_shapes=[pltpu.VMEM((1, nl), F32), pltpu.VMEM((1, nl), F32), pltpu.VMEM((HEAD_DIM, nl), F32),
                        pltpu.VMEM((tkl, nl), F32), pltpu.VMEM((tkl, nl), F32)] + s_sems,
        compiler_params=_cp(2),
    )(qr, kr, vb, p, *s_ins)


def _flash_bwd(qr, kr, vb, p, o, dog, lse, att, kv, t_ctx, name):
    r = qr.shape[0]
    s_lat = r - t_ctx
    gw = GQA_GROUP * HEAD_DIM
    nkv = att // gw
    tq, tkl = _attn_tiles(t_ctx, s_lat, 1024)
    nq, nq_ctx, n_lat = r // tq, t_ctx // tq, s_lat // tkl
    nl = GQA_GROUP * tq

    def body(q_ref, k_ref, v_ref, g_ref, o_ref, dog_ref, lse_ref, dq_ref, dgate_ref, dk_ref, dv_ref, dq_s):
        qi = pl.program_id(1)

        @pl.when(qi == 0)
        def _():
            dk_ref[...] = jnp.zeros_like(dk_ref)
            dv_ref[...] = jnp.zeros_like(dv_ref)

        gate, ov, dogv = g_ref[...], o_ref[...], dog_ref[...]
        dgate_ref[...] = (dogv * ov * _dsilu(gate)).astype(BF16)
        do = dogv * _silu(gate)
        do4 = _stack_heads(do, tq)
        delta = jnp.sum((do4 * _stack_heads(ov, tq)).T, axis=0, keepdims=True)
        do4 = do4.astype(BF16)
        q4 = _stack_heads(q_ref[...], tq)
        lse_v = lse_ref[...]
        dq_s[...] = jnp.zeros_like(dq_s)

        def step(off, size):
            kb = k_ref[pl.ds(off, size), :]
            vv = v_ref[pl.ds(off, size), :]
            st = lax.dot_general(kb, q4, NT_DIMS, preferred_element_type=F32) * ATTN_SCALE_LOG2
            pe = jnp.exp2(st - lse_v)
            dp = lax.dot_general(vv, do4, NT_DIMS, preferred_element_type=F32)
            ds = (pe * (dp - delta) * ATTN_SCALE).astype(BF16)
            dv_ref[pl.ds(off, size), :] += jnp.dot(pe.astype(BF16), do4, preferred_element_type=F32)
            dk_ref[pl.ds(off, size), :] += jnp.dot(ds, q4, preferred_element_type=F32)
            dq_s[...] += lax.dot_general(kb, ds, TN_DIMS, preferred_element_type=F32)

        step(0, t_ctx)

        def lat(n, carry):
            step(pl.multiple_of(t_ctx + n * tkl, math.gcd(t_ctx, tkl)), tkl)
            return carry

        lax.fori_loop(0, jnp.where(qi < nq_ctx, 0, n_lat), lat, 0)
        dq4 = dq_s[...].T
        for g in range(GQA_GROUP):
            dq_ref[:, g * HEAD_DIM:(g + 1) * HEAD_DIM] = dq4[g * tq:(g + 1) * tq, :]

    qspec = pl.BlockSpec((tq, gw), lambda h, i: (i, h))
    kspec = pl.BlockSpec((r, HEAD_DIM), lambda h, i: (0, h))
    return pl.pallas_call(
        body, name=name, grid=(nkv, nq),
        in_specs=[qspec, kspec, kspec, pl.BlockSpec((tq, gw), lambda h, i: (i, att // gw + h)), qspec, qspec,
                  pl.BlockSpec((None, None, 1, nl), lambda h, i: (h, i, 0, 0))],
        out_specs=[qspec, qspec, kspec, kspec],
        out_shape=[jax.ShapeDtypeStruct((r, att), F32), jax.ShapeDtypeStruct((r, att), BF16),
                   jax.ShapeDtypeStruct((r, kv), F32), jax.ShapeDtypeStruct((r, kv), F32)],
        scratch_shapes=[pltpu.VMEM((HEAD_DIM, nl), F32)],
        compiler_params=_cp(2),
    )(qr, kr, vb, p, o, dog, lse)


def _rope_tables(t_ctx, s_lat):
    rows_n = s_lat // GRID_W
    row = jnp.repeat(jnp.arange(rows_n, dtype=F32), GRID_W)
    col = jnp.tile(jnp.arange(GRID_W, dtype=F32), rows_n)
    axis_dim = HEAD_DIM // 2
    inv_freq = ROPE_THETA ** (-jnp.arange(0, axis_dim, 2, dtype=F32) / axis_dim)
    ang_r = row[:, None] * inv_freq[None, :]
    ang_c = col[:, None] * inv_freq[None, :]
    cos = jnp.concatenate([jnp.cos(ang_r), jnp.cos(ang_r), jnp.cos(ang_c), jnp.cos(ang_c)], axis=1)
    sin = jnp.concatenate([-jnp.sin(ang_r), jnp.sin(ang_r), -jnp.sin(ang_c), jnp.sin(ang_c)], axis=1)
    cos = jnp.concatenate([jnp.ones((t_ctx, HEAD_DIM), F32), cos], axis=0)
    sin = jnp.concatenate([jnp.zeros((t_ctx, HEAD_DIM), F32), sin], axis=0)
    return cos, sin


def _pad_rows(a, rows):
    return jnp.pad(a, ((0, rows - a.shape[0]), (0, 0)))


def kernel(x, c, ctx, c_ctx, w_mod, b_mod, post_ln_g, post_ln_b, w_in_e, conv_a_w, conv_a_b, norm_a_g, norm_a_b, conv_b_w, w_out_e, w_in_o, q_norm_g, k_norm_g, w_out_o, loss_target, m_c_ctx, m_w_mod, m_b_mod, m_post_ln_g, m_post_ln_b, m_w_in_e, m_conv_a_w, m_conv_a_b, m_norm_a_g, m_norm_a_b, m_conv_b_w, m_w_out_e, m_w_in_o, m_q_norm_g, m_k_norm_g, m_w_out_o, v_c_ctx, v_w_mod, v_b_mod, v_post_ln_g, v_post_ln_b, v_w_in_e, v_conv_a_w, v_conv_a_b, v_norm_a_g, v_norm_a_b, v_conv_b_w, v_w_out_e, v_w_in_o, v_q_norm_g, v_k_norm_g, v_w_out_o):
    depth, d, mcols = w_mod.shape
    s_lat, t_ctx = x.shape[1], ctx.shape[1]
    n_even, n_odd = w_in_e.shape[0], w_in_o.shape[0]
    ka, kb = conv_a_w.shape[1], conv_b_w.shape[1]
    wch = conv_a_w.shape[2] * NDEV
    att = w_out_o.shape[1] * NDEV
    kv = (w_in_o.shape[2] * NDEV - 2 * att) // 2
    alpha = (2.0 * depth) ** 0.25
    me = 4 * lax.axis_index("x") + 2 * lax.axis_index("y") + lax.axis_index("c")

    c_all = _all_gather(_pad_rows(c, 8), "ag_c")[:, 0, :]
    c16 = jnp.concatenate([c_all, _pad_rows(c_ctx[None, :], 8)], axis=0)
    sc16 = _ew(_silu, BF16, "silu_c", c16)
    m_part = _mm(sc16, w_mod, "nn", F32, "mod_fwd", tm=16, tn=mcols, tk=d)
    m_all = _all_gather(m_part, "ag_mod")
    m_full = m_all.reshape(NDEV, 16, depth, mcols).transpose(2, 1, 0, 3).reshape(depth * 16, 3 * d)
    b16 = jnp.broadcast_to(b_mod[:, None, :], (depth, 16, 3 * d)).reshape(depth * 16, 3 * d)
    m_full = _ew(lambda a, b: a + b, F32, "mod_bias", m_full, b16).reshape(depth, 16, 3, d)
    modv = [jnp.stack([m_full[l, 8], lax.dynamic_index_in_dim(m_full[l], me, 0, keepdims=False)])
            for l in range(depth)]

    cw = jnp.concatenate([_pad_rows(conv_a_w[i], 32) for i in range(n_even)]
                         + [_pad_rows(conv_b_w[i], 8) for i in range(n_even)], axis=0)
    cw_all = _all_gather(cw, "ag_convw").transpose(1, 0, 2).reshape(cw.shape[0], wch)
    caw = [cw_all[32 * i:32 * i + ka] for i in range(n_even)]
    cbw = [cw_all[32 * n_even + 8 * i:32 * n_even + 8 * i + kb] for i in range(n_even)]

    sh_in = [_ew(lambda t: t, BF16, f"w_in{l}_bf16", (w_in_e if l % 2 == 0 else w_in_o)[l // 2]) for l in range(depth)]
    sh_out = [_ew(lambda t: t, BF16, f"w_out{l}_bf16", (w_out_e if l % 2 == 0 else w_out_o)[l // 2])
              for l in range(depth)]
    g_in = [None] * depth
    g_out = [None] * depth
    g_in[0] = _all_gather(sh_in[0], "ag_w_in0")

    def wanted(keys):
        return [(kind, l) for kind, l in keys if l < depth and (g_in if kind == "in" else g_out)[l] is None]

    def gather_jobs(keys):
        return [("gather", (sh_in if kind == "in" else sh_out)[l]) for kind, l in keys]

    def hosted(res, keys):
        if not keys:
            return res
        n_own = len(res) - len(keys)
        for (kind, l), g in zip(keys, res[n_own:]):
            if kind == "in":
                g_in[l] = g
            else:
                g_out[l] = g.reshape(1, NDEV * g.shape[1], d)
        return res[0] if n_own == 1 else res[:n_own]

    cos, sin = _rope_tables(t_ctx, s_lat)
    lnp = [jnp.stack([post_ln_g[l], post_ln_b[l]]) for l in range(depth)]

    xb = jnp.concatenate([ctx[0], x[0]], axis=0)
    saved = []
    for l in range(depth):
        i = l // 2
        h = _modulate(xb, modv[l], t_ctx, f"modulate{l}")
        if l % 2 == 0:
            keys = wanted([("out", l), ("in", l + 1)])
            p = hosted(_mm(h, g_in[l], "nn", F32, f"in_proj{l}", side=gather_jobs(keys)), keys)
            cat, u1 = _conv_fwd(p, caw[i], conv_a_b[i][None], norm_a_g[i][None], norm_a_b[i][None], cbw[i],
                                t_ctx, f"conv_fwd{l}")
            keys = wanted([("out", l + 1)])
            y = hosted(_mm(cat, g_out[l], "nn", F32, f"out_proj{l}", side=gather_jobs(keys)), keys)
            saved.append((xb, h, p, cat, u1, y))
        else:
            keys = wanted([("out", l)])
            p = hosted(_mm(h, g_in[l], "nn", F32, f"in_proj{l}", side=gather_jobs(keys)), keys)
            qr, kr, vb = _qk_fwd(p, cos, sin, q_norm_g[i][None], k_norm_g[i][None], att, kv, t_ctx, f"qk_fwd{l}")
            keys = wanted([("in", l + 1), ("out", l + 1)])
            o, og, lse = hosted(_flash_fwd(qr, kr, vb, p, att, t_ctx, f"flash_fwd{l}", side=gather_jobs(keys)), keys)
            y = _mm(og, g_out[l], "nn", F32, f"out_proj{l}")
            saved.append((xb, h, p, qr, kr, vb, o, og, lse, y))
        xb = _postln_fwd(xb, y, modv[l], lnp[l], alpha, t_ctx, f"postln_fwd{l}")

    loss_blk, dxb = _loss_grad(xb, loss_target[0], t_ctx, "loss")
    loss = lax.psum(loss_blk[0, 0], ("x", "y", "c"))

    small = {}
    dmod = [None] * depth
    recv_in, recv_out = [None] * depth, [None] * depth
    for l in reversed(range(depth)):
        i = l // 2
        sv = saved[l]
        x_in, h, p, y = sv[0], sv[1], sv[2], sv[-1]
        dy, dxa, s_ln = _postln_bwd(dxb, x_in, y, modv[l], lnp[l], alpha, t_ctx, f"postln_bwd{l}")
        small[f"ln{l}"] = s_ln
        mixed = sv[3] if l % 2 == 0 else sv[7]
        k_out = g_out[l].shape[1]
        part = _mm(mixed, dy, "tn", BF16, f"d_w_out{l}", tm=1024, tk=768).reshape(NDEV, k_out // NDEV, d)
        dmixed, recv_out[l] = _mm(dy, g_out[l], "nt", F32, f"d_mixed{l}", side=[("exchange", part)])
        if l % 2 == 0:
            u1 = sv[4]
            du1, dag, s_c1 = _conv_bwd1(dmixed, p, u1, norm_a_g[i][None], norm_a_b[i][None], t_ctx,
                                        f"conv_bwd1_{l}")
            dp, dcaw, dcbw = _conv_bwd2(du1, dmixed, p, dag, caw[i], cbw[i], t_ctx, f"conv_bwd2_{l}")
            small[f"c1_{i}"], small[f"caw{i}"], small[f"cbw{i}"] = s_c1, dcaw, dcbw
        else:
            qr, kr, vb, o, _, lse = sv[3:9]
            dqr, dgate, dkr, dvr = _flash_bwd(qr, kr, vb, p, o, dmixed, lse, att, kv, t_ctx, f"flash_bwd{l}")
            dq, dk, s_qk = _qk_bwd(p, dqr, dkr, cos, sin, q_norm_g[i][None], k_norm_g[i][None], att, kv, t_ctx,
                                   f"qk_bwd{l}")
            small[f"qk{i}"] = jnp.pad(s_qk, ((0, 0), (0, d - HEAD_DIM)))
            dp = jnp.concatenate([dq, dgate, dk, dvr.astype(BF16)], axis=1)
        part = _mm(h, dp, "tn", BF16, f"d_w_in{l}", tm=d, tk=768, out_nd=NDEV)
        dh, recv_in[l] = _mm(dp, g_in[l], "nt", F32, f"d_h{l}", side=[("exchange", part)])
        dxb, s_mod = _mod_bwd(dh, x_in, dxa, modv[l], t_ctx, f"mod_bwd{l}")
        dmod[l] = jnp.stack([jnp.stack([s_mod[0], s_mod[1], s_ln[2]]), jnp.stack([s_mod[2], s_mod[3], s_ln[3]])])
    grad_x = dxb[t_ctx:][None]

    dm_loc = jnp.stack(dmod).reshape(depth * 2, 3 * d)
    dm_all = _all_gather(dm_loc, "ag_dmod").reshape(NDEV, depth, 2, 3 * d)
    dm_ctx = _sum_lead(dm_all[:, :, 0, :], "sum_dmod_ctx")
    dm16 = jnp.concatenate([dm_all[:, :, 1, :].transpose(1, 0, 2), dm_ctx[:, None, :],
                            jnp.zeros((depth, 7, 3 * d), F32)], axis=1)
    g_b_mod = _sum_lead(dm16.transpose(1, 0, 2), "sum_b_mod")
    dm16_me = lax.dynamic_slice_in_dim(dm16.reshape(depth, 16, NDEV, mcols), me, 1, axis=2)
    dm16_me = dm16_me.reshape(depth, 16, mcols).transpose(1, 0, 2).reshape(16, depth * mcols)
    g_w_mod = _mm(sc16, dm16_me, "tn", F32, "mod_bwd_w", tm=d, tn=mcols, tk=16, out_nd=depth)
    dsc16 = _mm(dm16_me, w_mod, "nt", F32, "mod_bwd_c", tm=16, tn=d, tk=mcols)
    small["c_ctx"] = dsc16[8:16]

    names = sorted(small)
    offs, rows = {}, 0
    for nme in names:
        offs[nme] = rows
        rows += small[nme].shape[0]
    sm_all = _all_gather(jnp.concatenate([small[nme] for nme in names], axis=0), "ag_small")
    sm = _sum_lead(sm_all, "sum_small")

    def part(nme, lo, hi):
        return sm[offs[nme] + lo:offs[nme] + hi]

    g_post_ln_g = jnp.concatenate([part(f"ln{l}", 0, 1) for l in range(depth)], axis=0)
    g_post_ln_b = jnp.concatenate([part(f"ln{l}", 1, 2) for l in range(depth)], axis=0)
    g_norm_a_g = jnp.concatenate([part(f"c1_{i}", 0, 1) for i in range(n_even)], axis=0)
    g_norm_a_b = jnp.concatenate([part(f"c1_{i}", 1, 2) for i in range(n_even)], axis=0)
    g_conv_a_b = jnp.concatenate([part(f"c1_{i}", 2, 3) for i in range(n_even)], axis=0)
    g_q_norm_g = jnp.concatenate([part(f"qk{i}", 0, 1)[:, :HEAD_DIM] for i in range(n_odd)], axis=0)
    g_k_norm_g = jnp.concatenate([part(f"qk{i}", 1, 2)[:, :HEAD_DIM] for i in range(n_odd)], axis=0)
    wsh = wch // NDEV
    g_conv_a_w = jnp.stack([lax.dynamic_slice_in_dim(part(f"caw{i}", 0, ka), me * wsh, wsh, axis=1)
                            for i in range(n_even)])
    g_conv_b_w = jnp.stack([lax.dynamic_slice_in_dim(part(f"cbw{i}", 0, kb), me * wsh, wsh, axis=1)
                            for i in range(n_even)])
    g_c_ctx = _ew(lambda a, b: a * _dsilu(b), F32, "d_c_ctx", part("c_ctx", 0, 8), _pad_rows(c_ctx[None, :], 8))[0]

    def updated(recv, layers, prefix, w, m, v):
        outs = [_reduce_adamw(recv[l], w[j], m[j], v[j], f"adamw_{prefix}{j}") for j, l in enumerate(layers)]
        return [jnp.stack([o[t] for o in outs]) for t in range(4)]

    evens, odds = range(0, depth, 2), range(1, depth, 2)
    r_w_in_e = updated(recv_in, evens, "w_in_e", w_in_e, m_w_in_e, v_w_in_e)
    r_w_out_e = updated(recv_out, evens, "w_out_e", w_out_e, m_w_out_e, v_w_out_e)
    r_w_in_o = updated(recv_in, odds, "w_in_o", w_in_o, m_w_in_o, v_w_in_o)
    r_w_out_o = updated(recv_out, odds, "w_out_o", w_out_o, m_w_out_o, v_w_out_o)

    grads = {
        "c_ctx": g_c_ctx, "w_mod": g_w_mod, "b_mod": g_b_mod, "post_ln_g": g_post_ln_g, "post_ln_b": g_post_ln_b,
        "conv_a_w": g_conv_a_w, "conv_a_b": g_conv_a_b, "norm_a_g": g_norm_a_g, "norm_a_b": g_norm_a_b,
        "conv_b_w": g_conv_b_w, "q_norm_g": g_q_norm_g, "k_norm_g": g_k_norm_g,
    }
    state = {
        "c_ctx": (c_ctx, m_c_ctx, v_c_ctx), "w_mod": (w_mod, m_w_mod, v_w_mod), "b_mod": (b_mod, m_b_mod, v_b_mod),
        "post_ln_g": (post_ln_g, m_post_ln_g, v_post_ln_g), "post_ln_b": (post_ln_b, m_post_ln_b, v_post_ln_b),
        "conv_a_w": (conv_a_w, m_conv_a_w, v_conv_a_w), "conv_a_b": (conv_a_b, m_conv_a_b, v_conv_a_b),
        "norm_a_g": (norm_a_g, m_norm_a_g, v_norm_a_g), "norm_a_b": (norm_a_b, m_norm_a_b, v_norm_a_b),
        "conv_b_w": (conv_b_w, m_conv_b_w, v_conv_b_w), "q_norm_g": (q_norm_g, m_q_norm_g, v_q_norm_g),
        "k_norm_g": (k_norm_g, m_k_norm_g, v_k_norm_g),
    }
    res = {"w_in_e": r_w_in_e, "w_out_e": r_w_out_e, "w_in_o": r_w_in_o, "w_out_o": r_w_out_o}
    for nme, g in grads.items():
        w, m, v = state[nme]
        res[nme] = [g] + _adamw(w, g, m, v, f"adamw_{nme}")
    order = ["c_ctx", "w_mod", "b_mod", "post_ln_g", "post_ln_b", "w_in_e", "conv_a_w", "conv_a_b", "norm_a_g",
             "norm_a_b", "conv_b_w", "w_out_e", "w_in_o", "q_norm_g", "k_norm_g", "w_out_o"]
    return (loss, grad_x, *[res[nme][0] for nme in order], *[res[nme][1] for nme in order],
            *[res[nme][2] for nme in order], *[res[nme][3] for nme in order])
```

```python
import functools
import math

import jax
import jax.numpy as jnp
from jax import lax
from jax.experimental import pallas as pl
from jax.experimental.pallas import tpu as pltpu

F32 = jnp.float32
BF16 = jnp.bfloat16

NDEV = 8
GRID_W = 64
HEAD_DIM = 128
GQA_GROUP = 4
ROPE_THETA = 10000.0
LN_EPS = 1e-5
RMS_EPS = 1e-6
ATTN_SCALE = HEAD_DIM ** -0.5
ADAM_LR = 0.001
ADAM_B1 = 0.9
ADAM_B2 = 0.999
ADAM_EPS = 1e-08
ADAM_WD = 0.01
ADAM_STEP = 10
HALO = 16
LANES = 128
VMEM_LIMIT = 56 * 1024 * 1024
MESH = pl.DeviceIdType.MESH


def _cp(n_axes):
    return pltpu.CompilerParams(dimension_semantics=("arbitrary",) * n_axes, vmem_limit_bytes=VMEM_LIMIT)


def _div(dim, pref, mult):
    t = min(pref, dim) // mult * mult
    while t >= mult:
        if dim % t == 0:
            return t
        t -= mult
    return dim


def _sigmoid(x):
    return 1.0 / (1.0 + jnp.exp(-x))


def _silu(x):
    return x * _sigmoid(x)


def _dsilu(x):
    s = _sigmoid(x)
    return s * (1.0 + x * (1.0 - s))


def _ln_norm(z):
    mu = jnp.mean(z, axis=-1, keepdims=True)
    zc = z - mu
    var = jnp.mean(zc * zc, axis=-1, keepdims=True)
    rstd = lax.rsqrt(var + LN_EPS)
    return zc * rstd, rstd


def _ln_bwd(dn, n, rstd):
    return rstd * (dn - jnp.mean(dn, axis=-1, keepdims=True) - n * jnp.mean(dn * n, axis=-1, keepdims=True))


def _partner(x):
    lane = lax.broadcasted_iota(jnp.int32, x.shape, 1)
    return jnp.where((lane % 64) < 32, pltpu.roll(x, 96, 1), pltpu.roll(x, 32, 1))


N_PEERS = NDEV - 1
HBM_SPEC = pl.BlockSpec(memory_space=pltpu.HBM)


def _side_plan(jobs):
    xs = [x for _, x in jobs]
    out_shape = [jax.ShapeDtypeStruct((NDEV,) + x.shape[-2:], x.dtype) for x in xs]
    sems = [pltpu.SemaphoreType.DMA((2 * N_PEERS + 1,)) for _ in jobs]
    return xs, [HBM_SPEC] * len(jobs), out_shape, [HBM_SPEC] * len(jobs), sems


def _side_copies(kind, x_ref, out_ref, sems):
    mx, my, mc = lax.axis_index("x"), lax.axis_index("y"), lax.axis_index("c")
    me = 4 * mx + 2 * my + mc
    own = x_ref.at[me] if kind == "exchange" else x_ref
    copies = [pltpu.make_async_copy(own, out_ref.at[me], sems.at[2 * N_PEERS])]
    for k in range(1, NDEV):
        px, py, pc = mx ^ ((k >> 2) & 1), my ^ ((k >> 1) & 1), mc ^ (k & 1)
        src = x_ref.at[4 * px + 2 * py + pc] if kind == "exchange" else x_ref
        copies.append(pltpu.make_async_remote_copy(
            src_ref=src, dst_ref=out_ref.at[me], send_sem=sems.at[k - 1], recv_sem=sems.at[N_PEERS + k - 1],
            device_id=(px, py, pc), device_id_type=MESH))
    return copies


def _side_run(jobs, in_refs, out_refs, sem_refs, first, last):
    if not jobs:
        return (lambda: None), (lambda: None)

    def start():
        @pl.when(first)
        def _():
            for (kind, _), x_ref, o_ref, sems in zip(jobs, in_refs, out_refs, sem_refs):
                for cp in _side_copies(kind, x_ref, o_ref, sems):
                    cp.start()

    def finish():
        @pl.when(last)
        def _():
            for (kind, _), x_ref, o_ref, sems in zip(jobs, in_refs, out_refs, sem_refs):
                for cp in _side_copies(kind, x_ref, o_ref, sems):
                    cp.wait()

    return start, finish


def _grid_ends(grid):
    first = last = None
    for ax, n in enumerate(grid):
        i = pl.program_id(ax)
        f, l = i == 0, i == n - 1
        first = f if first is None else jnp.logical_and(first, f)
        last = l if last is None else jnp.logical_and(last, l)
    return first, last


def _mm(a, b, mode, out_dtype, name, tm=768, tn=2048, tk=2048, out_nd=1, side=(), mod_bwd=None):
    if mode == "nn":
        m, kdim = a.shape
        nd, _, ns = b.shape
        tm, tn, tk = _div(m, tm, 8), _div(ns, tn, 128), _div(kdim, tk, 128)
        nbs = ns // tn
        grid = (m // tm, nd * nbs, kdim // tk)
        a_spec = pl.BlockSpec((tm, tk), lambda i, j, k: (i, k))
        b_spec = pl.BlockSpec((None, tk, tn), lambda i, j, k: (j // nbs, k, j % nbs))
        o_spec = pl.BlockSpec((tm, tn), lambda i, j, k: (i, j))
        out_shape = (m, nd * ns)
        dims = (((1,), (0,)), ((), ()))
    elif mode == "nt":
        m, _ = a.shape
        nd, ko, ns = b.shape
        tm, tn, tk = _div(m, tm, 8), _div(ko, tn, 128), _div(ns, tk, 128)
        kbs = ns // tk
        grid = (m // tm, ko // tn, nd * kbs)
        a_spec = pl.BlockSpec((tm, tk), lambda i, j, k: (i, k))
        b_spec = pl.BlockSpec((None, tn, tk), lambda i, j, k: (k // kbs, j, k % kbs))
        o_spec = pl.BlockSpec((tm, tn), lambda i, j, k: (i, j))
        out_shape = (m, ko)
        dims = (((1,), (1,)), ((), ()))
    else:
        m, kdim = a.shape
        n = b.shape[1]
        ns = n // out_nd
        tm, tn, tk = _div(kdim, tm, 128), _div(ns, tn, 128), _div(m, tk, 16)
        nbs = ns // tn
        grid = (kdim // tm, out_nd * nbs, m // tk)
        a_spec = pl.BlockSpec((tk, tm), lambda i, j, k: (k, i))
        b_spec = pl.BlockSpec((tk, tn), lambda i, j, k: (k, j))
        o_spec = pl.BlockSpec((None, tm, tn), lambda i, j, k: (j // nbs, i, j % nbs))
        out_shape = (out_nd, kdim, ns)
        dims = (((0,), (0,)), ((), ()))
    nk = grid[2]
    nj = len(side)
    s_ins, s_in_specs, s_out_shape, s_out_specs, s_sems = _side_plan(side)
    e_ins, e_in_specs, e_out_shape, e_out_specs = [], [], [], []
    if mod_bwd is not None:
        assert mode == "nt" and grid[1] == 1
        x_in, dxa, modv, t_ctx = mod_bwd
        e_ins = [x_in, dxa, modv]
        e_in_specs = [pl.BlockSpec((tm, tn), lambda i, j, k: (i, 0)), pl.BlockSpec((tm, tn), lambda i, j, k: (i, 0)),
                      pl.BlockSpec(modv.shape, lambda i, j, k: (0, 0, 0))]
        e_out_shape = [jax.ShapeDtypeStruct((8, tn), F32)]
        e_out_specs = [pl.BlockSpec((8, tn), lambda i, j, k: (0, 0))]
    ne, neo = len(e_ins), len(e_out_shape)

    def body(a_ref, b_ref, *rest):
        e_in, s_in = rest[:ne], rest[ne:ne + nj]
        o_ref, e_out, s_out = rest[ne + nj], rest[ne + nj + 1:ne + nj + 1 + neo], rest[ne + nj + 1 + neo:ne + 2 * nj + 1 + neo]
        scratch = rest[ne + 2 * nj + 1 + neo:]
        acc_ref, s_sem = (None, scratch) if nk == 1 else (scratch[0], scratch[1:])
        first, last = _grid_ends(grid)
        start, finish = _side_run(side, s_in, s_out, s_sem, first, last)
        start()

        def prod():
            return lax.dot_general(a_ref[...].astype(BF16), b_ref[...].astype(BF16), dims,
                                   preferred_element_type=F32)

        def emit(val):
            if mod_bwd is None:
                o_ref[...] = val.astype(o_ref.dtype)
                return
            x_ref, dxa_ref, mv_ref = e_in
            rows = pl.program_id(0) * tm + lax.broadcasted_iota(jnp.int32, (tm, 1), 0)
            is_ctx = rows < t_ctx
            o_ref[...] = dxa_ref[...] + val * (1.0 + jnp.where(is_ctx, mv_ref[0, 1:2, :], mv_ref[1, 1:2, :]))
            dsc = val * x_ref[...]
            for row, (t, keep_ctx) in enumerate(((val, True), (dsc, True), (val, False), (dsc, False))):
                e_out[0][row:row + 1, :] += jnp.sum(jnp.where(is_ctx == keep_ctx, t, 0.0), axis=0, keepdims=True)

        if mod_bwd is not None:
            @pl.when(first)
            def _():
                e_out[0][...] = jnp.zeros_like(e_out[0])

        if nk == 1:
            emit(prod())
        else:
            k = pl.program_id(2)

            @pl.when(k == 0)
            def _():
                acc_ref[...] = jnp.zeros_like(acc_ref)

            acc_ref[...] += prod()

            @pl.when(k == nk - 1)
            def _():
                emit(acc_ref[...])
        finish()

    outs = pl.pallas_call(
        body, name=name, grid=grid, in_specs=[a_spec, b_spec] + e_in_specs + s_in_specs,
        out_specs=[o_spec] + e_out_specs + s_out_specs,
        out_shape=[jax.ShapeDtypeStruct(out_shape, out_dtype)] + e_out_shape + s_out_shape,
        scratch_shapes=([] if nk == 1 else [pltpu.VMEM((tm, tn), F32)]) + s_sems, compiler_params=_cp(3),
    )(a, b, *e_ins, *s_ins)
    return outs if len(outs) > 1 else outs[0]


def _ew(fn, out_dtype, name, *xs):
    rows, cols = xs[0].shape
    tr = rows if rows <= 64 else _div(rows, 256, 16)

    def body(*refs):
        refs[-1][...] = fn(*[r[...] for r in refs[:-1]]).astype(out_dtype)

    spec = pl.BlockSpec((tr, cols), lambda i: (i, 0))
    return pl.pallas_call(
        body, name=name, grid=(rows // tr,), in_specs=[spec] * len(xs), out_specs=spec,
        out_shape=jax.ShapeDtypeStruct((rows, cols), out_dtype), compiler_params=_cp(1),
    )(*xs)


def _sum_lead(x, name):
    n, rows, cols = x.shape
    tr = _div(rows, 64, 8)

    def body(x_ref, o_ref):
        acc = x_ref[0]
        for s in range(1, n):
            acc = acc + x_ref[s]
        o_ref[...] = acc

    return pl.pallas_call(
        body, name=name, grid=(rows // tr,),
        in_specs=[pl.BlockSpec((n, tr, cols), lambda i: (0, i, 0))],
        out_specs=pl.BlockSpec((tr, cols), lambda i: (i, 0)),
        out_shape=jax.ShapeDtypeStruct((rows, cols), F32), compiler_params=_cp(1),
    )(x)


def _adam_math(w, g, m, v):
    m = ADAM_B1 * m + (1.0 - ADAM_B1) * g
    v = ADAM_B2 * v + (1.0 - ADAM_B2) * (g * g)
    m_hat = m / (1.0 - ADAM_B1 ** ADAM_STEP)
    v_hat = v / (1.0 - ADAM_B2 ** ADAM_STEP)
    delta = -ADAM_LR * (m_hat / (jnp.sqrt(v_hat) + ADAM_EPS) + ADAM_WD * w)
    return delta, m, v


def _adamw(w, g, m, v, name):
    shape = w.shape
    cols = shape[-1]
    w2, g2, m2, v2 = [t.reshape(-1, cols) for t in (w, g, m, v)]
    rows = w2.shape[0]
    tr = rows if rows <= 512 else _div(rows, 256, 8)

    def body(w_ref, g_ref, m_ref, v_ref, d_ref, nm_ref, nv_ref):
        d, nm, nv = _adam_math(w_ref[...], g_ref[...], m_ref[...], v_ref[...])
        d_ref[...] = d
        nm_ref[...] = nm
        nv_ref[...] = nv

    spec = pl.BlockSpec((tr, cols), lambda i: (i, 0))
    outs = pl.pallas_call(
        body, name=name, grid=(rows // tr,), in_specs=[spec] * 4, out_specs=[spec] * 3,
        out_shape=[jax.ShapeDtypeStruct((rows, cols), F32)] * 3, compiler_params=_cp(1),
    )(w2, g2, m2, v2)
    return [o.reshape(shape) for o in outs]


def _reduce_adamw(parts, w, m, v, name):
    shape = w.shape
    n, rows, cols = parts.shape
    w2, m2, v2 = [t.reshape(rows, cols) for t in (w, m, v)]
    tr = _div(rows, 128, 16)

    def body(p_ref, w_ref, m_ref, v_ref, g_ref, d_ref, nm_ref, nv_ref):
        g = p_ref[0].astype(F32)
        for s in range(1, n):
            g = g + p_ref[s].astype(F32)
        d, nm, nv = _adam_math(w_ref[...], g, m_ref[...], v_ref[...])
        g_ref[...] = g
        d_ref[...] = d
        nm_ref[...] = nm
        nv_ref[...] = nv

    spec = pl.BlockSpec((tr, cols), lambda i: (i, 0))
    outs = pl.pallas_call(
        body, name=name, grid=(rows // tr,),
        in_specs=[pl.BlockSpec((n, tr, cols), lambda i: (0, i, 0))] + [spec] * 3, out_specs=[spec] * 4,
        out_shape=[jax.ShapeDtypeStruct((rows, cols), F32)] * 4, compiler_params=_cp(1),
    )(parts, w2, m2, v2)
    return [o.reshape(shape) for o in outs]


def _all_gather(x, name):
    rows, cols = x.shape

    def body(x_ref, out_ref, send_sems, recv_sems, local_sem):
        mx, my, mc = lax.axis_index("x"), lax.axis_index("y"), lax.axis_index("c")
        me, sibling = (mx, my, mc), (mx, my, 1 - mc)
        chips = [(1 - mx, my), (mx, 1 - my), (1 - mx, 1 - my)]

        def slab(px, py, pc):
            return out_ref.at[4 * px + 2 * py + pc]

        def copy(k, block, to, src=None):
            return pltpu.make_async_remote_copy(
                src_ref=slab(*block) if src is None else src, dst_ref=slab(*block),
                send_sem=send_sems.at[k], recv_sem=recv_sems.at[k], device_id=to, device_id_type=MESH)

        mine = pltpu.make_async_copy(x_ref, slab(*me), local_sem)
        mine.start()
        first = [copy(0, me, sibling, src=x_ref)]
        first += [copy(1 + j, me, (*chip, mc), src=x_ref) for j, chip in enumerate(chips)]
        for cp in first:
            cp.start()
        passed = [copy(4 + j, (*chip, mc), sibling) for j, chip in enumerate(chips)]
        for j, chip in enumerate(chips):
            copy(1 + j, (*chip, mc), me).wait_recv()
            passed[j].start()
        copy(0, sibling, me).wait_recv()
        for j, chip in enumerate(chips):
            copy(4 + j, (*chip, 1 - mc), me).wait_recv()
        for cp in first + passed:
            cp.wait_send()
        mine.wait()

    return pl.pallas_call(
        body, name=name, out_shape=jax.ShapeDtypeStruct((NDEV, rows, cols), x.dtype),
        in_specs=[pl.BlockSpec(memory_space=pltpu.HBM)], out_specs=pl.BlockSpec(memory_space=pltpu.HBM),
        scratch_shapes=[pltpu.SemaphoreType.DMA((7,)), pltpu.SemaphoreType.DMA((7,)), pltpu.SemaphoreType.DMA],
    )(x)


def _row_tile(t_ctx, pref=256):
    return _div(t_ctx, pref, 8)


def _mod_spec(d, nctx):
    return pl.BlockSpec((None, 3, d), lambda i: (jnp.where(i >= nctx, 1, 0), 0, 0))


def _modulate(xb, modv, t_ctx, name):
    r, d = xb.shape
    tm = _row_tile(t_ctx)
    nctx = t_ctx // tm

    def body(x_ref, mv_ref, h_ref):
        h_ref[...] = (x_ref[...] * (1.0 + mv_ref[1:2, :]) + mv_ref[0:1, :]).astype(BF16)

    row = pl.BlockSpec((tm, d), lambda i: (i, 0))
    return pl.pallas_call(
        body, name=name, grid=(r // tm,), in_specs=[row, _mod_spec(d, nctx)], out_specs=row,
        out_shape=jax.ShapeDtypeStruct((r, d), BF16), compiler_params=_cp(1),
    )(xb, modv)


def _postln_fwd(xb, y, modv, lnp, alpha, next_modv, t_ctx, name):
    r, d = xb.shape
    tm = _row_tile(t_ctx)
    nctx = t_ctx // tm

    def body(x_ref, y_ref, mv_ref, ln_ref, nmv_ref, o_ref, h_ref):
        n, _ = _ln_norm(alpha * x_ref[...] + mv_ref[2:3, :] * y_ref[...])
        xo = n * ln_ref[0:1, :] + ln_ref[1:2, :]
        o_ref[...] = xo
        h_ref[...] = (xo * (1.0 + nmv_ref[1:2, :]) + nmv_ref[0:1, :]).astype(BF16)

    row = pl.BlockSpec((tm, d), lambda i: (i, 0))
    return pl.pallas_call(
        body, name=name, grid=(r // tm,),
        in_specs=[row, row, _mod_spec(d, nctx), pl.BlockSpec((2, d), lambda i: (0, 0)), _mod_spec(d, nctx)],
        out_specs=[row, row],
        out_shape=[jax.ShapeDtypeStruct((r, d), F32), jax.ShapeDtypeStruct((r, d), BF16)], compiler_params=_cp(1),
    )(xb, y, modv, lnp, next_modv)


def _postln_bwd(dxn, xb, y, modv, lnp, alpha, t_ctx, name, from_loss=False):
    r, d = xb.shape
    tm = _row_tile(t_ctx)
    nctx = t_ctx // tm

    def body(dxn_ref, x_ref, y_ref, mv_ref, ln_ref, *outs):
        dy_ref, dxa_ref, s_ref = outs[-3:]
        i = pl.program_id(0)
        is_ctx = i < nctx

        @pl.when(i == 0)
        def _():
            s_ref[...] = jnp.zeros_like(s_ref)
            if from_loss:
                outs[0][...] = jnp.zeros_like(outs[0])

        yv = y_ref[...]
        gate = mv_ref[2:3, :]
        n, rstd = _ln_norm(alpha * x_ref[...] + gate * yv)
        if from_loss:
            e = jnp.where(is_ctx, 0.0, n * ln_ref[0:1, :] + ln_ref[1:2, :] - dxn_ref[...])
            outs[0][...] += 0.5 * jnp.sum(jnp.sum(e * e, axis=1, keepdims=True), axis=0, keepdims=True) / d
            dxn_v = e / d
        else:
            dxn_v = dxn_ref[...]
        dz = _ln_bwd(dxn_v * ln_ref[0:1, :], n, rstd)
        dy_ref[...] = (gate * dz).astype(BF16)
        dxa_ref[...] = alpha * dz
        s_ref[0:1, :] += jnp.sum(dxn_v * n, axis=0, keepdims=True)
        s_ref[1:2, :] += jnp.sum(dxn_v, axis=0, keepdims=True)
        dgate = jnp.sum(dz * yv, axis=0, keepdims=True)
        s_ref[2:3, :] += jnp.where(is_ctx, dgate, 0.0)
        s_ref[3:4, :] += jnp.where(is_ctx, 0.0, dgate)

    row = pl.BlockSpec((tm, d), lambda i: (i, 0))
    first = pl.BlockSpec((tm, d), lambda i: (jnp.maximum(i - nctx, 0), 0)) if from_loss else row
    loss_spec = [pl.BlockSpec((8, 128), lambda i: (0, 0))] if from_loss else []
    loss_shape = [jax.ShapeDtypeStruct((8, 128), F32)] if from_loss else []
    return pl.pallas_call(
        body, name=name, grid=(r // tm,),
        in_specs=[first, row, row, _mod_spec(d, nctx), pl.BlockSpec((2, d), lambda i: (0, 0))],
        out_specs=loss_spec + [row, row, pl.BlockSpec((8, d), lambda i: (0, 0))],
        out_shape=loss_shape + [jax.ShapeDtypeStruct((r, d), BF16), jax.ShapeDtypeStruct((r, d), F32),
                                jax.ShapeDtypeStruct((8, d), F32)],
        compiler_params=_cp(1),
    )(dxn, xb, y, modv, lnp)


def _conv_specs(r, w, tm, cblk):
    hb = tm // HALO
    last = r // HALO - 1
    main = pl.BlockSpec((tm, w), lambda i: (i, cblk))
    top = pl.BlockSpec((HALO, w), lambda i: (jnp.maximum(i * hb - 1, 0), cblk))
    bot = pl.BlockSpec((HALO, w), lambda i: (jnp.minimum((i + 1) * hb, last), cblk))
    return [main, top, bot]


def _fill_pad(pad_ref, main, top, bot, top_ok, bot_ok, tm):
    pad_ref[0:HALO, :] = jnp.where(top_ok, top, 0.0)
    pad_ref[HALO:HALO + tm, :] = main
    pad_ref[HALO + tm:2 * HALO + tm, :] = jnp.where(bot_ok, bot, 0.0)


def _edges(i, nctx, nr):
    top_ok = jnp.logical_and(i != 0, i != nctx)
    bot_ok = jnp.logical_and(i != nctx - 1, i != nr - 1)
    return top_ok, bot_ok


def _rot_fill(rot_ref, pad_ref, n):
    rot_ref[0, 0:n, :] = pad_ref[0:n, :]
    for b in range(1, 8):
        rot_ref[b, 0:n, :] = pad_ref[pl.ds(b, n), :]


def _tap(rot_ref, off, tm):
    return rot_ref[off % 8, pl.ds(off - off % 8, tm), :]


def _rows8(x):
    return jnp.sum(x.reshape(x.shape[0] // 8, 8, x.shape[1]), axis=0)


def _conv_fwd(p, caw, cab, nag, nab, cbw, t_ctx, name):
    r = p.shape[0]
    w = p.shape[1] // 7
    ka, kb = caw.shape[0], cbw.shape[0]
    tm = _row_tile(t_ctx, 128)
    nctx, nr = t_ctx // tm, r // tm
    n = tm + 2 * HALO - 8

    def body(av, avt, avb, ag, agt, agb, agate, bx, bxt, bxb, bb, bc, bct, bcb, bgate,
             caw_ref, cab_ref, nag_ref, nab_ref, cbw_ref, cat_ref, u1_ref, pad, rot, v_s):
        top_ok, bot_ok = _edges(pl.program_id(0), nctx, nr)

        def chunk(c, carry):
            cols = pl.ds(pl.multiple_of(c * LANES, LANES), LANES)
            _fill_pad(pad, av[:, cols] * _sigmoid(ag[:, cols]), avt[:, cols] * _sigmoid(agt[:, cols]),
                      avb[:, cols] * _sigmoid(agb[:, cols]), top_ok, bot_ok, tm)
            _rot_fill(rot, pad, n)
            u1 = jnp.zeros((tm, LANES), F32) + cab_ref[:, cols]
            for k in range(ka):
                u1 = u1 + caw_ref[k:k + 1, cols] * _tap(rot, HALO - ka // 2 + k, tm)
            u1_ref[:, cols] = u1
            _fill_pad(pad, bc[:, cols] * bx[:, cols], bct[:, cols] * bxt[:, cols], bcb[:, cols] * bxb[:, cols],
                      top_ok, bot_ok, tm)
            v = jnp.zeros((tm, LANES), F32)
            for k in range(kb):
                v = v + cbw_ref[k:k + 1, cols] * pad[pl.ds(HALO - kb // 2 + k, tm), :]
            v_s[:, cols] = v
            return carry

        lax.fori_loop(0, w // LANES, chunk, 0)
        nrm, _ = _ln_norm(u1_ref[...])
        a_out = _silu(nrm * nag_ref[...] + nab_ref[...]) * _silu(agate[...])
        cat_ref[:, 0:w] = a_out.astype(BF16)
        cat_ref[:, w:2 * w] = (bb[...] * v_s[...] * _silu(bgate[...])).astype(BF16)

    def main(cblk):
        return pl.BlockSpec((tm, w), lambda i: (i, cblk))

    def whole(a):
        return pl.BlockSpec(a.shape, lambda i: (0, 0))

    in_specs = (_conv_specs(r, w, tm, 0) + _conv_specs(r, w, tm, 1) + [main(2)] + _conv_specs(r, w, tm, 3)
                + [main(4)] + _conv_specs(r, w, tm, 5) + [main(6)]
                + [whole(caw), whole(cab), whole(nag), whole(nab), whole(cbw)])
    return pl.pallas_call(
        body, name=name, grid=(nr,), in_specs=in_specs,
        out_specs=[pl.BlockSpec((tm, 2 * w), lambda i: (i, 0)), pl.BlockSpec((tm, w), lambda i: (i, 0))],
        out_shape=[jax.ShapeDtypeStruct((r, 2 * w), BF16), jax.ShapeDtypeStruct((r, w), F32)],
        scratch_shapes=[pltpu.VMEM((tm + 2 * HALO, LANES), F32), pltpu.VMEM((8, n, LANES), F32),
                        pltpu.VMEM((tm, w), F32)],
        compiler_params=_cp(1),
    )(*([p] * 15), caw, cab, nag, nab, cbw)


def _conv_bwd1(dcat, p, u1, nag, nab, t_ctx, name):
    r, w = u1.shape
    tm = _row_tile(t_ctx, 128)

    def body(da_ref, agate_ref, u1_ref, nag_ref, nab_ref, du1_ref, dag_ref, s_ref):
        @pl.when(pl.program_id(0) == 0)
        def _():
            s_ref[...] = jnp.zeros_like(s_ref)

        n, rstd = _ln_norm(u1_ref[...])
        g = nag_ref[...]
        u2 = n * g + nab_ref[...]
        da = da_ref[...]
        ag = agate_ref[...]
        dag_ref[...] = (da * _silu(u2) * _dsilu(ag)).astype(BF16)
        du2 = da * _silu(ag) * _dsilu(u2)
        du1 = _ln_bwd(du2 * g, n, rstd)
        du1_ref[...] = du1
        s_ref[0:1, :] += jnp.sum(du2 * n, axis=0, keepdims=True)
        s_ref[1:2, :] += jnp.sum(du2, axis=0, keepdims=True)
        s_ref[2:3, :] += jnp.sum(du1, axis=0, keepdims=True)

    def win(cblk):
        return pl.BlockSpec((tm, w), lambda i: (i, cblk))

    one = pl.BlockSpec((1, w), lambda i: (0, 0))
    return pl.pallas_call(
        body, name=name, grid=(r // tm,), in_specs=[win(0), win(2), win(0), one, one],
        out_specs=[win(0), win(0), pl.BlockSpec((8, w), lambda i: (0, 0))],
        out_shape=[jax.ShapeDtypeStruct((r, w), F32), jax.ShapeDtypeStruct((r, w), BF16),
                   jax.ShapeDtypeStruct((8, w), F32)],
        compiler_params=_cp(1),
    )(dcat, p, u1, nag, nab)


def _conv_bwd2(du1, dcat, p, dag, caw, cbw, t_ctx, name):
    r, w = du1.shape
    ka, kb = caw.shape[0], cbw.shape[0]
    tm = _row_tile(t_ctx, 128)
    nctx, nr = t_ctx // tm, r // tm
    n = tm + 2 * HALO - 8

    def body(du, dut, dub, av, avt, avb, ag, agt, agb, db, dbt, dbb, bx, bxt, bxb, bb, bbt, bbb,
             bc, bct, bcb, bg, bgt, bgb, dag_ref, caw_ref, cbw_ref, dp_ref, dcaw_ref, dcbw_ref,
             pad, pad2, rot_u, rot_d, acc_a, acc_b):
        i = pl.program_id(0)
        top_ok, bot_ok = _edges(i, nctx, nr)

        @pl.when(i == 0)
        def _():
            acc_a[...] = jnp.zeros_like(acc_a)
            acc_b[...] = jnp.zeros_like(acc_b)

        def chunk(c, carry):
            c0 = pl.multiple_of(c * LANES, LANES)
            cols = pl.ds(c0, LANES)

            def seg(s):
                return pl.ds(pl.multiple_of(s * w + c0, LANES), LANES)

            sg = _sigmoid(ag[:, cols])
            av_m = av[:, cols]
            _fill_pad(pad, av_m * sg, avt[:, cols] * _sigmoid(agt[:, cols]), avb[:, cols] * _sigmoid(agb[:, cols]),
                      top_ok, bot_ok, tm)
            _rot_fill(rot_u, pad, n)
            du_m = du[:, cols]
            _fill_pad(pad, du_m, dut[:, cols], dub[:, cols], top_ok, bot_ok, tm)
            _rot_fill(rot_d, pad, n)
            du0 = jnp.zeros((tm, LANES), F32)
            for k in range(ka):
                du0 = du0 + caw_ref[k:k + 1, cols] * _tap(rot_d, HALO + ka // 2 - k, tm)
                acc_a[k, :, cols] += _rows8(du_m * _tap(rot_u, HALO - ka // 2 + k, tm))
            dp_ref[:, seg(0)] = (du0 * sg).astype(BF16)
            dp_ref[:, seg(1)] = (du0 * av_m * sg * (1.0 - sg)).astype(BF16)
            dp_ref[:, seg(2)] = dag_ref[:, cols]

            bc_m, bx_m = bc[:, cols], bx[:, cols]
            _fill_pad(pad, bc_m * bx_m, bct[:, cols] * bxt[:, cols], bcb[:, cols] * bxb[:, cols],
                      top_ok, bot_ok, tm)
            v = jnp.zeros((tm, LANES), F32)
            for k in range(kb):
                v = v + cbw_ref[k:k + 1, cols] * pad[pl.ds(HALO - kb // 2 + k, tm), :]
            db_m, bb_m, bg_m = db[:, cols], bb[:, cols], bg[:, cols]
            sbg = _silu(bg_m)
            dv_m = db_m * bb_m * sbg
            _fill_pad(pad2, dv_m, dbt[:, cols] * bbt[:, cols] * _silu(bgt[:, cols]),
                      dbb[:, cols] * bbb[:, cols] * _silu(bgb[:, cols]), top_ok, bot_ok, tm)
            dw0 = jnp.zeros((tm, LANES), F32)
            for k in range(kb):
                dw0 = dw0 + cbw_ref[k:k + 1, cols] * pad2[pl.ds(HALO + kb // 2 - k, tm), :]
                acc_b[k, :, cols] += _rows8(dv_m * pad[pl.ds(HALO - kb // 2 + k, tm), :])
            dp_ref[:, seg(3)] = (dw0 * bc_m).astype(BF16)
            dp_ref[:, seg(4)] = (db_m * v * sbg).astype(BF16)
            dp_ref[:, seg(5)] = (dw0 * bx_m).astype(BF16)
            dp_ref[:, seg(6)] = (db_m * bb_m * v * _dsilu(bg_m)).astype(BF16)
            return carry

        lax.fori_loop(0, w // LANES, chunk, 0)

        @pl.when(i == nr - 1)
        def _():
            dcaw_ref[...] = jnp.zeros_like(dcaw_ref)
            dcbw_ref[...] = jnp.zeros_like(dcbw_ref)
            for k in range(ka):
                dcaw_ref[k:k + 1, :] = jnp.sum(acc_a[k], axis=0, keepdims=True)
            for k in range(kb):
                dcbw_ref[k:k + 1, :] = jnp.sum(acc_b[k], axis=0, keepdims=True)

    def whole(a):
        return pl.BlockSpec(a.shape, lambda i: (0, 0))

    in_specs = (_conv_specs(r, w, tm, 0) + _conv_specs(r, w, tm, 0) + _conv_specs(r, w, tm, 1)
                + _conv_specs(r, w, tm, 1) + _conv_specs(r, w, tm, 3) + _conv_specs(r, w, tm, 4)
                + _conv_specs(r, w, tm, 5) + _conv_specs(r, w, tm, 6)
                + [pl.BlockSpec((tm, w), lambda i: (i, 0)), whole(caw), whole(cbw)])
    pad_t = pltpu.VMEM((tm + 2 * HALO, LANES), F32)
    rot_t = pltpu.VMEM((8, n, LANES), F32)
    return pl.pallas_call(
        body, name=name, grid=(nr,), in_specs=in_specs,
        out_specs=[pl.BlockSpec((tm, 7 * w), lambda i: (i, 0)), pl.BlockSpec((32, w), lambda i: (0, 0)),
                   pl.BlockSpec((8, w), lambda i: (0, 0))],
        out_shape=[jax.ShapeDtypeStruct((r, 7 * w), BF16), jax.ShapeDtypeStruct((32, w), F32),
                   jax.ShapeDtypeStruct((8, w), F32)],
        scratch_shapes=[pad_t, pad_t, rot_t, rot_t, pltpu.VMEM((32, 8, w), F32), pltpu.VMEM((8, 8, w), F32)],
        compiler_params=_cp(1),
    )(*([du1] * 3), *([p] * 6), *([dcat] * 3), *([p] * 12), dag, caw, cbw)


def _rms(xh):
    r = lax.rsqrt(jnp.mean(xh * xh, axis=-1, keepdims=True) + RMS_EPS)
    return xh * r, r


def _qk_fwd(p, cos, sin, qg, kg, att, kv, t_ctx, name):
    r = p.shape[0]
    tm = _row_tile(t_ctx)

    def body(q_ref, k_ref, v_ref, cos_ref, sin_ref, qg_ref, kg_ref, qr_ref, kr_ref, vb_ref):
        cs, sn = cos_ref[...], sin_ref[...]
        for src, g_ref, dst, nh in ((q_ref, qg_ref, qr_ref, att // HEAD_DIM), (k_ref, kg_ref, kr_ref, kv // HEAD_DIM)):
            for h in range(nh):
                cols = slice(h * HEAD_DIM, (h + 1) * HEAD_DIM)
                n, _ = _rms(src[:, cols])
                n = n * g_ref[...]
                dst[:, cols] = (n * cs + _partner(n) * sn).astype(BF16)
        vb_ref[...] = v_ref[...].astype(BF16)

    def rows(width, cblk):
        return pl.BlockSpec((tm, width), lambda i: (i, cblk))

    one = pl.BlockSpec((1, HEAD_DIM), lambda i: (0, 0))
    return pl.pallas_call(
        body, name=name, grid=(r // tm,),
        in_specs=[rows(att, 0), rows(kv, 2 * att // kv), rows(kv, 2 * att // kv + 1),
                  rows(HEAD_DIM, 0), rows(HEAD_DIM, 0), one, one],
        out_specs=[rows(att, 0), rows(kv, 0), rows(kv, 0)],
        out_shape=[jax.ShapeDtypeStruct((r, att), BF16), jax.ShapeDtypeStruct((r, kv), BF16),
                   jax.ShapeDtypeStruct((r, kv), BF16)],
        compiler_params=_cp(1),
    )(p, p, p, cos, sin, qg, kg)


def _qk_bwd(p, dqr, dkr, cos, sin, qg, kg, att, kv, t_ctx, name):
    r = p.shape[0]
    tm = _row_tile(t_ctx)

    def body(q_ref, k_ref, dqr_ref, dkr_ref, cos_ref, sin_ref, qg_ref, kg_ref, dq_ref, dk_ref, s_ref):
        @pl.when(pl.program_id(0) == 0)
        def _():
            s_ref[...] = jnp.zeros_like(s_ref)

        cs, sn = cos_ref[...], sin_ref[...]
        for row, (src, dsrc, g_ref, dst, nh) in enumerate((
                (q_ref, dqr_ref, qg_ref, dq_ref, att // HEAD_DIM), (k_ref, dkr_ref, kg_ref, dk_ref, kv // HEAD_DIM))):
            dg = jnp.zeros((1, HEAD_DIM), F32)
            for h in range(nh):
                cols = slice(h * HEAD_DIM, (h + 1) * HEAD_DIM)
                n0, rr = _rms(src[:, cols])
                d = dsrc[:, cols]
                dng = d * cs + _partner(d * sn)
                dg = dg + jnp.sum(dng * n0, axis=0, keepdims=True)
                dn0 = dng * g_ref[...]
                dst[:, cols] = (rr * (dn0 - n0 * jnp.mean(dn0 * n0, axis=-1, keepdims=True))).astype(BF16)
            s_ref[row:row + 1, :] += dg

    def rows(width, cblk):
        return pl.BlockSpec((tm, width), lambda i: (i, cblk))

    one = pl.BlockSpec((1, HEAD_DIM), lambda i: (0, 0))
    return pl.pallas_call(
        body, name=name, grid=(r // tm,),
        in_specs=[rows(att, 0), rows(kv, 2 * att // kv), rows(att, 0), rows(kv, 0),
                  rows(HEAD_DIM, 0), rows(HEAD_DIM, 0), one, one],
        out_specs=[rows(att, 0), rows(kv, 0), pl.BlockSpec((8, HEAD_DIM), lambda i: (0, 0))],
        out_shape=[jax.ShapeDtypeStruct((r, att), BF16), jax.ShapeDtypeStruct((r, kv), BF16),
                   jax.ShapeDtypeStruct((8, HEAD_DIM), F32)],
        compiler_params=_cp(1),
    )(p, p, dqr, dkr, cos, sin, qg, kg)


def _stack_heads(x, tq):
    return jnp.concatenate([x[:, g * HEAD_DIM:(g + 1) * HEAD_DIM] for g in range(GQA_GROUP)], axis=0)


def _attn_tiles(t_ctx, s_lat, tkl_pref):
    return _div(t_ctx, 256, 8), _div(s_lat, tkl_pref, 8)


NT_DIMS = (((1,), (1,)), ((), ()))
TN_DIMS = (((0,), (0,)), ((), ()))
ATTN_SCALE_LOG2 = ATTN_SCALE * math.log2(math.e)


def _flash_fwd(qr, kr, vb, p, att, t_ctx, name, side=()):
    r = qr.shape[0]
    s_lat = r - t_ctx
    gw = GQA_GROUP * HEAD_DIM
    nkv = att // gw
    tq = _div(t_ctx, 256, 8)
    tkl = _div(s_lat // 2, 1024, 8)
    nq, nq_ctx, n_lat = r // tq, t_ctx // tq, s_lat // tkl
    nl = GQA_GROUP * tq
    align = math.gcd(t_ctx, tkl)
    nj = len(side)
    s_ins, s_in_specs, s_out_shape, s_out_specs, s_sems = _side_plan(side)

    def body(q_ref, k_ref, v_ref, g_ref, *rest):
        s_in, (o_ref, og_ref, lse_ref), s_out = rest[:nj], rest[nj:nj + 3], rest[nj + 3:2 * nj + 3]
        (m_s, l_s, acc_s, st_a, st_b), s_sem = rest[2 * nj + 3:2 * nj + 8], rest[2 * nj + 8:]
        start, finish = _side_run(side, s_in, s_out, s_sem, *_grid_ends((nkv, nq)))
        start()
        qi = pl.program_id(1)
        q4 = _stack_heads(q_ref[...], tq)
        m_s[...] = jnp.full_like(m_s, -1e30)
        l_s[...] = jnp.zeros_like(l_s)
        acc_s[...] = jnp.zeros_like(acc_s)

        def scores(off, size):
            return lax.dot_general(k_ref[pl.ds(off, size), :], q4, NT_DIMS,
                                   preferred_element_type=F32) * ATTN_SCALE_LOG2

        def update(st, off, size):
            m_old = m_s[...]
            m_new = jnp.maximum(m_old, jnp.max(st, axis=0, keepdims=True))
            pe = jnp.exp2(st - m_new)
            a = jnp.exp2(m_old - m_new)
            l_s[...] = a * l_s[...] + jnp.sum(pe, axis=0, keepdims=True)
            acc_s[...] = a * acc_s[...] + lax.dot_general(v_ref[pl.ds(off, size), :], pe.astype(BF16), TN_DIMS,
                                                          preferred_element_type=F32)
            m_s[...] = m_new

        def lat_off(n):
            return pl.multiple_of(t_ctx + jnp.minimum(n, n_lat - 1) * tkl, align)

        is_lat = qi >= nq_ctx

        @pl.when(is_lat)
        def _():
            st_a[...] = scores(lat_off(0), tkl)

        update(scores(0, t_ctx), 0, t_ctx)

        def pair(j, carry):
            n = 2 * j
            st_b[...] = scores(lat_off(n + 1), tkl)
            update(st_a[...], lat_off(n), tkl)
            st_a[...] = scores(lat_off(n + 2), tkl)
            update(st_b[...], lat_off(n + 1), tkl)
            return carry

        lax.fori_loop(0, jnp.where(is_lat, n_lat // 2, 0), pair, 0)
        o4 = (acc_s[...] / l_s[...]).T
        lse_ref[...] = m_s[...] + jnp.log2(l_s[...])
        sg = _silu(g_ref[...])
        for g in range(GQA_GROUP):
            cols = slice(g * HEAD_DIM, (g + 1) * HEAD_DIM)
            og = o4[g * tq:(g + 1) * tq, :]
            o_ref[:, cols] = og
            og_ref[:, cols] = (og * sg[:, cols]).astype(BF16)
        finish()

    qspec = pl.BlockSpec((tq, gw), lambda h, i: (i, h))
    kspec = pl.BlockSpec((r, HEAD_DIM), lambda h, i: (0, h))
    return pl.pallas_call(
        body, name=name, grid=(nkv, nq),
        in_specs=[qspec, kspec, kspec, pl.BlockSpec((tq, gw), lambda h, i: (i, att // gw + h))] + s_in_specs,
        out_specs=[qspec, qspec, pl.BlockSpec((None, None, 1, nl), lambda h, i: (h, i, 0, 0))] + s_out_specs,
        out_shape=[jax.ShapeDtypeStruct((r, att), F32), jax.ShapeDtypeStruct((r, att), BF16),
                   jax.ShapeDtypeStruct((nkv, nq, 1, nl), F32)] + s_out_shape,
        scratch_shapes=[pltpu.VMEM((1, nl), F32), pltpu.VMEM((1, nl), F32), pltpu.VMEM((HEAD_DIM, nl), F32),
                        pltpu.VMEM((tkl, nl), F32), pltpu.VMEM((tkl, nl), F32)] + s_sems,
        compiler_params=_cp(2),
    )(qr, kr, vb, p, *s_ins)


def _flash_bwd(qr, kr, vb, p, o, dog, lse, att, kv, t_ctx, name):
    r = qr.shape[0]
    s_lat = r - t_ctx
    gw = GQA_GROUP * HEAD_DIM
    nkv = att // gw
    tq, tkl = _attn_tiles(t_ctx, s_lat, 1024)
    nq, nq_ctx, n_lat = r // tq, t_ctx // tq, s_lat // tkl
    nl = GQA_GROUP * tq

    def body(q_ref, k_ref, v_ref, g_ref, o_ref, dog_ref, lse_ref, dq_ref, dgate_ref, dk_ref, dv_ref, dq_s):
        qi = pl.program_id(1)

        @pl.when(qi == 0)
        def _():
            dk_ref[...] = jnp.zeros_like(dk_ref)
            dv_ref[...] = jnp.zeros_like(dv_ref)

        gate, ov, dogv = g_ref[...], o_ref[...], dog_ref[...]
        dgate_ref[...] = (dogv * ov * _dsilu(gate)).astype(BF16)
        do = dogv * _silu(gate)
        do4 = _stack_heads(do, tq)
        delta = jnp.sum((do4 * _stack_heads(ov, tq)).T, axis=0, keepdims=True)
        do4 = do4.astype(BF16)
        q4 = _stack_heads(q_ref[...], tq)
        lse_v = lse_ref[...]
        dq_s[...] = jnp.zeros_like(dq_s)

        def step(off, size):
            kb = k_ref[pl.ds(off, size), :]
            vv = v_ref[pl.ds(off, size), :]
            st = lax.dot_general(kb, q4, NT_DIMS, preferred_element_type=F32) * ATTN_SCALE_LOG2
            pe = jnp.exp2(st - lse_v)
            dp = lax.dot_general(vv, do4, NT_DIMS, preferred_element_type=F32)
            ds = (pe * (dp - delta) * ATTN_SCALE).astype(BF16)
            dv_ref[pl.ds(off, size), :] += jnp.dot(pe.astype(BF16), do4, preferred_element_type=F32)
            dk_ref[pl.ds(off, size), :] += jnp.dot(ds, q4, preferred_element_type=F32)
            dq_s[...] += lax.dot_general(kb, ds, TN_DIMS, preferred_element_type=F32)

        step(0, t_ctx)

        def lat(n, carry):
            step(pl.multiple_of(t_ctx + n * tkl, math.gcd(t_ctx, tkl)), tkl)
            return carry

        lax.fori_loop(0, jnp.where(qi < nq_ctx, 0, n_lat), lat, 0)
        dq4 = dq_s[...].T
        for g in range(GQA_GROUP):
            dq_ref[:, g * HEAD_DIM:(g + 1) * HEAD_DIM] = dq4[g * tq:(g + 1) * tq, :]

    qspec = pl.BlockSpec((tq, gw), lambda h, i: (i, h))
    kspec = pl.BlockSpec((r, HEAD_DIM), lambda h, i: (0, h))
    return pl.pallas_call(
        body, name=name, grid=(nkv, nq),
        in_specs=[qspec, kspec, kspec, pl.BlockSpec((tq, gw), lambda h, i: (i, att // gw + h)), qspec, qspec,
                  pl.BlockSpec((None, None, 1, nl), lambda h, i: (h, i, 0, 0))],
        out_specs=[qspec, qspec, kspec, kspec],
        out_shape=[jax.ShapeDtypeStruct((r, att), F32), jax.ShapeDtypeStruct((r, att), BF16),
                   jax.ShapeDtypeStruct((r, kv), F32), jax.ShapeDtypeStruct((r, kv), F32)],
        scratch_shapes=[pltpu.VMEM((HEAD_DIM, nl), F32)],
        compiler_params=_cp(2),
    )(qr, kr, vb, p, o, dog, lse)


def _rope_tables(t_ctx, s_lat):
    rows_n = s_lat // GRID_W
    row = jnp.repeat(jnp.arange(rows_n, dtype=F32), GRID_W)
    col = jnp.tile(jnp.arange(GRID_W, dtype=F32), rows_n)
    axis_dim = HEAD_DIM // 2
    inv_freq = ROPE_THETA ** (-jnp.arange(0, axis_dim, 2, dtype=F32) / axis_dim)
    ang_r = row[:, None] * inv_freq[None, :]
    ang_c = col[:, None] * inv_freq[None, :]
    cos = jnp.concatenate([jnp.cos(ang_r), jnp.cos(ang_r), jnp.cos(ang_c), jnp.cos(ang_c)], axis=1)
    sin = jnp.concatenate([-jnp.sin(ang_r), jnp.sin(ang_r), -jnp.sin(ang_c), jnp.sin(ang_c)], axis=1)
    cos = jnp.concatenate([jnp.ones((t_ctx, HEAD_DIM), F32), cos], axis=0)
    sin = jnp.concatenate([jnp.zeros((t_ctx, HEAD_DIM), F32), sin], axis=0)
    return cos, sin


def _pad_rows(a, rows):
    return jnp.pad(a, ((0, rows - a.shape[0]), (0, 0)))


def kernel(x, c, ctx, c_ctx, w_mod, b_mod, post_ln_g, post_ln_b, w_in_e, conv_a_w, conv_a_b, norm_a_g, norm_a_b, conv_b_w, w_out_e, w_in_o, q_norm_g, k_norm_g, w_out_o, loss_target, m_c_ctx, m_w_mod, m_b_mod, m_post_ln_g, m_post_ln_b, m_w_in_e, m_conv_a_w, m_conv_a_b, m_norm_a_g, m_norm_a_b, m_conv_b_w, m_w_out_e, m_w_in_o, m_q_norm_g, m_k_norm_g, m_w_out_o, v_c_ctx, v_w_mod, v_b_mod, v_post_ln_g, v_post_ln_b, v_w_in_e, v_conv_a_w, v_conv_a_b, v_norm_a_g, v_norm_a_b, v_conv_b_w, v_w_out_e, v_w_in_o, v_q_norm_g, v_k_norm_g, v_w_out_o):
    depth, d, mcols = w_mod.shape
    s_lat, t_ctx = x.shape[1], ctx.shape[1]
    n_even, n_odd = w_in_e.shape[0], w_in_o.shape[0]
    ka, kb = conv_a_w.shape[1], conv_b_w.shape[1]
    wch = conv_a_w.shape[2] * NDEV
    att = w_out_o.shape[1] * NDEV
    kv = (w_in_o.shape[2] * NDEV - 2 * att) // 2
    alpha = (2.0 * depth) ** 0.25
    me = 4 * lax.axis_index("x") + 2 * lax.axis_index("y") + lax.axis_index("c")

    c_all = _all_gather(_pad_rows(c, 8), "ag_c")[:, 0, :]
    c16 = jnp.concatenate([c_all, _pad_rows(c_ctx[None, :], 8)], axis=0)
    sc16 = _ew(_silu, BF16, "silu_c", c16)
    m_part = _mm(sc16, w_mod, "nn", F32, "mod_fwd", tm=16, tn=mcols, tk=d)
    m_all = _all_gather(m_part, "ag_mod")
    m_full = m_all.reshape(NDEV, 16, depth, mcols).transpose(2, 1, 0, 3).reshape(depth * 16, 3 * d)
    b16 = jnp.broadcast_to(b_mod[:, None, :], (depth, 16, 3 * d)).reshape(depth * 16, 3 * d)
    m_full = _ew(lambda a, b: a + b, F32, "mod_bias", m_full, b16).reshape(depth, 16, 3, d)
    modv = [jnp.stack([m_full[l, 8], lax.dynamic_index_in_dim(m_full[l], me, 0, keepdims=False)])
            for l in range(depth)]

    cw = jnp.concatenate([_pad_rows(conv_a_w[i], 32) for i in range(n_even)]
                         + [_pad_rows(conv_b_w[i], 8) for i in range(n_even)], axis=0)
    cw_all = _all_gather(cw, "ag_convw").transpose(1, 0, 2).reshape(cw.shape[0], wch)
    caw = [cw_all[32 * i:32 * i + ka] for i in range(n_even)]
    cbw = [cw_all[32 * n_even + 8 * i:32 * n_even + 8 * i + kb] for i in range(n_even)]

    sh_in = [_ew(lambda t: t, BF16, f"w_in{l}_bf16", (w_in_e if l % 2 == 0 else w_in_o)[l // 2]) for l in range(depth)]
    sh_out = [_ew(lambda t: t, BF16, f"w_out{l}_bf16", (w_out_e if l % 2 == 0 else w_out_o)[l // 2])
              for l in range(depth)]
    g_in = [None] * depth
    g_out = [None] * depth
    g_in[0] = _all_gather(sh_in[0], "ag_w_in0")

    def wanted(keys):
        return [(kind, l) for kind, l in keys if l < depth and (g_in if kind == "in" else g_out)[l] is None]

    def gather_jobs(keys):
        return [("gather", (sh_in if kind == "in" else sh_out)[l]) for kind, l in keys]

    def hosted(res, keys):
        if not keys:
            return res
        n_own = len(res) - len(keys)
        for (kind, l), g in zip(keys, res[n_own:]):
            if kind == "in":
                g_in[l] = g
            else:
                g_out[l] = g.reshape(1, NDEV * g.shape[1], d)
        return res[0] if n_own == 1 else res[:n_own]

    cos, sin = _rope_tables(t_ctx, s_lat)
    lnp = [jnp.stack([post_ln_g[l], post_ln_b[l]]) for l in range(depth)]

    xb = jnp.concatenate([ctx[0], x[0]], axis=0)
    saved = []
    h = _modulate(xb, modv[0], t_ctx, "modulate0")
    for l in range(depth):
        i = l // 2
        if l % 2 == 0:
            keys = wanted([("out", l), ("in", l + 1)])
            p = hosted(_mm(h, g_in[l], "nn", F32, f"in_proj{l}", side=gather_jobs(keys)), keys)
            cat, u1 = _conv_fwd(p, caw[i], conv_a_b[i][None], norm_a_g[i][None], norm_a_b[i][None], cbw[i],
                                t_ctx, f"conv_fwd{l}")
            keys = wanted([("out", l + 1)])
            y = hosted(_mm(cat, g_out[l], "nn", F32, f"out_proj{l}", side=gather_jobs(keys)), keys)
            saved.append((xb, h, p, cat, u1, y))
        else:
            keys = wanted([("out", l)])
            p = hosted(_mm(h, g_in[l], "nn", F32, f"in_proj{l}", side=gather_jobs(keys)), keys)
            qr, kr, vb = _qk_fwd(p, cos, sin, q_norm_g[i][None], k_norm_g[i][None], att, kv, t_ctx, f"qk_fwd{l}")
            keys = wanted([("in", l + 1), ("out", l + 1)])
            o, og, lse = hosted(_flash_fwd(qr, kr, vb, p, att, t_ctx, f"flash_fwd{l}", side=gather_jobs(keys)), keys)
            y = _mm(og, g_out[l], "nn", F32, f"out_proj{l}")
            saved.append((xb, h, p, qr, kr, vb, o, og, lse, y))
        if l + 1 < depth:
            xb, h = _postln_fwd(xb, y, modv[l], lnp[l], alpha, modv[l + 1], t_ctx, f"postln_fwd{l}")

    small = {}
    dmod = [None] * depth
    recv_in, recv_out = [None] * depth, [None] * depth
    for l in reversed(range(depth)):
        i = l // 2
        sv = saved[l]
        x_in, h, p, y = sv[0], sv[1], sv[2], sv[-1]
        if l == depth - 1:
            loss_blk, dy, dxa, s_ln = _postln_bwd(loss_target[0], x_in, y, modv[l], lnp[l], alpha, t_ctx,
                                                  f"loss_postln_bwd{l}", from_loss=True)
            loss = lax.psum(loss_blk[0, 0], ("x", "y", "c"))
        else:
            dy, dxa, s_ln = _postln_bwd(dxb, x_in, y, modv[l], lnp[l], alpha, t_ctx, f"postln_bwd{l}")
        small[f"ln{l}"] = s_ln
        mixed = sv[3] if l % 2 == 0 else sv[7]
        k_out = g_out[l].shape[1]
        part = _mm(mixed, dy, "tn", BF16, f"d_w_out{l}", tm=1024, tk=768).reshape(NDEV, k_out // NDEV, d)
        dmixed, recv_out[l] = _mm(dy, g_out[l], "nt", F32, f"d_mixed{l}", side=[("exchange", part)])
        if l % 2 == 0:
            u1 = sv[4]
            du1, dag, s_c1 = _conv_bwd1(dmixed, p, u1, norm_a_g[i][None], norm_a_b[i][None], t_ctx,
                                        f"conv_bwd1_{l}")
            dp, dcaw, dcbw = _conv_bwd2(du1, dmixed, p, dag, caw[i], cbw[i], t_ctx, f"conv_bwd2_{l}")
            small[f"c1_{i}"], small[f"caw{i}"], small[f"cbw{i}"] = s_c1, dcaw, dcbw
        else:
            qr, kr, vb, o, _, lse = sv[3:9]
            dqr, dgate, dkr, dvr = _flash_bwd(qr, kr, vb, p, o, dmixed, lse, att, kv, t_ctx, f"flash_bwd{l}")
            dq, dk, s_qk = _qk_bwd(p, dqr, dkr, cos, sin, q_norm_g[i][None], k_norm_g[i][None], att, kv, t_ctx,
                                   f"qk_bwd{l}")
            small[f"qk{i}"] = jnp.pad(s_qk, ((0, 0), (0, d - HEAD_DIM)))
            dp = jnp.concatenate([dq, dgate, dk, dvr.astype(BF16)], axis=1)
        part = _mm(h, dp, "tn", BF16, f"d_w_in{l}", tm=d, tk=768, out_nd=NDEV)
        dxb, s_mod, recv_in[l] = _mm(dp, g_in[l], "nt", F32, f"d_h{l}", tm=384, side=[("exchange", part)],
                                     mod_bwd=(x_in, dxa, modv[l], t_ctx))
        dmod[l] = jnp.stack([jnp.stack([s_mod[0], s_mod[1], s_ln[2]]), jnp.stack([s_mod[2], s_mod[3], s_ln[3]])])
    grad_x = dxb[t_ctx:][None]

    dm_loc = jnp.stack(dmod).reshape(depth * 2, 3 * d)
    dm_all = _all_gather(dm_loc, "ag_dmod").reshape(NDEV, depth, 2, 3 * d)
    dm_ctx = _sum_lead(dm_all[:, :, 0, :], "sum_dmod_ctx")
    dm16 = jnp.concatenate([dm_all[:, :, 1, :].transpose(1, 0, 2), dm_ctx[:, None, :],
                            jnp.zeros((depth, 7, 3 * d), F32)], axis=1)
    g_b_mod = _sum_lead(dm16.transpose(1, 0, 2), "sum_b_mod")
    dm16_me = lax.dynamic_slice_in_dim(dm16.reshape(depth, 16, NDEV, mcols), me, 1, axis=2)
    dm16_me = dm16_me.reshape(depth, 16, mcols).transpose(1, 0, 2).reshape(16, depth * mcols)
    g_w_mod = _mm(sc16, dm16_me, "tn", F32, "mod_bwd_w", tm=d, tn=mcols, tk=16, out_nd=depth)
    dsc16 = _mm(dm16_me, w_mod, "nt", F32, "mod_bwd_c", tm=16, tn=d, tk=mcols)
    small["c_ctx"] = dsc16[8:16]

    names = sorted(small)
    offs, rows = {}, 0
    for nme in names:
        offs[nme] = rows
        rows += small[nme].shape[0]
    sm_all = _all_gather(jnp.concatenate([small[nme] for nme in names], axis=0), "ag_small")
    sm = _sum_lead(sm_all, "sum_small")

    def part(nme, lo, hi):
        return sm[offs[nme] + lo:offs[nme] + hi]

    g_post_ln_g = jnp.concatenate([part(f"ln{l}", 0, 1) for l in range(depth)], axis=0)
    g_post_ln_b = jnp.concatenate([part(f"ln{l}", 1, 2) for l in range(depth)], axis=0)
    g_norm_a_g = jnp.concatenate([part(f"c1_{i}", 0, 1) for i in range(n_even)], axis=0)
    g_norm_a_b = jnp.concatenate([part(f"c1_{i}", 1, 2) for i in range(n_even)], axis=0)
    g_conv_a_b = jnp.concatenate([part(f"c1_{i}", 2, 3) for i in range(n_even)], axis=0)
    g_q_norm_g = jnp.concatenate([part(f"qk{i}", 0, 1)[:, :HEAD_DIM] for i in range(n_odd)], axis=0)
    g_k_norm_g = jnp.concatenate([part(f"qk{i}", 1, 2)[:, :HEAD_DIM] for i in range(n_odd)], axis=0)
    wsh = wch // NDEV
    g_conv_a_w = jnp.stack([lax.dynamic_slice_in_dim(part(f"caw{i}", 0, ka), me * wsh, wsh, axis=1)
                            for i in range(n_even)])
    g_conv_b_w = jnp.stack([lax.dynamic_slice_in_dim(part(f"cbw{i}", 0, kb), me * wsh, wsh, axis=1)
                            for i in range(n_even)])
    g_c_ctx = _ew(lambda a, b: a * _dsilu(b), F32, "d_c_ctx", part("c_ctx", 0, 8), _pad_rows(c_ctx[None, :], 8))[0]

    def updated(recv, layers, prefix, w, m, v):
        outs = [_reduce_adamw(recv[l], w[j], m[j], v[j], f"adamw_{prefix}{j}") for j, l in enumerate(layers)]
        return [jnp.stack([o[t] for o in outs]) for t in range(4)]

    evens, odds = range(0, depth, 2), range(1, depth, 2)
    r_w_in_e = updated(recv_in, evens, "w_in_e", w_in_e, m_w_in_e, v_w_in_e)
    r_w_out_e = updated(recv_out, evens, "w_out_e", w_out_e, m_w_out_e, v_w_out_e)
    r_w_in_o = updated(recv_in, odds, "w_in_o", w_in_o, m_w_in_o, v_w_in_o)
    r_w_out_o = updated(recv_out, odds, "w_out_o", w_out_o, m_w_out_o, v_w_out_o)

    grads = {
        "c_ctx": g_c_ctx, "w_mod": g_w_mod, "b_mod": g_b_mod, "post_ln_g": g_post_ln_g, "post_ln_b": g_post_ln_b,
        "conv_a_w": g_conv_a_w, "conv_a_b": g_conv_a_b, "norm_a_g": g_norm_a_g, "norm_a_b": g_norm_a_b,
        "conv_b_w": g_conv_b_w, "q_norm_g": g_q_norm_g, "k_norm_g": g_k_norm_g,
    }
    state = {
        "c_ctx": (c_ctx, m_c_ctx, v_c_ctx), "w_mod": (w_mod, m_w_mod, v_w_mod), "b_mod": (b_mod, m_b_mod, v_b_mod),
        "post_ln_g": (post_ln_g, m_post_ln_g, v_post_ln_g), "post_ln_b": (post_ln_b, m_post_ln_b, v_post_ln_b),
        "conv_a_w": (conv_a_w, m_conv_a_w, v_conv_a_w), "conv_a_b": (conv_a_b, m_conv_a_b, v_conv_a_b),
        "norm_a_g": (norm_a_g, m_norm_a_g, v_norm_a_g), "norm_a_b": (norm_a_b, m_norm_a_b, v_norm_a_b),
        "conv_b_w": (conv_b_w, m_conv_b_w, v_conv_b_w), "q_norm_g": (q_norm_g, m_q_norm_g, v_q_norm_g),
        "k_norm_g": (k_norm_g, m_k_norm_g, v_k_norm_g),
    }
    res = {"w_in_e": r_w_in_e, "w_out_e": r_w_out_e, "w_in_o": r_w_in_o, "w_out_o": r_w_out_o}
    for nme, g in grads.items():
        w, m, v = state[nme]
        res[nme] = [g] + _adamw(w, g, m, v, f"adamw_{nme}")
    order = ["c_ctx", "w_mod", "b_mod", "post_ln_g", "post_ln_b", "w_in_e", "conv_a_w", "conv_a_b", "norm_a_g",
             "norm_a_b", "conv_b_w", "w_out_e", "w_in_o", "q_norm_g", "k_norm_g", "w_out_o"]
    return (loss, grad_x, *[res[nme][0] for nme in order], *[res[nme][1] for nme in order],
            *[res[nme][2] for nme in order], *[res[nme][3] for nme in order])
```

```python
import functools
import math

import jax
import jax.numpy as jnp
from jax import lax
from jax.experimental import pallas as pl
from jax.experimental.pallas import tpu as pltpu

F32 = jnp.float32
BF16 = jnp.bfloat16

NDEV = 8
GRID_W = 64
HEAD_DIM = 128
GQA_GROUP = 4
ROPE_THETA = 10000.0
LN_EPS = 1e-5
RMS_EPS = 1e-6
ATTN_SCALE = HEAD_DIM ** -0.5
ADAM_LR = 0.001
ADAM_B1 = 0.9
ADAM_B2 = 0.999
ADAM_EPS = 1e-08
ADAM_WD = 0.01
ADAM_STEP = 10
HALO = 16
LANES = 128
VMEM_LIMIT = 56 * 1024 * 1024
MESH = pl.DeviceIdType.MESH


def _cp(n_axes):
    return pltpu.CompilerParams(dimension_semantics=("arbitrary",) * n_axes, vmem_limit_bytes=VMEM_LIMIT)


def _div(dim, pref, mult):
    t = min(pref, dim) // mult * mult
    while t >= mult:
        if dim % t == 0:
            return t
        t -= mult
    return dim


def _sigmoid(x):
    return 1.0 / (1.0 + jnp.exp(-x))


def _silu(x):
    return x * _sigmoid(x)


def _dsilu(x):
    s = _sigmoid(x)
    return s * (1.0 + x * (1.0 - s))


def _ln_norm(z):
    mu = jnp.mean(z, axis=-1, keepdims=True)
    zc = z - mu
    var = jnp.mean(zc * zc, axis=-1, keepdims=True)
    rstd = lax.rsqrt(var + LN_EPS)
    return zc * rstd, rstd


def _ln_bwd(dn, n, rstd):
    return rstd * (dn - jnp.mean(dn, axis=-1, keepdims=True) - n * jnp.mean(dn * n, axis=-1, keepdims=True))


def _partner(x):
    lane = lax.broadcasted_iota(jnp.int32, x.shape, 1)
    return jnp.where((lane % 64) < 32, pltpu.roll(x, 96, 1), pltpu.roll(x, 32, 1))


N_PEERS = NDEV - 1
HBM_SPEC = pl.BlockSpec(memory_space=pltpu.HBM)


def _side_plan(jobs):
    xs = [x for _, x in jobs]
    out_shape = [jax.ShapeDtypeStruct((NDEV,) + x.shape[-2:], x.dtype) for x in xs]
    sems = [pltpu.SemaphoreType.DMA((2 * N_PEERS + 1,)) for _ in jobs]
    return xs, [HBM_SPEC] * len(jobs), out_shape, [HBM_SPEC] * len(jobs), sems


def _side_copies(kind, x_ref, out_ref, sems):
    mx, my, mc = lax.axis_index("x"), lax.axis_index("y"), lax.axis_index("c")
    me = 4 * mx + 2 * my + mc
    own = x_ref.at[me] if kind == "exchange" else x_ref
    copies = [pltpu.make_async_copy(own, out_ref.at[me], sems.at[2 * N_PEERS])]
    for k in range(1, NDEV):
        px, py, pc = mx ^ ((k >> 2) & 1), my ^ ((k >> 1) & 1), mc ^ (k & 1)
        src = x_ref.at[4 * px + 2 * py + pc] if kind == "exchange" else x_ref
        copies.append(pltpu.make_async_remote_copy(
            src_ref=src, dst_ref=out_ref.at[me], send_sem=sems.at[k - 1], recv_sem=sems.at[N_PEERS + k - 1],
            device_id=(px, py, pc), device_id_type=MESH))
    return copies


def _side_run(jobs, in_refs, out_refs, sem_refs, first, last):
    if not jobs:
        return (lambda: None), (lambda: None)

    def start():
        @pl.when(first)
        def _():
            for (kind, _), x_ref, o_ref, sems in zip(jobs, in_refs, out_refs, sem_refs):
                for cp in _side_copies(kind, x_ref, o_ref, sems):
                    cp.start()

    def finish():
        @pl.when(last)
        def _():
            for (kind, _), x_ref, o_ref, sems in zip(jobs, in_refs, out_refs, sem_refs):
                for cp in _side_copies(kind, x_ref, o_ref, sems):
                    cp.wait()

    return start, finish


def _grid_ends(grid):
    first = last = None
    for ax, n in enumerate(grid):
        i = pl.program_id(ax)
        f, l = i == 0, i == n - 1
        first = f if first is None else jnp.logical_and(first, f)
        last = l if last is None else jnp.logical_and(last, l)
    return first, last


def _mm(a, b, mode, out_dtype, name, tm=768, tn=2048, tk=2048, out_nd=1, side=(), mod_bwd=None):
    if mode == "nn":
        m, kdim = a.shape
        nd, _, ns = b.shape
        tm, tn, tk = _div(m, tm, 8), _div(ns, tn, 128), _div(kdim, tk, 128)
        nbs = ns // tn
        grid = (m // tm, nd * nbs, kdim // tk)
        a_spec = pl.BlockSpec((tm, tk), lambda i, j, k: (i, k))
        b_spec = pl.BlockSpec((None, tk, tn), lambda i, j, k: (j // nbs, k, j % nbs))
        o_spec = pl.BlockSpec((tm, tn), lambda i, j, k: (i, j))
        out_shape = (m, nd * ns)
        dims = (((1,), (0,)), ((), ()))
    elif mode == "nt":
        m, _ = a.shape
        nd, ko, ns = b.shape
        tm, tn, tk = _div(m, tm, 8), _div(ko, tn, 128), _div(ns, tk, 128)
        kbs = ns // tk
        grid = (m // tm, ko // tn, nd * kbs)
        a_spec = pl.BlockSpec((tm, tk), lambda i, j, k: (i, k))
        b_spec = pl.BlockSpec((None, tn, tk), lambda i, j, k: (k // kbs, j, k % kbs))
        o_spec = pl.BlockSpec((tm, tn), lambda i, j, k: (i, j))
        out_shape = (m, ko)
        dims = (((1,), (1,)), ((), ()))
    else:
        m, kdim = a.shape
        n = b.shape[1]
        ns = n // out_nd
        tm, tn, tk = _div(kdim, tm, 128), _div(ns, tn, 128), _div(m, tk, 16)
        nbs = ns // tn
        grid = (kdim // tm, out_nd * nbs, m // tk)
        a_spec = pl.BlockSpec((tk, tm), lambda i, j, k: (k, i))
        b_spec = pl.BlockSpec((tk, tn), lambda i, j, k: (k, j))
        o_spec = pl.BlockSpec((None, tm, tn), lambda i, j, k: (j // nbs, i, j % nbs))
        out_shape = (out_nd, kdim, ns)
        dims = (((0,), (0,)), ((), ()))
    nk = grid[2]
    nj = len(side)
    s_ins, s_in_specs, s_out_shape, s_out_specs, s_sems = _side_plan(side)
    e_ins, e_in_specs, e_out_shape, e_out_specs = [], [], [], []
    if mod_bwd is not None:
        assert mode == "nt" and grid[1] == 1
        x_in, dxa, modv, t_ctx = mod_bwd
        e_ins = [x_in, dxa, modv]
        e_in_specs = [pl.BlockSpec((tm, tn), lambda i, j, k: (i, 0)), pl.BlockSpec((tm, tn), lambda i, j, k: (i, 0)),
                      pl.BlockSpec(modv.shape, lambda i, j, k: (0, 0, 0))]
        e_out_shape = [jax.ShapeDtypeStruct((8, tn), F32)]
        e_out_specs = [pl.BlockSpec((8, tn), lambda i, j, k: (0, 0))]
    ne, neo = len(e_ins), len(e_out_shape)

    def body(a_ref, b_ref, *rest):
        e_in, s_in = rest[:ne], rest[ne:ne + nj]
        o_ref, e_out, s_out = rest[ne + nj], rest[ne + nj + 1:ne + nj + 1 + neo], rest[ne + nj + 1 + neo:ne + 2 * nj + 1 + neo]
        scratch = rest[ne + 2 * nj + 1 + neo:]
        acc_ref, s_sem = (None, scratch) if nk == 1 else (scratch[0], scratch[1:])
        first, last = _grid_ends(grid)
        start, finish = _side_run(side, s_in, s_out, s_sem, first, last)
        start()

        def prod():
            return lax.dot_general(a_ref[...].astype(BF16), b_ref[...].astype(BF16), dims,
                                   preferred_element_type=F32)

        def emit(val):
            if mod_bwd is None:
                o_ref[...] = val.astype(o_ref.dtype)
                return
            x_ref, dxa_ref, mv_ref = e_in
            rows = pl.program_id(0) * tm + lax.broadcasted_iota(jnp.int32, (tm, 1), 0)
            is_ctx = rows < t_ctx
            o_ref[...] = dxa_ref[...] + val * (1.0 + jnp.where(is_ctx, mv_ref[0, 1:2, :], mv_ref[1, 1:2, :]))
            dsc = val * x_ref[...]
            for row, (t, keep_ctx) in enumerate(((val, True), (dsc, True), (val, False), (dsc, False))):
                e_out[0][row:row + 1, :] += jnp.sum(jnp.where(is_ctx == keep_ctx, t, 0.0), axis=0, keepdims=True)

        if mod_bwd is not None:
            @pl.when(first)
            def _():
                e_out[0][...] = jnp.zeros_like(e_out[0])

        if nk == 1:
            emit(prod())
        else:
            k = pl.program_id(2)

            @pl.when(k == 0)
            def _():
                acc_ref[...] = jnp.zeros_like(acc_ref)

            acc_ref[...] += prod()

            @pl.when(k == nk - 1)
            def _():
                emit(acc_ref[...])
        finish()

    outs = pl.pallas_call(
        body, name=name, grid=grid, in_specs=[a_spec, b_spec] + e_in_specs + s_in_specs,
        out_specs=[o_spec] + e_out_specs + s_out_specs,
        out_shape=[jax.ShapeDtypeStruct(out_shape, out_dtype)] + e_out_shape + s_out_shape,
        scratch_shapes=([] if nk == 1 else [pltpu.VMEM((tm, tn), F32)]) + s_sems, compiler_params=_cp(3),
    )(a, b, *e_ins, *s_ins)
    return outs if len(outs) > 1 else outs[0]


def _ew(fn, out_dtype, name, *xs):
    rows, cols = xs[0].shape
    tr = rows if rows <= 64 else _div(rows, 256, 16)

    def body(*refs):
        refs[-1][...] = fn(*[r[...] for r in refs[:-1]]).astype(out_dtype)

    spec = pl.BlockSpec((tr, cols), lambda i: (i, 0))
    return pl.pallas_call(
        body, name=name, grid=(rows // tr,), in_specs=[spec] * len(xs), out_specs=spec,
        out_shape=jax.ShapeDtypeStruct((rows, cols), out_dtype), compiler_params=_cp(1),
    )(*xs)


def _sum_lead(x, name):
    n, rows, cols = x.shape
    tr = _div(rows, 64, 8)

    def body(x_ref, o_ref):
        acc = x_ref[0]
        for s in range(1, n):
            acc = acc + x_ref[s]
        o_ref[...] = acc

    return pl.pallas_call(
        body, name=name, grid=(rows // tr,),
        in_specs=[pl.BlockSpec((n, tr, cols), lambda i: (0, i, 0))],
        out_specs=pl.BlockSpec((tr, cols), lambda i: (i, 0)),
        out_shape=jax.ShapeDtypeStruct((rows, cols), F32), compiler_params=_cp(1),
    )(x)


def _adam_math(w, g, m, v):
    m = ADAM_B1 * m + (1.0 - ADAM_B1) * g
    v = ADAM_B2 * v + (1.0 - ADAM_B2) * (g * g)
    m_hat = m / (1.0 - ADAM_B1 ** ADAM_STEP)
    v_hat = v / (1.0 - ADAM_B2 ** ADAM_STEP)
    delta = -ADAM_LR * (m_hat / (jnp.sqrt(v_hat) + ADAM_EPS) + ADAM_WD * w)
    return delta, m, v


def _adamw(w, g, m, v, name):
    shape = w.shape
    cols = shape[-1]
    w2, g2, m2, v2 = [t.reshape(-1, cols) for t in (w, g, m, v)]
    rows = w2.shape[0]
    tr = rows if rows <= 512 else _div(rows, 256, 8)

    def body(w_ref, g_ref, m_ref, v_ref, d_ref, nm_ref, nv_ref):
        d, nm, nv = _adam_math(w_ref[...], g_ref[...], m_ref[...], v_ref[...])
        d_ref[...] = d
        nm_ref[...] = nm
        nv_ref[...] = nv

    spec = pl.BlockSpec((tr, cols), lambda i: (i, 0))
    outs = pl.pallas_call(
        body, name=name, grid=(rows // tr,), in_specs=[spec] * 4, out_specs=[spec] * 3,
        out_shape=[jax.ShapeDtypeStruct((rows, cols), F32)] * 3, compiler_params=_cp(1),
    )(w2, g2, m2, v2)
    return [o.reshape(shape) for o in outs]


def _reduce_adamw(parts, w, m, v, name):
    shape = w.shape
    n, rows, cols = parts.shape
    w2, m2, v2 = [t.reshape(rows, cols) for t in (w, m, v)]
    tr = _div(rows, 128, 16)

    def body(p_ref, w_ref, m_ref, v_ref, g_ref, d_ref, nm_ref, nv_ref):
        g = p_ref[0].astype(F32)
        for s in range(1, n):
            g = g + p_ref[s].astype(F32)
        d, nm, nv = _adam_math(w_ref[...], g, m_ref[...], v_ref[...])
        g_ref[...] = g
        d_ref[...] = d
        nm_ref[...] = nm
        nv_ref[...] = nv

    spec = pl.BlockSpec((tr, cols), lambda i: (i, 0))
    outs = pl.pallas_call(
        body, name=name, grid=(rows // tr,),
        in_specs=[pl.BlockSpec((n, tr, cols), lambda i: (0, i, 0))] + [spec] * 3, out_specs=[spec] * 4,
        out_shape=[jax.ShapeDtypeStruct((rows, cols), F32)] * 4, compiler_params=_cp(1),
    )(parts, w2, m2, v2)
    return [o.reshape(shape) for o in outs]


def _all_gather(x, name):
    rows, cols = x.shape

    def body(x_ref, out_ref, send_sems, recv_sems, local_sem):
        mx, my, mc = lax.axis_index("x"), lax.axis_index("y"), lax.axis_index("c")
        me, sibling = (mx, my, mc), (mx, my, 1 - mc)
        chips = [(1 - mx, my), (mx, 1 - my), (1 - mx, 1 - my)]

        def slab(px, py, pc):
            return out_ref.at[4 * px + 2 * py + pc]

        def copy(k, block, to, src=None):
            return pltpu.make_async_remote_copy(
                src_ref=slab(*block) if src is None else src, dst_ref=slab(*block),
                send_sem=send_sems.at[k], recv_sem=recv_sems.at[k], device_id=to, device_id_type=MESH)

        mine = pltpu.make_async_copy(x_ref, slab(*me), local_sem)
        mine.start()
        first = [copy(0, me, sibling, src=x_ref)]
        first += [copy(1 + j, me, (*chip, mc), src=x_ref) for j, chip in enumerate(chips)]
        for cp in first:
            cp.start()
        passed = [copy(4 + j, (*chip, mc), sibling) for j, chip in enumerate(chips)]
        for j, chip in enumerate(chips):
            copy(1 + j, (*chip, mc), me).wait_recv()
            passed[j].start()
        copy(0, sibling, me).wait_recv()
        for j, chip in enumerate(chips):
            copy(4 + j, (*chip, 1 - mc), me).wait_recv()
        for cp in first + passed:
            cp.wait_send()
        mine.wait()

    return pl.pallas_call(
        body, name=name, out_shape=jax.ShapeDtypeStruct((NDEV, rows, cols), x.dtype),
        in_specs=[pl.BlockSpec(memory_space=pltpu.HBM)], out_specs=pl.BlockSpec(memory_space=pltpu.HBM),
        scratch_shapes=[pltpu.SemaphoreType.DMA((7,)), pltpu.SemaphoreType.DMA((7,)), pltpu.SemaphoreType.DMA],
    )(x)


def _row_tile(t_ctx, pref=256):
    return _div(t_ctx, pref, 8)


def _mod_spec(d, nctx):
    return pl.BlockSpec((None, 3, d), lambda i: (jnp.where(i >= nctx, 1, 0), 0, 0))


def _modulate(xb, modv, t_ctx, name):
    r, d = xb.shape
    tm = _row_tile(t_ctx)
    nctx = t_ctx // tm

    def body(x_ref, mv_ref, h_ref):
        h_ref[...] = (x_ref[...] * (1.0 + mv_ref[1:2, :]) + mv_ref[0:1, :]).astype(BF16)

    row = pl.BlockSpec((tm, d), lambda i: (i, 0))
    return pl.pallas_call(
        body, name=name, grid=(r // tm,), in_specs=[row, _mod_spec(d, nctx)], out_specs=row,
        out_shape=jax.ShapeDtypeStruct((r, d), BF16), compiler_params=_cp(1),
    )(xb, modv)


def _postln_fwd(xb, y, modv, lnp, alpha, next_modv, t_ctx, name):
    r, d = xb.shape
    tm = _row_tile(t_ctx)
    nctx = t_ctx // tm

    def body(x_ref, y_ref, mv_ref, ln_ref, nmv_ref, o_ref, h_ref):
        n, _ = _ln_norm(alpha * x_ref[...] + mv_ref[2:3, :] * y_ref[...])
        xo = n * ln_ref[0:1, :] + ln_ref[1:2, :]
        o_ref[...] = xo
        h_ref[...] = (xo * (1.0 + nmv_ref[1:2, :]) + nmv_ref[0:1, :]).astype(BF16)

    row = pl.BlockSpec((tm, d), lambda i: (i, 0))
    return pl.pallas_call(
        body, name=name, grid=(r // tm,),
        in_specs=[row, row, _mod_spec(d, nctx), pl.BlockSpec((2, d), lambda i: (0, 0)), _mod_spec(d, nctx)],
        out_specs=[row, row],
        out_shape=[jax.ShapeDtypeStruct((r, d), F32), jax.ShapeDtypeStruct((r, d), BF16)], compiler_params=_cp(1),
    )(xb, y, modv, lnp, next_modv)


def _postln_bwd(dxn, xb, y, modv, lnp, alpha, t_ctx, name, from_loss=False):
    r, d = xb.shape
    tm = _row_tile(t_ctx)
    nctx = t_ctx // tm

    def body(dxn_ref, x_ref, y_ref, mv_ref, ln_ref, *outs):
        dy_ref, dxa_ref, s_ref = outs[-3:]
        i = pl.program_id(0)
        is_ctx = i < nctx

        @pl.when(i == 0)
        def _():
            s_ref[...] = jnp.zeros_like(s_ref)
            if from_loss:
                outs[0][...] = jnp.zeros_like(outs[0])

        yv = y_ref[...]
        gate = mv_ref[2:3, :]
        n, rstd = _ln_norm(alpha * x_ref[...] + gate * yv)
        if from_loss:
            e = jnp.where(is_ctx, 0.0, n * ln_ref[0:1, :] + ln_ref[1:2, :] - dxn_ref[...])
            outs[0][...] += 0.5 * jnp.sum(jnp.sum(e * e, axis=1, keepdims=True), axis=0, keepdims=True) / d
            dxn_v = e / d
        else:
            dxn_v = dxn_ref[...]
        dz = _ln_bwd(dxn_v * ln_ref[0:1, :], n, rstd)
        dy_ref[...] = (gate * dz).astype(BF16)
        dxa_ref[...] = alpha * dz
        s_ref[0:1, :] += jnp.sum(dxn_v * n, axis=0, keepdims=True)
        s_ref[1:2, :] += jnp.sum(dxn_v, axis=0, keepdims=True)
        dgate = jnp.sum(dz * yv, axis=0, keepdims=True)
        s_ref[2:3, :] += jnp.where(is_ctx, dgate, 0.0)
        s_ref[3:4, :] += jnp.where(is_ctx, 0.0, dgate)

    row = pl.BlockSpec((tm, d), lambda i: (i, 0))
    first = pl.BlockSpec((tm, d), lambda i: (jnp.maximum(i - nctx, 0), 0)) if from_loss else row
    loss_spec = [pl.BlockSpec((8, 128), lambda i: (0, 0))] if from_loss else []
    loss_shape = [jax.ShapeDtypeStruct((8, 128), F32)] if from_loss else []
    return pl.pallas_call(
        body, name=name, grid=(r // tm,),
        in_specs=[first, row, row, _mod_spec(d, nctx), pl.BlockSpec((2, d), lambda i: (0, 0))],
        out_specs=loss_spec + [row, row, pl.BlockSpec((8, d), lambda i: (0, 0))],
        out_shape=loss_shape + [jax.ShapeDtypeStruct((r, d), BF16), jax.ShapeDtypeStruct((r, d), F32),
                                jax.ShapeDtypeStruct((8, d), F32)],
        compiler_params=_cp(1),
    )(dxn, xb, y, modv, lnp)


def _conv_specs(r, w, tm, cblk):
    hb = tm // HALO
    last = r // HALO - 1
    main = pl.BlockSpec((tm, w), lambda i: (i, cblk))
    top = pl.BlockSpec((HALO, w), lambda i: (jnp.maximum(i * hb - 1, 0), cblk))
    bot = pl.BlockSpec((HALO, w), lambda i: (jnp.minimum((i + 1) * hb, last), cblk))
    return [main, top, bot]


def _fill_pad(pad_ref, main, top, bot, top_ok, bot_ok, tm):
    pad_ref[0:HALO, :] = jnp.where(top_ok, top, 0.0)
    pad_ref[HALO:HALO + tm, :] = main
    pad_ref[HALO + tm:2 * HALO + tm, :] = jnp.where(bot_ok, bot, 0.0)


def _edges(i, nctx, nr):
    top_ok = jnp.logical_and(i != 0, i != nctx)
    bot_ok = jnp.logical_and(i != nctx - 1, i != nr - 1)
    return top_ok, bot_ok


def _rot_fill(rot_ref, pad_ref, n):
    rot_ref[0, 0:n, :] = pad_ref[0:n, :]
    for b in range(1, 8):
        rot_ref[b, 0:n, :] = pad_ref[pl.ds(b, n), :]


def _tap(rot_ref, off, tm):
    return rot_ref[off % 8, pl.ds(off - off % 8, tm), :]


def _rows8(x):
    return jnp.sum(x.reshape(x.shape[0] // 8, 8, x.shape[1]), axis=0)


def _conv_fwd(p, caw, cab, nag, nab, cbw, t_ctx, name):
    r = p.shape[0]
    w = p.shape[1] // 7
    ka, kb = caw.shape[0], cbw.shape[0]
    tm = _row_tile(t_ctx, 128)
    nctx, nr = t_ctx // tm, r // tm
    n = tm + 2 * HALO - 8

    def body(av, avt, avb, ag, agt, agb, agate, bx, bxt, bxb, bb, bc, bct, bcb, bgate,
             caw_ref, cab_ref, nag_ref, nab_ref, cbw_ref, cat_ref, u1_ref, pad, rot, v_s):
        top_ok, bot_ok = _edges(pl.program_id(0), nctx, nr)

        def chunk(c, carry):
            cols = pl.ds(pl.multiple_of(c * LANES, LANES), LANES)
            _fill_pad(pad, av[:, cols] * _sigmoid(ag[:, cols]), avt[:, cols] * _sigmoid(agt[:, cols]),
                      avb[:, cols] * _sigmoid(agb[:, cols]), top_ok, bot_ok, tm)
            _rot_fill(rot, pad, n)
            u1 = jnp.zeros((tm, LANES), F32) + cab_ref[:, cols]
            for k in range(ka):
                u1 = u1 + caw_ref[k:k + 1, cols] * _tap(rot, HALO - ka // 2 + k, tm)
            u1_ref[:, cols] = u1
            _fill_pad(pad, bc[:, cols] * bx[:, cols], bct[:, cols] * bxt[:, cols], bcb[:, cols] * bxb[:, cols],
                      top_ok, bot_ok, tm)
            v = jnp.zeros((tm, LANES), F32)
            for k in range(kb):
                v = v + cbw_ref[k:k + 1, cols] * pad[pl.ds(HALO - kb // 2 + k, tm), :]
            v_s[:, cols] = v
            return carry

        lax.fori_loop(0, w // LANES, chunk, 0)
        nrm, _ = _ln_norm(u1_ref[...])
        a_out = _silu(nrm * nag_ref[...] + nab_ref[...]) * _silu(agate[...])
        cat_ref[:, 0:w] = a_out.astype(BF16)
        cat_ref[:, w:2 * w] = (bb[...] * v_s[...] * _silu(bgate[...])).astype(BF16)

    def main(cblk):
        return pl.BlockSpec((tm, w), lambda i: (i, cblk))

    def whole(a):
        return pl.BlockSpec(a.shape, lambda i: (0, 0))

    in_specs = (_conv_specs(r, w, tm, 0) + _conv_specs(r, w, tm, 1) + [main(2)] + _conv_specs(r, w, tm, 3)
                + [main(4)] + _conv_specs(r, w, tm, 5) + [main(6)]
                + [whole(caw), whole(cab), whole(nag), whole(nab), whole(cbw)])
    return pl.pallas_call(
        body, name=name, grid=(nr,), in_specs=in_specs,
        out_specs=[pl.BlockSpec((tm, 2 * w), lambda i: (i, 0)), pl.BlockSpec((tm, w), lambda i: (i, 0))],
        out_shape=[jax.ShapeDtypeStruct((r, 2 * w), BF16), jax.ShapeDtypeStruct((r, w), F32)],
        scratch_shapes=[pltpu.VMEM((tm + 2 * HALO, LANES), F32), pltpu.VMEM((8, n, LANES), F32),
                        pltpu.VMEM((tm, w), F32)],
        compiler_params=_cp(1),
    )(*([p] * 15), caw, cab, nag, nab, cbw)


def _conv_bwd1(dcat, p, u1, nag, nab, t_ctx, name):
    r, w = u1.shape
    tm = _row_tile(t_ctx, 128)

    def body(da_ref, agate_ref, u1_ref, nag_ref, nab_ref, du1_ref, dag_ref, s_ref):
        @pl.when(pl.program_id(0) == 0)
        def _():
            s_ref[...] = jnp.zeros_like(s_ref)

        n, rstd = _ln_norm(u1_ref[...])
        g = nag_ref[...]
        u2 = n * g + nab_ref[...]
        da = da_ref[...]
        ag = agate_ref[...]
        dag_ref[...] = (da * _silu(u2) * _dsilu(ag)).astype(BF16)
        du2 = da * _silu(ag) * _dsilu(u2)
        du1 = _ln_bwd(du2 * g, n, rstd)
        du1_ref[...] = du1
        s_ref[0:1, :] += jnp.sum(du2 * n, axis=0, keepdims=True)
        s_ref[1:2, :] += jnp.sum(du2, axis=0, keepdims=True)
        s_ref[2:3, :] += jnp.sum(du1, axis=0, keepdims=True)

    def win(cblk):
        return pl.BlockSpec((tm, w), lambda i: (i, cblk))

    one = pl.BlockSpec((1, w), lambda i: (0, 0))
    return pl.pallas_call(
        body, name=name, grid=(r // tm,), in_specs=[win(0), win(2), win(0), one, one],
        out_specs=[win(0), win(0), pl.BlockSpec((8, w), lambda i: (0, 0))],
        out_shape=[jax.ShapeDtypeStruct((r, w), F32), jax.ShapeDtypeStruct((r, w), BF16),
                   jax.ShapeDtypeStruct((8, w), F32)],
        compiler_params=_cp(1),
    )(dcat, p, u1, nag, nab)


def _conv_bwd2(du1, dcat, p, dag, caw, cbw, t_ctx, name):
    r, w = du1.shape
    ka, kb = caw.shape[0], cbw.shape[0]
    tm = _row_tile(t_ctx, 128)
    nctx, nr = t_ctx // tm, r // tm
    n = tm + 2 * HALO - 8

    def body(du, dut, dub, av, avt, avb, ag, agt, agb, db, dbt, dbb, bx, bxt, bxb, bb, bbt, bbb,
             bc, bct, bcb, bg, bgt, bgb, dag_ref, caw_ref, cbw_ref, dp_ref, dcaw_ref, dcbw_ref,
             pad, pad2, rot_u, rot_d, acc_a, acc_b):
        i = pl.program_id(0)
        top_ok, bot_ok = _edges(i, nctx, nr)

        @pl.when(i == 0)
        def _():
            acc_a[...] = jnp.zeros_like(acc_a)
            acc_b[...] = jnp.zeros_like(acc_b)

        def chunk(c, carry):
            c0 = pl.multiple_of(c * LANES, LANES)
            cols = pl.ds(c0, LANES)

            def seg(s):
                return pl.ds(pl.multiple_of(s * w + c0, LANES), LANES)

            sg = _sigmoid(ag[:, cols])
            av_m = av[:, cols]
            _fill_pad(pad, av_m * sg, avt[:, cols] * _sigmoid(agt[:, cols]), avb[:, cols] * _sigmoid(agb[:, cols]),
                      top_ok, bot_ok, tm)
            _rot_fill(rot_u, pad, n)
            du_m = du[:, cols]
            _fill_pad(pad, du_m, dut[:, cols], dub[:, cols], top_ok, bot_ok, tm)
            _rot_fill(rot_d, pad, n)
            du0 = jnp.zeros((tm, LANES), F32)
            for k in range(ka):
                du0 = du0 + caw_ref[k:k + 1, cols] * _tap(rot_d, HALO + ka // 2 - k, tm)
                acc_a[k, :, cols] += _rows8(du_m * _tap(rot_u, HALO - ka // 2 + k, tm))
            dp_ref[:, seg(0)] = (du0 * sg).astype(BF16)
            dp_ref[:, seg(1)] = (du0 * av_m * sg * (1.0 - sg)).astype(BF16)
            dp_ref[:, seg(2)] = dag_ref[:, cols]

            bc_m, bx_m = bc[:, cols], bx[:, cols]
            _fill_pad(pad, bc_m * bx_m, bct[:, cols] * bxt[:, cols], bcb[:, cols] * bxb[:, cols],
                      top_ok, bot_ok, tm)
            v = jnp.zeros((tm, LANES), F32)
            for k in range(kb):
                v = v + cbw_ref[k:k + 1, cols] * pad[pl.ds(HALO - kb // 2 + k, tm), :]
            db_m, bb_m, bg_m = db[:, cols], bb[:, cols], bg[:, cols]
            sbg = _silu(bg_m)
            dv_m = db_m * bb_m * sbg
            _fill_pad(pad2, dv_m, dbt[:, cols] * bbt[:, cols] * _silu(bgt[:, cols]),
                      dbb[:, cols] * bbb[:, cols] * _silu(bgb[:, cols]), top_ok, bot_ok, tm)
            dw0 = jnp.zeros((tm, LANES), F32)
            for k in range(kb):
                dw0 = dw0 + cbw_ref[k:k + 1, cols] * pad2[pl.ds(HALO + kb // 2 - k, tm), :]
                acc_b[k, :, cols] += _rows8(dv_m * pad[pl.ds(HALO - kb // 2 + k, tm), :])
            dp_ref[:, seg(3)] = (dw0 * bc_m).astype(BF16)
            dp_ref[:, seg(4)] = (db_m * v * sbg).astype(BF16)
            dp_ref[:, seg(5)] = (dw0 * bx_m).astype(BF16)
            dp_ref[:, seg(6)] = (db_m * bb_m * v * _dsilu(bg_m)).astype(BF16)
            return carry

        lax.fori_loop(0, w // LANES, chunk, 0)

        @pl.when(i == nr - 1)
        def _():
            dcaw_ref[...] = jnp.zeros_like(dcaw_ref)
            dcbw_ref[...] = jnp.zeros_like(dcbw_ref)
            for k in range(ka):
                dcaw_ref[k:k + 1, :] = jnp.sum(acc_a[k], axis=0, keepdims=True)
            for k in range(kb):
                dcbw_ref[k:k + 1, :] = jnp.sum(acc_b[k], axis=0, keepdims=True)

    def whole(a):
        return pl.BlockSpec(a.shape, lambda i: (0, 0))

    in_specs = (_conv_specs(r, w, tm, 0) + _conv_specs(r, w, tm, 0) + _conv_specs(r, w, tm, 1)
                + _conv_specs(r, w, tm, 1) + _conv_specs(r, w, tm, 3) + _conv_specs(r, w, tm, 4)
                + _conv_specs(r, w, tm, 5) + _conv_specs(r, w, tm, 6)
                + [pl.BlockSpec((tm, w), lambda i: (i, 0)), whole(caw), whole(cbw)])
    pad_t = pltpu.VMEM((tm + 2 * HALO, LANES), F32)
    rot_t = pltpu.VMEM((8, n, LANES), F32)
    return pl.pallas_call(
        body, name=name, grid=(nr,), in_specs=in_specs,
        out_specs=[pl.BlockSpec((tm, 7 * w), lambda i: (i, 0)), pl.BlockSpec((32, w), lambda i: (0, 0)),
                   pl.BlockSpec((8, w), lambda i: (0, 0))],
        out_shape=[jax.ShapeDtypeStruct((r, 7 * w), BF16), jax.ShapeDtypeStruct((32, w), F32),
                   jax.ShapeDtypeStruct((8, w), F32)],
        scratch_shapes=[pad_t, pad_t, rot_t, rot_t, pltpu.VMEM((32, 8, w), F32), pltpu.VMEM((8, 8, w), F32)],
        compiler_params=_cp(1),
    )(*([du1] * 3), *([p] * 6), *([dcat] * 3), *([p] * 12), dag, caw, cbw)


def _rms(xh):
    r = lax.rsqrt(jnp.mean(xh * xh, axis=-1, keepdims=True) + RMS_EPS)
    return xh * r, r


def _qk_fwd(p, cos, sin, qg, kg, att, kv, t_ctx, name):
    r = p.shape[0]
    tm = _row_tile(t_ctx)

    def body(q_ref, k_ref, v_ref, cos_ref, sin_ref, qg_ref, kg_ref, qr_ref, kr_ref, vb_ref):
        cs, sn = cos_ref[...], sin_ref[...]
        for src, g_ref, dst, nh, fac in ((q_ref, qg_ref, qr_ref, att // HEAD_DIM, 1.0),
                                         (k_ref, kg_ref, kr_ref, kv // HEAD_DIM, ATTN_SCALE_LOG2)):
            for h in range(nh):
                cols = slice(h * HEAD_DIM, (h + 1) * HEAD_DIM)
                n, _ = _rms(src[:, cols])
                n = n * g_ref[...]
                rot = n * cs + _partner(n) * sn
                dst[:, cols] = (rot if fac == 1.0 else rot * fac).astype(BF16)
        vb_ref[...] = v_ref[...].astype(BF16)

    def rows(width, cblk):
        return pl.BlockSpec((tm, width), lambda i: (i, cblk))

    one = pl.BlockSpec((1, HEAD_DIM), lambda i: (0, 0))
    return pl.pallas_call(
        body, name=name, grid=(r // tm,),
        in_specs=[rows(att, 0), rows(kv, 2 * att // kv), rows(kv, 2 * att // kv + 1),
                  rows(HEAD_DIM, 0), rows(HEAD_DIM, 0), one, one],
        out_specs=[rows(att, 0), rows(kv, 0), rows(kv, 0)],
        out_shape=[jax.ShapeDtypeStruct((r, att), BF16), jax.ShapeDtypeStruct((r, kv), BF16),
                   jax.ShapeDtypeStruct((r, kv), BF16)],
        compiler_params=_cp(1),
    )(p, p, p, cos, sin, qg, kg)


def _qk_bwd(p, dqr, dkr, cos, sin, qg, kg, att, kv, t_ctx, name):
    r = p.shape[0]
    tm = _row_tile(t_ctx)

    def body(q_ref, k_ref, dqr_ref, dkr_ref, cos_ref, sin_ref, qg_ref, kg_ref, dq_ref, dk_ref, s_ref):
        @pl.when(pl.program_id(0) == 0)
        def _():
            s_ref[...] = jnp.zeros_like(s_ref)

        cs, sn = cos_ref[...], sin_ref[...]
        for row, (src, dsrc, g_ref, dst, nh) in enumerate((
                (q_ref, dqr_ref, qg_ref, dq_ref, att // HEAD_DIM), (k_ref, dkr_ref, kg_ref, dk_ref, kv // HEAD_DIM))):
            dg = jnp.zeros((1, HEAD_DIM), F32)
            for h in range(nh):
                cols = slice(h * HEAD_DIM, (h + 1) * HEAD_DIM)
                n0, rr = _rms(src[:, cols])
                d = dsrc[:, cols]
                dng = d * cs + _partner(d * sn)
                dg = dg + jnp.sum(dng * n0, axis=0, keepdims=True)
                dn0 = dng * g_ref[...]
                dst[:, cols] = (rr * (dn0 - n0 * jnp.mean(dn0 * n0, axis=-1, keepdims=True))).astype(BF16)
            s_ref[row:row + 1, :] += dg

    def rows(width, cblk):
        return pl.BlockSpec((tm, width), lambda i: (i, cblk))

    one = pl.BlockSpec((1, HEAD_DIM), lambda i: (0, 0))
    return pl.pallas_call(
        body, name=name, grid=(r // tm,),
        in_specs=[rows(att, 0), rows(kv, 2 * att // kv), rows(att, 0), rows(kv, 0),
                  rows(HEAD_DIM, 0), rows(HEAD_DIM, 0), one, one],
        out_specs=[rows(att, 0), rows(kv, 0), pl.BlockSpec((8, HEAD_DIM), lambda i: (0, 0))],
        out_shape=[jax.ShapeDtypeStruct((r, att), BF16), jax.ShapeDtypeStruct((r, kv), BF16),
                   jax.ShapeDtypeStruct((8, HEAD_DIM), F32)],
        compiler_params=_cp(1),
    )(p, p, dqr, dkr, cos, sin, qg, kg)


def _stack_heads(x, tq):
    return jnp.concatenate([x[:, g * HEAD_DIM:(g + 1) * HEAD_DIM] for g in range(GQA_GROUP)], axis=0)


def _attn_tiles(t_ctx, s_lat, tkl_pref):
    return _div(t_ctx, 256, 8), _div(s_lat, tkl_pref, 8)


NT_DIMS = (((1,), (1,)), ((), ()))
TN_DIMS = (((0,), (0,)), ((), ()))
ATTN_SCALE_LOG2 = ATTN_SCALE * math.log2(math.e)


def _flash_fwd(qr, kr, vb, p, att, t_ctx, name, side=()):
    r = qr.shape[0]
    s_lat = r - t_ctx
    gw = GQA_GROUP * HEAD_DIM
    nkv = att // gw
    tq = _div(t_ctx, 256, 8)
    tkl = _div(s_lat // 2, 1024, 8)
    nq, nq_ctx, n_lat = r // tq, t_ctx // tq, s_lat // tkl
    nl = GQA_GROUP * tq
    align = math.gcd(t_ctx, tkl)
    nj = len(side)
    s_ins, s_in_specs, s_out_shape, s_out_specs, s_sems = _side_plan(side)

    def body(q_ref, k_ref, v_ref, g_ref, *rest):
        s_in, (o_ref, og_ref, lse_ref), s_out = rest[:nj], rest[nj:nj + 3], rest[nj + 3:2 * nj + 3]
        (m_s, l_s, acc_s, st_a, st_b), s_sem = rest[2 * nj + 3:2 * nj + 8], rest[2 * nj + 8:]
        start, finish = _side_run(side, s_in, s_out, s_sem, *_grid_ends((nkv, nq)))
        start()
        qi = pl.program_id(1)
        q4 = _stack_heads(q_ref[...], tq)
        m_s[...] = jnp.full_like(m_s, -1e30)
        l_s[...] = jnp.zeros_like(l_s)
        acc_s[...] = jnp.zeros_like(acc_s)

        def scores(off, size):
            return lax.dot_general(k_ref[pl.ds(off, size), :], q4, NT_DIMS, preferred_element_type=F32)

        def update(st, off, size):
            m_old = m_s[...]
            m_new = jnp.maximum(m_old, jnp.max(st, axis=0, keepdims=True))
            pe = jnp.exp2(st - m_new)
            a = jnp.exp2(m_old - m_new)
            l_s[...] = a * l_s[...] + jnp.sum(pe, axis=0, keepdims=True)
            acc_s[...] = a * acc_s[...] + lax.dot_general(v_ref[pl.ds(off, size), :], pe.astype(BF16), TN_DIMS,
                                                          preferred_element_type=F32)
            m_s[...] = m_new

        def lat_off(n):
            return pl.multiple_of(t_ctx + jnp.minimum(n, n_lat - 1) * tkl, align)

        is_lat = qi >= nq_ctx

        @pl.when(is_lat)
        def _():
            st_a[...] = scores(lat_off(0), tkl)

        update(scores(0, t_ctx), 0, t_ctx)

        def pair(j, carry):
            n = 2 * j
            st_b[...] = scores(lat_off(n + 1), tkl)
            update(st_a[...], lat_off(n), tkl)
            st_a[...] = scores(lat_off(n + 2), tkl)
            update(st_b[...], lat_off(n + 1), tkl)
            return carry

        lax.fori_loop(0, jnp.where(is_lat, n_lat // 2, 0), pair, 0)
        o4 = (acc_s[...] / l_s[...]).T
        lse_ref[...] = m_s[...] + jnp.log2(l_s[...])
        sg = _silu(g_ref[...])
        for g in range(GQA_GROUP):
            cols = slice(g * HEAD_DIM, (g + 1) * HEAD_DIM)
            og = o4[g * tq:(g + 1) * tq, :]
            o_ref[:, cols] = og
            og_ref[:, cols] = (og * sg[:, cols]).astype(BF16)
        finish()

    qspec = pl.BlockSpec((tq, gw), lambda h, i: (i, h))
    kspec = pl.BlockSpec((r, HEAD_DIM), lambda h, i: (0, h))
    return pl.pallas_call(
        body, name=name, grid=(nkv, nq),
        in_specs=[qspec, kspec, kspec, pl.BlockSpec((tq, gw), lambda h, i: (i, att // gw + h))] + s_in_specs,
        out_specs=[qspec, qspec, pl.BlockSpec((None, None, 1, nl), lambda h, i: (h, i, 0, 0))] + s_out_specs,
        out_shape=[jax.ShapeDtypeStruct((r, att), F32), jax.ShapeDtypeStruct((r, att), BF16),
                   jax.ShapeDtypeStruct((nkv, nq, 1, nl), F32)] + s_out_shape,
        scratch_shapes=[pltpu.VMEM((1, nl), F32), pltpu.VMEM((1, nl), F32), pltpu.VMEM((HEAD_DIM, nl), F32),
                        pltpu.VMEM((tkl, nl), F32), pltpu.VMEM((tkl, nl), F32)] + s_sems,
        compiler_params=_cp(2),
    )(qr, kr, vb, p, *s_ins)


def _flash_bwd(qr, kr, vb, p, o, dog, lse, att, kv, t_ctx, name, side=()):
    r = qr.shape[0]
    s_lat = r - t_ctx
    gw = GQA_GROUP * HEAD_DIM
    nkv = att // gw
    tq, tkl = _attn_tiles(t_ctx, s_lat, 1024)
    nq, nq_ctx, n_lat = r // tq, t_ctx // tq, s_lat // tkl
    nl = GQA_GROUP * tq
    nj = len(side)
    s_ins, s_in_specs, s_out_shape, s_out_specs, s_sems = _side_plan(side)

    def body(q_ref, k_ref, v_ref, g_ref, o_ref, dog_ref, lse_ref, *rest):
        s_in, (dq_ref, dgate_ref, dk_ref, dv_ref), s_out = rest[:nj], rest[nj:nj + 4], rest[nj + 4:2 * nj + 4]
        dq_s, s_sem = rest[2 * nj + 4], rest[2 * nj + 5:]
        start, finish = _side_run(side, s_in, s_out, s_sem, *_grid_ends((nkv, nq)))
        start()
        qi = pl.program_id(1)

        @pl.when(qi == 0)
        def _():
            dk_ref[...] = jnp.zeros_like(dk_ref)
            dv_ref[...] = jnp.zeros_like(dv_ref)

        gate, ov, dogv = g_ref[...], o_ref[...], dog_ref[...]
        dgate_ref[...] = (dogv * ov * _dsilu(gate)).astype(BF16)
        do = dogv * _silu(gate)
        do4 = _stack_heads(do, tq)
        delta = jnp.sum((do4 * _stack_heads(ov, tq)).T, axis=0, keepdims=True)
        do4 = do4.astype(BF16)
        q4 = _stack_heads(q_ref[...], tq)
        lse_v = lse_ref[...]
        dq_s[...] = jnp.zeros_like(dq_s)

        def step(off, size):
            kb = k_ref[pl.ds(off, size), :]
            vv = v_ref[pl.ds(off, size), :]
            st = lax.dot_general(kb, q4, NT_DIMS, preferred_element_type=F32)
            pe = jnp.exp2(st - lse_v)
            dp = lax.dot_general(vv, do4, NT_DIMS, preferred_element_type=F32)
            ds = (pe * (dp - delta) * ATTN_SCALE).astype(BF16)
            dv_ref[pl.ds(off, size), :] += jnp.dot(pe.astype(BF16), do4, preferred_element_type=F32)
            dk_ref[pl.ds(off, size), :] += jnp.dot(ds, q4, preferred_element_type=F32)
            dq_s[...] += lax.dot_general(kb, ds, TN_DIMS, preferred_element_type=F32)

        step(0, t_ctx)

        def lat(n, carry):
            step(pl.multiple_of(t_ctx + n * tkl, math.gcd(t_ctx, tkl)), tkl)
            return carry

        lax.fori_loop(0, jnp.where(qi < nq_ctx, 0, n_lat), lat, 0)
        dq4 = dq_s[...].T * (1.0 / ATTN_SCALE_LOG2)
        for g in range(GQA_GROUP):
            dq_ref[:, g * HEAD_DIM:(g + 1) * HEAD_DIM] = dq4[g * tq:(g + 1) * tq, :]
        finish()

    qspec = pl.BlockSpec((tq, gw), lambda h, i: (i, h))
    kspec = pl.BlockSpec((r, HEAD_DIM), lambda h, i: (0, h))
    return pl.pallas_call(
        body, name=name, grid=(nkv, nq),
        in_specs=[qspec, kspec, kspec, pl.BlockSpec((tq, gw), lambda h, i: (i, att // gw + h)), qspec, qspec,
                  pl.BlockSpec((None, None, 1, nl), lambda h, i: (h, i, 0, 0))] + s_in_specs,
        out_specs=[qspec, qspec, kspec, kspec] + s_out_specs,
        out_shape=[jax.ShapeDtypeStruct((r, att), F32), jax.ShapeDtypeStruct((r, att), BF16),
                   jax.ShapeDtypeStruct((r, kv), F32), jax.ShapeDtypeStruct((r, kv), F32)] + s_out_shape,
        scratch_shapes=[pltpu.VMEM((HEAD_DIM, nl), F32)] + s_sems,
        compiler_params=_cp(2),
    )(qr, kr, vb, p, o, dog, lse, *s_ins)


def _rope_tables(t_ctx, s_lat):
    rows_n = s_lat // GRID_W
    row = jnp.repeat(jnp.arange(rows_n, dtype=F32), GRID_W)
    col = jnp.tile(jnp.arange(GRID_W, dtype=F32), rows_n)
    axis_dim = HEAD_DIM // 2
    inv_freq = ROPE_THETA ** (-jnp.arange(0, axis_dim, 2, dtype=F32) / axis_dim)
    ang_r = row[:, None] * inv_freq[None, :]
    ang_c = col[:, None] * inv_freq[None, :]
    cos = jnp.concatenate([jnp.cos(ang_r), jnp.cos(ang_r), jnp.cos(ang_c), jnp.cos(ang_c)], axis=1)
    sin = jnp.concatenate([-jnp.sin(ang_r), jnp.sin(ang_r), -jnp.sin(ang_c), jnp.sin(ang_c)], axis=1)
    cos = jnp.concatenate([jnp.ones((t_ctx, HEAD_DIM), F32), cos], axis=0)
    sin = jnp.concatenate([jnp.zeros((t_ctx, HEAD_DIM), F32), sin], axis=0)
    return cos, sin


def _pad_rows(a, rows):
    return jnp.pad(a, ((0, rows - a.shape[0]), (0, 0)))


def kernel(x, c, ctx, c_ctx, w_mod, b_mod, post_ln_g, post_ln_b, w_in_e, conv_a_w, conv_a_b, norm_a_g, norm_a_b, conv_b_w, w_out_e, w_in_o, q_norm_g, k_norm_g, w_out_o, loss_target, m_c_ctx, m_w_mod, m_b_mod, m_post_ln_g, m_post_ln_b, m_w_in_e, m_conv_a_w, m_conv_a_b, m_norm_a_g, m_norm_a_b, m_conv_b_w, m_w_out_e, m_w_in_o, m_q_norm_g, m_k_norm_g, m_w_out_o, v_c_ctx, v_w_mod, v_b_mod, v_post_ln_g, v_post_ln_b, v_w_in_e, v_conv_a_w, v_conv_a_b, v_norm_a_g, v_norm_a_b, v_conv_b_w, v_w_out_e, v_w_in_o, v_q_norm_g, v_k_norm_g, v_w_out_o):
    depth, d, mcols = w_mod.shape
    s_lat, t_ctx = x.shape[1], ctx.shape[1]
    n_even, n_odd = w_in_e.shape[0], w_in_o.shape[0]
    ka, kb = conv_a_w.shape[1], conv_b_w.shape[1]
    wch = conv_a_w.shape[2] * NDEV
    att = w_out_o.shape[1] * NDEV
    kv = (w_in_o.shape[2] * NDEV - 2 * att) // 2
    alpha = (2.0 * depth) ** 0.25
    me = 4 * lax.axis_index("x") + 2 * lax.axis_index("y") + lax.axis_index("c")

    c_all = _all_gather(_pad_rows(c, 8), "ag_c")[:, 0, :]
    c16 = jnp.concatenate([c_all, _pad_rows(c_ctx[None, :], 8)], axis=0)
    sc16 = _ew(_silu, BF16, "silu_c", c16)
    m_part = _mm(sc16, w_mod, "nn", F32, "mod_fwd", tm=16, tn=mcols, tk=d)
    m_all = _all_gather(m_part, "ag_mod")
    m_full = m_all.reshape(NDEV, 16, depth, mcols).transpose(2, 1, 0, 3).reshape(depth * 16, 3 * d)
    b16 = jnp.broadcast_to(b_mod[:, None, :], (depth, 16, 3 * d)).reshape(depth * 16, 3 * d)
    m_full = _ew(lambda a, b: a + b, F32, "mod_bias", m_full, b16).reshape(depth, 16, 3, d)
    modv = [jnp.stack([m_full[l, 8], lax.dynamic_index_in_dim(m_full[l], me, 0, keepdims=False)])
            for l in range(depth)]

    cw = jnp.concatenate([_pad_rows(conv_a_w[i], 32) for i in range(n_even)]
                         + [_pad_rows(conv_b_w[i], 8) for i in range(n_even)], axis=0)
    cw_all = _all_gather(cw, "ag_convw").transpose(1, 0, 2).reshape(cw.shape[0], wch)
    caw = [cw_all[32 * i:32 * i + ka] for i in range(n_even)]
    cbw = [cw_all[32 * n_even + 8 * i:32 * n_even + 8 * i + kb] for i in range(n_even)]

    sh_in = [_ew(lambda t: t, BF16, f"w_in{l}_bf16", (w_in_e if l % 2 == 0 else w_in_o)[l // 2]) for l in range(depth)]
    sh_out = [_ew(lambda t: t, BF16, f"w_out{l}_bf16", (w_out_e if l % 2 == 0 else w_out_o)[l // 2])
              for l in range(depth)]
    g_in = [None] * depth
    g_out = [None] * depth
    g_in[0] = _all_gather(sh_in[0], "ag_w_in0")

    def wanted(keys):
        return [(kind, l) for kind, l in keys if l < depth and (g_in if kind == "in" else g_out)[l] is None]

    def gather_jobs(keys):
        return [("gather", (sh_in if kind == "in" else sh_out)[l]) for kind, l in keys]

    def hosted(res, keys):
        if not keys:
            return res
        n_own = len(res) - len(keys)
        for (kind, l), g in zip(keys, res[n_own:]):
            if kind == "in":
                g_in[l] = g
            else:
                g_out[l] = g.reshape(1, NDEV * g.shape[1], d)
        return res[0] if n_own == 1 else res[:n_own]

    cos, sin = _rope_tables(t_ctx, s_lat)
    lnp = [jnp.stack([post_ln_g[l], post_ln_b[l]]) for l in range(depth)]

    xb = jnp.concatenate([ctx[0], x[0]], axis=0)
    saved = []
    h = _modulate(xb, modv[0], t_ctx, "modulate0")
    for l in range(depth):
        i = l // 2
        if l % 2 == 0:
            keys = wanted([("out", l), ("in", l + 1)])
            p = hosted(_mm(h, g_in[l], "nn", F32, f"in_proj{l}", side=gather_jobs(keys)), keys)
            cat, u1 = _conv_fwd(p, caw[i], conv_a_b[i][None], norm_a_g[i][None], norm_a_b[i][None], cbw[i],
                                t_ctx, f"conv_fwd{l}")
            keys = wanted([("out", l + 1)])
            y = hosted(_mm(cat, g_out[l], "nn", F32, f"out_proj{l}", side=gather_jobs(keys)), keys)
            saved.append((xb, h, p, cat, u1, y))
        else:
            keys = wanted([("out", l)])
            p = hosted(_mm(h, g_in[l], "nn", F32, f"in_proj{l}", side=gather_jobs(keys)), keys)
            qr, kr, vb = _qk_fwd(p, cos, sin, q_norm_g[i][None], k_norm_g[i][None], att, kv, t_ctx, f"qk_fwd{l}")
            keys = wanted([("in", l + 1), ("out", l + 1)])
            o, og, lse = hosted(_flash_fwd(qr, kr, vb, p, att, t_ctx, f"flash_fwd{l}", side=gather_jobs(keys)), keys)
            y = _mm(og, g_out[l], "nn", F32, f"out_proj{l}")
            saved.append((xb, h, p, qr, kr, vb, o, og, lse, y))
        if l + 1 < depth:
            xb, h = _postln_fwd(xb, y, modv[l], lnp[l], alpha, modv[l + 1], t_ctx, f"postln_fwd{l}")

    small = {}
    dmod = [None] * depth
    recv_in, recv_out = [None] * depth, [None] * depth
    carried = []
    for l in reversed(range(depth)):
        i = l // 2
        sv = saved[l]
        x_in, h, p, y = sv[0], sv[1], sv[2], sv[-1]
        if l == depth - 1:
            loss_blk, dy, dxa, s_ln = _postln_bwd(loss_target[0], x_in, y, modv[l], lnp[l], alpha, t_ctx,
                                                  f"loss_postln_bwd{l}", from_loss=True)
            loss = lax.psum(loss_blk[0, 0], ("x", "y", "c"))
        else:
            dy, dxa, s_ln = _postln_bwd(dxb, x_in, y, modv[l], lnp[l], alpha, t_ctx, f"postln_bwd{l}")
        small[f"ln{l}"] = s_ln
        mixed = sv[3] if l % 2 == 0 else sv[7]
        k_out = g_out[l].shape[1]
        part = _mm(mixed, dy, "tn", BF16, f"d_w_out{l}", tm=1024, tk=768).reshape(NDEV, k_out // NDEV, d)
        dmixed, recv_out[l] = _mm(dy, g_out[l], "nt", F32, f"d_mixed{l}", side=[("exchange", part)])
        if l % 2 == 0:
            u1 = sv[4]
            du1, dag, s_c1 = _conv_bwd1(dmixed, p, u1, norm_a_g[i][None], norm_a_b[i][None], t_ctx,
                                        f"conv_bwd1_{l}")
            dp, dcaw, dcbw = _conv_bwd2(du1, dmixed, p, dag, caw[i], cbw[i], t_ctx, f"conv_bwd2_{l}")
            small[f"c1_{i}"], small[f"caw{i}"], small[f"cbw{i}"] = s_c1, dcaw, dcbw
        else:
            qr, kr, vb, o, _, lse = sv[3:9]
            dqr, dgate, dkr, dvr, *got = _flash_bwd(qr, kr, vb, p, o, dmixed, lse, att, kv, t_ctx, f"flash_bwd{l}",
                                                    side=carried)
            if carried:
                recv_in[l + 1], carried = got[0], []
            dq, dk, s_qk = _qk_bwd(p, dqr, dkr, cos, sin, q_norm_g[i][None], k_norm_g[i][None], att, kv, t_ctx,
                                   f"qk_bwd{l}")
            small[f"qk{i}"] = jnp.pad(s_qk, ((0, 0), (0, d - HEAD_DIM)))
            dp = jnp.concatenate([dq, dgate, dk, dvr.astype(BF16)], axis=1)
        part = _mm(h, dp, "tn", BF16, f"d_w_in{l}", tm=d, tk=768, out_nd=NDEV, side=carried)
        if carried:
            (part, recv_in[l + 1]), carried = part, []
        last_jobs = [("exchange", part)] if l == 0 else []
        dxb, s_mod, *got = _mm(dp, g_in[l], "nt", F32, f"d_h{l}", tm=384, side=last_jobs,
                               mod_bwd=(x_in, dxa, modv[l], t_ctx))
        if l == 0:
            recv_in[0] = got[0]
        else:
            carried = [("exchange", part)]
        dmod[l] = jnp.stack([jnp.stack([s_mod[0], s_mod[1], s_ln[2]]), jnp.stack([s_mod[2], s_mod[3], s_ln[3]])])
    grad_x = dxb[t_ctx:][None]

    dm_loc = jnp.stack(dmod).reshape(depth * 2, 3 * d)
    dm_all = _all_gather(dm_loc, "ag_dmod").reshape(NDEV, depth, 2, 3 * d)
    dm_ctx = _sum_lead(dm_all[:, :, 0, :], "sum_dmod_ctx")
    dm16 = jnp.concatenate([dm_all[:, :, 1, :].transpose(1, 0, 2), dm_ctx[:, None, :],
                            jnp.zeros((depth, 7, 3 * d), F32)], axis=1)
    g_b_mod = _sum_lead(dm16.transpose(1, 0, 2), "sum_b_mod")
    dm16_me = lax.dynamic_slice_in_dim(dm16.reshape(depth, 16, NDEV, mcols), me, 1, axis=2)
    dm16_me = dm16_me.reshape(depth, 16, mcols).transpose(1, 0, 2).reshape(16, depth * mcols)
    g_w_mod = _mm(sc16, dm16_me, "tn", F32, "mod_bwd_w", tm=d, tn=mcols, tk=16, out_nd=depth)
    dsc16 = _mm(dm16_me, w_mod, "nt", F32, "mod_bwd_c", tm=16, tn=d, tk=mcols)
    small["c_ctx"] = dsc16[8:16]

    names = sorted(small)
    offs, rows = {}, 0
    for nme in names:
        offs[nme] = rows
        rows += small[nme].shape[0]
    sm_all = _all_gather(jnp.concatenate([small[nme] for nme in names], axis=0), "ag_small")
    sm = _sum_lead(sm_all, "sum_small")

    def part(nme, lo, hi):
        return sm[offs[nme] + lo:offs[nme] + hi]

    g_post_ln_g = jnp.concatenate([part(f"ln{l}", 0, 1) for l in range(depth)], axis=0)
    g_post_ln_b = jnp.concatenate([part(f"ln{l}", 1, 2) for l in range(depth)], axis=0)
    g_norm_a_g = jnp.concatenate([part(f"c1_{i}", 0, 1) for i in range(n_even)], axis=0)
    g_norm_a_b = jnp.concatenate([part(f"c1_{i}", 1, 2) for i in range(n_even)], axis=0)
    g_conv_a_b = jnp.concatenate([part(f"c1_{i}", 2, 3) for i in range(n_even)], axis=0)
    g_q_norm_g = jnp.concatenate([part(f"qk{i}", 0, 1)[:, :HEAD_DIM] for i in range(n_odd)], axis=0)
    g_k_norm_g = jnp.concatenate([part(f"qk{i}", 1, 2)[:, :HEAD_DIM] for i in range(n_odd)], axis=0)
    wsh = wch // NDEV
    g_conv_a_w = jnp.stack([lax.dynamic_slice_in_dim(part(f"caw{i}", 0, ka), me * wsh, wsh, axis=1)
                            for i in range(n_even)])
    g_conv_b_w = jnp.stack([lax.dynamic_slice_in_dim(part(f"cbw{i}", 0, kb), me * wsh, wsh, axis=1)
                            for i in range(n_even)])
    g_c_ctx = _ew(lambda a, b: a * _dsilu(b), F32, "d_c_ctx", part("c_ctx", 0, 8), _pad_rows(c_ctx[None, :], 8))[0]

    def updated(recv, layers, prefix, w, m, v):
        outs = [_reduce_adamw(recv[l], w[j], m[j], v[j], f"adamw_{prefix}{j}") for j, l in enumerate(layers)]
        return [jnp.stack([o[t] for o in outs]) for t in range(4)]

    evens, odds = range(0, depth, 2), range(1, depth, 2)
    r_w_in_e = updated(recv_in, evens, "w_in_e", w_in_e, m_w_in_e, v_w_in_e)
    r_w_out_e = updated(recv_out, evens, "w_out_e", w_out_e, m_w_out_e, v_w_out_e)
    r_w_in_o = updated(recv_in, odds, "w_in_o", w_in_o, m_w_in_o, v_w_in_o)
    r_w_out_o = updated(recv_out, odds, "w_out_o", w_out_o, m_w_out_o, v_w_out_o)

    grads = {
        "c_ctx": g_c_ctx, "w_mod": g_w_mod, "b_mod": g_b_mod, "post_ln_g": g_post_ln_g, "post_ln_b": g_post_ln_b,
        "conv_a_w": g_conv_a_w, "conv_a_b": g_conv_a_b, "norm_a_g": g_norm_a_g, "norm_a_b": g_norm_a_b,
        "conv_b_w": g_conv_b_w, "q_norm_g": g_q_norm_g, "k_norm_g": g_k_norm_g,
    }
    state = {
        "c_ctx": (c_ctx, m_c_ctx, v_c_ctx), "w_mod": (w_mod, m_w_mod, v_w_mod), "b_mod": (b_mod, m_b_mod, v_b_mod),
        "post_ln_g": (post_ln_g, m_post_ln_g, v_post_ln_g), "post_ln_b": (post_ln_b, m_post_ln_b, v_post_ln_b),
        "conv_a_w": (conv_a_w, m_conv_a_w, v_conv_a_w), "conv_a_b": (conv_a_b, m_conv_a_b, v_conv_a_b),
        "norm_a_g": (norm_a_g, m_norm_a_g, v_norm_a_g), "norm_a_b": (norm_a_b, m_norm_a_b, v_norm_a_b),
        "conv_b_w": (conv_b_w, m_conv_b_w, v_conv_b_w), "q_norm_g": (q_norm_g, m_q_norm_g, v_q_norm_g),
        "k_norm_g": (k_norm_g, m_k_norm_g, v_k_norm_g),
    }
    res = {"w_in_e": r_w_in_e, "w_out_e": r_w_out_e, "w_in_o": r_w_in_o, "w_out_o": r_w_out_o}
    for nme, g in grads.items():
        w, m, v = state[nme]
        res[nme] = [g] + _adamw(w, g, m, v, f"adamw_{nme}")
    order = ["c_ctx", "w_mod", "b_mod", "post_ln_g", "post_ln_b", "w_in_e", "conv_a_w", "conv_a_b", "norm_a_g",
             "norm_a_b", "conv_b_w", "w_out_e", "w_in_o", "q_norm_g", "k_norm_g", "w_out_o"]
    return (loss, grad_x, *[res[nme][0] for nme in order], *[res[nme][1] for nme in order],
            *[res[nme][2] for nme in order], *[res[nme][3] for nme in order])
```

```python
import functools
import math

import jax
import jax.numpy as jnp
from jax import lax
from jax.experimental import pallas as pl
from jax.experimental.pallas import tpu as pltpu

F32 = jnp.float32
BF16 = jnp.bfloat16

NDEV = 8
GRID_W = 64
HEAD_DIM = 128
GQA_GROUP = 4
ROPE_THETA = 10000.0
LN_EPS = 1e-5
RMS_EPS = 1e-6
ATTN_SCALE = HEAD_DIM ** -0.5
ADAM_LR = 0.001
ADAM_B1 = 0.9
ADAM_B2 = 0.999
ADAM_EPS = 1e-08
ADAM_WD = 0.01
ADAM_STEP = 10
HALO = 16
LANES = 128
VMEM_LIMIT = 56 * 1024 * 1024
MESH = pl.DeviceIdType.MESH


def _cp(n_axes):
    return pltpu.CompilerParams(dimension_semantics=("arbitrary",) * n_axes, vmem_limit_bytes=VMEM_LIMIT)


def _div(dim, pref, mult):
    t = min(pref, dim) // mult * mult
    while t >= mult:
        if dim % t == 0:
            return t
        t -= mult
    return dim


def _sigmoid(x):
    return 1.0 / (1.0 + jnp.exp(-x))


def _silu(x):
    return x * _sigmoid(x)


def _dsilu(x):
    s = _sigmoid(x)
    return s * (1.0 + x * (1.0 - s))


def _ln_norm(z):
    mu = jnp.mean(z, axis=-1, keepdims=True)
    zc = z - mu
    var = jnp.mean(zc * zc, axis=-1, keepdims=True)
    rstd = lax.rsqrt(var + LN_EPS)
    return zc * rstd, rstd


def _ln_bwd(dn, n, rstd):
    return rstd * (dn - jnp.mean(dn, axis=-1, keepdims=True) - n * jnp.mean(dn * n, axis=-1, keepdims=True))


def _partner(x):
    lane = lax.broadcasted_iota(jnp.int32, x.shape, 1)
    return jnp.where((lane % 64) < 32, pltpu.roll(x, 96, 1), pltpu.roll(x, 32, 1))


N_PEERS = NDEV - 1
HBM_SPEC = pl.BlockSpec(memory_space=pltpu.HBM)


def _side_plan(jobs):
    xs = [x for _, x in jobs]
    out_shape = [jax.ShapeDtypeStruct((NDEV,) + x.shape[-2:], x.dtype) for x in xs]
    sems = [pltpu.SemaphoreType.DMA((2 * N_PEERS + 1,)) for _ in jobs]
    return xs, [HBM_SPEC] * len(jobs), out_shape, [HBM_SPEC] * len(jobs), sems


def _side_copies(kind, x_ref, out_ref, sems):
    mx, my, mc = lax.axis_index("x"), lax.axis_index("y"), lax.axis_index("c")
    me = 4 * mx + 2 * my + mc
    own = x_ref.at[me] if kind == "exchange" else x_ref
    copies = [pltpu.make_async_copy(own, out_ref.at[me], sems.at[2 * N_PEERS])]
    for k in range(1, NDEV):
        px, py, pc = mx ^ ((k >> 2) & 1), my ^ ((k >> 1) & 1), mc ^ (k & 1)
        src = x_ref.at[4 * px + 2 * py + pc] if kind == "exchange" else x_ref
        copies.append(pltpu.make_async_remote_copy(
            src_ref=src, dst_ref=out_ref.at[me], send_sem=sems.at[k - 1], recv_sem=sems.at[N_PEERS + k - 1],
            device_id=(px, py, pc), device_id_type=MESH))
    return copies


def _side_run(jobs, in_refs, out_refs, sem_refs, first, last):
    if not jobs:
        return (lambda: None), (lambda: None)

    def start():
        @pl.when(first)
        def _():
            for (kind, _), x_ref, o_ref, sems in zip(jobs, in_refs, out_refs, sem_refs):
                for cp in _side_copies(kind, x_ref, o_ref, sems):
                    cp.start()

    def finish():
        @pl.when(last)
        def _():
            for (kind, _), x_ref, o_ref, sems in zip(jobs, in_refs, out_refs, sem_refs):
                for cp in _side_copies(kind, x_ref, o_ref, sems):
                    cp.wait()

    return start, finish


def _grid_ends(grid):
    first = last = None
    for ax, n in enumerate(grid):
        i = pl.program_id(ax)
        f, l = i == 0, i == n - 1
        first = f if first is None else jnp.logical_and(first, f)
        last = l if last is None else jnp.logical_and(last, l)
    return first, last


def _mm(a, b, mode, out_dtype, name, tm=768, tn=2048, tk=2048, out_nd=1, side=(), mod_bwd=None):
    if mode == "nn":
        m, kdim = a.shape
        nd, _, ns = b.shape
        tm, tn, tk = _div(m, tm, 8), _div(ns, tn, 128), _div(kdim, tk, 128)
        nbs = ns // tn
        grid = (m // tm, nd * nbs, kdim // tk)
        a_spec = pl.BlockSpec((tm, tk), lambda i, j, k: (i, k))
        b_spec = pl.BlockSpec((None, tk, tn), lambda i, j, k: (j // nbs, k, j % nbs))
        o_spec = pl.BlockSpec((tm, tn), lambda i, j, k: (i, j))
        out_shape = (m, nd * ns)
        dims = (((1,), (0,)), ((), ()))
    elif mode == "nt":
        m, _ = a.shape
        nd, ko, ns = b.shape
        tm, tn, tk = _div(m, tm, 8), _div(ko, tn, 128), _div(ns, tk, 128)
        kbs = ns // tk
        grid = (m // tm, ko // tn, nd * kbs)
        a_spec = pl.BlockSpec((tm, tk), lambda i, j, k: (i, k))
        b_spec = pl.BlockSpec((None, tn, tk), lambda i, j, k: (k // kbs, j, k % kbs))
        o_spec = pl.BlockSpec((tm, tn), lambda i, j, k: (i, j))
        out_shape = (m, ko)
        dims = (((1,), (1,)), ((), ()))
    else:
        m, kdim = a.shape
        n = b.shape[1]
        ns = n // out_nd
        tm, tn, tk = _div(kdim, tm, 128), _div(ns, tn, 128), _div(m, tk, 16)
        nbs = ns // tn
        grid = (kdim // tm, out_nd * nbs, m // tk)
        a_spec = pl.BlockSpec((tk, tm), lambda i, j, k: (k, i))
        b_spec = pl.BlockSpec((tk, tn), lambda i, j, k: (k, j))
        o_spec = pl.BlockSpec((None, tm, tn), lambda i, j, k: (j // nbs, i, j % nbs))
        out_shape = (out_nd, kdim, ns)
        dims = (((0,), (0,)), ((), ()))
    nk = grid[2]
    nj = len(side)
    s_ins, s_in_specs, s_out_shape, s_out_specs, s_sems = _side_plan(side)
    e_ins, e_in_specs, e_out_shape, e_out_specs = [], [], [], []
    if mod_bwd is not None:
        assert mode == "nt" and grid[1] == 1
        x_in, dxa, modv, t_ctx = mod_bwd
        e_ins = [x_in, dxa, modv]
        e_in_specs = [pl.BlockSpec((tm, tn), lambda i, j, k: (i, 0)), pl.BlockSpec((tm, tn), lambda i, j, k: (i, 0)),
                      pl.BlockSpec(modv.shape, lambda i, j, k: (0, 0, 0))]
        e_out_shape = [jax.ShapeDtypeStruct((8, tn), F32)]
        e_out_specs = [pl.BlockSpec((8, tn), lambda i, j, k: (0, 0))]
    ne, neo = len(e_ins), len(e_out_shape)

    def body(a_ref, b_ref, *rest):
        e_in, s_in = rest[:ne], rest[ne:ne + nj]
        o_ref, e_out, s_out = rest[ne + nj], rest[ne + nj + 1:ne + nj + 1 + neo], rest[ne + nj + 1 + neo:ne + 2 * nj + 1 + neo]
        scratch = rest[ne + 2 * nj + 1 + neo:]
        acc_ref, s_sem = (None, scratch) if nk == 1 else (scratch[0], scratch[1:])
        first, last = _grid_ends(grid)
        start, finish = _side_run(side, s_in, s_out, s_sem, first, last)
        start()

        def prod():
            return lax.dot_general(a_ref[...].astype(BF16), b_ref[...].astype(BF16), dims,
                                   preferred_element_type=F32)

        def emit(val):
            if mod_bwd is None:
                o_ref[...] = val.astype(o_ref.dtype)
                return
            x_ref, dxa_ref, mv_ref = e_in
            rows = pl.program_id(0) * tm + lax.broadcasted_iota(jnp.int32, (tm, 1), 0)
            is_ctx = rows < t_ctx
            o_ref[...] = dxa_ref[...] + val * (1.0 + jnp.where(is_ctx, mv_ref[0, 1:2, :], mv_ref[1, 1:2, :]))
            dsc = val * x_ref[...]
            for row, (t, keep_ctx) in enumerate(((val, True), (dsc, True), (val, False), (dsc, False))):
                e_out[0][row:row + 1, :] += jnp.sum(jnp.where(is_ctx == keep_ctx, t, 0.0), axis=0, keepdims=True)

        if mod_bwd is not None:
            @pl.when(first)
            def _():
                e_out[0][...] = jnp.zeros_like(e_out[0])

        if nk == 1:
            emit(prod())
        else:
            k = pl.program_id(2)

            @pl.when(k == 0)
            def _():
                acc_ref[...] = jnp.zeros_like(acc_ref)

            acc_ref[...] += prod()

            @pl.when(k == nk - 1)
            def _():
                emit(acc_ref[...])
        finish()

    outs = pl.pallas_call(
        body, name=name, grid=grid, in_specs=[a_spec, b_spec] + e_in_specs + s_in_specs,
        out_specs=[o_spec] + e_out_specs + s_out_specs,
        out_shape=[jax.ShapeDtypeStruct(out_shape, out_dtype)] + e_out_shape + s_out_shape,
        scratch_shapes=([] if nk == 1 else [pltpu.VMEM((tm, tn), F32)]) + s_sems, compiler_params=_cp(3),
    )(a, b, *e_ins, *s_ins)
    return outs if len(outs) > 1 else outs[0]


def _ew(fn, out_dtype, name, *xs):
    rows, cols = xs[0].shape
    tr = rows if rows <= 64 else _div(rows, 256, 16)

    def body(*refs):
        refs[-1][...] = fn(*[r[...] for r in refs[:-1]]).astype(out_dtype)

    spec = pl.BlockSpec((tr, cols), lambda i: (i, 0))
    return pl.pallas_call(
        body, name=name, grid=(rows // tr,), in_specs=[spec] * len(xs), out_specs=spec,
        out_shape=jax.ShapeDtypeStruct((rows, cols), out_dtype), compiler_params=_cp(1),
    )(*xs)


def _sum_lead(x, name):
    n, rows, cols = x.shape
    tr = _div(rows, 64, 8)

    def body(x_ref, o_ref):
        acc = x_ref[0]
        for s in range(1, n):
            acc = acc + x_ref[s]
        o_ref[...] = acc

    return pl.pallas_call(
        body, name=name, grid=(rows // tr,),
        in_specs=[pl.BlockSpec((n, tr, cols), lambda i: (0, i, 0))],
        out_specs=pl.BlockSpec((tr, cols), lambda i: (i, 0)),
        out_shape=jax.ShapeDtypeStruct((rows, cols), F32), compiler_params=_cp(1),
    )(x)


def _adam_math(w, g, m, v):
    m = ADAM_B1 * m + (1.0 - ADAM_B1) * g
    v = ADAM_B2 * v + (1.0 - ADAM_B2) * (g * g)
    m_hat = m / (1.0 - ADAM_B1 ** ADAM_STEP)
    v_hat = v / (1.0 - ADAM_B2 ** ADAM_STEP)
    delta = -ADAM_LR * (m_hat / (jnp.sqrt(v_hat) + ADAM_EPS) + ADAM_WD * w)
    return delta, m, v


def _adamw(w, g, m, v, name):
    shape = w.shape
    cols = shape[-1]
    w2, g2, m2, v2 = [t.reshape(-1, cols) for t in (w, g, m, v)]
    rows = w2.shape[0]
    tr = rows if rows <= 512 else _div(rows, 256, 8)

    def body(w_ref, g_ref, m_ref, v_ref, d_ref, nm_ref, nv_ref):
        d, nm, nv = _adam_math(w_ref[...], g_ref[...], m_ref[...], v_ref[...])
        d_ref[...] = d
        nm_ref[...] = nm
        nv_ref[...] = nv

    spec = pl.BlockSpec((tr, cols), lambda i: (i, 0))
    outs = pl.pallas_call(
        body, name=name, grid=(rows // tr,), in_specs=[spec] * 4, out_specs=[spec] * 3,
        out_shape=[jax.ShapeDtypeStruct((rows, cols), F32)] * 3, compiler_params=_cp(1),
    )(w2, g2, m2, v2)
    return [o.reshape(shape) for o in outs]


def _reduce_adamw(parts, w, m, v, name):
    shape = w.shape
    n, rows, cols = parts.shape
    w2, m2, v2 = [t.reshape(rows, cols) for t in (w, m, v)]
    tr = _div(rows, 128, 16)

    def body(p_ref, w_ref, m_ref, v_ref, g_ref, d_ref, nm_ref, nv_ref):
        g = p_ref[0].astype(F32)
        for s in range(1, n):
            g = g + p_ref[s].astype(F32)
        d, nm, nv = _adam_math(w_ref[...], g, m_ref[...], v_ref[...])
        g_ref[...] = g
        d_ref[...] = d
        nm_ref[...] = nm
        nv_ref[...] = nv

    spec = pl.BlockSpec((tr, cols), lambda i: (i, 0))
    outs = pl.pallas_call(
        body, name=name, grid=(rows // tr,),
        in_specs=[pl.BlockSpec((n, tr, cols), lambda i: (0, i, 0))] + [spec] * 3, out_specs=[spec] * 4,
        out_shape=[jax.ShapeDtypeStruct((rows, cols), F32)] * 4, compiler_params=_cp(1),
    )(parts, w2, m2, v2)
    return [o.reshape(shape) for o in outs]


def _all_gather(x, name):
    rows, cols = x.shape

    def body(x_ref, out_ref, send_sems, recv_sems, local_sem):
        mx, my, mc = lax.axis_index("x"), lax.axis_index("y"), lax.axis_index("c")
        me, sibling = (mx, my, mc), (mx, my, 1 - mc)
        chips = [(1 - mx, my), (mx, 1 - my), (1 - mx, 1 - my)]

        def slab(px, py, pc):
            return out_ref.at[4 * px + 2 * py + pc]

        def copy(k, block, to, src=None):
            return pltpu.make_async_remote_copy(
                src_ref=slab(*block) if src is None else src, dst_ref=slab(*block),
                send_sem=send_sems.at[k], recv_sem=recv_sems.at[k], device_id=to, device_id_type=MESH)

        mine = pltpu.make_async_copy(x_ref, slab(*me), local_sem)
        mine.start()
        first = [copy(0, me, sibling, src=x_ref)]
        first += [copy(1 + j, me, (*chip, mc), src=x_ref) for j, chip in enumerate(chips)]
        for cp in first:
            cp.start()
        passed = [copy(4 + j, (*chip, mc), sibling) for j, chip in enumerate(chips)]
        for j, chip in enumerate(chips):
            copy(1 + j, (*chip, mc), me).wait_recv()
            passed[j].start()
        copy(0, sibling, me).wait_recv()
        for j, chip in enumerate(chips):
            copy(4 + j, (*chip, 1 - mc), me).wait_recv()
        for cp in first + passed:
            cp.wait_send()
        mine.wait()

    return pl.pallas_call(
        body, name=name, out_shape=jax.ShapeDtypeStruct((NDEV, rows, cols), x.dtype),
        in_specs=[pl.BlockSpec(memory_space=pltpu.HBM)], out_specs=pl.BlockSpec(memory_space=pltpu.HBM),
        scratch_shapes=[pltpu.SemaphoreType.DMA((7,)), pltpu.SemaphoreType.DMA((7,)), pltpu.SemaphoreType.DMA],
    )(x)


def _row_tile(t_ctx, pref=256):
    return _div(t_ctx, pref, 8)


def _mod_spec(d, nctx):
    return pl.BlockSpec((None, 3, d), lambda i: (jnp.where(i >= nctx, 1, 0), 0, 0))


def _modulate(xb, modv, t_ctx, name):
    r, d = xb.shape
    tm = _row_tile(t_ctx)
    nctx = t_ctx // tm

    def body(x_ref, mv_ref, h_ref):
        h_ref[...] = (x_ref[...] * (1.0 + mv_ref[1:2, :]) + mv_ref[0:1, :]).astype(BF16)

    row = pl.BlockSpec((tm, d), lambda i: (i, 0))
    return pl.pallas_call(
        body, name=name, grid=(r // tm,), in_specs=[row, _mod_spec(d, nctx)], out_specs=row,
        out_shape=jax.ShapeDtypeStruct((r, d), BF16), compiler_params=_cp(1),
    )(xb, modv)


def _postln_fwd(xb, y, modv, lnp, alpha, next_modv, t_ctx, name):
    r, d = xb.shape
    tm = _row_tile(t_ctx)
    nctx = t_ctx // tm

    def body(x_ref, y_ref, mv_ref, ln_ref, nmv_ref, o_ref, h_ref):
        n, _ = _ln_norm(alpha * x_ref[...] + mv_ref[2:3, :] * y_ref[...])
        xo = n * ln_ref[0:1, :] + ln_ref[1:2, :]
        o_ref[...] = xo
        h_ref[...] = (xo * (1.0 + nmv_ref[1:2, :]) + nmv_ref[0:1, :]).astype(BF16)

    row = pl.BlockSpec((tm, d), lambda i: (i, 0))
    return pl.pallas_call(
        body, name=name, grid=(r // tm,),
        in_specs=[row, row, _mod_spec(d, nctx), pl.BlockSpec((2, d), lambda i: (0, 0)), _mod_spec(d, nctx)],
        out_specs=[row, row],
        out_shape=[jax.ShapeDtypeStruct((r, d), F32), jax.ShapeDtypeStruct((r, d), BF16)], compiler_params=_cp(1),
    )(xb, y, modv, lnp, next_modv)


def _postln_bwd(dxn, xb, y, modv, lnp, alpha, t_ctx, name, from_loss=False, mod_next=None):
    r, d = xb.shape
    tm = _row_tile(t_ctx)
    nctx = t_ctx // tm
    n_lead = 1 if from_loss else 0

    def body(dxn_ref, x_ref, y_ref, mv_ref, ln_ref, *rest):
        extra_in, outs = (rest[:2], rest[2:]) if mod_next is not None else ((), rest)
        dy_ref, dxa_ref, s_ref = outs[n_lead:n_lead + 3]
        i = pl.program_id(0)
        is_ctx = i < nctx

        @pl.when(i == 0)
        def _():
            for o_ref in outs[:n_lead] + outs[n_lead + 2:]:
                o_ref[...] = jnp.zeros_like(o_ref)

        yv = y_ref[...]
        gate = mv_ref[2:3, :]
        n, rstd = _ln_norm(alpha * x_ref[...] + gate * yv)
        if from_loss:
            e = jnp.where(is_ctx, 0.0, n * ln_ref[0:1, :] + ln_ref[1:2, :] - dxn_ref[...])
            outs[0][...] += 0.5 * jnp.sum(jnp.sum(e * e, axis=1, keepdims=True), axis=0, keepdims=True) / d
            dxn_v = e / d
        elif mod_next is not None:
            dxan_ref, nmv_ref = extra_in
            dh = dxn_ref[...]
            dxn_v = dxan_ref[...] + dh * (1.0 + nmv_ref[1:2, :])
            dshift = jnp.sum(dh, axis=0, keepdims=True)
            dscale = jnp.sum(dh * (n * ln_ref[0:1, :] + ln_ref[1:2, :]), axis=0, keepdims=True)
            sn_ref = outs[-1]
            sn_ref[0:1, :] += jnp.where(is_ctx, dshift, 0.0)
            sn_ref[1:2, :] += jnp.where(is_ctx, dscale, 0.0)
            sn_ref[2:3, :] += jnp.where(is_ctx, 0.0, dshift)
            sn_ref[3:4, :] += jnp.where(is_ctx, 0.0, dscale)
        else:
            dxn_v = dxn_ref[...]
        dz = _ln_bwd(dxn_v * ln_ref[0:1, :], n, rstd)
        dy_ref[...] = (gate * dz).astype(BF16)
        dxa_ref[...] = alpha * dz
        s_ref[0:1, :] += jnp.sum(dxn_v * n, axis=0, keepdims=True)
        s_ref[1:2, :] += jnp.sum(dxn_v, axis=0, keepdims=True)
        dgate = jnp.sum(dz * yv, axis=0, keepdims=True)
        s_ref[2:3, :] += jnp.where(is_ctx, dgate, 0.0)
        s_ref[3:4, :] += jnp.where(is_ctx, 0.0, dgate)

    row = pl.BlockSpec((tm, d), lambda i: (i, 0))
    acc = pl.BlockSpec((8, d), lambda i: (0, 0))
    first = pl.BlockSpec((tm, d), lambda i: (jnp.maximum(i - nctx, 0), 0)) if from_loss else row
    loss_spec = [pl.BlockSpec((8, 128), lambda i: (0, 0))] if from_loss else []
    loss_shape = [jax.ShapeDtypeStruct((8, 128), F32)] if from_loss else []
    extra = [] if mod_next is None else list(mod_next)
    return pl.pallas_call(
        body, name=name, grid=(r // tm,),
        in_specs=[first, row, row, _mod_spec(d, nctx), pl.BlockSpec((2, d), lambda i: (0, 0))]
        + ([row, _mod_spec(d, nctx)] if extra else []),
        out_specs=loss_spec + [row, row, acc] + ([acc] if extra else []),
        out_shape=loss_shape + [jax.ShapeDtypeStruct((r, d), BF16), jax.ShapeDtypeStruct((r, d), F32),
                                jax.ShapeDtypeStruct((8, d), F32)]
        + ([jax.ShapeDtypeStruct((8, d), F32)] if extra else []),
        compiler_params=_cp(1),
    )(dxn, xb, y, modv, lnp, *extra)


def _conv_specs(r, w, tm, cblk):
    hb = tm // HALO
    last = r // HALO - 1
    main = pl.BlockSpec((tm, w), lambda i: (i, cblk))
    top = pl.BlockSpec((HALO, w), lambda i: (jnp.maximum(i * hb - 1, 0), cblk))
    bot = pl.BlockSpec((HALO, w), lambda i: (jnp.minimum((i + 1) * hb, last), cblk))
    return [main, top, bot]


def _fill_pad(pad_ref, main, top, bot, top_ok, bot_ok, tm):
    pad_ref[0:HALO, :] = jnp.where(top_ok, top, 0.0)
    pad_ref[HALO:HALO + tm, :] = main
    pad_ref[HALO + tm:2 * HALO + tm, :] = jnp.where(bot_ok, bot, 0.0)


def _edges(i, nctx, nr):
    top_ok = jnp.logical_and(i != 0, i != nctx)
    bot_ok = jnp.logical_and(i != nctx - 1, i != nr - 1)
    return top_ok, bot_ok


def _rot_fill(rot_ref, pad_ref, n):
    rot_ref[0, 0:n, :] = pad_ref[0:n, :]
    for b in range(1, 8):
        rot_ref[b, 0:n, :] = pad_ref[pl.ds(b, n), :]


def _tap(rot_ref, off, tm):
    return rot_ref[off % 8, pl.ds(off - off % 8, tm), :]


def _rows8(x):
    return jnp.sum(x.reshape(x.shape[0] // 8, 8, x.shape[1]), axis=0)


def _conv_fwd(p, caw, cab, nag, nab, cbw, t_ctx, name):
    r = p.shape[0]
    w = p.shape[1] // 7
    ka, kb = caw.shape[0], cbw.shape[0]
    tm = _row_tile(t_ctx, 128)
    nctx, nr = t_ctx // tm, r // tm
    n = tm + 2 * HALO - 8

    def body(av, avt, avb, ag, agt, agb, agate, bx, bxt, bxb, bb, bc, bct, bcb, bgate,
             caw_ref, cab_ref, nag_ref, nab_ref, cbw_ref, cat_ref, u1_ref, pad, rot, v_s):
        top_ok, bot_ok = _edges(pl.program_id(0), nctx, nr)

        def chunk(c, carry):
            cols = pl.ds(pl.multiple_of(c * LANES, LANES), LANES)
            _fill_pad(pad, av[:, cols] * _sigmoid(ag[:, cols]), avt[:, cols] * _sigmoid(agt[:, cols]),
                      avb[:, cols] * _sigmoid(agb[:, cols]), top_ok, bot_ok, tm)
            _rot_fill(rot, pad, n)
            u1 = jnp.zeros((tm, LANES), F32) + cab_ref[:, cols]
            for k in range(ka):
                u1 = u1 + caw_ref[k:k + 1, cols] * _tap(rot, HALO - ka // 2 + k, tm)
            u1_ref[:, cols] = u1
            _fill_pad(pad, bc[:, cols] * bx[:, cols], bct[:, cols] * bxt[:, cols], bcb[:, cols] * bxb[:, cols],
                      top_ok, bot_ok, tm)
            v = jnp.zeros((tm, LANES), F32)
            for k in range(kb):
                v = v + cbw_ref[k:k + 1, cols] * pad[pl.ds(HALO - kb // 2 + k, tm), :]
            v_s[:, cols] = v
            return carry

        lax.fori_loop(0, w // LANES, chunk, 0)
        nrm, _ = _ln_norm(u1_ref[...])
        a_out = _silu(nrm * nag_ref[...] + nab_ref[...]) * _silu(agate[...])
        cat_ref[:, 0:w] = a_out.astype(BF16)
        cat_ref[:, w:2 * w] = (bb[...] * v_s[...] * _silu(bgate[...])).astype(BF16)

    def main(cblk):
        return pl.BlockSpec((tm, w), lambda i: (i, cblk))

    def whole(a):
        return pl.BlockSpec(a.shape, lambda i: (0, 0))

    in_specs = (_conv_specs(r, w, tm, 0) + _conv_specs(r, w, tm, 1) + [main(2)] + _conv_specs(r, w, tm, 3)
                + [main(4)] + _conv_specs(r, w, tm, 5) + [main(6)]
                + [whole(caw), whole(cab), whole(nag), whole(nab), whole(cbw)])
    return pl.pallas_call(
        body, name=name, grid=(nr,), in_specs=in_specs,
        out_specs=[pl.BlockSpec((tm, 2 * w), lambda i: (i, 0)), pl.BlockSpec((tm, w), lambda i: (i, 0))],
        out_shape=[jax.ShapeDtypeStruct((r, 2 * w), BF16), jax.ShapeDtypeStruct((r, w), F32)],
        scratch_shapes=[pltpu.VMEM((tm + 2 * HALO, LANES), F32), pltpu.VMEM((8, n, LANES), F32),
                        pltpu.VMEM((tm, w), F32)],
        compiler_params=_cp(1),
    )(*([p] * 15), caw, cab, nag, nab, cbw)


def _conv_bwd1(dcat, p, u1, nag, nab, t_ctx, name):
    r, w = u1.shape
    tm = _row_tile(t_ctx, 128)

    def body(da_ref, agate_ref, u1_ref, nag_ref, nab_ref, du1_ref, dag_ref, s_ref):
        @pl.when(pl.program_id(0) == 0)
        def _():
            s_ref[...] = jnp.zeros_like(s_ref)

        n, rstd = _ln_norm(u1_ref[...])
        g = nag_ref[...]
        u2 = n * g + nab_ref[...]
        da = da_ref[...]
        ag = agate_ref[...]
        dag_ref[...] = (da * _silu(u2) * _dsilu(ag)).astype(BF16)
        du2 = da * _silu(ag) * _dsilu(u2)
        du1 = _ln_bwd(du2 * g, n, rstd)
        du1_ref[...] = du1
        s_ref[0:1, :] += jnp.sum(du2 * n, axis=0, keepdims=True)
        s_ref[1:2, :] += jnp.sum(du2, axis=0, keepdims=True)
        s_ref[2:3, :] += jnp.sum(du1, axis=0, keepdims=True)

    def win(cblk):
        return pl.BlockSpec((tm, w), lambda i: (i, cblk))

    one = pl.BlockSpec((1, w), lambda i: (0, 0))
    return pl.pallas_call(
        body, name=name, grid=(r // tm,), in_specs=[win(0), win(2), win(0), one, one],
        out_specs=[win(0), win(0), pl.BlockSpec((8, w), lambda i: (0, 0))],
        out_shape=[jax.ShapeDtypeStruct((r, w), F32), jax.ShapeDtypeStruct((r, w), BF16),
                   jax.ShapeDtypeStruct((8, w), F32)],
        compiler_params=_cp(1),
    )(dcat, p, u1, nag, nab)


def _conv_bwd2(du1, dcat, p, dag, caw, cbw, t_ctx, name):
    r, w = du1.shape
    ka, kb = caw.shape[0], cbw.shape[0]
    tm = _row_tile(t_ctx, 128)
    nctx, nr = t_ctx // tm, r // tm
    n = tm + 2 * HALO - 8

    def body(du, dut, dub, av, avt, avb, ag, agt, agb, db, dbt, dbb, bx, bxt, bxb, bb, bbt, bbb,
             bc, bct, bcb, bg, bgt, bgb, dag_ref, caw_ref, cbw_ref, dp_ref, dcaw_ref, dcbw_ref,
             pad, pad2, rot_u, rot_d, acc_a, acc_b):
        i = pl.program_id(0)
        top_ok, bot_ok = _edges(i, nctx, nr)

        @pl.when(i == 0)
        def _():
            acc_a[...] = jnp.zeros_like(acc_a)
            acc_b[...] = jnp.zeros_like(acc_b)

        def chunk(c, carry):
            c0 = pl.multiple_of(c * LANES, LANES)
            cols = pl.ds(c0, LANES)

            def seg(s):
                return pl.ds(pl.multiple_of(s * w + c0, LANES), LANES)

            sg = _sigmoid(ag[:, cols])
            av_m = av[:, cols]
            _fill_pad(pad, av_m * sg, avt[:, cols] * _sigmoid(agt[:, cols]), avb[:, cols] * _sigmoid(agb[:, cols]),
                      top_ok, bot_ok, tm)
            _rot_fill(rot_u, pad, n)
            du_m = du[:, cols]
            _fill_pad(pad, du_m, dut[:, cols], dub[:, cols], top_ok, bot_ok, tm)
            _rot_fill(rot_d, pad, n)
            du0 = jnp.zeros((tm, LANES), F32)
            for k in range(ka):
                du0 = du0 + caw_ref[k:k + 1, cols] * _tap(rot_d, HALO + ka // 2 - k, tm)
                acc_a[k, :, cols] += _rows8(du_m * _tap(rot_u, HALO - ka // 2 + k, tm))
            dp_ref[:, seg(0)] = (du0 * sg).astype(BF16)
            dp_ref[:, seg(1)] = (du0 * av_m * sg * (1.0 - sg)).astype(BF16)
            dp_ref[:, seg(2)] = dag_ref[:, cols]

            bc_m, bx_m = bc[:, cols], bx[:, cols]
            _fill_pad(pad, bc_m * bx_m, bct[:, cols] * bxt[:, cols], bcb[:, cols] * bxb[:, cols],
                      top_ok, bot_ok, tm)
            v = jnp.zeros((tm, LANES), F32)
            for k in range(kb):
                v = v + cbw_ref[k:k + 1, cols] * pad[pl.ds(HALO - kb // 2 + k, tm), :]
            db_m, bb_m, bg_m = db[:, cols], bb[:, cols], bg[:, cols]
            sbg = _silu(bg_m)
            dv_m = db_m * bb_m * sbg
            _fill_pad(pad2, dv_m, dbt[:, cols] * bbt[:, cols] * _silu(bgt[:, cols]),
                      dbb[:, cols] * bbb[:, cols] * _silu(bgb[:, cols]), top_ok, bot_ok, tm)
            dw0 = jnp.zeros((tm, LANES), F32)
            for k in range(kb):
                dw0 = dw0 + cbw_ref[k:k + 1, cols] * pad2[pl.ds(HALO + kb // 2 - k, tm), :]
                acc_b[k, :, cols] += _rows8(dv_m * pad[pl.ds(HALO - kb // 2 + k, tm), :])
            dp_ref[:, seg(3)] = (dw0 * bc_m).astype(BF16)
            dp_ref[:, seg(4)] = (db_m * v * sbg).astype(BF16)
            dp_ref[:, seg(5)] = (dw0 * bx_m).astype(BF16)
            dp_ref[:, seg(6)] = (db_m * bb_m * v * _dsilu(bg_m)).astype(BF16)
            return carry

        lax.fori_loop(0, w // LANES, chunk, 0)

        @pl.when(i == nr - 1)
        def _():
            dcaw_ref[...] = jnp.zeros_like(dcaw_ref)
            dcbw_ref[...] = jnp.zeros_like(dcbw_ref)
            for k in range(ka):
                dcaw_ref[k:k + 1, :] = jnp.sum(acc_a[k], axis=0, keepdims=True)
            for k in range(kb):
                dcbw_ref[k:k + 1, :] = jnp.sum(acc_b[k], axis=0, keepdims=True)

    def whole(a):
        return pl.BlockSpec(a.shape, lambda i: (0, 0))

    in_specs = (_conv_specs(r, w, tm, 0) + _conv_specs(r, w, tm, 0) + _conv_specs(r, w, tm, 1)
                + _conv_specs(r, w, tm, 1) + _conv_specs(r, w, tm, 3) + _conv_specs(r, w, tm, 4)
                + _conv_specs(r, w, tm, 5) + _conv_specs(r, w, tm, 6)
                + [pl.BlockSpec((tm, w), lambda i: (i, 0)), whole(caw), whole(cbw)])
    pad_t = pltpu.VMEM((tm + 2 * HALO, LANES), F32)
    rot_t = pltpu.VMEM((8, n, LANES), F32)
    return pl.pallas_call(
        body, name=name, grid=(nr,), in_specs=in_specs,
        out_specs=[pl.BlockSpec((tm, 7 * w), lambda i: (i, 0)), pl.BlockSpec((32, w), lambda i: (0, 0)),
                   pl.BlockSpec((8, w), lambda i: (0, 0))],
        out_shape=[jax.ShapeDtypeStruct((r, 7 * w), BF16), jax.ShapeDtypeStruct((32, w), F32),
                   jax.ShapeDtypeStruct((8, w), F32)],
        scratch_shapes=[pad_t, pad_t, rot_t, rot_t, pltpu.VMEM((32, 8, w), F32), pltpu.VMEM((8, 8, w), F32)],
        compiler_params=_cp(1),
    )(*([du1] * 3), *([p] * 6), *([dcat] * 3), *([p] * 12), dag, caw, cbw)


def _rms(xh):
    r = lax.rsqrt(jnp.mean(xh * xh, axis=-1, keepdims=True) + RMS_EPS)
    return xh * r, r


def _qk_fwd(p, cos, sin, qg, kg, att, kv, t_ctx, name):
    r = p.shape[0]
    tm = _row_tile(t_ctx)

    def body(q_ref, k_ref, v_ref, cos_ref, sin_ref, qg_ref, kg_ref, qr_ref, kr_ref, vb_ref):
        cs, sn = cos_ref[...], sin_ref[...]
        for src, g_ref, dst, nh, fac in ((q_ref, qg_ref, qr_ref, att // HEAD_DIM, 1.0),
                                         (k_ref, kg_ref, kr_ref, kv // HEAD_DIM, ATTN_SCALE_LOG2)):
            for h in range(nh):
                cols = slice(h * HEAD_DIM, (h + 1) * HEAD_DIM)
                n, _ = _rms(src[:, cols])
                n = n * g_ref[...]
                rot = n * cs + _partner(n) * sn
                dst[:, cols] = (rot if fac == 1.0 else rot * fac).astype(BF16)
        vb_ref[...] = v_ref[...].astype(BF16)

    def rows(width, cblk):
        return pl.BlockSpec((tm, width), lambda i: (i, cblk))

    one = pl.BlockSpec((1, HEAD_DIM), lambda i: (0, 0))
    return pl.pallas_call(
        body, name=name, grid=(r // tm,),
        in_specs=[rows(att, 0), rows(kv, 2 * att // kv), rows(kv, 2 * att // kv + 1),
                  rows(HEAD_DIM, 0), rows(HEAD_DIM, 0), one, one],
        out_specs=[rows(att, 0), rows(kv, 0), rows(kv, 0)],
        out_shape=[jax.ShapeDtypeStruct((r, att), BF16), jax.ShapeDtypeStruct((r, kv), BF16),
                   jax.ShapeDtypeStruct((r, kv), BF16)],
        compiler_params=_cp(1),
    )(p, p, p, cos, sin, qg, kg)


def _qk_bwd(p, dqr, dgate, dkr, dvr, cos, sin, qg, kg, att, kv, t_ctx, name):
    r = p.shape[0]
    tm = _row_tile(t_ctx)

    def body(q_ref, k_ref, dqr_ref, dgate_ref, dkr_ref, dvr_ref, cos_ref, sin_ref, qg_ref, kg_ref, dp_ref, s_ref):
        @pl.when(pl.program_id(0) == 0)
        def _():
            s_ref[...] = jnp.zeros_like(s_ref)

        cs, sn = cos_ref[...], sin_ref[...]
        for row, (src, dsrc, g_ref, off, nh) in enumerate((
                (q_ref, dqr_ref, qg_ref, 0, att // HEAD_DIM), (k_ref, dkr_ref, kg_ref, 2 * att, kv // HEAD_DIM))):
            dg = jnp.zeros((1, HEAD_DIM), F32)
            for h in range(nh):
                cols = slice(h * HEAD_DIM, (h + 1) * HEAD_DIM)
                n0, rr = _rms(src[:, cols])
                d = dsrc[:, cols]
                dng = d * cs + _partner(d * sn)
                dg = dg + jnp.sum(dng * n0, axis=0, keepdims=True)
                dn0 = dng * g_ref[...]
                dp_ref[:, off + h * HEAD_DIM:off + (h + 1) * HEAD_DIM] = (
                    rr * (dn0 - n0 * jnp.mean(dn0 * n0, axis=-1, keepdims=True))).astype(BF16)
            s_ref[row:row + 1, :] += dg
        dp_ref[:, att:2 * att] = dgate_ref[...]
        dp_ref[:, 2 * att + kv:2 * att + 2 * kv] = dvr_ref[...].astype(BF16)

    def rows(width, cblk):
        return pl.BlockSpec((tm, width), lambda i: (i, cblk))

    one = pl.BlockSpec((1, HEAD_DIM), lambda i: (0, 0))
    return pl.pallas_call(
        body, name=name, grid=(r // tm,),
        in_specs=[rows(att, 0), rows(kv, 2 * att // kv), rows(att, 0), rows(att, 0), rows(kv, 0), rows(kv, 0),
                  rows(HEAD_DIM, 0), rows(HEAD_DIM, 0), one, one],
        out_specs=[rows(2 * att + 2 * kv, 0), pl.BlockSpec((8, HEAD_DIM), lambda i: (0, 0))],
        out_shape=[jax.ShapeDtypeStruct((r, 2 * att + 2 * kv), BF16), jax.ShapeDtypeStruct((8, HEAD_DIM), F32)],
        compiler_params=_cp(1),
    )(p, p, dqr, dgate, dkr, dvr, cos, sin, qg, kg)


def _stack_heads(x, tq):
    return jnp.concatenate([x[:, g * HEAD_DIM:(g + 1) * HEAD_DIM] for g in range(GQA_GROUP)], axis=0)


def _attn_tiles(t_ctx, s_lat, tkl_pref):
    return _div(t_ctx, 256, 8), _div(s_lat, tkl_pref, 8)


NT_DIMS = (((1,), (1,)), ((), ()))
TN_DIMS = (((0,), (0,)), ((), ()))
ATTN_SCALE_LOG2 = ATTN_SCALE * math.log2(math.e)


def _flash_fwd(qr, kr, vb, p, att, t_ctx, name, side=()):
    r = qr.shape[0]
    s_lat = r - t_ctx
    gw = GQA_GROUP * HEAD_DIM
    nkv = att // gw
    tq = _div(t_ctx, 256, 8)
    tkl = _div(s_lat // 2, 1024, 8)
    nq, nq_ctx, n_lat = r // tq, t_ctx // tq, s_lat // tkl
    nl = GQA_GROUP * tq
    align = math.gcd(t_ctx, tkl)
    nj = len(side)
    s_ins, s_in_specs, s_out_shape, s_out_specs, s_sems = _side_plan(side)

    def body(q_ref, k_ref, v_ref, g_ref, *rest):
        s_in, (o_ref, og_ref, lse_ref), s_out = rest[:nj], rest[nj:nj + 3], rest[nj + 3:2 * nj + 3]
        (m_s, l_s, acc_s, st_a, st_b), s_sem = rest[2 * nj + 3:2 * nj + 8], rest[2 * nj + 8:]
        start, finish = _side_run(side, s_in, s_out, s_sem, *_grid_ends((nkv, nq)))
        start()
        qi = pl.program_id(1)
        q4 = _stack_heads(q_ref[...], tq)
        m_s[...] = jnp.full_like(m_s, -1e30)
        l_s[...] = jnp.zeros_like(l_s)
        acc_s[...] = jnp.zeros_like(acc_s)

        def scores(off, size):
            return lax.dot_general(k_ref[pl.ds(off, size), :], q4, NT_DIMS, preferred_element_type=F32)

        def update(st, off, size):
            m_old = m_s[...]
            m_new = jnp.maximum(m_old, jnp.max(st, axis=0, keepdims=True))
            pe = jnp.exp2(st - m_new)
            a = jnp.exp2(m_old - m_new)
            l_s[...] = a * l_s[...] + jnp.sum(pe, axis=0, keepdims=True)
            acc_s[...] = a * acc_s[...] + lax.dot_general(v_ref[pl.ds(off, size), :], pe.astype(BF16), TN_DIMS,
                                                          preferred_element_type=F32)
            m_s[...] = m_new

        def lat_off(n):
            return pl.multiple_of(t_ctx + jnp.minimum(n, n_lat - 1) * tkl, align)

        is_lat = qi >= nq_ctx

        @pl.when(is_lat)
        def _():
            st_a[...] = scores(lat_off(0), tkl)

        update(scores(0, t_ctx), 0, t_ctx)

        def pair(j, carry):
            n = 2 * j
            st_b[...] = scores(lat_off(n + 1), tkl)
            update(st_a[...], lat_off(n), tkl)
            st_a[...] = scores(lat_off(n + 2), tkl)
            update(st_b[...], lat_off(n + 1), tkl)
            return carry

        lax.fori_loop(0, jnp.where(is_lat, n_lat // 2, 0), pair, 0)
        o4 = (acc_s[...] / l_s[...]).T
        lse_ref[...] = m_s[...] + jnp.log2(l_s[...])
        sg = _silu(g_ref[...])
        for g in range(GQA_GROUP):
            cols = slice(g * HEAD_DIM, (g + 1) * HEAD_DIM)
            og = o4[g * tq:(g + 1) * tq, :]
            o_ref[:, cols] = og
            og_ref[:, cols] = (og * sg[:, cols]).astype(BF16)
        finish()

    qspec = pl.BlockSpec((tq, gw), lambda h, i: (i, h))
    kspec = pl.BlockSpec((r, HEAD_DIM), lambda h, i: (0, h))
    return pl.pallas_call(
        body, name=name, grid=(nkv, nq),
        in_specs=[qspec, kspec, kspec, pl.BlockSpec((tq, gw), lambda h, i: (i, att // gw + h))] + s_in_specs,
        out_specs=[qspec, qspec, pl.BlockSpec((None, None, 1, nl), lambda h, i: (h, i, 0, 0))] + s_out_specs,
        out_shape=[jax.ShapeDtypeStruct((r, att), F32), jax.ShapeDtypeStruct((r, att), BF16),
                   jax.ShapeDtypeStruct((nkv, nq, 1, nl), F32)] + s_out_shape,
        scratch_shapes=[pltpu.VMEM((1, nl), F32), pltpu.VMEM((1, nl), F32), pltpu.VMEM((HEAD_DIM, nl), F32),
                        pltpu.VMEM((tkl, nl), F32), pltpu.VMEM((tkl, nl), F32)] + s_sems,
        compiler_params=_cp(2),
    )(qr, kr, vb, p, *s_ins)


def _flash_bwd(qr, kr, vb, p, o, dog, lse, att, kv, t_ctx, name, side=()):
    r = qr.shape[0]
    s_lat = r - t_ctx
    gw = GQA_GROUP * HEAD_DIM
    nkv = att // gw
    tq, tkl = _attn_tiles(t_ctx, s_lat, 1024)
    nq, nq_ctx, n_lat = r // tq, t_ctx // tq, s_lat // tkl
    nl = GQA_GROUP * tq
    nj = len(side)
    s_ins, s_in_specs, s_out_shape, s_out_specs, s_sems = _side_plan(side)

    def body(q_ref, k_ref, v_ref, g_ref, o_ref, dog_ref, lse_ref, *rest):
        s_in, (dq_ref, dgate_ref, dk_ref, dv_ref), s_out = rest[:nj], rest[nj:nj + 4], rest[nj + 4:2 * nj + 4]
        dq_s, s_sem = rest[2 * nj + 4], rest[2 * nj + 5:]
        start, finish = _side_run(side, s_in, s_out, s_sem, *_grid_ends((nkv, nq)))
        start()
        qi = pl.program_id(1)

        @pl.when(qi == 0)
        def _():
            dk_ref[...] = jnp.zeros_like(dk_ref)
            dv_ref[...] = jnp.zeros_like(dv_ref)

        gate, ov, dogv = g_ref[...], o_ref[...], dog_ref[...]
        dgate_ref[...] = (dogv * ov * _dsilu(gate)).astype(BF16)
        do = dogv * _silu(gate)
        do4 = _stack_heads(do, tq)
        delta = jnp.sum((do4 * _stack_heads(ov, tq)).T, axis=0, keepdims=True)
        do4 = do4.astype(BF16)
        q4 = _stack_heads(q_ref[...], tq)
        lse_v = lse_ref[...]
        dq_s[...] = jnp.zeros_like(dq_s)

        def step(off, size):
            kb = k_ref[pl.ds(off, size), :]
            vv = v_ref[pl.ds(off, size), :]
            st = lax.dot_general(kb, q4, NT_DIMS, preferred_element_type=F32)
            pe = jnp.exp2(st - lse_v)
            dp = lax.dot_general(vv, do4, NT_DIMS, preferred_element_type=F32)
            ds = (pe * (dp - delta) * ATTN_SCALE).astype(BF16)
            dv_ref[pl.ds(off, size), :] += jnp.dot(pe.astype(BF16), do4, preferred_element_type=F32)
            dk_ref[pl.ds(off, size), :] += jnp.dot(ds, q4, preferred_element_type=F32)
            dq_s[...] += lax.dot_general(kb, ds, TN_DIMS, preferred_element_type=F32)

        step(0, t_ctx)

        def lat(n, carry):
            step(pl.multiple_of(t_ctx + n * tkl, math.gcd(t_ctx, tkl)), tkl)
            return carry

        lax.fori_loop(0, jnp.where(qi < nq_ctx, 0, n_lat), lat, 0)
        dq4 = dq_s[...].T * (1.0 / ATTN_SCALE_LOG2)
        for g in range(GQA_GROUP):
            dq_ref[:, g * HEAD_DIM:(g + 1) * HEAD_DIM] = dq4[g * tq:(g + 1) * tq, :]
        finish()

    qspec = pl.BlockSpec((tq, gw), lambda h, i: (i, h))
    kspec = pl.BlockSpec((r, HEAD_DIM), lambda h, i: (0, h))
    return pl.pallas_call(
        body, name=name, grid=(nkv, nq),
        in_specs=[qspec, kspec, kspec, pl.BlockSpec((tq, gw), lambda h, i: (i, att // gw + h)), qspec, qspec,
                  pl.BlockSpec((None, None, 1, nl), lambda h, i: (h, i, 0, 0))] + s_in_specs,
        out_specs=[qspec, qspec, kspec, kspec] + s_out_specs,
        out_shape=[jax.ShapeDtypeStruct((r, att), F32), jax.ShapeDtypeStruct((r, att), BF16),
                   jax.ShapeDtypeStruct((r, kv), F32), jax.ShapeDtypeStruct((r, kv), F32)] + s_out_shape,
        scratch_shapes=[pltpu.VMEM((HEAD_DIM, nl), F32)] + s_sems,
        compiler_params=_cp(2),
    )(qr, kr, vb, p, o, dog, lse, *s_ins)


def _rope_tables(t_ctx, s_lat):
    rows_n = s_lat // GRID_W
    row = jnp.repeat(jnp.arange(rows_n, dtype=F32), GRID_W)
    col = jnp.tile(jnp.arange(GRID_W, dtype=F32), rows_n)
    axis_dim = HEAD_DIM // 2
    inv_freq = ROPE_THETA ** (-jnp.arange(0, axis_dim, 2, dtype=F32) / axis_dim)
    ang_r = row[:, None] * inv_freq[None, :]
    ang_c = col[:, None] * inv_freq[None, :]
    cos = jnp.concatenate([jnp.cos(ang_r), jnp.cos(ang_r), jnp.cos(ang_c), jnp.cos(ang_c)], axis=1)
    sin = jnp.concatenate([-jnp.sin(ang_r), jnp.sin(ang_r), -jnp.sin(ang_c), jnp.sin(ang_c)], axis=1)
    cos = jnp.concatenate([jnp.ones((t_ctx, HEAD_DIM), F32), cos], axis=0)
    sin = jnp.concatenate([jnp.zeros((t_ctx, HEAD_DIM), F32), sin], axis=0)
    return cos, sin


def _pad_rows(a, rows):
    return jnp.pad(a, ((0, rows - a.shape[0]), (0, 0)))


def kernel(x, c, ctx, c_ctx, w_mod, b_mod, post_ln_g, post_ln_b, w_in_e, conv_a_w, conv_a_b, norm_a_g, norm_a_b, conv_b_w, w_out_e, w_in_o, q_norm_g, k_norm_g, w_out_o, loss_target, m_c_ctx, m_w_mod, m_b_mod, m_post_ln_g, m_post_ln_b, m_w_in_e, m_conv_a_w, m_conv_a_b, m_norm_a_g, m_norm_a_b, m_conv_b_w, m_w_out_e, m_w_in_o, m_q_norm_g, m_k_norm_g, m_w_out_o, v_c_ctx, v_w_mod, v_b_mod, v_post_ln_g, v_post_ln_b, v_w_in_e, v_conv_a_w, v_conv_a_b, v_norm_a_g, v_norm_a_b, v_conv_b_w, v_w_out_e, v_w_in_o, v_q_norm_g, v_k_norm_g, v_w_out_o):
    depth, d, mcols = w_mod.shape
    s_lat, t_ctx = x.shape[1], ctx.shape[1]
    n_even, n_odd = w_in_e.shape[0], w_in_o.shape[0]
    ka, kb = conv_a_w.shape[1], conv_b_w.shape[1]
    wch = conv_a_w.shape[2] * NDEV
    att = w_out_o.shape[1] * NDEV
    kv = (w_in_o.shape[2] * NDEV - 2 * att) // 2
    alpha = (2.0 * depth) ** 0.25
    me = 4 * lax.axis_index("x") + 2 * lax.axis_index("y") + lax.axis_index("c")

    c_all = _all_gather(_pad_rows(c, 8), "ag_c")[:, 0, :]
    c16 = jnp.concatenate([c_all, _pad_rows(c_ctx[None, :], 8)], axis=0)
    sc16 = _ew(_silu, BF16, "silu_c", c16)
    m_part = _mm(sc16, w_mod, "nn", F32, "mod_fwd", tm=16, tn=mcols, tk=d)
    m_all = _all_gather(m_part, "ag_mod")
    m_full = m_all.reshape(NDEV, 16, depth, mcols).transpose(2, 1, 0, 3).reshape(depth * 16, 3 * d)
    b16 = jnp.broadcast_to(b_mod[:, None, :], (depth, 16, 3 * d)).reshape(depth * 16, 3 * d)
    m_full = _ew(lambda a, b: a + b, F32, "mod_bias", m_full, b16).reshape(depth, 16, 3, d)
    modv = [jnp.stack([m_full[l, 8], lax.dynamic_index_in_dim(m_full[l], me, 0, keepdims=False)])
            for l in range(depth)]

    cw = jnp.concatenate([_pad_rows(conv_a_w[i], 32) for i in range(n_even)]
                         + [_pad_rows(conv_b_w[i], 8) for i in range(n_even)], axis=0)
    cw_all = _all_gather(cw, "ag_convw").transpose(1, 0, 2).reshape(cw.shape[0], wch)
    caw = [cw_all[32 * i:32 * i + ka] for i in range(n_even)]
    cbw = [cw_all[32 * n_even + 8 * i:32 * n_even + 8 * i + kb] for i in range(n_even)]

    sh_in = [_ew(lambda t: t, BF16, f"w_in{l}_bf16", (w_in_e if l % 2 == 0 else w_in_o)[l // 2]) for l in range(depth)]
    sh_out = [_ew(lambda t: t, BF16, f"w_out{l}_bf16", (w_out_e if l % 2 == 0 else w_out_o)[l // 2])
              for l in range(depth)]
    g_in = [None] * depth
    g_out = [None] * depth
    g_in[0] = _all_gather(sh_in[0], "ag_w_in0")

    def wanted(keys):
        return [(kind, l) for kind, l in keys if l < depth and (g_in if kind == "in" else g_out)[l] is None]

    def gather_jobs(keys):
        return [("gather", (sh_in if kind == "in" else sh_out)[l]) for kind, l in keys]

    def hosted(res, keys):
        if not keys:
            return res
        n_own = len(res) - len(keys)
        for (kind, l), g in zip(keys, res[n_own:]):
            if kind == "in":
                g_in[l] = g
            else:
                g_out[l] = g.reshape(1, NDEV * g.shape[1], d)
        return res[0] if n_own == 1 else res[:n_own]

    cos, sin = _rope_tables(t_ctx, s_lat)
    lnp = [jnp.stack([post_ln_g[l], post_ln_b[l]]) for l in range(depth)]

    xb = jnp.concatenate([ctx[0], x[0]], axis=0)
    saved = []
    h = _modulate(xb, modv[0], t_ctx, "modulate0")
    for l in range(depth):
        i = l // 2
        if l % 2 == 0:
            keys = wanted([("out", l), ("in", l + 1)])
            p = hosted(_mm(h, g_in[l], "nn", F32, f"in_proj{l}", side=gather_jobs(keys)), keys)
            cat, u1 = _conv_fwd(p, caw[i], conv_a_b[i][None], norm_a_g[i][None], norm_a_b[i][None], cbw[i],
                                t_ctx, f"conv_fwd{l}")
            keys = wanted([("out", l + 1)])
            y = hosted(_mm(cat, g_out[l], "nn", F32, f"out_proj{l}", side=gather_jobs(keys)), keys)
            saved.append((xb, h, p, cat, u1, y))
        else:
            keys = wanted([("out", l)])
            p = hosted(_mm(h, g_in[l], "nn", F32, f"in_proj{l}", side=gather_jobs(keys)), keys)
            qr, kr, vb = _qk_fwd(p, cos, sin, q_norm_g[i][None], k_norm_g[i][None], att, kv, t_ctx, f"qk_fwd{l}")
            keys = wanted([("in", l + 1), ("out", l + 1)])
            o, og, lse = hosted(_flash_fwd(qr, kr, vb, p, att, t_ctx, f"flash_fwd{l}", side=gather_jobs(keys)), keys)
            y = _mm(og, g_out[l], "nn", F32, f"out_proj{l}")
            saved.append((xb, h, p, qr, kr, vb, o, og, lse, y))
        if l + 1 < depth:
            xb, h = _postln_fwd(xb, y, modv[l], lnp[l], alpha, modv[l + 1], t_ctx, f"postln_fwd{l}")

    small = {}
    dmod = [None] * depth
    recv_in, recv_out = [None] * depth, [None] * depth
    carried = []
    for l in reversed(range(depth)):
        i = l // 2
        sv = saved[l]
        x_in, h, p, y = sv[0], sv[1], sv[2], sv[-1]
        if l == depth - 1:
            loss_blk, dy, dxa, s_ln = _postln_bwd(loss_target[0], x_in, y, modv[l], lnp[l], alpha, t_ctx,
                                                  f"loss_postln_bwd{l}", from_loss=True)
            loss = lax.psum(loss_blk[0, 0], ("x", "y", "c"))
        else:
            dy, dxa, s_ln, s_mod = _postln_bwd(dh, x_in, y, modv[l], lnp[l], alpha, t_ctx, f"postln_bwd{l}",
                                               mod_next=(dxa, modv[l + 1]))
            dmod[l + 1] = jnp.stack([jnp.stack([s_mod[0], s_mod[1], s_ln_above[2]]),
                                     jnp.stack([s_mod[2], s_mod[3], s_ln_above[3]])])
        s_ln_above = s_ln
        small[f"ln{l}"] = s_ln
        mixed = sv[3] if l % 2 == 0 else sv[7]
        k_out = g_out[l].shape[1]
        part = _mm(mixed, dy, "tn", BF16, f"d_w_out{l}", tm=1024, tk=768).reshape(NDEV, k_out // NDEV, d)
        dmixed, recv_out[l] = _mm(dy, g_out[l], "nt", F32, f"d_mixed{l}", side=[("exchange", part)])
        if l % 2 == 0:
            u1 = sv[4]
            du1, dag, s_c1 = _conv_bwd1(dmixed, p, u1, norm_a_g[i][None], norm_a_b[i][None], t_ctx,
                                        f"conv_bwd1_{l}")
            dp, dcaw, dcbw = _conv_bwd2(du1, dmixed, p, dag, caw[i], cbw[i], t_ctx, f"conv_bwd2_{l}")
            small[f"c1_{i}"], small[f"caw{i}"], small[f"cbw{i}"] = s_c1, dcaw, dcbw
        else:
            qr, kr, vb, o, _, lse = sv[3:9]
            dqr, dgate, dkr, dvr, *got = _flash_bwd(qr, kr, vb, p, o, dmixed, lse, att, kv, t_ctx, f"flash_bwd{l}",
                                                    side=carried)
            if carried:
                recv_in[l + 1], carried = got[0], []
            dp, s_qk = _qk_bwd(p, dqr, dgate, dkr, dvr, cos, sin, q_norm_g[i][None], k_norm_g[i][None], att, kv,
                               t_ctx, f"qk_bwd{l}")
            small[f"qk{i}"] = jnp.pad(s_qk, ((0, 0), (0, d - HEAD_DIM)))
        part = _mm(h, dp, "tn", BF16, f"d_w_in{l}", tm=d, tk=768, out_nd=NDEV, side=carried)
        if carried:
            (part, recv_in[l + 1]), carried = part, []
        if l == 0:
            dxb, s_mod, recv_in[0] = _mm(dp, g_in[0], "nt", F32, "d_h0", tm=384, side=[("exchange", part)],
                                         mod_bwd=(x_in, dxa, modv[0], t_ctx))
            dmod[0] = jnp.stack([jnp.stack([s_mod[0], s_mod[1], s_ln[2]]), jnp.stack([s_mod[2], s_mod[3], s_ln[3]])])
        else:
            dh = _mm(dp, g_in[l], "nt", F32, f"d_h{l}")
            carried = [("exchange", part)]
    grad_x = dxb[t_ctx:][None]

    dm_loc = jnp.stack(dmod).reshape(depth * 2, 3 * d)
    dm_all = _all_gather(dm_loc, "ag_dmod").reshape(NDEV, depth, 2, 3 * d)
    dm_ctx = _sum_lead(dm_all[:, :, 0, :], "sum_dmod_ctx")
    dm16 = jnp.concatenate([dm_all[:, :, 1, :].transpose(1, 0, 2), dm_ctx[:, None, :],
                            jnp.zeros((depth, 7, 3 * d), F32)], axis=1)
    g_b_mod = _sum_lead(dm16.transpose(1, 0, 2), "sum_b_mod")
    dm16_me = lax.dynamic_slice_in_dim(dm16.reshape(depth, 16, NDEV, mcols), me, 1, axis=2)
    dm16_me = dm16_me.reshape(depth, 16, mcols).transpose(1, 0, 2).reshape(16, depth * mcols)
    g_w_mod = _mm(sc16, dm16_me, "tn", F32, "mod_bwd_w", tm=d, tn=mcols, tk=16, out_nd=depth)
    dsc16 = _mm(dm16_me, w_mod, "nt", F32, "mod_bwd_c", tm=16, tn=d, tk=mcols)
    small["c_ctx"] = dsc16[8:16]

    names = sorted(small)
    offs, rows = {}, 0
    for nme in names:
        offs[nme] = rows
        rows += small[nme].shape[0]
    sm_all = _all_gather(jnp.concatenate([small[nme] for nme in names], axis=0), "ag_small")
    sm = _sum_lead(sm_all, "sum_small")

    def part(nme, lo, hi):
        return sm[offs[nme] + lo:offs[nme] + hi]

    g_post_ln_g = jnp.concatenate([part(f"ln{l}", 0, 1) for l in range(depth)], axis=0)
    g_post_ln_b = jnp.concatenate([part(f"ln{l}", 1, 2) for l in range(depth)], axis=0)
    g_norm_a_g = jnp.concatenate([part(f"c1_{i}", 0, 1) for i in range(n_even)], axis=0)
    g_norm_a_b = jnp.concatenate([part(f"c1_{i}", 1, 2) for i in range(n_even)], axis=0)
    g_conv_a_b = jnp.concatenate([part(f"c1_{i}", 2, 3) for i in range(n_even)], axis=0)
    g_q_norm_g = jnp.concatenate([part(f"qk{i}", 0, 1)[:, :HEAD_DIM] for i in range(n_odd)], axis=0)
    g_k_norm_g = jnp.concatenate([part(f"qk{i}", 1, 2)[:, :HEAD_DIM] for i in range(n_odd)], axis=0)
    wsh = wch // NDEV
    g_conv_a_w = jnp.stack([lax.dynamic_slice_in_dim(part(f"caw{i}", 0, ka), me * wsh, wsh, axis=1)
                            for i in range(n_even)])
    g_conv_b_w = jnp.stack([lax.dynamic_slice_in_dim(part(f"cbw{i}", 0, kb), me * wsh, wsh, axis=1)
                            for i in range(n_even)])
    g_c_ctx = _ew(lambda a, b: a * _dsilu(b), F32, "d_c_ctx", part("c_ctx", 0, 8), _pad_rows(c_ctx[None, :], 8))[0]

    def updated(recv, layers, prefix, w, m, v):
        outs = [_reduce_adamw(recv[l], w[j], m[j], v[j], f"adamw_{prefix}{j}") for j, l in enumerate(layers)]
        return [jnp.stack([o[t] for o in outs]) for t in range(4)]

    evens, odds = range(0, depth, 2), range(1, depth, 2)
    r_w_in_e = updated(recv_in, evens, "w_in_e", w_in_e, m_w_in_e, v_w_in_e)
    r_w_out_e = updated(recv_out, evens, "w_out_e", w_out_e, m_w_out_e, v_w_out_e)
    r_w_in_o = updated(recv_in, odds, "w_in_o", w_in_o, m_w_in_o, v_w_in_o)
    r_w_out_o = updated(recv_out, odds, "w_out_o", w_out_o, m_w_out_o, v_w_out_o)

    grads = {
        "c_ctx": g_c_ctx, "w_mod": g_w_mod, "b_mod": g_b_mod, "post_ln_g": g_post_ln_g, "post_ln_b": g_post_ln_b,
        "conv_a_w": g_conv_a_w, "conv_a_b": g_conv_a_b, "norm_a_g": g_norm_a_g, "norm_a_b": g_norm_a_b,
        "conv_b_w": g_conv_b_w, "q_norm_g": g_q_norm_g, "k_norm_g": g_k_norm_g,
    }
    state = {
        "c_ctx": (c_ctx, m_c_ctx, v_c_ctx), "w_mod": (w_mod, m_w_mod, v_w_mod), "b_mod": (b_mod, m_b_mod, v_b_mod),
        "post_ln_g": (post_ln_g, m_post_ln_g, v_post_ln_g), "post_ln_b": (post_ln_b, m_post_ln_b, v_post_ln_b),
        "conv_a_w": (conv_a_w, m_conv_a_w, v_conv_a_w), "conv_a_b": (conv_a_b, m_conv_a_b, v_conv_a_b),
        "norm_a_g": (norm_a_g, m_norm_a_g, v_norm_a_g), "norm_a_b": (norm_a_b, m_norm_a_b, v_norm_a_b),
        "conv_b_w": (conv_b_w, m_conv_b_w, v_conv_b_w), "q_norm_g": (q_norm_g, m_q_norm_g, v_q_norm_g),
        "k_norm_g": (k_norm_g, m_k_norm_g, v_k_norm_g),
    }
    res = {"w_in_e": r_w_in_e, "w_out_e": r_w_out_e, "w_in_o": r_w_in_o, "w_out_o": r_w_out_o}
    for nme, g in grads.items():
        w, m, v = state[nme]
        res[nme] = [g] + _adamw(w, g, m, v, f"adamw_{nme}")
    order = ["c_ctx", "w_mod", "b_mod", "post_ln_g", "post_ln_b", "w_in_e", "conv_a_w", "conv_a_b", "norm_a_g",
             "norm_a_b", "conv_b_w", "w_out_e", "w_in_o", "q_norm_g", "k_norm_g", "w_out_o"]
    return (loss, grad_x, *[res[nme][0] for nme in order], *[res[nme][1] for nme in order],
            *[res[nme][2] for nme in order], *[res[nme][3] for nme in order])
```

```python
import functools
import math

import jax
import jax.numpy as jnp
from jax import lax
from jax.experimental import pallas as pl
from jax.experimental.pallas import tpu as pltpu

F32 = jnp.float32
BF16 = jnp.bfloat16

NDEV = 8
GRID_W = 64
HEAD_DIM = 128
GQA_GROUP = 4
ROPE_THETA = 10000.0
LN_EPS = 1e-5
RMS_EPS = 1e-6
ATTN_SCALE = HEAD_DIM ** -0.5
ADAM_LR = 0.001
ADAM_B1 = 0.9
ADAM_B2 = 0.999
ADAM_EPS = 1e-08
ADAM_WD = 0.01
ADAM_STEP = 10
HALO = 16
LANES = 128
VMEM_LIMIT = 56 * 1024 * 1024
ROWS_WIDE = 1056
ROWS_EPILOGUE = 528
MESH = pl.DeviceIdType.MESH


def _cp(n_axes):
    return pltpu.CompilerParams(dimension_semantics=("arbitrary",) * n_axes, vmem_limit_bytes=VMEM_LIMIT)


def _div(dim, pref, mult):
    t = min(pref, dim) // mult * mult
    while t >= mult:
        if dim % t == 0:
            return t
        t -= mult
    return dim


def _sigmoid(x):
    return 1.0 / (1.0 + jnp.exp(-x))


def _silu(x):
    return x * _sigmoid(x)


def _dsilu(x):
    s = _sigmoid(x)
    return s * (1.0 + x * (1.0 - s))


def _ln_norm(z):
    mu = jnp.mean(z, axis=-1, keepdims=True)
    zc = z - mu
    var = jnp.mean(zc * zc, axis=-1, keepdims=True)
    rstd = lax.rsqrt(var + LN_EPS)
    return zc * rstd, rstd


def _ln_bwd(dn, n, rstd):
    return rstd * (dn - jnp.mean(dn, axis=-1, keepdims=True) - n * jnp.mean(dn * n, axis=-1, keepdims=True))


def _partner(x):
    lane = lax.broadcasted_iota(jnp.int32, x.shape, 1)
    return jnp.where((lane % 64) < 32, pltpu.roll(x, 96, 1), pltpu.roll(x, 32, 1))


N_PEERS = NDEV - 1
HBM_SPEC = pl.BlockSpec(memory_space=pltpu.HBM)


def _side_plan(jobs):
    xs = [x for _, x in jobs]
    out_shape = [jax.ShapeDtypeStruct((NDEV,) + x.shape[-2:], x.dtype) for x in xs]
    sems = [pltpu.SemaphoreType.DMA((2 * N_PEERS + 1,)) for _ in jobs]
    return xs, [HBM_SPEC] * len(jobs), out_shape, [HBM_SPEC] * len(jobs), sems


def _side_copies(kind, x_ref, out_ref, sems):
    mx, my, mc = lax.axis_index("x"), lax.axis_index("y"), lax.axis_index("c")
    me = 4 * mx + 2 * my + mc
    own = x_ref.at[me] if kind == "exchange" else x_ref
    copies = [pltpu.make_async_copy(own, out_ref.at[me], sems.at[2 * N_PEERS])]
    for k in range(1, NDEV):
        px, py, pc = mx ^ ((k >> 2) & 1), my ^ ((k >> 1) & 1), mc ^ (k & 1)
        src = x_ref.at[4 * px + 2 * py + pc] if kind == "exchange" else x_ref
        copies.append(pltpu.make_async_remote_copy(
            src_ref=src, dst_ref=out_ref.at[me], send_sem=sems.at[k - 1], recv_sem=sems.at[N_PEERS + k - 1],
            device_id=(px, py, pc), device_id_type=MESH))
    return copies


def _side_run(jobs, in_refs, out_refs, sem_refs, first, last):
    if not jobs:
        return (lambda: None), (lambda: None)

    def start():
        @pl.when(first)
        def _():
            for (kind, _), x_ref, o_ref, sems in zip(jobs, in_refs, out_refs, sem_refs):
                for cp in _side_copies(kind, x_ref, o_ref, sems):
                    cp.start()

    def finish():
        @pl.when(last)
        def _():
            for (kind, _), x_ref, o_ref, sems in zip(jobs, in_refs, out_refs, sem_refs):
                for cp in _side_copies(kind, x_ref, o_ref, sems):
                    cp.wait()

    return start, finish


def _grid_ends(grid):
    first = last = None
    for ax, n in enumerate(grid):
        i = pl.program_id(ax)
        f, l = i == 0, i == n - 1
        first = f if first is None else jnp.logical_and(first, f)
        last = l if last is None else jnp.logical_and(last, l)
    return first, last


def _mm(a, b, mode, out_dtype, name, tm=768, tn=2048, tk=2048, out_nd=1, side=(), mod_bwd=None):
    if mode == "nn":
        m, kdim = a.shape
        nd, _, ns = b.shape
        tm, tn, tk = _div(m, tm, 8), _div(ns, tn, 128), _div(kdim, tk, 128)
        nbs = ns // tn
        grid = (m // tm, nd * nbs, kdim // tk)
        a_spec = pl.BlockSpec((tm, tk), lambda i, j, k: (i, k))
        b_spec = pl.BlockSpec((None, tk, tn), lambda i, j, k: (j // nbs, k, j % nbs))
        o_spec = pl.BlockSpec((tm, tn), lambda i, j, k: (i, j))
        out_shape = (m, nd * ns)
        dims = (((1,), (0,)), ((), ()))
    elif mode == "nt":
        m, _ = a.shape
        nd, ko, ns = b.shape
        tm, tn, tk = _div(m, tm, 8), _div(ko, tn, 128), _div(ns, tk, 128)
        kbs = ns // tk
        grid = (m // tm, ko // tn, nd * kbs)
        a_spec = pl.BlockSpec((tm, tk), lambda i, j, k: (i, k))
        b_spec = pl.BlockSpec((None, tn, tk), lambda i, j, k: (k // kbs, j, k % kbs))
        o_spec = pl.BlockSpec((tm, tn), lambda i, j, k: (i, j))
        out_shape = (m, ko)
        dims = (((1,), (1,)), ((), ()))
    else:
        m, kdim = a.shape
        n = b.shape[1]
        ns = n // out_nd
        tm, tn, tk = _div(kdim, tm, 128), _div(ns, tn, 128), _div(m, tk, 16)
        nbs = ns // tn
        grid = (kdim // tm, out_nd * nbs, m // tk)
        a_spec = pl.BlockSpec((tk, tm), lambda i, j, k: (k, i))
        b_spec = pl.BlockSpec((tk, tn), lambda i, j, k: (k, j))
        o_spec = pl.BlockSpec((None, tm, tn), lambda i, j, k: (j // nbs, i, j % nbs))
        out_shape = (out_nd, kdim, ns)
        dims = (((0,), (0,)), ((), ()))
    nk = grid[2]
    nj = len(side)
    s_ins, s_in_specs, s_out_shape, s_out_specs, s_sems = _side_plan(side)
    e_ins, e_in_specs, e_out_shape, e_out_specs = [], [], [], []
    if mod_bwd is not None:
        assert mode == "nt" and grid[1] == 1
        x_in, dxa, modv, t_ctx = mod_bwd
        e_ins = [x_in, dxa, modv]
        e_in_specs = [pl.BlockSpec((tm, tn), lambda i, j, k: (i, 0)), pl.BlockSpec((tm, tn), lambda i, j, k: (i, 0)),
                      pl.BlockSpec(modv.shape, lambda i, j, k: (0, 0, 0))]
        e_out_shape = [jax.ShapeDtypeStruct((8, tn), F32)]
        e_out_specs = [pl.BlockSpec((8, tn), lambda i, j, k: (0, 0))]
    ne, neo = len(e_ins), len(e_out_shape)

    def body(a_ref, b_ref, *rest):
        e_in, s_in = rest[:ne], rest[ne:ne + nj]
        o_ref, e_out, s_out = rest[ne + nj], rest[ne + nj + 1:ne + nj + 1 + neo], rest[ne + nj + 1 + neo:ne + 2 * nj + 1 + neo]
        scratch = rest[ne + 2 * nj + 1 + neo:]
        acc_ref, s_sem = (None, scratch) if nk == 1 else (scratch[0], scratch[1:])
        first, last = _grid_ends(grid)
        start, finish = _side_run(side, s_in, s_out, s_sem, first, last)
        start()

        def prod():
            return lax.dot_general(a_ref[...].astype(BF16), b_ref[...].astype(BF16), dims,
                                   preferred_element_type=F32)

        def emit(val):
            if mod_bwd is None:
                o_ref[...] = val.astype(o_ref.dtype)
                return
            x_ref, dxa_ref, mv_ref = e_in
            rows = pl.program_id(0) * tm + lax.broadcasted_iota(jnp.int32, (tm, 1), 0)
            is_ctx = rows < t_ctx
            o_ref[...] = dxa_ref[...] + val * (1.0 + jnp.where(is_ctx, mv_ref[0, 1:2, :], mv_ref[1, 1:2, :]))
            dsc = val * x_ref[...]
            for row, (t, keep_ctx) in enumerate(((val, True), (dsc, True), (val, False), (dsc, False))):
                e_out[0][row:row + 1, :] += jnp.sum(jnp.where(is_ctx == keep_ctx, t, 0.0), axis=0, keepdims=True)

        if mod_bwd is not None:
            @pl.when(first)
            def _():
                e_out[0][...] = jnp.zeros_like(e_out[0])

        if nk == 1:
            emit(prod())
        else:
            k = pl.program_id(2)

            @pl.when(k == 0)
            def _():
                acc_ref[...] = jnp.zeros_like(acc_ref)

            acc_ref[...] += prod()

            @pl.when(k == nk - 1)
            def _():
                emit(acc_ref[...])
        finish()

    outs = pl.pallas_call(
        body, name=name, grid=grid, in_specs=[a_spec, b_spec] + e_in_specs + s_in_specs,
        out_specs=[o_spec] + e_out_specs + s_out_specs,
        out_shape=[jax.ShapeDtypeStruct(out_shape, out_dtype)] + e_out_shape + s_out_shape,
        scratch_shapes=([] if nk == 1 else [pltpu.VMEM((tm, tn), F32)]) + s_sems, compiler_params=_cp(3),
    )(a, b, *e_ins, *s_ins)
    return outs if len(outs) > 1 else outs[0]


def _ew(fn, out_dtype, name, *xs):
    rows, cols = xs[0].shape
    tr = rows if rows <= 64 else _div(rows, 256, 16)

    def body(*refs):
        refs[-1][...] = fn(*[r[...] for r in refs[:-1]]).astype(out_dtype)

    spec = pl.BlockSpec((tr, cols), lambda i: (i, 0))
    return pl.pallas_call(
        body, name=name, grid=(rows // tr,), in_specs=[spec] * len(xs), out_specs=spec,
        out_shape=jax.ShapeDtypeStruct((rows, cols), out_dtype), compiler_params=_cp(1),
    )(*xs)


def _sum_lead(x, name):
    n, rows, cols = x.shape
    tr = _div(rows, 64, 8)

    def body(x_ref, o_ref):
        acc = x_ref[0]
        for s in range(1, n):
            acc = acc + x_ref[s]
        o_ref[...] = acc

    return pl.pallas_call(
        body, name=name, grid=(rows // tr,),
        in_specs=[pl.BlockSpec((n, tr, cols), lambda i: (0, i, 0))],
        out_specs=pl.BlockSpec((tr, cols), lambda i: (i, 0)),
        out_shape=jax.ShapeDtypeStruct((rows, cols), F32), compiler_params=_cp(1),
    )(x)


def _adam_math(w, g, m, v):
    m = ADAM_B1 * m + (1.0 - ADAM_B1) * g
    v = ADAM_B2 * v + (1.0 - ADAM_B2) * (g * g)
    m_hat = m / (1.0 - ADAM_B1 ** ADAM_STEP)
    v_hat = v / (1.0 - ADAM_B2 ** ADAM_STEP)
    delta = -ADAM_LR * (m_hat / (jnp.sqrt(v_hat) + ADAM_EPS) + ADAM_WD * w)
    return delta, m, v


def _adamw(w, g, m, v, name):
    shape = w.shape
    cols = shape[-1]
    w2, g2, m2, v2 = [t.reshape(-1, cols) for t in (w, g, m, v)]
    rows = w2.shape[0]
    tr = rows if rows <= 512 else _div(rows, 256, 8)

    def body(w_ref, g_ref, m_ref, v_ref, d_ref, nm_ref, nv_ref):
        d, nm, nv = _adam_math(w_ref[...], g_ref[...], m_ref[...], v_ref[...])
        d_ref[...] = d
        nm_ref[...] = nm
        nv_ref[...] = nv

    spec = pl.BlockSpec((tr, cols), lambda i: (i, 0))
    outs = pl.pallas_call(
        body, name=name, grid=(rows // tr,), in_specs=[spec] * 4, out_specs=[spec] * 3,
        out_shape=[jax.ShapeDtypeStruct((rows, cols), F32)] * 3, compiler_params=_cp(1),
    )(w2, g2, m2, v2)
    return [o.reshape(shape) for o in outs]


def _reduce_adamw(parts, w, m, v, name):
    shape = w.shape
    n, rows, cols = parts.shape
    w2, m2, v2 = [t.reshape(rows, cols) for t in (w, m, v)]
    tr = _div(rows, 128, 16)

    def body(p_ref, w_ref, m_ref, v_ref, g_ref, d_ref, nm_ref, nv_ref):
        g = p_ref[0].astype(F32)
        for s in range(1, n):
            g = g + p_ref[s].astype(F32)
        d, nm, nv = _adam_math(w_ref[...], g, m_ref[...], v_ref[...])
        g_ref[...] = g
        d_ref[...] = d
        nm_ref[...] = nm
        nv_ref[...] = nv

    spec = pl.BlockSpec((tr, cols), lambda i: (i, 0))
    outs = pl.pallas_call(
        body, name=name, grid=(rows // tr,),
        in_specs=[pl.BlockSpec((n, tr, cols), lambda i: (0, i, 0))] + [spec] * 3, out_specs=[spec] * 4,
        out_shape=[jax.ShapeDtypeStruct((rows, cols), F32)] * 4, compiler_params=_cp(1),
    )(parts, w2, m2, v2)
    return [o.reshape(shape) for o in outs]


def _all_gather(x, name):
    rows, cols = x.shape

    def body(x_ref, out_ref, send_sems, recv_sems, local_sem):
        mx, my, mc = lax.axis_index("x"), lax.axis_index("y"), lax.axis_index("c")
        me, sibling = (mx, my, mc), (mx, my, 1 - mc)
        chips = [(1 - mx, my), (mx, 1 - my), (1 - mx, 1 - my)]

        def slab(px, py, pc):
            return out_ref.at[4 * px + 2 * py + pc]

        def copy(k, block, to, src=None):
            return pltpu.make_async_remote_copy(
                src_ref=slab(*block) if src is None else src, dst_ref=slab(*block),
                send_sem=send_sems.at[k], recv_sem=recv_sems.at[k], device_id=to, device_id_type=MESH)

        mine = pltpu.make_async_copy(x_ref, slab(*me), local_sem)
        mine.start()
        first = [copy(0, me, sibling, src=x_ref)]
        first += [copy(1 + j, me, (*chip, mc), src=x_ref) for j, chip in enumerate(chips)]
        for cp in first:
            cp.start()
        passed = [copy(4 + j, (*chip, mc), sibling) for j, chip in enumerate(chips)]
        for j, chip in enumerate(chips):
            copy(1 + j, (*chip, mc), me).wait_recv()
            passed[j].start()
        copy(0, sibling, me).wait_recv()
        for j, chip in enumerate(chips):
            copy(4 + j, (*chip, 1 - mc), me).wait_recv()
        for cp in first + passed:
            cp.wait_send()
        mine.wait()

    return pl.pallas_call(
        body, name=name, out_shape=jax.ShapeDtypeStruct((NDEV, rows, cols), x.dtype),
        in_specs=[pl.BlockSpec(memory_space=pltpu.HBM)], out_specs=pl.BlockSpec(memory_space=pltpu.HBM),
        scratch_shapes=[pltpu.SemaphoreType.DMA((7,)), pltpu.SemaphoreType.DMA((7,)), pltpu.SemaphoreType.DMA],
    )(x)


def _row_tile(t_ctx, pref=256):
    return _div(t_ctx, pref, 8)


def _mod_spec(d, nctx):
    return pl.BlockSpec((None, 3, d), lambda i: (jnp.where(i >= nctx, 1, 0), 0, 0))


def _modulate(xb, modv, t_ctx, name):
    r, d = xb.shape
    tm = _row_tile(t_ctx)
    nctx = t_ctx // tm

    def body(x_ref, mv_ref, h_ref):
        h_ref[...] = (x_ref[...] * (1.0 + mv_ref[1:2, :]) + mv_ref[0:1, :]).astype(BF16)

    row = pl.BlockSpec((tm, d), lambda i: (i, 0))
    return pl.pallas_call(
        body, name=name, grid=(r // tm,), in_specs=[row, _mod_spec(d, nctx)], out_specs=row,
        out_shape=jax.ShapeDtypeStruct((r, d), BF16), compiler_params=_cp(1),
    )(xb, modv)


def _postln_fwd(xb, y, modv, lnp, alpha, next_modv, t_ctx, name):
    r, d = xb.shape
    tm = _row_tile(t_ctx)
    nctx = t_ctx // tm

    def body(x_ref, y_ref, mv_ref, ln_ref, nmv_ref, o_ref, h_ref):
        n, _ = _ln_norm(alpha * x_ref[...] + mv_ref[2:3, :] * y_ref[...])
        xo = n * ln_ref[0:1, :] + ln_ref[1:2, :]
        o_ref[...] = xo
        h_ref[...] = (xo * (1.0 + nmv_ref[1:2, :]) + nmv_ref[0:1, :]).astype(BF16)

    row = pl.BlockSpec((tm, d), lambda i: (i, 0))
    return pl.pallas_call(
        body, name=name, grid=(r // tm,),
        in_specs=[row, row, _mod_spec(d, nctx), pl.BlockSpec((2, d), lambda i: (0, 0)), _mod_spec(d, nctx)],
        out_specs=[row, row],
        out_shape=[jax.ShapeDtypeStruct((r, d), F32), jax.ShapeDtypeStruct((r, d), BF16)], compiler_params=_cp(1),
    )(xb, y, modv, lnp, next_modv)


def _postln_bwd(dxn, xb, y, modv, lnp, alpha, t_ctx, name, from_loss=False, mod_next=None):
    r, d = xb.shape
    tm = _row_tile(t_ctx)
    nctx = t_ctx // tm
    n_lead = 1 if from_loss else 0

    def body(dxn_ref, x_ref, y_ref, mv_ref, ln_ref, *rest):
        extra_in, outs = (rest[:2], rest[2:]) if mod_next is not None else ((), rest)
        dy_ref, dxa_ref, s_ref = outs[n_lead:n_lead + 3]
        i = pl.program_id(0)
        is_ctx = i < nctx

        @pl.when(i == 0)
        def _():
            for o_ref in outs[:n_lead] + outs[n_lead + 2:]:
                o_ref[...] = jnp.zeros_like(o_ref)

        yv = y_ref[...]
        gate = mv_ref[2:3, :]
        n, rstd = _ln_norm(alpha * x_ref[...] + gate * yv)
        if from_loss:
            e = jnp.where(is_ctx, 0.0, n * ln_ref[0:1, :] + ln_ref[1:2, :] - dxn_ref[...])
            outs[0][...] += 0.5 * jnp.sum(jnp.sum(e * e, axis=1, keepdims=True), axis=0, keepdims=True) / d
            dxn_v = e / d
        elif mod_next is not None:
            dxan_ref, nmv_ref = extra_in
            dh = dxn_ref[...]
            dxn_v = dxan_ref[...] + dh * (1.0 + nmv_ref[1:2, :])
            dshift = jnp.sum(dh, axis=0, keepdims=True)
            dscale = jnp.sum(dh * (n * ln_ref[0:1, :] + ln_ref[1:2, :]), axis=0, keepdims=True)
            sn_ref = outs[-1]
            sn_ref[0:1, :] += jnp.where(is_ctx, dshift, 0.0)
            sn_ref[1:2, :] += jnp.where(is_ctx, dscale, 0.0)
            sn_ref[2:3, :] += jnp.where(is_ctx, 0.0, dshift)
            sn_ref[3:4, :] += jnp.where(is_ctx, 0.0, dscale)
        else:
            dxn_v = dxn_ref[...]
        dz = _ln_bwd(dxn_v * ln_ref[0:1, :], n, rstd)
        dy_ref[...] = (gate * dz).astype(BF16)
        dxa_ref[...] = alpha * dz
        s_ref[0:1, :] += jnp.sum(dxn_v * n, axis=0, keepdims=True)
        s_ref[1:2, :] += jnp.sum(dxn_v, axis=0, keepdims=True)
        dgate = jnp.sum(dz * yv, axis=0, keepdims=True)
        s_ref[2:3, :] += jnp.where(is_ctx, dgate, 0.0)
        s_ref[3:4, :] += jnp.where(is_ctx, 0.0, dgate)

    row = pl.BlockSpec((tm, d), lambda i: (i, 0))
    acc = pl.BlockSpec((8, d), lambda i: (0, 0))
    first = pl.BlockSpec((tm, d), lambda i: (jnp.maximum(i - nctx, 0), 0)) if from_loss else row
    loss_spec = [pl.BlockSpec((8, 128), lambda i: (0, 0))] if from_loss else []
    loss_shape = [jax.ShapeDtypeStruct((8, 128), F32)] if from_loss else []
    extra = [] if mod_next is None else list(mod_next)
    return pl.pallas_call(
        body, name=name, grid=(r // tm,),
        in_specs=[first, row, row, _mod_spec(d, nctx), pl.BlockSpec((2, d), lambda i: (0, 0))]
        + ([row, _mod_spec(d, nctx)] if extra else []),
        out_specs=loss_spec + [row, row, acc] + ([acc] if extra else []),
        out_shape=loss_shape + [jax.ShapeDtypeStruct((r, d), BF16), jax.ShapeDtypeStruct((r, d), F32),
                                jax.ShapeDtypeStruct((8, d), F32)]
        + ([jax.ShapeDtypeStruct((8, d), F32)] if extra else []),
        compiler_params=_cp(1),
    )(dxn, xb, y, modv, lnp, *extra)


def _conv_specs(r, w, tm, cblk):
    hb = tm // HALO
    last = r // HALO - 1
    main = pl.BlockSpec((tm, w), lambda i: (i, cblk))
    top = pl.BlockSpec((HALO, w), lambda i: (jnp.maximum(i * hb - 1, 0), cblk))
    bot = pl.BlockSpec((HALO, w), lambda i: (jnp.minimum((i + 1) * hb, last), cblk))
    return [main, top, bot]


def _fill_pad(pad_ref, main, top, bot, top_ok, bot_ok, tm):
    pad_ref[0:HALO, :] = jnp.where(top_ok, top, 0.0)
    pad_ref[HALO:HALO + tm, :] = main
    pad_ref[HALO + tm:2 * HALO + tm, :] = jnp.where(bot_ok, bot, 0.0)


def _edges(i, nctx, nr):
    top_ok = jnp.logical_and(i != 0, i != nctx)
    bot_ok = jnp.logical_and(i != nctx - 1, i != nr - 1)
    return top_ok, bot_ok


def _rot_fill(rot_ref, pad_ref, n):
    rot_ref[0, 0:n, :] = pad_ref[0:n, :]
    for b in range(1, 8):
        rot_ref[b, 0:n, :] = pad_ref[pl.ds(b, n), :]


def _tap(rot_ref, off, tm):
    return rot_ref[off % 8, pl.ds(off - off % 8, tm), :]


def _rows8(x):
    return jnp.sum(x.reshape(x.shape[0] // 8, 8, x.shape[1]), axis=0)


def _conv_fwd(p, caw, cab, nag, nab, cbw, t_ctx, name):
    r = p.shape[0]
    w = p.shape[1] // 7
    ka, kb = caw.shape[0], cbw.shape[0]
    tm = _row_tile(t_ctx, 128)
    nctx, nr = t_ctx // tm, r // tm
    n = tm + 2 * HALO - 8

    def body(av, avt, avb, ag, agt, agb, agate, bx, bxt, bxb, bb, bc, bct, bcb, bgate,
             caw_ref, cab_ref, nag_ref, nab_ref, cbw_ref, cat_ref, u1_ref, pad, rot, v_s):
        top_ok, bot_ok = _edges(pl.program_id(0), nctx, nr)

        def chunk(c, carry):
            cols = pl.ds(pl.multiple_of(c * LANES, LANES), LANES)
            _fill_pad(pad, av[:, cols] * _sigmoid(ag[:, cols]), avt[:, cols] * _sigmoid(agt[:, cols]),
                      avb[:, cols] * _sigmoid(agb[:, cols]), top_ok, bot_ok, tm)
            _rot_fill(rot, pad, n)
            u1 = jnp.zeros((tm, LANES), F32) + cab_ref[:, cols]
            for k in range(ka):
                u1 = u1 + caw_ref[k:k + 1, cols] * _tap(rot, HALO - ka // 2 + k, tm)
            u1_ref[:, cols] = u1
            _fill_pad(pad, bc[:, cols] * bx[:, cols], bct[:, cols] * bxt[:, cols], bcb[:, cols] * bxb[:, cols],
                      top_ok, bot_ok, tm)
            v = jnp.zeros((tm, LANES), F32)
            for k in range(kb):
                v = v + cbw_ref[k:k + 1, cols] * pad[pl.ds(HALO - kb // 2 + k, tm), :]
            v_s[:, cols] = v
            return carry

        lax.fori_loop(0, w // LANES, chunk, 0)
        nrm, _ = _ln_norm(u1_ref[...])
        a_out = _silu(nrm * nag_ref[...] + nab_ref[...]) * _silu(agate[...])
        cat_ref[:, 0:w] = a_out.astype(BF16)
        cat_ref[:, w:2 * w] = (bb[...] * v_s[...] * _silu(bgate[...])).astype(BF16)

    def main(cblk):
        return pl.BlockSpec((tm, w), lambda i: (i, cblk))

    def whole(a):
        return pl.BlockSpec(a.shape, lambda i: (0, 0))

    in_specs = (_conv_specs(r, w, tm, 0) + _conv_specs(r, w, tm, 1) + [main(2)] + _conv_specs(r, w, tm, 3)
                + [main(4)] + _conv_specs(r, w, tm, 5) + [main(6)]
                + [whole(caw), whole(cab), whole(nag), whole(nab), whole(cbw)])
    return pl.pallas_call(
        body, name=name, grid=(nr,), in_specs=in_specs,
        out_specs=[pl.BlockSpec((tm, 2 * w), lambda i: (i, 0)), pl.BlockSpec((tm, w), lambda i: (i, 0))],
        out_shape=[jax.ShapeDtypeStruct((r, 2 * w), BF16), jax.ShapeDtypeStruct((r, w), F32)],
        scratch_shapes=[pltpu.VMEM((tm + 2 * HALO, LANES), F32), pltpu.VMEM((8, n, LANES), F32),
                        pltpu.VMEM((tm, w), F32)],
        compiler_params=_cp(1),
    )(*([p] * 15), caw, cab, nag, nab, cbw)


def _conv_bwd1(dcat, p, u1, nag, nab, t_ctx, name):
    r, w = u1.shape
    tm = _row_tile(t_ctx, 128)

    def body(da_ref, agate_ref, u1_ref, nag_ref, nab_ref, du1_ref, dag_ref, s_ref):
        @pl.when(pl.program_id(0) == 0)
        def _():
            s_ref[...] = jnp.zeros_like(s_ref)

        n, rstd = _ln_norm(u1_ref[...])
        g = nag_ref[...]
        u2 = n * g + nab_ref[...]
        da = da_ref[...]
        ag = agate_ref[...]
        dag_ref[...] = (da * _silu(u2) * _dsilu(ag)).astype(BF16)
        du2 = da * _silu(ag) * _dsilu(u2)
        du1 = _ln_bwd(du2 * g, n, rstd)
        du1_ref[...] = du1
        s_ref[0:1, :] += jnp.sum(du2 * n, axis=0, keepdims=True)
        s_ref[1:2, :] += jnp.sum(du2, axis=0, keepdims=True)
        s_ref[2:3, :] += jnp.sum(du1, axis=0, keepdims=True)

    def win(cblk):
        return pl.BlockSpec((tm, w), lambda i: (i, cblk))

    one = pl.BlockSpec((1, w), lambda i: (0, 0))
    return pl.pallas_call(
        body, name=name, grid=(r // tm,), in_specs=[win(0), win(2), win(0), one, one],
        out_specs=[win(0), win(0), pl.BlockSpec((8, w), lambda i: (0, 0))],
        out_shape=[jax.ShapeDtypeStruct((r, w), F32), jax.ShapeDtypeStruct((r, w), BF16),
                   jax.ShapeDtypeStruct((8, w), F32)],
        compiler_params=_cp(1),
    )(dcat, p, u1, nag, nab)


def _conv_bwd2(du1, dcat, p, dag, caw, cbw, t_ctx, name):
    r, w = du1.shape
    ka, kb = caw.shape[0], cbw.shape[0]
    tm = _row_tile(t_ctx, 128)
    nctx, nr = t_ctx // tm, r // tm
    n = tm + 2 * HALO - 8

    def body(du, dut, dub, av, avt, avb, ag, agt, agb, db, dbt, dbb, bx, bxt, bxb, bb, bbt, bbb,
             bc, bct, bcb, bg, bgt, bgb, dag_ref, caw_ref, cbw_ref, dp_ref, dcaw_ref, dcbw_ref,
             pad, pad2, rot_u, rot_d, acc_a, acc_b):
        i = pl.program_id(0)
        top_ok, bot_ok = _edges(i, nctx, nr)

        @pl.when(i == 0)
        def _():
            acc_a[...] = jnp.zeros_like(acc_a)
            acc_b[...] = jnp.zeros_like(acc_b)

        def chunk(c, carry):
            c0 = pl.multiple_of(c * LANES, LANES)
            cols = pl.ds(c0, LANES)

            def seg(s):
                return pl.ds(pl.multiple_of(s * w + c0, LANES), LANES)

            sg = _sigmoid(ag[:, cols])
            av_m = av[:, cols]
            _fill_pad(pad, av_m * sg, avt[:, cols] * _sigmoid(agt[:, cols]), avb[:, cols] * _sigmoid(agb[:, cols]),
                      top_ok, bot_ok, tm)
            _rot_fill(rot_u, pad, n)
            du_m = du[:, cols]
            _fill_pad(pad, du_m, dut[:, cols], dub[:, cols], top_ok, bot_ok, tm)
            _rot_fill(rot_d, pad, n)
            du0 = jnp.zeros((tm, LANES), F32)
            for k in range(ka):
                du0 = du0 + caw_ref[k:k + 1, cols] * _tap(rot_d, HALO + ka // 2 - k, tm)
                acc_a[k, :, cols] += _rows8(du_m * _tap(rot_u, HALO - ka // 2 + k, tm))
            dp_ref[:, seg(0)] = (du0 * sg).astype(BF16)
            dp_ref[:, seg(1)] = (du0 * av_m * sg * (1.0 - sg)).astype(BF16)
            dp_ref[:, seg(2)] = dag_ref[:, cols]

            bc_m, bx_m = bc[:, cols], bx[:, cols]
            _fill_pad(pad, bc_m * bx_m, bct[:, cols] * bxt[:, cols], bcb[:, cols] * bxb[:, cols],
                      top_ok, bot_ok, tm)
            v = jnp.zeros((tm, LANES), F32)
            for k in range(kb):
                v = v + cbw_ref[k:k + 1, cols] * pad[pl.ds(HALO - kb // 2 + k, tm), :]
            db_m, bb_m, bg_m = db[:, cols], bb[:, cols], bg[:, cols]
            sbg = _silu(bg_m)
            dv_m = db_m * bb_m * sbg
            _fill_pad(pad2, dv_m, dbt[:, cols] * bbt[:, cols] * _silu(bgt[:, cols]),
                      dbb[:, cols] * bbb[:, cols] * _silu(bgb[:, cols]), top_ok, bot_ok, tm)
            dw0 = jnp.zeros((tm, LANES), F32)
            for k in range(kb):
                dw0 = dw0 + cbw_ref[k:k + 1, cols] * pad2[pl.ds(HALO + kb // 2 - k, tm), :]
                acc_b[k, :, cols] += _rows8(dv_m * pad[pl.ds(HALO - kb // 2 + k, tm), :])
            dp_ref[:, seg(3)] = (dw0 * bc_m).astype(BF16)
            dp_ref[:, seg(4)] = (db_m * v * sbg).astype(BF16)
            dp_ref[:, seg(5)] = (dw0 * bx_m).astype(BF16)
            dp_ref[:, seg(6)] = (db_m * bb_m * v * _dsilu(bg_m)).astype(BF16)
            return carry

        lax.fori_loop(0, w // LANES, chunk, 0)

        @pl.when(i == nr - 1)
        def _():
            dcaw_ref[...] = jnp.zeros_like(dcaw_ref)
            dcbw_ref[...] = jnp.zeros_like(dcbw_ref)
            for k in range(ka):
                dcaw_ref[k:k + 1, :] = jnp.sum(acc_a[k], axis=0, keepdims=True)
            for k in range(kb):
                dcbw_ref[k:k + 1, :] = jnp.sum(acc_b[k], axis=0, keepdims=True)

    def whole(a):
        return pl.BlockSpec(a.shape, lambda i: (0, 0))

    in_specs = (_conv_specs(r, w, tm, 0) + _conv_specs(r, w, tm, 0) + _conv_specs(r, w, tm, 1)
                + _conv_specs(r, w, tm, 1) + _conv_specs(r, w, tm, 3) + _conv_specs(r, w, tm, 4)
                + _conv_specs(r, w, tm, 5) + _conv_specs(r, w, tm, 6)
                + [pl.BlockSpec((tm, w), lambda i: (i, 0)), whole(caw), whole(cbw)])
    pad_t = pltpu.VMEM((tm + 2 * HALO, LANES), F32)
    rot_t = pltpu.VMEM((8, n, LANES), F32)
    return pl.pallas_call(
        body, name=name, grid=(nr,), in_specs=in_specs,
        out_specs=[pl.BlockSpec((tm, 7 * w), lambda i: (i, 0)), pl.BlockSpec((32, w), lambda i: (0, 0)),
                   pl.BlockSpec((8, w), lambda i: (0, 0))],
        out_shape=[jax.ShapeDtypeStruct((r, 7 * w), BF16), jax.ShapeDtypeStruct((32, w), F32),
                   jax.ShapeDtypeStruct((8, w), F32)],
        scratch_shapes=[pad_t, pad_t, rot_t, rot_t, pltpu.VMEM((32, 8, w), F32), pltpu.VMEM((8, 8, w), F32)],
        compiler_params=_cp(1),
    )(*([du1] * 3), *([p] * 6), *([dcat] * 3), *([p] * 12), dag, caw, cbw)


def _rms(xh):
    r = lax.rsqrt(jnp.mean(xh * xh, axis=-1, keepdims=True) + RMS_EPS)
    return xh * r, r


def _qk_fwd(p, cos, sin, qg, kg, att, kv, t_ctx, name):
    r = p.shape[0]
    tm = _row_tile(t_ctx)

    def body(q_ref, k_ref, v_ref, cos_ref, sin_ref, qg_ref, kg_ref, qr_ref, kr_ref, vb_ref):
        cs, sn = cos_ref[...], sin_ref[...]
        for src, g_ref, dst, nh, fac in ((q_ref, qg_ref, qr_ref, att // HEAD_DIM, 1.0),
                                         (k_ref, kg_ref, kr_ref, kv // HEAD_DIM, ATTN_SCALE_LOG2)):
            for h in range(nh):
                cols = slice(h * HEAD_DIM, (h + 1) * HEAD_DIM)
                n, _ = _rms(src[:, cols])
                n = n * g_ref[...]
                rot = n * cs + _partner(n) * sn
                dst[:, cols] = (rot if fac == 1.0 else rot * fac).astype(BF16)
        vb_ref[...] = v_ref[...].astype(BF16)

    def rows(width, cblk):
        return pl.BlockSpec((tm, width), lambda i: (i, cblk))

    one = pl.BlockSpec((1, HEAD_DIM), lambda i: (0, 0))
    return pl.pallas_call(
        body, name=name, grid=(r // tm,),
        in_specs=[rows(att, 0), rows(kv, 2 * att // kv), rows(kv, 2 * att // kv + 1),
                  rows(HEAD_DIM, 0), rows(HEAD_DIM, 0), one, one],
        out_specs=[rows(att, 0), rows(kv, 0), rows(kv, 0)],
        out_shape=[jax.ShapeDtypeStruct((r, att), BF16), jax.ShapeDtypeStruct((r, kv), BF16),
                   jax.ShapeDtypeStruct((r, kv), BF16)],
        compiler_params=_cp(1),
    )(p, p, p, cos, sin, qg, kg)


def _qk_bwd(p, dqr, dgate, dkr, dvr, cos, sin, qg, kg, att, kv, t_ctx, name):
    r = p.shape[0]
    tm = _row_tile(t_ctx)

    def body(q_ref, k_ref, dqr_ref, dgate_ref, dkr_ref, dvr_ref, cos_ref, sin_ref, qg_ref, kg_ref, dp_ref, s_ref):
        @pl.when(pl.program_id(0) == 0)
        def _():
            s_ref[...] = jnp.zeros_like(s_ref)

        cs, sn = cos_ref[...], sin_ref[...]
        for row, (src, dsrc, g_ref, off, nh) in enumerate((
                (q_ref, dqr_ref, qg_ref, 0, att // HEAD_DIM), (k_ref, dkr_ref, kg_ref, 2 * att, kv // HEAD_DIM))):
            dg = jnp.zeros((1, HEAD_DIM), F32)
            for h in range(nh):
                cols = slice(h * HEAD_DIM, (h + 1) * HEAD_DIM)
                n0, rr = _rms(src[:, cols])
                d = dsrc[:, cols]
                dng = d * cs + _partner(d * sn)
                dg = dg + jnp.sum(dng * n0, axis=0, keepdims=True)
                dn0 = dng * g_ref[...]
                dp_ref[:, off + h * HEAD_DIM:off + (h + 1) * HEAD_DIM] = (
                    rr * (dn0 - n0 * jnp.mean(dn0 * n0, axis=-1, keepdims=True))).astype(BF16)
            s_ref[row:row + 1, :] += dg
        dp_ref[:, att:2 * att] = dgate_ref[...]
        dp_ref[:, 2 * att + kv:2 * att + 2 * kv] = dvr_ref[...].astype(BF16)

    def rows(width, cblk):
        return pl.BlockSpec((tm, width), lambda i: (i, cblk))

    one = pl.BlockSpec((1, HEAD_DIM), lambda i: (0, 0))
    return pl.pallas_call(
        body, name=name, grid=(r // tm,),
        in_specs=[rows(att, 0), rows(kv, 2 * att // kv), rows(att, 0), rows(att, 0), rows(kv, 0), rows(kv, 0),
                  rows(HEAD_DIM, 0), rows(HEAD_DIM, 0), one, one],
        out_specs=[rows(2 * att + 2 * kv, 0), pl.BlockSpec((8, HEAD_DIM), lambda i: (0, 0))],
        out_shape=[jax.ShapeDtypeStruct((r, 2 * att + 2 * kv), BF16), jax.ShapeDtypeStruct((8, HEAD_DIM), F32)],
        compiler_params=_cp(1),
    )(p, p, dqr, dgate, dkr, dvr, cos, sin, qg, kg)


def _stack_heads(x, tq):
    return jnp.concatenate([x[:, g * HEAD_DIM:(g + 1) * HEAD_DIM] for g in range(GQA_GROUP)], axis=0)


def _attn_tiles(t_ctx, s_lat, tkl_pref):
    return _div(t_ctx, 256, 8), _div(s_lat, tkl_pref, 8)


NT_DIMS = (((1,), (1,)), ((), ()))
TN_DIMS = (((0,), (0,)), ((), ()))
ATTN_SCALE_LOG2 = ATTN_SCALE * math.log2(math.e)


def _flash_fwd(qr, kr, vb, p, att, t_ctx, name, side=()):
    r = qr.shape[0]
    s_lat = r - t_ctx
    gw = GQA_GROUP * HEAD_DIM
    nkv = att // gw
    tq = _div(t_ctx, 256, 8)
    tkl = _div(s_lat // 2, 1024, 8)
    nq, nq_ctx, n_lat = r // tq, t_ctx // tq, s_lat // tkl
    nl = GQA_GROUP * tq
    align = math.gcd(t_ctx, tkl)
    nj = len(side)
    s_ins, s_in_specs, s_out_shape, s_out_specs, s_sems = _side_plan(side)

    def body(q_ref, k_ref, v_ref, g_ref, *rest):
        s_in, (o_ref, og_ref, lse_ref), s_out = rest[:nj], rest[nj:nj + 3], rest[nj + 3:2 * nj + 3]
        (m_s, l_s, acc_s, st_a, st_b), s_sem = rest[2 * nj + 3:2 * nj + 8], rest[2 * nj + 8:]
        start, finish = _side_run(side, s_in, s_out, s_sem, *_grid_ends((nkv, nq)))
        start()
        qi = pl.program_id(1)
        q4 = _stack_heads(q_ref[...], tq)
        m_s[...] = jnp.full_like(m_s, -1e30)
        l_s[...] = jnp.zeros_like(l_s)
        acc_s[...] = jnp.zeros_like(acc_s)

        def scores(off, size):
            return lax.dot_general(k_ref[pl.ds(off, size), :], q4, NT_DIMS, preferred_element_type=F32)

        def update(st, off, size):
            m_old = m_s[...]
            m_new = jnp.maximum(m_old, jnp.max(st, axis=0, keepdims=True))
            pe = jnp.exp2(st - m_new)
            a = jnp.exp2(m_old - m_new)
            l_s[...] = a * l_s[...] + jnp.sum(pe, axis=0, keepdims=True)
            acc_s[...] = a * acc_s[...] + lax.dot_general(v_ref[pl.ds(off, size), :], pe.astype(BF16), TN_DIMS,
                                                          preferred_element_type=F32)
            m_s[...] = m_new

        def lat_off(n):
            return pl.multiple_of(t_ctx + jnp.minimum(n, n_lat - 1) * tkl, align)

        is_lat = qi >= nq_ctx

        @pl.when(is_lat)
        def _():
            st_a[...] = scores(lat_off(0), tkl)

        update(scores(0, t_ctx), 0, t_ctx)

        def pair(j, carry):
            n = 2 * j
            st_b[...] = scores(lat_off(n + 1), tkl)
            update(st_a[...], lat_off(n), tkl)
            st_a[...] = scores(lat_off(n + 2), tkl)
            update(st_b[...], lat_off(n + 1), tkl)
            return carry

        lax.fori_loop(0, jnp.where(is_lat, n_lat // 2, 0), pair, 0)
        o4 = (acc_s[...] / l_s[...]).T
        lse_ref[...] = m_s[...] + jnp.log2(l_s[...])
        sg = _silu(g_ref[...])
        for g in range(GQA_GROUP):
            cols = slice(g * HEAD_DIM, (g + 1) * HEAD_DIM)
            og = o4[g * tq:(g + 1) * tq, :]
            o_ref[:, cols] = og
            og_ref[:, cols] = (og * sg[:, cols]).astype(BF16)
        finish()

    qspec = pl.BlockSpec((tq, gw), lambda h, i: (i, h))
    kspec = pl.BlockSpec((r, HEAD_DIM), lambda h, i: (0, h))
    return pl.pallas_call(
        body, name=name, grid=(nkv, nq),
        in_specs=[qspec, kspec, kspec, pl.BlockSpec((tq, gw), lambda h, i: (i, att // gw + h))] + s_in_specs,
        out_specs=[qspec, qspec, pl.BlockSpec((None, None, 1, nl), lambda h, i: (h, i, 0, 0))] + s_out_specs,
        out_shape=[jax.ShapeDtypeStruct((r, att), F32), jax.ShapeDtypeStruct((r, att), BF16),
                   jax.ShapeDtypeStruct((nkv, nq, 1, nl), F32)] + s_out_shape,
        scratch_shapes=[pltpu.VMEM((1, nl), F32), pltpu.VMEM((1, nl), F32), pltpu.VMEM((HEAD_DIM, nl), F32),
                        pltpu.VMEM((tkl, nl), F32), pltpu.VMEM((tkl, nl), F32)] + s_sems,
        compiler_params=_cp(2),
    )(qr, kr, vb, p, *s_ins)


def _flash_bwd(qr, kr, vb, p, o, dog, lse, att, kv, t_ctx, name, side=()):
    r = qr.shape[0]
    s_lat = r - t_ctx
    gw = GQA_GROUP * HEAD_DIM
    nkv = att // gw
    tq, tkl = _attn_tiles(t_ctx, s_lat, 1024)
    nq, nq_ctx, n_lat = r // tq, t_ctx // tq, s_lat // tkl
    nl = GQA_GROUP * tq
    nj = len(side)
    s_ins, s_in_specs, s_out_shape, s_out_specs, s_sems = _side_plan(side)

    def body(q_ref, k_ref, v_ref, g_ref, o_ref, dog_ref, lse_ref, *rest):
        s_in, (dq_ref, dgate_ref, dk_ref, dv_ref), s_out = rest[:nj], rest[nj:nj + 4], rest[nj + 4:2 * nj + 4]
        dq_s, s_sem = rest[2 * nj + 4], rest[2 * nj + 5:]
        start, finish = _side_run(side, s_in, s_out, s_sem, *_grid_ends((nkv, nq)))
        start()
        qi = pl.program_id(1)

        @pl.when(qi == 0)
        def _():
            dk_ref[...] = jnp.zeros_like(dk_ref)
            dv_ref[...] = jnp.zeros_like(dv_ref)

        gate, ov, dogv = g_ref[...], o_ref[...], dog_ref[...]
        dgate_ref[...] = (dogv * ov * _dsilu(gate)).astype(BF16)
        do = dogv * _silu(gate)
        do4 = _stack_heads(do, tq)
        delta = jnp.sum((do4 * _stack_heads(ov, tq)).T, axis=0, keepdims=True)
        do4 = do4.astype(BF16)
        q4 = _stack_heads(q_ref[...], tq)
        lse_v = lse_ref[...]
        dq_s[...] = jnp.zeros_like(dq_s)

        def step(off, size):
            kb = k_ref[pl.ds(off, size), :]
            vv = v_ref[pl.ds(off, size), :]
            st = lax.dot_general(kb, q4, NT_DIMS, preferred_element_type=F32)
            pe = jnp.exp2(st - lse_v)
            dp = lax.dot_general(vv, do4, NT_DIMS, preferred_element_type=F32)
            ds = (pe * (dp - delta) * ATTN_SCALE).astype(BF16)
            dv_ref[pl.ds(off, size), :] += jnp.dot(pe.astype(BF16), do4, preferred_element_type=F32)
            dk_ref[pl.ds(off, size), :] += jnp.dot(ds, q4, preferred_element_type=F32)
            dq_s[...] += lax.dot_general(kb, ds, TN_DIMS, preferred_element_type=F32)

        step(0, t_ctx)

        def lat(n, carry):
            step(pl.multiple_of(t_ctx + n * tkl, math.gcd(t_ctx, tkl)), tkl)
            return carry

        lax.fori_loop(0, jnp.where(qi < nq_ctx, 0, n_lat), lat, 0)
        dq4 = dq_s[...].T * (1.0 / ATTN_SCALE_LOG2)
        for g in range(GQA_GROUP):
            dq_ref[:, g * HEAD_DIM:(g + 1) * HEAD_DIM] = dq4[g * tq:(g + 1) * tq, :]
        finish()

    qspec = pl.BlockSpec((tq, gw), lambda h, i: (i, h))
    kspec = pl.BlockSpec((r, HEAD_DIM), lambda h, i: (0, h))
    return pl.pallas_call(
        body, name=name, grid=(nkv, nq),
        in_specs=[qspec, kspec, kspec, pl.BlockSpec((tq, gw), lambda h, i: (i, att // gw + h)), qspec, qspec,
                  pl.BlockSpec((None, None, 1, nl), lambda h, i: (h, i, 0, 0))] + s_in_specs,
        out_specs=[qspec, qspec, kspec, kspec] + s_out_specs,
        out_shape=[jax.ShapeDtypeStruct((r, att), F32), jax.ShapeDtypeStruct((r, att), BF16),
                   jax.ShapeDtypeStruct((r, kv), F32), jax.ShapeDtypeStruct((r, kv), F32)] + s_out_shape,
        scratch_shapes=[pltpu.VMEM((HEAD_DIM, nl), F32)] + s_sems,
        compiler_params=_cp(2),
    )(qr, kr, vb, p, o, dog, lse, *s_ins)


def _rope_tables(t_ctx, s_lat):
    rows_n = s_lat // GRID_W
    row = jnp.repeat(jnp.arange(rows_n, dtype=F32), GRID_W)
    col = jnp.tile(jnp.arange(GRID_W, dtype=F32), rows_n)
    axis_dim = HEAD_DIM // 2
    inv_freq = ROPE_THETA ** (-jnp.arange(0, axis_dim, 2, dtype=F32) / axis_dim)
    ang_r = row[:, None] * inv_freq[None, :]
    ang_c = col[:, None] * inv_freq[None, :]
    cos = jnp.concatenate([jnp.cos(ang_r), jnp.cos(ang_r), jnp.cos(ang_c), jnp.cos(ang_c)], axis=1)
    sin = jnp.concatenate([-jnp.sin(ang_r), jnp.sin(ang_r), -jnp.sin(ang_c), jnp.sin(ang_c)], axis=1)
    cos = jnp.concatenate([jnp.ones((t_ctx, HEAD_DIM), F32), cos], axis=0)
    sin = jnp.concatenate([jnp.zeros((t_ctx, HEAD_DIM), F32), sin], axis=0)
    return cos, sin


def _pad_rows(a, rows):
    return jnp.pad(a, ((0, rows - a.shape[0]), (0, 0)))


def kernel(x, c, ctx, c_ctx, w_mod, b_mod, post_ln_g, post_ln_b, w_in_e, conv_a_w, conv_a_b, norm_a_g, norm_a_b, conv_b_w, w_out_e, w_in_o, q_norm_g, k_norm_g, w_out_o, loss_target, m_c_ctx, m_w_mod, m_b_mod, m_post_ln_g, m_post_ln_b, m_w_in_e, m_conv_a_w, m_conv_a_b, m_norm_a_g, m_norm_a_b, m_conv_b_w, m_w_out_e, m_w_in_o, m_q_norm_g, m_k_norm_g, m_w_out_o, v_c_ctx, v_w_mod, v_b_mod, v_post_ln_g, v_post_ln_b, v_w_in_e, v_conv_a_w, v_conv_a_b, v_norm_a_g, v_norm_a_b, v_conv_b_w, v_w_out_e, v_w_in_o, v_q_norm_g, v_k_norm_g, v_w_out_o):
    depth, d, mcols = w_mod.shape
    s_lat, t_ctx = x.shape[1], ctx.shape[1]
    n_even, n_odd = w_in_e.shape[0], w_in_o.shape[0]
    ka, kb = conv_a_w.shape[1], conv_b_w.shape[1]
    wch = conv_a_w.shape[2] * NDEV
    att = w_out_o.shape[1] * NDEV
    kv = (w_in_o.shape[2] * NDEV - 2 * att) // 2
    alpha = (2.0 * depth) ** 0.25
    me = 4 * lax.axis_index("x") + 2 * lax.axis_index("y") + lax.axis_index("c")

    c_all = _all_gather(_pad_rows(c, 8), "ag_c")[:, 0, :]
    c16 = jnp.concatenate([c_all, _pad_rows(c_ctx[None, :], 8)], axis=0)
    sc16 = _ew(_silu, BF16, "silu_c", c16)
    m_part = _mm(sc16, w_mod, "nn", F32, "mod_fwd", tm=16, tn=mcols, tk=d)
    m_all = _all_gather(m_part, "ag_mod")
    m_full = m_all.reshape(NDEV, 16, depth, mcols).transpose(2, 1, 0, 3).reshape(depth * 16, 3 * d)
    b16 = jnp.broadcast_to(b_mod[:, None, :], (depth, 16, 3 * d)).reshape(depth * 16, 3 * d)
    m_full = _ew(lambda a, b: a + b, F32, "mod_bias", m_full, b16).reshape(depth, 16, 3, d)
    modv = [jnp.stack([m_full[l, 8], lax.dynamic_index_in_dim(m_full[l], me, 0, keepdims=False)])
            for l in range(depth)]

    cw = jnp.concatenate([_pad_rows(conv_a_w[i], 32) for i in range(n_even)]
                         + [_pad_rows(conv_b_w[i], 8) for i in range(n_even)], axis=0)
    cw_all = _all_gather(cw, "ag_convw").transpose(1, 0, 2).reshape(cw.shape[0], wch)
    caw = [cw_all[32 * i:32 * i + ka] for i in range(n_even)]
    cbw = [cw_all[32 * n_even + 8 * i:32 * n_even + 8 * i + kb] for i in range(n_even)]

    sh_in = [_ew(lambda t: t, BF16, f"w_in{l}_bf16", (w_in_e if l % 2 == 0 else w_in_o)[l // 2]) for l in range(depth)]
    sh_out = [_ew(lambda t: t, BF16, f"w_out{l}_bf16", (w_out_e if l % 2 == 0 else w_out_o)[l // 2])
              for l in range(depth)]
    g_in = [None] * depth
    g_out = [None] * depth
    g_in[0] = _all_gather(sh_in[0], "ag_w_in0")

    def wanted(keys):
        return [(kind, l) for kind, l in keys if l < depth and (g_in if kind == "in" else g_out)[l] is None]

    def gather_jobs(keys):
        return [("gather", (sh_in if kind == "in" else sh_out)[l]) for kind, l in keys]

    def hosted(res, keys):
        if not keys:
            return res
        n_own = len(res) - len(keys)
        for (kind, l), g in zip(keys, res[n_own:]):
            if kind == "in":
                g_in[l] = g
            else:
                g_out[l] = g.reshape(1, NDEV * g.shape[1], d)
        return res[0] if n_own == 1 else res[:n_own]

    cos, sin = _rope_tables(t_ctx, s_lat)
    lnp = [jnp.stack([post_ln_g[l], post_ln_b[l]]) for l in range(depth)]

    xb = jnp.concatenate([ctx[0], x[0]], axis=0)
    saved = []
    h = _modulate(xb, modv[0], t_ctx, "modulate0")
    for l in range(depth):
        i = l // 2
        if l % 2 == 0:
            keys = wanted([("out", l), ("in", l + 1)])
            p = hosted(_mm(h, g_in[l], "nn", F32, f"in_proj{l}", tm=ROWS_WIDE, side=gather_jobs(keys)), keys)
            cat, u1 = _conv_fwd(p, caw[i], conv_a_b[i][None], norm_a_g[i][None], norm_a_b[i][None], cbw[i],
                                t_ctx, f"conv_fwd{l}")
            keys = wanted([("out", l + 1)])
            y = hosted(_mm(cat, g_out[l], "nn", F32, f"out_proj{l}", side=gather_jobs(keys)), keys)
            saved.append((xb, h, p, cat, u1, y))
        else:
            keys = wanted([("out", l)])
            p = hosted(_mm(h, g_in[l], "nn", F32, f"in_proj{l}", tm=ROWS_WIDE, side=gather_jobs(keys)), keys)
            qr, kr, vb = _qk_fwd(p, cos, sin, q_norm_g[i][None], k_norm_g[i][None], att, kv, t_ctx, f"qk_fwd{l}")
            keys = wanted([("in", l + 1), ("out", l + 1)])
            o, og, lse = hosted(_flash_fwd(qr, kr, vb, p, att, t_ctx, f"flash_fwd{l}", side=gather_jobs(keys)), keys)
            y = _mm(og, g_out[l], "nn", F32, f"out_proj{l}")
            saved.append((xb, h, p, qr, kr, vb, o, og, lse, y))
        if l + 1 < depth:
            xb, h = _postln_fwd(xb, y, modv[l], lnp[l], alpha, modv[l + 1], t_ctx, f"postln_fwd{l}")

    small = {}
    dmod = [None] * depth
    recv_in, recv_out = [None] * depth, [None] * depth
    carried = []
    for l in reversed(range(depth)):
        i = l // 2
        sv = saved[l]
        x_in, h, p, y = sv[0], sv[1], sv[2], sv[-1]
        if l == depth - 1:
            loss_blk, dy, dxa, s_ln = _postln_bwd(loss_target[0], x_in, y, modv[l], lnp[l], alpha, t_ctx,
                                                  f"loss_postln_bwd{l}", from_loss=True)
            loss = lax.psum(loss_blk[0, 0], ("x", "y", "c"))
        else:
            dy, dxa, s_ln, s_mod = _postln_bwd(dh, x_in, y, modv[l], lnp[l], alpha, t_ctx, f"postln_bwd{l}",
                                               mod_next=(dxa, modv[l + 1]))
            dmod[l + 1] = jnp.stack([jnp.stack([s_mod[0], s_mod[1], s_ln_above[2]]),
                                     jnp.stack([s_mod[2], s_mod[3], s_ln_above[3]])])
        s_ln_above = s_ln
        small[f"ln{l}"] = s_ln
        mixed = sv[3] if l % 2 == 0 else sv[7]
        k_out = g_out[l].shape[1]
        part = _mm(mixed, dy, "tn", BF16, f"d_w_out{l}", tm=1024, tk=768).reshape(NDEV, k_out // NDEV, d)
        dmixed, recv_out[l] = _mm(dy, g_out[l], "nt", F32, f"d_mixed{l}", tm=ROWS_WIDE, side=[("exchange", part)])
        if l % 2 == 0:
            u1 = sv[4]
            du1, dag, s_c1 = _conv_bwd1(dmixed, p, u1, norm_a_g[i][None], norm_a_b[i][None], t_ctx,
                                        f"conv_bwd1_{l}")
            dp, dcaw, dcbw = _conv_bwd2(du1, dmixed, p, dag, caw[i], cbw[i], t_ctx, f"conv_bwd2_{l}")
            small[f"c1_{i}"], small[f"caw{i}"], small[f"cbw{i}"] = s_c1, dcaw, dcbw
        else:
            qr, kr, vb, o, _, lse = sv[3:9]
            dqr, dgate, dkr, dvr, *got = _flash_bwd(qr, kr, vb, p, o, dmixed, lse, att, kv, t_ctx, f"flash_bwd{l}",
                                                    side=carried)
            if carried:
                recv_in[l + 1], carried = got[0], []
            dp, s_qk = _qk_bwd(p, dqr, dgate, dkr, dvr, cos, sin, q_norm_g[i][None], k_norm_g[i][None], att, kv,
                               t_ctx, f"qk_bwd{l}")
            small[f"qk{i}"] = jnp.pad(s_qk, ((0, 0), (0, d - HEAD_DIM)))
        part = _mm(h, dp, "tn", BF16, f"d_w_in{l}", tm=d, tk=768, out_nd=NDEV, side=carried)
        if carried:
            (part, recv_in[l + 1]), carried = part, []
        if l == 0:
            dxb, s_mod, recv_in[0] = _mm(dp, g_in[0], "nt", F32, "d_h0", tm=ROWS_EPILOGUE, side=[("exchange", part)],
                                         mod_bwd=(x_in, dxa, modv[0], t_ctx))
            dmod[0] = jnp.stack([jnp.stack([s_mod[0], s_mod[1], s_ln[2]]), jnp.stack([s_mod[2], s_mod[3], s_ln[3]])])
        else:
            dh = _mm(dp, g_in[l], "nt", F32, f"d_h{l}", tm=ROWS_WIDE)
            carried = [("exchange", part)]
    grad_x = dxb[t_ctx:][None]

    dm_loc = jnp.stack(dmod).reshape(depth * 2, 3 * d)
    dm_all = _all_gather(dm_loc, "ag_dmod").reshape(NDEV, depth, 2, 3 * d)
    dm_ctx = _sum_lead(dm_all[:, :, 0, :], "sum_dmod_ctx")
    dm16 = jnp.concatenate([dm_all[:, :, 1, :].transpose(1, 0, 2), dm_ctx[:, None, :],
                            jnp.zeros((depth, 7, 3 * d), F32)], axis=1)
    g_b_mod = _sum_lead(dm16.transpose(1, 0, 2), "sum_b_mod")
    dm16_me = lax.dynamic_slice_in_dim(dm16.reshape(depth, 16, NDEV, mcols), me, 1, axis=2)
    dm16_me = dm16_me.reshape(depth, 16, mcols).transpose(1, 0, 2).reshape(16, depth * mcols)
    g_w_mod = _mm(sc16, dm16_me, "tn", F32, "mod_bwd_w", tm=d, tn=mcols, tk=16, out_nd=depth)
    dsc16 = _mm(dm16_me, w_mod, "nt", F32, "mod_bwd_c", tm=16, tn=d, tk=mcols)
    small["c_ctx"] = dsc16[8:16]

    names = sorted(small)
    offs, rows = {}, 0
    for nme in names:
        offs[nme] = rows
        rows += small[nme].shape[0]
    sm_all = _all_gather(jnp.concatenate([small[nme] for nme in names], axis=0), "ag_small")
    sm = _sum_lead(sm_all, "sum_small")

    def part(nme, lo, hi):
        return sm[offs[nme] + lo:offs[nme] + hi]

    g_post_ln_g = jnp.concatenate([part(f"ln{l}", 0, 1) for l in range(depth)], axis=0)
    g_post_ln_b = jnp.concatenate([part(f"ln{l}", 1, 2) for l in range(depth)], axis=0)
    g_norm_a_g = jnp.concatenate([part(f"c1_{i}", 0, 1) for i in range(n_even)], axis=0)
    g_norm_a_b = jnp.concatenate([part(f"c1_{i}", 1, 2) for i in range(n_even)], axis=0)
    g_conv_a_b = jnp.concatenate([part(f"c1_{i}", 2, 3) for i in range(n_even)], axis=0)
    g_q_norm_g = jnp.concatenate([part(f"qk{i}", 0, 1)[:, :HEAD_DIM] for i in range(n_odd)], axis=0)
    g_k_norm_g = jnp.concatenate([part(f"qk{i}", 1, 2)[:, :HEAD_DIM] for i in range(n_odd)], axis=0)
    wsh = wch // NDEV
    g_conv_a_w = jnp.stack([lax.dynamic_slice_in_dim(part(f"caw{i}", 0, ka), me * wsh, wsh, axis=1)
                            for i in range(n_even)])
    g_conv_b_w = jnp.stack([lax.dynamic_slice_in_dim(part(f"cbw{i}", 0, kb), me * wsh, wsh, axis=1)
                            for i in range(n_even)])
    g_c_ctx = _ew(lambda a, b: a * _dsilu(b), F32, "d_c_ctx", part("c_ctx", 0, 8), _pad_rows(c_ctx[None, :], 8))[0]

    def updated(recv, layers, prefix, w, m, v):
        outs = [_reduce_adamw(recv[l], w[j], m[j], v[j], f"adamw_{prefix}{j}") for j, l in enumerate(layers)]
        return [jnp.stack([o[t] for o in outs]) for t in range(4)]

    evens, odds = range(0, depth, 2), range(1, depth, 2)
    r_w_in_e = updated(recv_in, evens, "w_in_e", w_in_e, m_w_in_e, v_w_in_e)
    r_w_out_e = updated(recv_out, evens, "w_out_e", w_out_e, m_w_out_e, v_w_out_e)
    r_w_in_o = updated(recv_in, odds, "w_in_o", w_in_o, m_w_in_o, v_w_in_o)
    r_w_out_o = updated(recv_out, odds, "w_out_o", w_out_o, m_w_out_o, v_w_out_o)

    grads = {
        "c_ctx": g_c_ctx, "w_mod": g_w_mod, "b_mod": g_b_mod, "post_ln_g": g_post_ln_g, "post_ln_b": g_post_ln_b,
        "conv_a_w": g_conv_a_w, "conv_a_b": g_conv_a_b, "norm_a_g": g_norm_a_g, "norm_a_b": g_norm_a_b,
        "conv_b_w": g_conv_b_w, "q_norm_g": g_q_norm_g, "k_norm_g": g_k_norm_g,
    }
    state = {
        "c_ctx": (c_ctx, m_c_ctx, v_c_ctx), "w_mod": (w_mod, m_w_mod, v_w_mod), "b_mod": (b_mod, m_b_mod, v_b_mod),
        "post_ln_g": (post_ln_g, m_post_ln_g, v_post_ln_g), "post_ln_b": (post_ln_b, m_post_ln_b, v_post_ln_b),
        "conv_a_w": (conv_a_w, m_conv_a_w, v_conv_a_w), "conv_a_b": (conv_a_b, m_conv_a_b, v_conv_a_b),
        "norm_a_g": (norm_a_g, m_norm_a_g, v_norm_a_g), "norm_a_b": (norm_a_b, m_norm_a_b, v_norm_a_b),
        "conv_b_w": (conv_b_w, m_conv_b_w, v_conv_b_w), "q_norm_g": (q_norm_g, m_q_norm_g, v_q_norm_g),
        "k_norm_g": (k_norm_g, m_k_norm_g, v_k_norm_g),
    }
    res = {"w_in_e": r_w_in_e, "w_out_e": r_w_out_e, "w_in_o": r_w_in_o, "w_out_o": r_w_out_o}
    for nme, g in grads.items():
        w, m, v = state[nme]
        res[nme] = [g] + _adamw(w, g, m, v, f"adamw_{nme}")
    order = ["c_ctx", "w_mod", "b_mod", "post_ln_g", "post_ln_b", "w_in_e", "conv_a_w", "conv_a_b", "norm_a_g",
             "norm_a_b", "conv_b_w", "w_out_e", "w_in_o", "q_norm_g", "k_norm_g", "w_out_o"]
    return (loss, grad_x, *[res[nme][0] for nme in order], *[res[nme][1] for nme in order],
            *[res[nme][2] for nme in order], *[res[nme][3] for nme in order])
```

```python
import functools
import math

import jax
import jax.numpy as jnp
from jax import lax
from jax.experimental import pallas as pl
from jax.experimental.pallas import tpu as pltpu

F32 = jnp.float32
BF16 = jnp.bfloat16

NDEV = 8
GRID_W = 64
HEAD_DIM = 128
GQA_GROUP = 4
ROPE_THETA = 10000.0
LN_EPS = 1e-5
RMS_EPS = 1e-6
ATTN_SCALE = HEAD_DIM ** -0.5
ADAM_LR = 0.001
ADAM_B1 = 0.9
ADAM_B2 = 0.999
ADAM_EPS = 1e-08
ADAM_WD = 0.01
ADAM_STEP = 10
HALO = 16
LANES = 128
VMEM_LIMIT = 56 * 1024 * 1024
ROWS_WIDE = 1056
ROWS_EPILOGUE = 528
MESH = pl.DeviceIdType.MESH


def _cp(n_axes):
    return pltpu.CompilerParams(dimension_semantics=("arbitrary",) * n_axes, vmem_limit_bytes=VMEM_LIMIT)


def _div(dim, pref, mult):
    t = min(pref, dim) // mult * mult
    while t >= mult:
        if dim % t == 0:
            return t
        t -= mult
    return dim


def _sigmoid(x):
    return 1.0 / (1.0 + jnp.exp(-x))


def _silu(x):
    return x * _sigmoid(x)


def _dsilu(x):
    s = _sigmoid(x)
    return s * (1.0 + x * (1.0 - s))


def _ln_norm(z):
    mu = jnp.mean(z, axis=-1, keepdims=True)
    zc = z - mu
    var = jnp.mean(zc * zc, axis=-1, keepdims=True)
    rstd = lax.rsqrt(var + LN_EPS)
    return zc * rstd, rstd


def _ln_bwd(dn, n, rstd):
    return rstd * (dn - jnp.mean(dn, axis=-1, keepdims=True) - n * jnp.mean(dn * n, axis=-1, keepdims=True))


def _partner(x):
    lane = lax.broadcasted_iota(jnp.int32, x.shape, 1)
    return jnp.where((lane % 64) < 32, pltpu.roll(x, 96, 1), pltpu.roll(x, 32, 1))


N_PEERS = NDEV - 1
HBM_SPEC = pl.BlockSpec(memory_space=pltpu.HBM)


def _side_plan(jobs):
    xs = [x for _, x in jobs]
    out_shape = [jax.ShapeDtypeStruct((NDEV,) + x.shape[-2:], x.dtype) for x in xs]
    sems = [pltpu.SemaphoreType.DMA((2 * N_PEERS + 1,)) for _ in jobs]
    return xs, [HBM_SPEC] * len(jobs), out_shape, [HBM_SPEC] * len(jobs), sems


def _side_copies(kind, x_ref, out_ref, sems):
    mx, my, mc = lax.axis_index("x"), lax.axis_index("y"), lax.axis_index("c")
    me = 4 * mx + 2 * my + mc
    own = x_ref.at[me] if kind == "exchange" else x_ref
    copies = [pltpu.make_async_copy(own, out_ref.at[me], sems.at[2 * N_PEERS])]
    for k in range(1, NDEV):
        px, py, pc = mx ^ ((k >> 2) & 1), my ^ ((k >> 1) & 1), mc ^ (k & 1)
        src = x_ref.at[4 * px + 2 * py + pc] if kind == "exchange" else x_ref
        copies.append(pltpu.make_async_remote_copy(
            src_ref=src, dst_ref=out_ref.at[me], send_sem=sems.at[k - 1], recv_sem=sems.at[N_PEERS + k - 1],
            device_id=(px, py, pc), device_id_type=MESH))
    return copies


def _side_run(jobs, in_refs, out_refs, sem_refs, first, last):
    if not jobs:
        return (lambda: None), (lambda: None)

    def start():
        @pl.when(first)
        def _():
            for (kind, _), x_ref, o_ref, sems in zip(jobs, in_refs, out_refs, sem_refs):
                for cp in _side_copies(kind, x_ref, o_ref, sems):
                    cp.start()

    def finish():
        @pl.when(last)
        def _():
            for (kind, _), x_ref, o_ref, sems in zip(jobs, in_refs, out_refs, sem_refs):
                for cp in _side_copies(kind, x_ref, o_ref, sems):
                    cp.wait()

    return start, finish


def _grid_ends(grid):
    first = last = None
    for ax, n in enumerate(grid):
        i = pl.program_id(ax)
        f, l = i == 0, i == n - 1
        first = f if first is None else jnp.logical_and(first, f)
        last = l if last is None else jnp.logical_and(last, l)
    return first, last


def _mm(a, b, mode, out_dtype, name, tm=768, tn=2048, tk=2048, out_nd=1, side=(), mod_bwd=None):
    if mode == "nn":
        m, kdim = a.shape
        nd, _, ns = b.shape
        tm, tn, tk = _div(m, tm, 8), _div(ns, tn, 128), _div(kdim, tk, 128)
        nbs = ns // tn
        grid = (m // tm, nd * nbs, kdim // tk)
        a_spec = pl.BlockSpec((tm, tk), lambda i, j, k: (i, k))
        b_spec = pl.BlockSpec((None, tk, tn), lambda i, j, k: (j // nbs, k, j % nbs))
        o_spec = pl.BlockSpec((tm, tn), lambda i, j, k: (i, j))
        out_shape = (m, nd * ns)
        dims = (((1,), (0,)), ((), ()))
    elif mode == "nt":
        m, _ = a.shape
        nd, ko, ns = b.shape
        tm, tn, tk = _div(m, tm, 8), _div(ko, tn, 128), _div(ns, tk, 128)
        kbs = ns // tk
        grid = (m // tm, ko // tn, nd * kbs)
        a_spec = pl.BlockSpec((tm, tk), lambda i, j, k: (i, k))
        b_spec = pl.BlockSpec((None, tn, tk), lambda i, j, k: (k // kbs, j, k % kbs))
        o_spec = pl.BlockSpec((tm, tn), lambda i, j, k: (i, j))
        out_shape = (m, ko)
        dims = (((1,), (1,)), ((), ()))
    else:
        m, kdim = a.shape
        n = b.shape[1]
        ns = n // out_nd
        tm, tn, tk = _div(kdim, tm, 128), _div(ns, tn, 128), _div(m, tk, 16)
        nbs = ns // tn
        grid = (kdim // tm, out_nd * nbs, m // tk)
        a_spec = pl.BlockSpec((tk, tm), lambda i, j, k: (k, i))
        b_spec = pl.BlockSpec((tk, tn), lambda i, j, k: (k, j))
        o_spec = pl.BlockSpec((None, tm, tn), lambda i, j, k: (j // nbs, i, j % nbs))
        out_shape = (out_nd, kdim, ns)
        dims = (((0,), (0,)), ((), ()))
    nk = grid[2]
    nj = len(side)
    s_ins, s_in_specs, s_out_shape, s_out_specs, s_sems = _side_plan(side)
    e_ins, e_in_specs, e_out_shape, e_out_specs = [], [], [], []
    if mod_bwd is not None:
        assert mode == "nt" and grid[1] == 1
        x_in, dxa, modv, t_ctx = mod_bwd
        e_ins = [x_in, dxa, modv]
        e_in_specs = [pl.BlockSpec((tm, tn), lambda i, j, k: (i, 0)), pl.BlockSpec((tm, tn), lambda i, j, k: (i, 0)),
                      pl.BlockSpec(modv.shape, lambda i, j, k: (0, 0, 0))]
        e_out_shape = [jax.ShapeDtypeStruct((8, tn), F32)]
        e_out_specs = [pl.BlockSpec((8, tn), lambda i, j, k: (0, 0))]
    ne, neo = len(e_ins), len(e_out_shape)

    def body(a_ref, b_ref, *rest):
        e_in, s_in = rest[:ne], rest[ne:ne + nj]
        o_ref, e_out, s_out = rest[ne + nj], rest[ne + nj + 1:ne + nj + 1 + neo], rest[ne + nj + 1 + neo:ne + 2 * nj + 1 + neo]
        scratch = rest[ne + 2 * nj + 1 + neo:]
        acc_ref, s_sem = (None, scratch) if nk == 1 else (scratch[0], scratch[1:])
        first, last = _grid_ends(grid)
        start, finish = _side_run(side, s_in, s_out, s_sem, first, last)
        start()

        def prod():
            return lax.dot_general(a_ref[...].astype(BF16), b_ref[...].astype(BF16), dims,
                                   preferred_element_type=F32)

        def emit(val):
            if mod_bwd is None:
                o_ref[...] = val.astype(o_ref.dtype)
                return
            x_ref, dxa_ref, mv_ref = e_in
            rows = pl.program_id(0) * tm + lax.broadcasted_iota(jnp.int32, (tm, 1), 0)
            is_ctx = rows < t_ctx
            o_ref[...] = dxa_ref[...] + val * (1.0 + jnp.where(is_ctx, mv_ref[0, 1:2, :], mv_ref[1, 1:2, :]))
            dsc = val * x_ref[...]
            for row, (t, keep_ctx) in enumerate(((val, True), (dsc, True), (val, False), (dsc, False))):
                e_out[0][row:row + 1, :] += jnp.sum(jnp.where(is_ctx == keep_ctx, t, 0.0), axis=0, keepdims=True)

        if mod_bwd is not None:
            @pl.when(first)
            def _():
                e_out[0][...] = jnp.zeros_like(e_out[0])

        if nk == 1:
            emit(prod())
        else:
            k = pl.program_id(2)

            @pl.when(k == 0)
            def _():
                acc_ref[...] = jnp.zeros_like(acc_ref)

            acc_ref[...] += prod()

            @pl.when(k == nk - 1)
            def _():
                emit(acc_ref[...])
        finish()

    outs = pl.pallas_call(
        body, name=name, grid=grid, in_specs=[a_spec, b_spec] + e_in_specs + s_in_specs,
        out_specs=[o_spec] + e_out_specs + s_out_specs,
        out_shape=[jax.ShapeDtypeStruct(out_shape, out_dtype)] + e_out_shape + s_out_shape,
        scratch_shapes=([] if nk == 1 else [pltpu.VMEM((tm, tn), F32)]) + s_sems, compiler_params=_cp(3),
    )(a, b, *e_ins, *s_ins)
    return outs if len(outs) > 1 else outs[0]


def _ew(fn, out_dtype, name, *xs):
    rows, cols = xs[0].shape
    tr = rows if rows <= 64 else _div(rows, 256, 16)

    def body(*refs):
        refs[-1][...] = fn(*[r[...] for r in refs[:-1]]).astype(out_dtype)

    spec = pl.BlockSpec((tr, cols), lambda i: (i, 0))
    return pl.pallas_call(
        body, name=name, grid=(rows // tr,), in_specs=[spec] * len(xs), out_specs=spec,
        out_shape=jax.ShapeDtypeStruct((rows, cols), out_dtype), compiler_params=_cp(1),
    )(*xs)


def _sum_lead(x, name):
    n, rows, cols = x.shape
    tr = _div(rows, 64, 8)

    def body(x_ref, o_ref):
        acc = x_ref[0]
        for s in range(1, n):
            acc = acc + x_ref[s]
        o_ref[...] = acc

    return pl.pallas_call(
        body, name=name, grid=(rows // tr,),
        in_specs=[pl.BlockSpec((n, tr, cols), lambda i: (0, i, 0))],
        out_specs=pl.BlockSpec((tr, cols), lambda i: (i, 0)),
        out_shape=jax.ShapeDtypeStruct((rows, cols), F32), compiler_params=_cp(1),
    )(x)


def _adam_math(w, g, m, v):
    m = ADAM_B1 * m + (1.0 - ADAM_B1) * g
    v = ADAM_B2 * v + (1.0 - ADAM_B2) * (g * g)
    m_hat = m / (1.0 - ADAM_B1 ** ADAM_STEP)
    v_hat = v / (1.0 - ADAM_B2 ** ADAM_STEP)
    delta = -ADAM_LR * (m_hat / (jnp.sqrt(v_hat) + ADAM_EPS) + ADAM_WD * w)
    return delta, m, v


def _adamw(w, g, m, v, name):
    shape = w.shape
    cols = shape[-1]
    w2, g2, m2, v2 = [t.reshape(-1, cols) for t in (w, g, m, v)]
    rows = w2.shape[0]
    tr = rows if rows <= 512 else _div(rows, 256, 8)

    def body(w_ref, g_ref, m_ref, v_ref, d_ref, nm_ref, nv_ref):
        d, nm, nv = _adam_math(w_ref[...], g_ref[...], m_ref[...], v_ref[...])
        d_ref[...] = d
        nm_ref[...] = nm
        nv_ref[...] = nv

    spec = pl.BlockSpec((tr, cols), lambda i: (i, 0))
    outs = pl.pallas_call(
        body, name=name, grid=(rows // tr,), in_specs=[spec] * 4, out_specs=[spec] * 3,
        out_shape=[jax.ShapeDtypeStruct((rows, cols), F32)] * 3, compiler_params=_cp(1),
    )(w2, g2, m2, v2)
    return [o.reshape(shape) for o in outs]


def _reduce_adamw(parts, w, m, v, name):
    shape = w.shape
    n, rows, cols = parts.shape
    w2, m2, v2 = [t.reshape(rows, cols) for t in (w, m, v)]
    tr = _div(rows, 128, 16)

    def body(p_ref, w_ref, m_ref, v_ref, g_ref, d_ref, nm_ref, nv_ref):
        g = p_ref[0].astype(F32)
        for s in range(1, n):
            g = g + p_ref[s].astype(F32)
        d, nm, nv = _adam_math(w_ref[...], g, m_ref[...], v_ref[...])
        g_ref[...] = g
        d_ref[...] = d
        nm_ref[...] = nm
        nv_ref[...] = nv

    spec = pl.BlockSpec((tr, cols), lambda i: (i, 0))
    outs = pl.pallas_call(
        body, name=name, grid=(rows // tr,),
        in_specs=[pl.BlockSpec((n, tr, cols), lambda i: (0, i, 0))] + [spec] * 3, out_specs=[spec] * 4,
        out_shape=[jax.ShapeDtypeStruct((rows, cols), F32)] * 4, compiler_params=_cp(1),
    )(parts, w2, m2, v2)
    return [o.reshape(shape) for o in outs]


def _all_gather(x, name):
    rows, cols = x.shape

    def body(x_ref, out_ref, send_sems, recv_sems, local_sem):
        mx, my, mc = lax.axis_index("x"), lax.axis_index("y"), lax.axis_index("c")
        me, sibling = (mx, my, mc), (mx, my, 1 - mc)
        chips = [(1 - mx, my), (mx, 1 - my), (1 - mx, 1 - my)]

        def slab(px, py, pc):
            return out_ref.at[4 * px + 2 * py + pc]

        def copy(k, block, to, src=None):
            return pltpu.make_async_remote_copy(
                src_ref=slab(*block) if src is None else src, dst_ref=slab(*block),
                send_sem=send_sems.at[k], recv_sem=recv_sems.at[k], device_id=to, device_id_type=MESH)

        mine = pltpu.make_async_copy(x_ref, slab(*me), local_sem)
        mine.start()
        first = [copy(0, me, sibling, src=x_ref)]
        first += [copy(1 + j, me, (*chip, mc), src=x_ref) for j, chip in enumerate(chips)]
        for cp in first:
            cp.start()
        passed = [copy(4 + j, (*chip, mc), sibling) for j, chip in enumerate(chips)]
        for j, chip in enumerate(chips):
            copy(1 + j, (*chip, mc), me).wait_recv()
            passed[j].start()
        copy(0, sibling, me).wait_recv()
        for j, chip in enumerate(chips):
            copy(4 + j, (*chip, 1 - mc), me).wait_recv()
        for cp in first + passed:
            cp.wait_send()
        mine.wait()

    return pl.pallas_call(
        body, name=name, out_shape=jax.ShapeDtypeStruct((NDEV, rows, cols), x.dtype),
        in_specs=[pl.BlockSpec(memory_space=pltpu.HBM)], out_specs=pl.BlockSpec(memory_space=pltpu.HBM),
        scratch_shapes=[pltpu.SemaphoreType.DMA((7,)), pltpu.SemaphoreType.DMA((7,)), pltpu.SemaphoreType.DMA],
    )(x)


def _row_tile(t_ctx, pref=256):
    return _div(t_ctx, pref, 8)


def _mod_spec(d, nctx):
    return pl.BlockSpec((None, 3, d), lambda i: (jnp.where(i >= nctx, 1, 0), 0, 0))


def _modulate(xb, modv, t_ctx, name):
    r, d = xb.shape
    tm = _row_tile(t_ctx)
    nctx = t_ctx // tm

    def body(x_ref, mv_ref, h_ref):
        h_ref[...] = (x_ref[...] * (1.0 + mv_ref[1:2, :]) + mv_ref[0:1, :]).astype(BF16)

    row = pl.BlockSpec((tm, d), lambda i: (i, 0))
    return pl.pallas_call(
        body, name=name, grid=(r // tm,), in_specs=[row, _mod_spec(d, nctx)], out_specs=row,
        out_shape=jax.ShapeDtypeStruct((r, d), BF16), compiler_params=_cp(1),
    )(xb, modv)


def _postln_fwd(xb, y, modv, lnp, alpha, next_modv, t_ctx, name):
    r, d = xb.shape
    tm = _row_tile(t_ctx)
    nctx = t_ctx // tm

    def body(x_ref, y_ref, mv_ref, ln_ref, nmv_ref, o_ref, h_ref):
        n, _ = _ln_norm(alpha * x_ref[...] + mv_ref[2:3, :] * y_ref[...])
        xo = n * ln_ref[0:1, :] + ln_ref[1:2, :]
        o_ref[...] = xo
        h_ref[...] = (xo * (1.0 + nmv_ref[1:2, :]) + nmv_ref[0:1, :]).astype(BF16)

    row = pl.BlockSpec((tm, d), lambda i: (i, 0))
    return pl.pallas_call(
        body, name=name, grid=(r // tm,),
        in_specs=[row, row, _mod_spec(d, nctx), pl.BlockSpec((2, d), lambda i: (0, 0)), _mod_spec(d, nctx)],
        out_specs=[row, row],
        out_shape=[jax.ShapeDtypeStruct((r, d), F32), jax.ShapeDtypeStruct((r, d), BF16)], compiler_params=_cp(1),
    )(xb, y, modv, lnp, next_modv)


def _postln_bwd(dxn, xb, y, modv, lnp, alpha, t_ctx, name, from_loss=False, mod_next=None):
    r, d = xb.shape
    tm = _row_tile(t_ctx)
    nctx = t_ctx // tm
    n_lead = 1 if from_loss else 0

    def body(dxn_ref, x_ref, y_ref, mv_ref, ln_ref, *rest):
        extra_in, outs = (rest[:2], rest[2:]) if mod_next is not None else ((), rest)
        dy_ref, dxa_ref, s_ref = outs[n_lead:n_lead + 3]
        i = pl.program_id(0)
        is_ctx = i < nctx

        @pl.when(i == 0)
        def _():
            for o_ref in outs[:n_lead] + outs[n_lead + 2:]:
                o_ref[...] = jnp.zeros_like(o_ref)

        yv = y_ref[...]
        gate = mv_ref[2:3, :]
        n, rstd = _ln_norm(alpha * x_ref[...] + gate * yv)
        if from_loss:
            e = jnp.where(is_ctx, 0.0, n * ln_ref[0:1, :] + ln_ref[1:2, :] - dxn_ref[...])
            outs[0][...] += 0.5 * jnp.sum(jnp.sum(e * e, axis=1, keepdims=True), axis=0, keepdims=True) / d
            dxn_v = e / d
        elif mod_next is not None:
            dxan_ref, nmv_ref = extra_in
            dh = dxn_ref[...]
            dxn_v = dxan_ref[...] + dh * (1.0 + nmv_ref[1:2, :])
            dshift = jnp.sum(dh, axis=0, keepdims=True)
            dscale = jnp.sum(dh * (n * ln_ref[0:1, :] + ln_ref[1:2, :]), axis=0, keepdims=True)
            sn_ref = outs[-1]
            sn_ref[0:1, :] += jnp.where(is_ctx, dshift, 0.0)
            sn_ref[1:2, :] += jnp.where(is_ctx, dscale, 0.0)
            sn_ref[2:3, :] += jnp.where(is_ctx, 0.0, dshift)
            sn_ref[3:4, :] += jnp.where(is_ctx, 0.0, dscale)
        else:
            dxn_v = dxn_ref[...]
        dz = _ln_bwd(dxn_v * ln_ref[0:1, :], n, rstd)
        dy_ref[...] = (gate * dz).astype(BF16)
        dxa_ref[...] = alpha * dz
        s_ref[0:1, :] += jnp.sum(dxn_v * n, axis=0, keepdims=True)
        s_ref[1:2, :] += jnp.sum(dxn_v, axis=0, keepdims=True)
        dgate = jnp.sum(dz * yv, axis=0, keepdims=True)
        s_ref[2:3, :] += jnp.where(is_ctx, dgate, 0.0)
        s_ref[3:4, :] += jnp.where(is_ctx, 0.0, dgate)

    row = pl.BlockSpec((tm, d), lambda i: (i, 0))
    acc = pl.BlockSpec((8, d), lambda i: (0, 0))
    first = pl.BlockSpec((tm, d), lambda i: (jnp.maximum(i - nctx, 0), 0)) if from_loss else row
    loss_spec = [pl.BlockSpec((8, 128), lambda i: (0, 0))] if from_loss else []
    loss_shape = [jax.ShapeDtypeStruct((8, 128), F32)] if from_loss else []
    extra = [] if mod_next is None else list(mod_next)
    return pl.pallas_call(
        body, name=name, grid=(r // tm,),
        in_specs=[first, row, row, _mod_spec(d, nctx), pl.BlockSpec((2, d), lambda i: (0, 0))]
        + ([row, _mod_spec(d, nctx)] if extra else []),
        out_specs=loss_spec + [row, row, acc] + ([acc] if extra else []),
        out_shape=loss_shape + [jax.ShapeDtypeStruct((r, d), BF16), jax.ShapeDtypeStruct((r, d), F32),
                                jax.ShapeDtypeStruct((8, d), F32)]
        + ([jax.ShapeDtypeStruct((8, d), F32)] if extra else []),
        compiler_params=_cp(1),
    )(dxn, xb, y, modv, lnp, *extra)


def _conv_specs(r, w, tm, cblk):
    hb = tm // HALO
    last = r // HALO - 1
    main = pl.BlockSpec((tm, w), lambda i: (i, cblk))
    top = pl.BlockSpec((HALO, w), lambda i: (jnp.maximum(i * hb - 1, 0), cblk))
    bot = pl.BlockSpec((HALO, w), lambda i: (jnp.minimum((i + 1) * hb, last), cblk))
    return [main, top, bot]


def _fill_pad(pad_ref, main, top, bot, top_ok, bot_ok, tm):
    pad_ref[0:HALO, :] = jnp.where(top_ok, top, 0.0)
    pad_ref[HALO:HALO + tm, :] = main
    pad_ref[HALO + tm:2 * HALO + tm, :] = jnp.where(bot_ok, bot, 0.0)


def _edges(i, nctx, nr):
    top_ok = jnp.logical_and(i != 0, i != nctx)
    bot_ok = jnp.logical_and(i != nctx - 1, i != nr - 1)
    return top_ok, bot_ok


def _rot_fill(rot_ref, pad_ref, n):
    rot_ref[0, 0:n, :] = pad_ref[0:n, :]
    for b in range(1, 8):
        rot_ref[b, 0:n, :] = pad_ref[pl.ds(b, n), :]


def _tap(rot_ref, off, tm):
    return rot_ref[off % 8, pl.ds(off - off % 8, tm), :]


def _rows8(x):
    return jnp.sum(x.reshape(x.shape[0] // 8, 8, x.shape[1]), axis=0)


def _conv_fwd(p, caw, cab, nag, nab, cbw, t_ctx, name):
    r = p.shape[0]
    w = p.shape[1] // 7
    ka, kb = caw.shape[0], cbw.shape[0]
    tm = _row_tile(t_ctx, 128)
    nctx, nr = t_ctx // tm, r // tm
    n = tm + 2 * HALO - 8

    def body(av, avt, avb, ag, agt, agb, agate, bx, bxt, bxb, bb, bc, bct, bcb, bgate,
             caw_ref, cab_ref, nag_ref, nab_ref, cbw_ref, cat_ref, u1_ref, pad, rot, v_s):
        top_ok, bot_ok = _edges(pl.program_id(0), nctx, nr)

        def chunk(c, carry):
            cols = pl.ds(pl.multiple_of(c * LANES, LANES), LANES)
            _fill_pad(pad, av[:, cols] * _sigmoid(ag[:, cols]), avt[:, cols] * _sigmoid(agt[:, cols]),
                      avb[:, cols] * _sigmoid(agb[:, cols]), top_ok, bot_ok, tm)
            _rot_fill(rot, pad, n)
            u1 = jnp.zeros((tm, LANES), F32) + cab_ref[:, cols]
            for k in range(ka):
                u1 = u1 + caw_ref[k:k + 1, cols] * _tap(rot, HALO - ka // 2 + k, tm)
            u1_ref[:, cols] = u1
            _fill_pad(pad, bc[:, cols] * bx[:, cols], bct[:, cols] * bxt[:, cols], bcb[:, cols] * bxb[:, cols],
                      top_ok, bot_ok, tm)
            v = jnp.zeros((tm, LANES), F32)
            for k in range(kb):
                v = v + cbw_ref[k:k + 1, cols] * pad[pl.ds(HALO - kb // 2 + k, tm), :]
            v_s[:, cols] = v
            return carry

        lax.fori_loop(0, w // LANES, chunk, 0)
        nrm, _ = _ln_norm(u1_ref[...])
        a_out = _silu(nrm * nag_ref[...] + nab_ref[...]) * _silu(agate[...])
        cat_ref[:, 0:w] = a_out.astype(BF16)
        cat_ref[:, w:2 * w] = (bb[...] * v_s[...] * _silu(bgate[...])).astype(BF16)

    def main(cblk):
        return pl.BlockSpec((tm, w), lambda i: (i, cblk))

    def whole(a):
        return pl.BlockSpec(a.shape, lambda i: (0, 0))

    in_specs = (_conv_specs(r, w, tm, 0) + _conv_specs(r, w, tm, 1) + [main(2)] + _conv_specs(r, w, tm, 3)
                + [main(4)] + _conv_specs(r, w, tm, 5) + [main(6)]
                + [whole(caw), whole(cab), whole(nag), whole(nab), whole(cbw)])
    return pl.pallas_call(
        body, name=name, grid=(nr,), in_specs=in_specs,
        out_specs=[pl.BlockSpec((tm, 2 * w), lambda i: (i, 0)), pl.BlockSpec((tm, w), lambda i: (i, 0))],
        out_shape=[jax.ShapeDtypeStruct((r, 2 * w), BF16), jax.ShapeDtypeStruct((r, w), F32)],
        scratch_shapes=[pltpu.VMEM((tm + 2 * HALO, LANES), F32), pltpu.VMEM((8, n, LANES), F32),
                        pltpu.VMEM((tm, w), F32)],
        compiler_params=_cp(1),
    )(*([p] * 15), caw, cab, nag, nab, cbw)


def _conv_bwd1(dcat, p, u1, nag, nab, t_ctx, name):
    r, w = u1.shape
    tm = _row_tile(t_ctx)

    def body(da_ref, agate_ref, u1_ref, nag_ref, nab_ref, du1_ref, dag_ref, s_ref):
        @pl.when(pl.program_id(0) == 0)
        def _():
            s_ref[...] = jnp.zeros_like(s_ref)

        n, rstd = _ln_norm(u1_ref[...])
        g = nag_ref[...]
        u2 = n * g + nab_ref[...]
        da = da_ref[...]
        ag = agate_ref[...]
        dag_ref[...] = (da * _silu(u2) * _dsilu(ag)).astype(BF16)
        du2 = da * _silu(ag) * _dsilu(u2)
        du1 = _ln_bwd(du2 * g, n, rstd)
        du1_ref[...] = du1
        s_ref[0:1, :] += jnp.sum(du2 * n, axis=0, keepdims=True)
        s_ref[1:2, :] += jnp.sum(du2, axis=0, keepdims=True)
        s_ref[2:3, :] += jnp.sum(du1, axis=0, keepdims=True)

    def win(cblk):
        return pl.BlockSpec((tm, w), lambda i: (i, cblk))

    one = pl.BlockSpec((1, w), lambda i: (0, 0))
    return pl.pallas_call(
        body, name=name, grid=(r // tm,), in_specs=[win(0), win(2), win(0), one, one],
        out_specs=[win(0), win(0), pl.BlockSpec((8, w), lambda i: (0, 0))],
        out_shape=[jax.ShapeDtypeStruct((r, w), F32), jax.ShapeDtypeStruct((r, w), BF16),
                   jax.ShapeDtypeStruct((8, w), F32)],
        compiler_params=_cp(1),
    )(dcat, p, u1, nag, nab)


def _conv_bwd2(du1, dcat, p, dag, caw, cbw, t_ctx, name):
    r, w = du1.shape
    ka, kb = caw.shape[0], cbw.shape[0]
    tm = _row_tile(t_ctx, 128)
    nctx, nr = t_ctx // tm, r // tm
    n = tm + 2 * HALO - 8

    def body(du, dut, dub, av, avt, avb, ag, agt, agb, db, dbt, dbb, bx, bxt, bxb, bb, bbt, bbb,
             bc, bct, bcb, bg, bgt, bgb, dag_ref, caw_ref, cbw_ref, dp_ref, dcaw_ref, dcbw_ref,
             pad, pad2, rot_u, rot_d, acc_a, acc_b):
        i = pl.program_id(0)
        top_ok, bot_ok = _edges(i, nctx, nr)

        @pl.when(i == 0)
        def _():
            acc_a[...] = jnp.zeros_like(acc_a)
            acc_b[...] = jnp.zeros_like(acc_b)

        def chunk(c, carry):
            c0 = pl.multiple_of(c * LANES, LANES)
            cols = pl.ds(c0, LANES)

            def seg(s):
                return pl.ds(pl.multiple_of(s * w + c0, LANES), LANES)

            sg = _sigmoid(ag[:, cols])
            av_m = av[:, cols]
            _fill_pad(pad, av_m * sg, avt[:, cols] * _sigmoid(agt[:, cols]), avb[:, cols] * _sigmoid(agb[:, cols]),
                      top_ok, bot_ok, tm)
            _rot_fill(rot_u, pad, n)
            du_m = du[:, cols]
            _fill_pad(pad, du_m, dut[:, cols], dub[:, cols], top_ok, bot_ok, tm)
            _rot_fill(rot_d, pad, n)
            du0 = jnp.zeros((tm, LANES), F32)
            for k in range(ka):
                du0 = du0 + caw_ref[k:k + 1, cols] * _tap(rot_d, HALO + ka // 2 - k, tm)
                acc_a[k, :, cols] += _rows8(du_m * _tap(rot_u, HALO - ka // 2 + k, tm))
            dp_ref[:, seg(0)] = (du0 * sg).astype(BF16)
            dp_ref[:, seg(1)] = (du0 * av_m * sg * (1.0 - sg)).astype(BF16)
            dp_ref[:, seg(2)] = dag_ref[:, cols]

            bc_m, bx_m = bc[:, cols], bx[:, cols]
            _fill_pad(pad, bc_m * bx_m, bct[:, cols] * bxt[:, cols], bcb[:, cols] * bxb[:, cols],
                      top_ok, bot_ok, tm)
            v = jnp.zeros((tm, LANES), F32)
            for k in range(kb):
                v = v + cbw_ref[k:k + 1, cols] * pad[pl.ds(HALO - kb // 2 + k, tm), :]
            db_m, bb_m, bg_m = db[:, cols], bb[:, cols], bg[:, cols]
            sbg = _silu(bg_m)
            dv_m = db_m * bb_m * sbg
            _fill_pad(pad2, dv_m, dbt[:, cols] * bbt[:, cols] * _silu(bgt[:, cols]),
                      dbb[:, cols] * bbb[:, cols] * _silu(bgb[:, cols]), top_ok, bot_ok, tm)
            dw0 = jnp.zeros((tm, LANES), F32)
            for k in range(kb):
                dw0 = dw0 + cbw_ref[k:k + 1, cols] * pad2[pl.ds(HALO + kb // 2 - k, tm), :]
                acc_b[k, :, cols] += _rows8(dv_m * pad[pl.ds(HALO - kb // 2 + k, tm), :])
            dp_ref[:, seg(3)] = (dw0 * bc_m).astype(BF16)
            dp_ref[:, seg(4)] = (db_m * v * sbg).astype(BF16)
            dp_ref[:, seg(5)] = (dw0 * bx_m).astype(BF16)
            dp_ref[:, seg(6)] = (db_m * bb_m * v * _dsilu(bg_m)).astype(BF16)
            return carry

        lax.fori_loop(0, w // LANES, chunk, 0)

        @pl.when(i == nr - 1)
        def _():
            dcaw_ref[...] = jnp.zeros_like(dcaw_ref)
            dcbw_ref[...] = jnp.zeros_like(dcbw_ref)
            for k in range(ka):
                dcaw_ref[k:k + 1, :] = jnp.sum(acc_a[k], axis=0, keepdims=True)
            for k in range(kb):
                dcbw_ref[k:k + 1, :] = jnp.sum(acc_b[k], axis=0, keepdims=True)

    def whole(a):
        return pl.BlockSpec(a.shape, lambda i: (0, 0))

    in_specs = (_conv_specs(r, w, tm, 0) + _conv_specs(r, w, tm, 0) + _conv_specs(r, w, tm, 1)
                + _conv_specs(r, w, tm, 1) + _conv_specs(r, w, tm, 3) + _conv_specs(r, w, tm, 4)
                + _conv_specs(r, w, tm, 5) + _conv_specs(r, w, tm, 6)
                + [pl.BlockSpec((tm, w), lambda i: (i, 0)), whole(caw), whole(cbw)])
    pad_t = pltpu.VMEM((tm + 2 * HALO, LANES), F32)
    rot_t = pltpu.VMEM((8, n, LANES), F32)
    return pl.pallas_call(
        body, name=name, grid=(nr,), in_specs=in_specs,
        out_specs=[pl.BlockSpec((tm, 7 * w), lambda i: (i, 0)), pl.BlockSpec((32, w), lambda i: (0, 0)),
                   pl.BlockSpec((8, w), lambda i: (0, 0))],
        out_shape=[jax.ShapeDtypeStruct((r, 7 * w), BF16), jax.ShapeDtypeStruct((32, w), F32),
                   jax.ShapeDtypeStruct((8, w), F32)],
        scratch_shapes=[pad_t, pad_t, rot_t, rot_t, pltpu.VMEM((32, 8, w), F32), pltpu.VMEM((8, 8, w), F32)],
        compiler_params=_cp(1),
    )(*([du1] * 3), *([p] * 6), *([dcat] * 3), *([p] * 12), dag, caw, cbw)


def _rms(xh):
    r = lax.rsqrt(jnp.mean(xh * xh, axis=-1, keepdims=True) + RMS_EPS)
    return xh * r, r


def _qk_fwd(p, cos, sin, qg, kg, att, kv, t_ctx, name):
    r = p.shape[0]
    tm = _row_tile(t_ctx)

    def body(q_ref, k_ref, v_ref, cos_ref, sin_ref, qg_ref, kg_ref, qr_ref, kr_ref, vb_ref):
        cs, sn = cos_ref[...], sin_ref[...]
        for src, g_ref, dst, nh, fac in ((q_ref, qg_ref, qr_ref, att // HEAD_DIM, 1.0),
                                         (k_ref, kg_ref, kr_ref, kv // HEAD_DIM, ATTN_SCALE_LOG2)):
            for h in range(nh):
                cols = slice(h * HEAD_DIM, (h + 1) * HEAD_DIM)
                n, _ = _rms(src[:, cols])
                n = n * g_ref[...]
                rot = n * cs + _partner(n) * sn
                dst[:, cols] = (rot if fac == 1.0 else rot * fac).astype(BF16)
        vb_ref[...] = v_ref[...].astype(BF16)

    def rows(width, cblk):
        return pl.BlockSpec((tm, width), lambda i: (i, cblk))

    one = pl.BlockSpec((1, HEAD_DIM), lambda i: (0, 0))
    return pl.pallas_call(
        body, name=name, grid=(r // tm,),
        in_specs=[rows(att, 0), rows(kv, 2 * att // kv), rows(kv, 2 * att // kv + 1),
                  rows(HEAD_DIM, 0), rows(HEAD_DIM, 0), one, one],
        out_specs=[rows(att, 0), rows(kv, 0), rows(kv, 0)],
        out_shape=[jax.ShapeDtypeStruct((r, att), BF16), jax.ShapeDtypeStruct((r, kv), BF16),
                   jax.ShapeDtypeStruct((r, kv), BF16)],
        compiler_params=_cp(1),
    )(p, p, p, cos, sin, qg, kg)


def _qk_bwd(p, dqr, dgate, dkr, dvr, cos, sin, qg, kg, att, kv, t_ctx, name):
    r = p.shape[0]
    tm = _row_tile(t_ctx)

    def body(q_ref, k_ref, dqr_ref, dgate_ref, dkr_ref, dvr_ref, cos_ref, sin_ref, qg_ref, kg_ref, dp_ref, s_ref):
        @pl.when(pl.program_id(0) == 0)
        def _():
            s_ref[...] = jnp.zeros_like(s_ref)

        cs, sn = cos_ref[...], sin_ref[...]
        for row, (src, dsrc, g_ref, off, nh) in enumerate((
                (q_ref, dqr_ref, qg_ref, 0, att // HEAD_DIM), (k_ref, dkr_ref, kg_ref, 2 * att, kv // HEAD_DIM))):
            dg = jnp.zeros((1, HEAD_DIM), F32)
            for h in range(nh):
                cols = slice(h * HEAD_DIM, (h + 1) * HEAD_DIM)
                n0, rr = _rms(src[:, cols])
                d = dsrc[:, cols]
                dng = d * cs + _partner(d * sn)
                dg = dg + jnp.sum(dng * n0, axis=0, keepdims=True)
                dn0 = dng * g_ref[...]
                dp_ref[:, off + h * HEAD_DIM:off + (h + 1) * HEAD_DIM] = (
                    rr * (dn0 - n0 * jnp.mean(dn0 * n0, axis=-1, keepdims=True))).astype(BF16)
            s_ref[row:row + 1, :] += dg
        dp_ref[:, att:2 * att] = dgate_ref[...]
        dp_ref[:, 2 * att + kv:2 * att + 2 * kv] = dvr_ref[...].astype(BF16)

    def rows(width, cblk):
        return pl.BlockSpec((tm, width), lambda i: (i, cblk))

    one = pl.BlockSpec((1, HEAD_DIM), lambda i: (0, 0))
    return pl.pallas_call(
        body, name=name, grid=(r // tm,),
        in_specs=[rows(att, 0), rows(kv, 2 * att // kv), rows(att, 0), rows(att, 0), rows(kv, 0), rows(kv, 0),
                  rows(HEAD_DIM, 0), rows(HEAD_DIM, 0), one, one],
        out_specs=[rows(2 * att + 2 * kv, 0), pl.BlockSpec((8, HEAD_DIM), lambda i: (0, 0))],
        out_shape=[jax.ShapeDtypeStruct((r, 2 * att + 2 * kv), BF16), jax.ShapeDtypeStruct((8, HEAD_DIM), F32)],
        compiler_params=_cp(1),
    )(p, p, dqr, dgate, dkr, dvr, cos, sin, qg, kg)


def _stack_heads(x, tq):
    return jnp.concatenate([x[:, g * HEAD_DIM:(g + 1) * HEAD_DIM] for g in range(GQA_GROUP)], axis=0)


def _attn_tiles(t_ctx, s_lat, tkl_pref):
    return _div(t_ctx, 256, 8), _div(s_lat, tkl_pref, 8)


NT_DIMS = (((1,), (1,)), ((), ()))
TN_DIMS = (((0,), (0,)), ((), ()))
ATTN_SCALE_LOG2 = ATTN_SCALE * math.log2(math.e)


def _flash_fwd(qr, kr, vb, p, att, t_ctx, name, side=()):
    r = qr.shape[0]
    s_lat = r - t_ctx
    gw = GQA_GROUP * HEAD_DIM
    nkv = att // gw
    tq = _div(t_ctx, 256, 8)
    tkl = _div(s_lat // 2, 1024, 8)
    nq, nq_ctx, n_lat = r // tq, t_ctx // tq, s_lat // tkl
    nl = GQA_GROUP * tq
    align = math.gcd(t_ctx, tkl)
    nj = len(side)
    s_ins, s_in_specs, s_out_shape, s_out_specs, s_sems = _side_plan(side)

    def body(q_ref, k_ref, v_ref, g_ref, *rest):
        s_in, (o_ref, og_ref, lse_ref), s_out = rest[:nj], rest[nj:nj + 3], rest[nj + 3:2 * nj + 3]
        (m_s, l_s, acc_s, st_a, st_b), s_sem = rest[2 * nj + 3:2 * nj + 8], rest[2 * nj + 8:]
        start, finish = _side_run(side, s_in, s_out, s_sem, *_grid_ends((nkv, nq)))
        start()
        qi = pl.program_id(1)
        q4 = _stack_heads(q_ref[...], tq)
        m_s[...] = jnp.full_like(m_s, -1e30)
        l_s[...] = jnp.zeros_like(l_s)
        acc_s[...] = jnp.zeros_like(acc_s)

        def scores(off, size):
            return lax.dot_general(k_ref[pl.ds(off, size), :], q4, NT_DIMS, preferred_element_type=F32)

        def update(st, off, size):
            m_old = m_s[...]
            m_new = jnp.maximum(m_old, jnp.max(st, axis=0, keepdims=True))
            pe = jnp.exp2(st - m_new)
            a = jnp.exp2(m_old - m_new)
            l_s[...] = a * l_s[...] + jnp.sum(pe, axis=0, keepdims=True)
            acc_s[...] = a * acc_s[...] + lax.dot_general(v_ref[pl.ds(off, size), :], pe.astype(BF16), TN_DIMS,
                                                          preferred_element_type=F32)
            m_s[...] = m_new

        def lat_off(n):
            return pl.multiple_of(t_ctx + jnp.minimum(n, n_lat - 1) * tkl, align)

        is_lat = qi >= nq_ctx

        @pl.when(is_lat)
        def _():
            st_a[...] = scores(lat_off(0), tkl)

        update(scores(0, t_ctx), 0, t_ctx)

        def pair(j, carry):
            n = 2 * j
            st_b[...] = scores(lat_off(n + 1), tkl)
            update(st_a[...], lat_off(n), tkl)
            st_a[...] = scores(lat_off(n + 2), tkl)
            update(st_b[...], lat_off(n + 1), tkl)
            return carry

        lax.fori_loop(0, jnp.where(is_lat, n_lat // 2, 0), pair, 0)
        o4 = (acc_s[...] / l_s[...]).T
        lse_ref[...] = m_s[...] + jnp.log2(l_s[...])
        sg = _silu(g_ref[...])
        for g in range(GQA_GROUP):
            cols = slice(g * HEAD_DIM, (g + 1) * HEAD_DIM)
            og = o4[g * tq:(g + 1) * tq, :]
            o_ref[:, cols] = og
            og_ref[:, cols] = (og * sg[:, cols]).astype(BF16)
        finish()

    qspec = pl.BlockSpec((tq, gw), lambda h, i: (i, h))
    kspec = pl.BlockSpec((r, HEAD_DIM), lambda h, i: (0, h))
    return pl.pallas_call(
        body, name=name, grid=(nkv, nq),
        in_specs=[qspec, kspec, kspec, pl.BlockSpec((tq, gw), lambda h, i: (i, att // gw + h))] + s_in_specs,
        out_specs=[qspec, qspec, pl.BlockSpec((None, None, 1, nl), lambda h, i: (h, i, 0, 0))] + s_out_specs,
        out_shape=[jax.ShapeDtypeStruct((r, att), F32), jax.ShapeDtypeStruct((r, att), BF16),
                   jax.ShapeDtypeStruct((nkv, nq, 1, nl), F32)] + s_out_shape,
        scratch_shapes=[pltpu.VMEM((1, nl), F32), pltpu.VMEM((1, nl), F32), pltpu.VMEM((HEAD_DIM, nl), F32),
                        pltpu.VMEM((tkl, nl), F32), pltpu.VMEM((tkl, nl), F32)] + s_sems,
        compiler_params=_cp(2),
    )(qr, kr, vb, p, *s_ins)


def _flash_bwd(qr, kr, vb, p, o, dog, lse, att, kv, t_ctx, name, side=()):
    r = qr.shape[0]
    s_lat = r - t_ctx
    gw = GQA_GROUP * HEAD_DIM
    nkv = att // gw
    tq, tkl = _attn_tiles(t_ctx, s_lat, 1024)
    nq, nq_ctx, n_lat = r // tq, t_ctx // tq, s_lat // tkl
    nl = GQA_GROUP * tq
    nj = len(side)
    s_ins, s_in_specs, s_out_shape, s_out_specs, s_sems = _side_plan(side)

    def body(q_ref, k_ref, v_ref, g_ref, o_ref, dog_ref, lse_ref, *rest):
        s_in, (dq_ref, dgate_ref, dk_ref, dv_ref), s_out = rest[:nj], rest[nj:nj + 4], rest[nj + 4:2 * nj + 4]
        dq_s, s_sem = rest[2 * nj + 4], rest[2 * nj + 5:]
        start, finish = _side_run(side, s_in, s_out, s_sem, *_grid_ends((nkv, nq)))
        start()
        qi = pl.program_id(1)

        @pl.when(qi == 0)
        def _():
            dk_ref[...] = jnp.zeros_like(dk_ref)
            dv_ref[...] = jnp.zeros_like(dv_ref)

        gate, ov, dogv = g_ref[...], o_ref[...], dog_ref[...]
        dgate_ref[...] = (dogv * ov * _dsilu(gate)).astype(BF16)
        do = dogv * _silu(gate)
        do4 = _stack_heads(do, tq)
        delta = jnp.sum((do4 * _stack_heads(ov, tq)).T, axis=0, keepdims=True)
        do4 = do4.astype(BF16)
        q4 = _stack_heads(q_ref[...], tq)
        lse_v = lse_ref[...]
        dq_s[...] = jnp.zeros_like(dq_s)

        def step(off, size):
            kb = k_ref[pl.ds(off, size), :]
            vv = v_ref[pl.ds(off, size), :]
            st = lax.dot_general(kb, q4, NT_DIMS, preferred_element_type=F32)
            pe = jnp.exp2(st - lse_v)
            dp = lax.dot_general(vv, do4, NT_DIMS, preferred_element_type=F32)
            ds = (pe * (dp - delta) * ATTN_SCALE).astype(BF16)
            dv_ref[pl.ds(off, size), :] += jnp.dot(pe.astype(BF16), do4, preferred_element_type=F32)
            dk_ref[pl.ds(off, size), :] += jnp.dot(ds, q4, preferred_element_type=F32)
            dq_s[...] += lax.dot_general(kb, ds, TN_DIMS, preferred_element_type=F32)

        step(0, t_ctx)

        def lat(n, carry):
            step(pl.multiple_of(t_ctx + n * tkl, math.gcd(t_ctx, tkl)), tkl)
            return carry

        lax.fori_loop(0, jnp.where(qi < nq_ctx, 0, n_lat), lat, 0)
        dq4 = dq_s[...].T * (1.0 / ATTN_SCALE_LOG2)
        for g in range(GQA_GROUP):
            dq_ref[:, g * HEAD_DIM:(g + 1) * HEAD_DIM] = dq4[g * tq:(g + 1) * tq, :]
        finish()

    qspec = pl.BlockSpec((tq, gw), lambda h, i: (i, h))
    kspec = pl.BlockSpec((r, HEAD_DIM), lambda h, i: (0, h))
    return pl.pallas_call(
        body, name=name, grid=(nkv, nq),
        in_specs=[qspec, kspec, kspec, pl.BlockSpec((tq, gw), lambda h, i: (i, att // gw + h)), qspec, qspec,
                  pl.BlockSpec((None, None, 1, nl), lambda h, i: (h, i, 0, 0))] + s_in_specs,
        out_specs=[qspec, qspec, kspec, kspec] + s_out_specs,
        out_shape=[jax.ShapeDtypeStruct((r, att), F32), jax.ShapeDtypeStruct((r, att), BF16),
                   jax.ShapeDtypeStruct((r, kv), F32), jax.ShapeDtypeStruct((r, kv), F32)] + s_out_shape,
        scratch_shapes=[pltpu.VMEM((HEAD_DIM, nl), F32)] + s_sems,
        compiler_params=_cp(2),
    )(qr, kr, vb, p, o, dog, lse, *s_ins)


def _rope_tables(t_ctx, s_lat):
    rows_n = s_lat // GRID_W
    row = jnp.repeat(jnp.arange(rows_n, dtype=F32), GRID_W)
    col = jnp.tile(jnp.arange(GRID_W, dtype=F32), rows_n)
    axis_dim = HEAD_DIM // 2
    inv_freq = ROPE_THETA ** (-jnp.arange(0, axis_dim, 2, dtype=F32) / axis_dim)
    ang_r = row[:, None] * inv_freq[None, :]
    ang_c = col[:, None] * inv_freq[None, :]
    cos = jnp.concatenate([jnp.cos(ang_r), jnp.cos(ang_r), jnp.cos(ang_c), jnp.cos(ang_c)], axis=1)
    sin = jnp.concatenate([-jnp.sin(ang_r), jnp.sin(ang_r), -jnp.sin(ang_c), jnp.sin(ang_c)], axis=1)
    cos = jnp.concatenate([jnp.ones((t_ctx, HEAD_DIM), F32), cos], axis=0)
    sin = jnp.concatenate([jnp.zeros((t_ctx, HEAD_DIM), F32), sin], axis=0)
    return cos, sin


def _pad_rows(a, rows):
    return jnp.pad(a, ((0, rows - a.shape[0]), (0, 0)))


def kernel(x, c, ctx, c_ctx, w_mod, b_mod, post_ln_g, post_ln_b, w_in_e, conv_a_w, conv_a_b, norm_a_g, norm_a_b, conv_b_w, w_out_e, w_in_o, q_norm_g, k_norm_g, w_out_o, loss_target, m_c_ctx, m_w_mod, m_b_mod, m_post_ln_g, m_post_ln_b, m_w_in_e, m_conv_a_w, m_conv_a_b, m_norm_a_g, m_norm_a_b, m_conv_b_w, m_w_out_e, m_w_in_o, m_q_norm_g, m_k_norm_g, m_w_out_o, v_c_ctx, v_w_mod, v_b_mod, v_post_ln_g, v_post_ln_b, v_w_in_e, v_conv_a_w, v_conv_a_b, v_norm_a_g, v_norm_a_b, v_conv_b_w, v_w_out_e, v_w_in_o, v_q_norm_g, v_k_norm_g, v_w_out_o):
    depth, d, mcols = w_mod.shape
    s_lat, t_ctx = x.shape[1], ctx.shape[1]
    n_even, n_odd = w_in_e.shape[0], w_in_o.shape[0]
    ka, kb = conv_a_w.shape[1], conv_b_w.shape[1]
    wch = conv_a_w.shape[2] * NDEV
    att = w_out_o.shape[1] * NDEV
    kv = (w_in_o.shape[2] * NDEV - 2 * att) // 2
    alpha = (2.0 * depth) ** 0.25
    me = 4 * lax.axis_index("x") + 2 * lax.axis_index("y") + lax.axis_index("c")

    c_all = _all_gather(_pad_rows(c, 8), "ag_c")[:, 0, :]
    c16 = jnp.concatenate([c_all, _pad_rows(c_ctx[None, :], 8)], axis=0)
    sc16 = _ew(_silu, BF16, "silu_c", c16)
    m_part = _mm(sc16, w_mod, "nn", F32, "mod_fwd", tm=16, tn=mcols, tk=d)
    m_all = _all_gather(m_part, "ag_mod")
    m_full = m_all.reshape(NDEV, 16, depth, mcols).transpose(2, 1, 0, 3).reshape(depth * 16, 3 * d)
    b16 = jnp.broadcast_to(b_mod[:, None, :], (depth, 16, 3 * d)).reshape(depth * 16, 3 * d)
    m_full = _ew(lambda a, b: a + b, F32, "mod_bias", m_full, b16).reshape(depth, 16, 3, d)
    modv = [jnp.stack([m_full[l, 8], lax.dynamic_index_in_dim(m_full[l], me, 0, keepdims=False)])
            for l in range(depth)]

    cw = jnp.concatenate([_pad_rows(conv_a_w[i], 32) for i in range(n_even)]
                         + [_pad_rows(conv_b_w[i], 8) for i in range(n_even)], axis=0)
    cw_all = _all_gather(cw, "ag_convw").transpose(1, 0, 2).reshape(cw.shape[0], wch)
    caw = [cw_all[32 * i:32 * i + ka] for i in range(n_even)]
    cbw = [cw_all[32 * n_even + 8 * i:32 * n_even + 8 * i + kb] for i in range(n_even)]

    sh_in = [_ew(lambda t: t, BF16, f"w_in{l}_bf16", (w_in_e if l % 2 == 0 else w_in_o)[l // 2]) for l in range(depth)]
    sh_out = [_ew(lambda t: t, BF16, f"w_out{l}_bf16", (w_out_e if l % 2 == 0 else w_out_o)[l // 2])
              for l in range(depth)]
    g_in = [None] * depth
    g_out = [None] * depth
    g_in[0] = _all_gather(sh_in[0], "ag_w_in0")

    def wanted(keys):
        return [(kind, l) for kind, l in keys if l < depth and (g_in if kind == "in" else g_out)[l] is None]

    def gather_jobs(keys):
        return [("gather", (sh_in if kind == "in" else sh_out)[l]) for kind, l in keys]

    def hosted(res, keys):
        if not keys:
            return res
        n_own = len(res) - len(keys)
        for (kind, l), g in zip(keys, res[n_own:]):
            if kind == "in":
                g_in[l] = g
            else:
                g_out[l] = g.reshape(1, NDEV * g.shape[1], d)
        return res[0] if n_own == 1 else res[:n_own]

    cos, sin = _rope_tables(t_ctx, s_lat)
    lnp = [jnp.stack([post_ln_g[l], post_ln_b[l]]) for l in range(depth)]

    xb = jnp.concatenate([ctx[0], x[0]], axis=0)
    saved = []
    h = _modulate(xb, modv[0], t_ctx, "modulate0")
    for l in range(depth):
        i = l // 2
        if l % 2 == 0:
            keys = wanted([("out", l), ("in", l + 1)])
            p = hosted(_mm(h, g_in[l], "nn", F32, f"in_proj{l}", tm=ROWS_WIDE, side=gather_jobs(keys)), keys)
            cat, u1 = _conv_fwd(p, caw[i], conv_a_b[i][None], norm_a_g[i][None], norm_a_b[i][None], cbw[i],
                                t_ctx, f"conv_fwd{l}")
            keys = wanted([("out", l + 1)])
            y = hosted(_mm(cat, g_out[l], "nn", F32, f"out_proj{l}", side=gather_jobs(keys)), keys)
            saved.append((xb, h, p, cat, u1, y))
        else:
            keys = wanted([("out", l)])
            p = hosted(_mm(h, g_in[l], "nn", F32, f"in_proj{l}", tm=ROWS_WIDE, side=gather_jobs(keys)), keys)
            qr, kr, vb = _qk_fwd(p, cos, sin, q_norm_g[i][None], k_norm_g[i][None], att, kv, t_ctx, f"qk_fwd{l}")
            keys = wanted([("in", l + 1), ("out", l + 1)])
            o, og, lse = hosted(_flash_fwd(qr, kr, vb, p, att, t_ctx, f"flash_fwd{l}", side=gather_jobs(keys)), keys)
            y = _mm(og, g_out[l], "nn", F32, f"out_proj{l}")
            saved.append((xb, h, p, qr, kr, vb, o, og, lse, y))
        if l + 1 < depth:
            xb, h = _postln_fwd(xb, y, modv[l], lnp[l], alpha, modv[l + 1], t_ctx, f"postln_fwd{l}")

    small = {}
    dmod = [None] * depth
    recv_in, recv_out = [None] * depth, [None] * depth
    carried = []
    for l in reversed(range(depth)):
        i = l // 2
        sv = saved[l]
        x_in, h, p, y = sv[0], sv[1], sv[2], sv[-1]
        if l == depth - 1:
            loss_blk, dy, dxa, s_ln = _postln_bwd(loss_target[0], x_in, y, modv[l], lnp[l], alpha, t_ctx,
                                                  f"loss_postln_bwd{l}", from_loss=True)
            loss = lax.psum(loss_blk[0, 0], ("x", "y", "c"))
        else:
            dy, dxa, s_ln, s_mod = _postln_bwd(dh, x_in, y, modv[l], lnp[l], alpha, t_ctx, f"postln_bwd{l}",
                                               mod_next=(dxa, modv[l + 1]))
            dmod[l + 1] = jnp.stack([jnp.stack([s_mod[0], s_mod[1], s_ln_above[2]]),
                                     jnp.stack([s_mod[2], s_mod[3], s_ln_above[3]])])
        s_ln_above = s_ln
        small[f"ln{l}"] = s_ln
        mixed = sv[3] if l % 2 == 0 else sv[7]
        k_out = g_out[l].shape[1]
        part = _mm(mixed, dy, "tn", BF16, f"d_w_out{l}", tm=1024, tk=ROWS_WIDE).reshape(NDEV, k_out // NDEV, d)
        dmixed, recv_out[l] = _mm(dy, g_out[l], "nt", F32, f"d_mixed{l}", side=[("exchange", part)])
        if l % 2 == 0:
            u1 = sv[4]
            du1, dag, s_c1 = _conv_bwd1(dmixed, p, u1, norm_a_g[i][None], norm_a_b[i][None], t_ctx,
                                        f"conv_bwd1_{l}")
            dp, dcaw, dcbw = _conv_bwd2(du1, dmixed, p, dag, caw[i], cbw[i], t_ctx, f"conv_bwd2_{l}")
            small[f"c1_{i}"], small[f"caw{i}"], small[f"cbw{i}"] = s_c1, dcaw, dcbw
        else:
            qr, kr, vb, o, _, lse = sv[3:9]
            dqr, dgate, dkr, dvr, *got = _flash_bwd(qr, kr, vb, p, o, dmixed, lse, att, kv, t_ctx, f"flash_bwd{l}",
                                                    side=carried)
            if carried:
                recv_in[l + 1], carried = got[0], []
            dp, s_qk = _qk_bwd(p, dqr, dgate, dkr, dvr, cos, sin, q_norm_g[i][None], k_norm_g[i][None], att, kv,
                               t_ctx, f"qk_bwd{l}")
            small[f"qk{i}"] = jnp.pad(s_qk, ((0, 0), (0, d - HEAD_DIM)))
        part = _mm(h, dp, "tn", BF16, f"d_w_in{l}", tm=d, tk=ROWS_WIDE, out_nd=NDEV, side=carried)
        if carried:
            (part, recv_in[l + 1]), carried = part, []
        if l == 0:
            dxb, s_mod, recv_in[0] = _mm(dp, g_in[0], "nt", F32, "d_h0", tm=ROWS_EPILOGUE, side=[("exchange", part)],
                                         mod_bwd=(x_in, dxa, modv[0], t_ctx))
            dmod[0] = jnp.stack([jnp.stack([s_mod[0], s_mod[1], s_ln[2]]), jnp.stack([s_mod[2], s_mod[3], s_ln[3]])])
        else:
            dh = _mm(dp, g_in[l], "nt", F32, f"d_h{l}", tm=ROWS_WIDE)
            carried = [("exchange", part)]
    grad_x = dxb[t_ctx:][None]

    dm_loc = jnp.stack(dmod).reshape(depth * 2, 3 * d)
    dm_all = _all_gather(dm_loc, "ag_dmod").reshape(NDEV, depth, 2, 3 * d)
    dm_ctx = _sum_lead(dm_all[:, :, 0, :], "sum_dmod_ctx")
    dm16 = jnp.concatenate([dm_all[:, :, 1, :].transpose(1, 0, 2), dm_ctx[:, None, :],
                            jnp.zeros((depth, 7, 3 * d), F32)], axis=1)
    g_b_mod = _sum_lead(dm16.transpose(1, 0, 2), "sum_b_mod")
    dm16_me = lax.dynamic_slice_in_dim(dm16.reshape(depth, 16, NDEV, mcols), me, 1, axis=2)
    dm16_me = dm16_me.reshape(depth, 16, mcols).transpose(1, 0, 2).reshape(16, depth * mcols)
    g_w_mod = _mm(sc16, dm16_me, "tn", F32, "mod_bwd_w", tm=d, tn=mcols, tk=16, out_nd=depth)
    dsc16 = _mm(dm16_me, w_mod, "nt", F32, "mod_bwd_c", tm=16, tn=d, tk=mcols)
    small["c_ctx"] = dsc16[8:16]

    names = sorted(small)
    offs, rows = {}, 0
    for nme in names:
        offs[nme] = rows
        rows += small[nme].shape[0]
    sm_all = _all_gather(jnp.concatenate([small[nme] for nme in names], axis=0), "ag_small")
    sm = _sum_lead(sm_all, "sum_small")

    def part(nme, lo, hi):
        return sm[offs[nme] + lo:offs[nme] + hi]

    g_post_ln_g = jnp.concatenate([part(f"ln{l}", 0, 1) for l in range(depth)], axis=0)
    g_post_ln_b = jnp.concatenate([part(f"ln{l}", 1, 2) for l in range(depth)], axis=0)
    g_norm_a_g = jnp.concatenate([part(f"c1_{i}", 0, 1) for i in range(n_even)], axis=0)
    g_norm_a_b = jnp.concatenate([part(f"c1_{i}", 1, 2) for i in range(n_even)], axis=0)
    g_conv_a_b = jnp.concatenate([part(f"c1_{i}", 2, 3) for i in range(n_even)], axis=0)
    g_q_norm_g = jnp.concatenate([part(f"qk{i}", 0, 1)[:, :HEAD_DIM] for i in range(n_odd)], axis=0)
    g_k_norm_g = jnp.concatenate([part(f"qk{i}", 1, 2)[:, :HEAD_DIM] for i in range(n_odd)], axis=0)
    wsh = wch // NDEV
    g_conv_a_w = jnp.stack([lax.dynamic_slice_in_dim(part(f"caw{i}", 0, ka), me * wsh, wsh, axis=1)
                            for i in range(n_even)])
    g_conv_b_w = jnp.stack([lax.dynamic_slice_in_dim(part(f"cbw{i}", 0, kb), me * wsh, wsh, axis=1)
                            for i in range(n_even)])
    g_c_ctx = _ew(lambda a, b: a * _dsilu(b), F32, "d_c_ctx", part("c_ctx", 0, 8), _pad_rows(c_ctx[None, :], 8))[0]

    def updated(recv, layers, prefix, w, m, v):
        outs = [_reduce_adamw(recv[l], w[j], m[j], v[j], f"adamw_{prefix}{j}") for j, l in enumerate(layers)]
        return [jnp.stack([o[t] for o in outs]) for t in range(4)]

    evens, odds = range(0, depth, 2), range(1, depth, 2)
    r_w_in_e = updated(recv_in, evens, "w_in_e", w_in_e, m_w_in_e, v_w_in_e)
    r_w_out_e = updated(recv_out, evens, "w_out_e", w_out_e, m_w_out_e, v_w_out_e)
    r_w_in_o = updated(recv_in, odds, "w_in_o", w_in_o, m_w_in_o, v_w_in_o)
    r_w_out_o = updated(recv_out, odds, "w_out_o", w_out_o, m_w_out_o, v_w_out_o)

    grads = {
        "c_ctx": g_c_ctx, "w_mod": g_w_mod, "b_mod": g_b_mod, "post_ln_g": g_post_ln_g, "post_ln_b": g_post_ln_b,
        "conv_a_w": g_conv_a_w, "conv_a_b": g_conv_a_b, "norm_a_g": g_norm_a_g, "norm_a_b": g_norm_a_b,
        "conv_b_w": g_conv_b_w, "q_norm_g": g_q_norm_g, "k_norm_g": g_k_norm_g,
    }
    state = {
        "c_ctx": (c_ctx, m_c_ctx, v_c_ctx), "w_mod": (w_mod, m_w_mod, v_w_mod), "b_mod": (b_mod, m_b_mod, v_b_mod),
        "post_ln_g": (post_ln_g, m_post_ln_g, v_post_ln_g), "post_ln_b": (post_ln_b, m_post_ln_b, v_post_ln_b),
        "conv_a_w": (conv_a_w, m_conv_a_w, v_conv_a_w), "conv_a_b": (conv_a_b, m_conv_a_b, v_conv_a_b),
        "norm_a_g": (norm_a_g, m_norm_a_g, v_norm_a_g), "norm_a_b": (norm_a_b, m_norm_a_b, v_norm_a_b),
        "conv_b_w": (conv_b_w, m_conv_b_w, v_conv_b_w), "q_norm_g": (q_norm_g, m_q_norm_g, v_q_norm_g),
        "k_norm_g": (k_norm_g, m_k_norm_g, v_k_norm_g),
    }
    res = {"w_in_e": r_w_in_e, "w_out_e": r_w_out_e, "w_in_o": r_w_in_o, "w_out_o": r_w_out_o}
    for nme, g in grads.items():
        w, m, v = state[nme]
        res[nme] = [g] + _adamw(w, g, m, v, f"adamw_{nme}")
    order = ["c_ctx", "w_mod", "b_mod", "post_ln_g", "post_ln_b", "w_in_e", "conv_a_w", "conv_a_b", "norm_a_g",
             "norm_a_b", "conv_b_w", "w_out_e", "w_in_o", "q_norm_g", "k_norm_g", "w_out_o"]
    return (loss, grad_x, *[res[nme][0] for nme in order], *[res[nme][1] for nme in order],
            *[res[nme][2] for nme in order], *[res[nme][3] for nme in order])
```

```python
import functools
import math

import jax
import jax.numpy as jnp
from jax import lax
from jax.experimental import pallas as pl
from jax.experimental.pallas import tpu as pltpu

F32 = jnp.float32
BF16 = jnp.bfloat16

NDEV = 8
GRID_W = 64
HEAD_DIM = 128
GQA_GROUP = 4
ROPE_THETA = 10000.0
LN_EPS = 1e-5
RMS_EPS = 1e-6
ATTN_SCALE = HEAD_DIM ** -0.5
ADAM_LR = 0.001
ADAM_B1 = 0.9
ADAM_B2 = 0.999
ADAM_EPS = 1e-08
ADAM_WD = 0.01
ADAM_STEP = 10
HALO = 16
LANES = 128
VMEM_LIMIT = 56 * 1024 * 1024
ROWS_WIDE = 1056
ROWS_EPILOGUE = 528
MESH = pl.DeviceIdType.MESH


def _cp(n_axes):
    return pltpu.CompilerParams(dimension_semantics=("arbitrary",) * n_axes, vmem_limit_bytes=VMEM_LIMIT)


def _div(dim, pref, mult):
    t = min(pref, dim) // mult * mult
    while t >= mult:
        if dim % t == 0:
            return t
        t -= mult
    return dim


def _sigmoid(x):
    return 1.0 / (1.0 + jnp.exp(-x))


def _silu(x):
    return x * _sigmoid(x)


def _dsilu(x):
    s = _sigmoid(x)
    return s * (1.0 + x * (1.0 - s))


def _ln_norm(z):
    mu = jnp.mean(z, axis=-1, keepdims=True)
    zc = z - mu
    var = jnp.mean(zc * zc, axis=-1, keepdims=True)
    rstd = lax.rsqrt(var + LN_EPS)
    return zc * rstd, rstd


def _ln_bwd(dn, n, rstd):
    return rstd * (dn - jnp.mean(dn, axis=-1, keepdims=True) - n * jnp.mean(dn * n, axis=-1, keepdims=True))


def _partner(x):
    lane = lax.broadcasted_iota(jnp.int32, x.shape, 1)
    return jnp.where((lane % 64) < 32, pltpu.roll(x, 96, 1), pltpu.roll(x, 32, 1))


N_PEERS = NDEV - 1
HBM_SPEC = pl.BlockSpec(memory_space=pltpu.HBM)


def _side_plan(jobs):
    xs = [x for _, x in jobs]
    out_shape = [jax.ShapeDtypeStruct((NDEV,) + x.shape[-2:], x.dtype) for x in xs]
    sems = [pltpu.SemaphoreType.DMA((2 * N_PEERS + 1,)) for _ in jobs]
    return xs, [HBM_SPEC] * len(jobs), out_shape, [HBM_SPEC] * len(jobs), sems


def _side_copies(kind, x_ref, out_ref, sems):
    mx, my, mc = lax.axis_index("x"), lax.axis_index("y"), lax.axis_index("c")
    me = 4 * mx + 2 * my + mc
    own = x_ref.at[me] if kind == "exchange" else x_ref
    copies = [pltpu.make_async_copy(own, out_ref.at[me], sems.at[2 * N_PEERS])]
    for k in range(1, NDEV):
        px, py, pc = mx ^ ((k >> 2) & 1), my ^ ((k >> 1) & 1), mc ^ (k & 1)
        src = x_ref.at[4 * px + 2 * py + pc] if kind == "exchange" else x_ref
        copies.append(pltpu.make_async_remote_copy(
            src_ref=src, dst_ref=out_ref.at[me], send_sem=sems.at[k - 1], recv_sem=sems.at[N_PEERS + k - 1],
            device_id=(px, py, pc), device_id_type=MESH))
    return copies


def _side_run(jobs, in_refs, out_refs, sem_refs, first, last):
    if not jobs:
        return (lambda: None), (lambda: None)

    def start():
        @pl.when(first)
        def _():
            for (kind, _), x_ref, o_ref, sems in zip(jobs, in_refs, out_refs, sem_refs):
                for cp in _side_copies(kind, x_ref, o_ref, sems):
                    cp.start()

    def finish():
        @pl.when(last)
        def _():
            for (kind, _), x_ref, o_ref, sems in zip(jobs, in_refs, out_refs, sem_refs):
                for cp in _side_copies(kind, x_ref, o_ref, sems):
                    cp.wait()

    return start, finish


def _grid_ends(grid):
    first = last = None
    for ax, n in enumerate(grid):
        i = pl.program_id(ax)
        f, l = i == 0, i == n - 1
        first = f if first is None else jnp.logical_and(first, f)
        last = l if last is None else jnp.logical_and(last, l)
    return first, last


def _mm(a, b, mode, out_dtype, name, tm=768, tn=2048, tk=2048, out_nd=1, side=(), mod_bwd=None):
    if mode == "nn":
        m, kdim = a.shape
        nd, _, ns = b.shape
        tm, tn, tk = _div(m, tm, 8), _div(ns, tn, 128), _div(kdim, tk, 128)
        nbs = ns // tn
        grid = (m // tm, nd * nbs, kdim // tk)
        a_spec = pl.BlockSpec((tm, tk), lambda i, j, k: (i, k))
        b_spec = pl.BlockSpec((None, tk, tn), lambda i, j, k: (j // nbs, k, j % nbs))
        o_spec = pl.BlockSpec((tm, tn), lambda i, j, k: (i, j))
        out_shape = (m, nd * ns)
        dims = (((1,), (0,)), ((), ()))
    elif mode == "nt":
        m, _ = a.shape
        nd, ko, ns = b.shape
        tm, tn, tk = _div(m, tm, 8), _div(ko, tn, 128), _div(ns, tk, 128)
        kbs = ns // tk
        grid = (m // tm, ko // tn, nd * kbs)
        a_spec = pl.BlockSpec((tm, tk), lambda i, j, k: (i, k))
        b_spec = pl.BlockSpec((None, tn, tk), lambda i, j, k: (k // kbs, j, k % kbs))
        o_spec = pl.BlockSpec((tm, tn), lambda i, j, k: (i, j))
        out_shape = (m, ko)
        dims = (((1,), (1,)), ((), ()))
    else:
        m, kdim = a.shape
        n = b.shape[1]
        ns = n // out_nd
        tm, tn, tk = _div(kdim, tm, 128), _div(ns, tn, 128), _div(m, tk, 16)
        nbs = ns // tn
        grid = (kdim // tm, out_nd * nbs, m // tk)
        a_spec = pl.BlockSpec((tk, tm), lambda i, j, k: (k, i))
        b_spec = pl.BlockSpec((tk, tn), lambda i, j, k: (k, j))
        o_spec = pl.BlockSpec((None, tm, tn), lambda i, j, k: (j // nbs, i, j % nbs))
        out_shape = (out_nd, kdim, ns)
        dims = (((0,), (0,)), ((), ()))
    nk = grid[2]
    nj = len(side)
    s_ins, s_in_specs, s_out_shape, s_out_specs, s_sems = _side_plan(side)
    e_ins, e_in_specs, e_out_shape, e_out_specs = [], [], [], []
    if mod_bwd is not None:
        assert mode == "nt" and grid[1] == 1
        x_in, dxa, modv, t_ctx = mod_bwd
        e_ins = [x_in, dxa, modv]
        e_in_specs = [pl.BlockSpec((tm, tn), lambda i, j, k: (i, 0)), pl.BlockSpec((tm, tn), lambda i, j, k: (i, 0)),
                      pl.BlockSpec(modv.shape, lambda i, j, k: (0, 0, 0))]
        e_out_shape = [jax.ShapeDtypeStruct((8, tn), F32)]
        e_out_specs = [pl.BlockSpec((8, tn), lambda i, j, k: (0, 0))]
    ne, neo = len(e_ins), len(e_out_shape)

    def body(a_ref, b_ref, *rest):
        e_in, s_in = rest[:ne], rest[ne:ne + nj]
        o_ref, e_out, s_out = rest[ne + nj], rest[ne + nj + 1:ne + nj + 1 + neo], rest[ne + nj + 1 + neo:ne + 2 * nj + 1 + neo]
        scratch = rest[ne + 2 * nj + 1 + neo:]
        acc_ref, s_sem = (None, scratch) if nk == 1 else (scratch[0], scratch[1:])
        first, last = _grid_ends(grid)
        start, finish = _side_run(side, s_in, s_out, s_sem, first, last)
        start()

        def prod():
            return lax.dot_general(a_ref[...].astype(BF16), b_ref[...].astype(BF16), dims,
                                   preferred_element_type=F32)

        def emit(val):
            if mod_bwd is None:
                o_ref[...] = val.astype(o_ref.dtype)
                return
            x_ref, dxa_ref, mv_ref = e_in
            rows = pl.program_id(0) * tm + lax.broadcasted_iota(jnp.int32, (tm, 1), 0)
            is_ctx = rows < t_ctx
            o_ref[...] = dxa_ref[...] + val * (1.0 + jnp.where(is_ctx, mv_ref[0, 1:2, :], mv_ref[1, 1:2, :]))
            dsc = val * x_ref[...]
            for row, (t, keep_ctx) in enumerate(((val, True), (dsc, True), (val, False), (dsc, False))):
                e_out[0][row:row + 1, :] += jnp.sum(jnp.where(is_ctx == keep_ctx, t, 0.0), axis=0, keepdims=True)

        if mod_bwd is not None:
            @pl.when(first)
            def _():
                e_out[0][...] = jnp.zeros_like(e_out[0])

        if nk == 1:
            emit(prod())
        else:
            k = pl.program_id(2)

            @pl.when(k == 0)
            def _():
                acc_ref[...] = jnp.zeros_like(acc_ref)

            acc_ref[...] += prod()

            @pl.when(k == nk - 1)
            def _():
                emit(acc_ref[...])
        finish()

    outs = pl.pallas_call(
        body, name=name, grid=grid, in_specs=[a_spec, b_spec] + e_in_specs + s_in_specs,
        out_specs=[o_spec] + e_out_specs + s_out_specs,
        out_shape=[jax.ShapeDtypeStruct(out_shape, out_dtype)] + e_out_shape + s_out_shape,
        scratch_shapes=([] if nk == 1 else [pltpu.VMEM((tm, tn), F32)]) + s_sems, compiler_params=_cp(3),
    )(a, b, *e_ins, *s_ins)
    return outs if len(outs) > 1 else outs[0]


def _ew(fn, out_dtype, name, *xs):
    rows, cols = xs[0].shape
    tr = rows if rows <= 64 else _div(rows, 256, 16)

    def body(*refs):
        refs[-1][...] = fn(*[r[...] for r in refs[:-1]]).astype(out_dtype)

    spec = pl.BlockSpec((tr, cols), lambda i: (i, 0))
    return pl.pallas_call(
        body, name=name, grid=(rows // tr,), in_specs=[spec] * len(xs), out_specs=spec,
        out_shape=jax.ShapeDtypeStruct((rows, cols), out_dtype), compiler_params=_cp(1),
    )(*xs)


def _sum_lead(x, name):
    n, rows, cols = x.shape
    tr = _div(rows, 64, 8)

    def body(x_ref, o_ref):
        acc = x_ref[0]
        for s in range(1, n):
            acc = acc + x_ref[s]
        o_ref[...] = acc

    return pl.pallas_call(
        body, name=name, grid=(rows // tr,),
        in_specs=[pl.BlockSpec((n, tr, cols), lambda i: (0, i, 0))],
        out_specs=pl.BlockSpec((tr, cols), lambda i: (i, 0)),
        out_shape=jax.ShapeDtypeStruct((rows, cols), F32), compiler_params=_cp(1),
    )(x)


def _adam_math(w, g, m, v):
    m = ADAM_B1 * m + (1.0 - ADAM_B1) * g
    v = ADAM_B2 * v + (1.0 - ADAM_B2) * (g * g)
    m_hat = m / (1.0 - ADAM_B1 ** ADAM_STEP)
    v_hat = v / (1.0 - ADAM_B2 ** ADAM_STEP)
    delta = -ADAM_LR * (m_hat / (jnp.sqrt(v_hat) + ADAM_EPS) + ADAM_WD * w)
    return delta, m, v


def _adamw(w, g, m, v, name):
    shape = w.shape
    cols = shape[-1]
    w2, g2, m2, v2 = [t.reshape(-1, cols) for t in (w, g, m, v)]
    rows = w2.shape[0]
    tr = rows if rows <= 512 else _div(rows, 256, 8)

    def body(w_ref, g_ref, m_ref, v_ref, d_ref, nm_ref, nv_ref):
        d, nm, nv = _adam_math(w_ref[...], g_ref[...], m_ref[...], v_ref[...])
        d_ref[...] = d
        nm_ref[...] = nm
        nv_ref[...] = nv

    spec = pl.BlockSpec((tr, cols), lambda i: (i, 0))
    outs = pl.pallas_call(
        body, name=name, grid=(rows // tr,), in_specs=[spec] * 4, out_specs=[spec] * 3,
        out_shape=[jax.ShapeDtypeStruct((rows, cols), F32)] * 3, compiler_params=_cp(1),
    )(w2, g2, m2, v2)
    return [o.reshape(shape) for o in outs]


def _reduce_adamw(parts, w, m, v, name):
    shape = w.shape
    n, rows, cols = parts.shape
    w2, m2, v2 = [t.reshape(rows, cols) for t in (w, m, v)]
    tr = _div(rows, 128, 16)

    def body(p_ref, w_ref, m_ref, v_ref, g_ref, d_ref, nm_ref, nv_ref):
        g = p_ref[0].astype(F32)
        for s in range(1, n):
            g = g + p_ref[s].astype(F32)
        d, nm, nv = _adam_math(w_ref[...], g, m_ref[...], v_ref[...])
        g_ref[...] = g
        d_ref[...] = d
        nm_ref[...] = nm
        nv_ref[...] = nv

    spec = pl.BlockSpec((tr, cols), lambda i: (i, 0))
    outs = pl.pallas_call(
        body, name=name, grid=(rows // tr,),
        in_specs=[pl.BlockSpec((n, tr, cols), lambda i: (0, i, 0))] + [spec] * 3, out_specs=[spec] * 4,
        out_shape=[jax.ShapeDtypeStruct((rows, cols), F32)] * 4, compiler_params=_cp(1),
    )(parts, w2, m2, v2)
    return [o.reshape(shape) for o in outs]


def _all_gather(x, name):
    rows, cols = x.shape

    def body(x_ref, out_ref, send_sems, recv_sems, local_sem):
        mx, my, mc = lax.axis_index("x"), lax.axis_index("y"), lax.axis_index("c")
        me, sibling = (mx, my, mc), (mx, my, 1 - mc)
        chips = [(1 - mx, my), (mx, 1 - my), (1 - mx, 1 - my)]

        def slab(px, py, pc):
            return out_ref.at[4 * px + 2 * py + pc]

        def copy(k, block, to, src=None):
            return pltpu.make_async_remote_copy(
                src_ref=slab(*block) if src is None else src, dst_ref=slab(*block),
                send_sem=send_sems.at[k], recv_sem=recv_sems.at[k], device_id=to, device_id_type=MESH)

        mine = pltpu.make_async_copy(x_ref, slab(*me), local_sem)
        mine.start()
        first = [copy(0, me, sibling, src=x_ref)]
        first += [copy(1 + j, me, (*chip, mc), src=x_ref) for j, chip in enumerate(chips)]
        for cp in first:
            cp.start()
        passed = [copy(4 + j, (*chip, mc), sibling) for j, chip in enumerate(chips)]
        for j, chip in enumerate(chips):
            copy(1 + j, (*chip, mc), me).wait_recv()
            passed[j].start()
        copy(0, sibling, me).wait_recv()
        for j, chip in enumerate(chips):
            copy(4 + j, (*chip, 1 - mc), me).wait_recv()
        for cp in first + passed:
            cp.wait_send()
        mine.wait()

    return pl.pallas_call(
        body, name=name, out_shape=jax.ShapeDtypeStruct((NDEV, rows, cols), x.dtype),
        in_specs=[pl.BlockSpec(memory_space=pltpu.HBM)], out_specs=pl.BlockSpec(memory_space=pltpu.HBM),
        scratch_shapes=[pltpu.SemaphoreType.DMA((7,)), pltpu.SemaphoreType.DMA((7,)), pltpu.SemaphoreType.DMA],
    )(x)


def _row_tile(t_ctx, pref=256):
    return _div(t_ctx, pref, 8)


def _mod_spec(d, nctx):
    return pl.BlockSpec((None, 3, d), lambda i: (jnp.where(i >= nctx, 1, 0), 0, 0))


def _modulate(xb, modv, t_ctx, name):
    r, d = xb.shape
    tm = _row_tile(t_ctx)
    nctx = t_ctx // tm

    def body(x_ref, mv_ref, h_ref):
        h_ref[...] = (x_ref[...] * (1.0 + mv_ref[1:2, :]) + mv_ref[0:1, :]).astype(BF16)

    row = pl.BlockSpec((tm, d), lambda i: (i, 0))
    return pl.pallas_call(
        body, name=name, grid=(r // tm,), in_specs=[row, _mod_spec(d, nctx)], out_specs=row,
        out_shape=jax.ShapeDtypeStruct((r, d), BF16), compiler_params=_cp(1),
    )(xb, modv)


def _postln_fwd(xb, y, modv, lnp, alpha, next_modv, t_ctx, name):
    r, d = xb.shape
    tm = _row_tile(t_ctx)
    nctx = t_ctx // tm

    def body(x_ref, y_ref, mv_ref, ln_ref, nmv_ref, o_ref, h_ref):
        n, _ = _ln_norm(alpha * x_ref[...] + mv_ref[2:3, :] * y_ref[...])
        xo = n * ln_ref[0:1, :] + ln_ref[1:2, :]
        o_ref[...] = xo
        h_ref[...] = (xo * (1.0 + nmv_ref[1:2, :]) + nmv_ref[0:1, :]).astype(BF16)

    row = pl.BlockSpec((tm, d), lambda i: (i, 0))
    return pl.pallas_call(
        body, name=name, grid=(r // tm,),
        in_specs=[row, row, _mod_spec(d, nctx), pl.BlockSpec((2, d), lambda i: (0, 0)), _mod_spec(d, nctx)],
        out_specs=[row, row],
        out_shape=[jax.ShapeDtypeStruct((r, d), F32), jax.ShapeDtypeStruct((r, d), BF16)], compiler_params=_cp(1),
    )(xb, y, modv, lnp, next_modv)


def _postln_bwd(dxn, xb, y, modv, lnp, alpha, t_ctx, name, from_loss=False, mod_next=None):
    r, d = xb.shape
    tm = _row_tile(t_ctx)
    nctx = t_ctx // tm
    n_lead = 1 if from_loss else 0

    def body(dxn_ref, x_ref, y_ref, mv_ref, ln_ref, *rest):
        extra_in, outs = (rest[:2], rest[2:]) if mod_next is not None else ((), rest)
        dy_ref, dxa_ref, s_ref = outs[n_lead:n_lead + 3]
        i = pl.program_id(0)
        is_ctx = i < nctx

        @pl.when(i == 0)
        def _():
            for o_ref in outs[:n_lead] + outs[n_lead + 2:]:
                o_ref[...] = jnp.zeros_like(o_ref)

        yv = y_ref[...]
        gate = mv_ref[2:3, :]
        n, rstd = _ln_norm(alpha * x_ref[...] + gate * yv)
        if from_loss:
            e = jnp.where(is_ctx, 0.0, n * ln_ref[0:1, :] + ln_ref[1:2, :] - dxn_ref[...])
            outs[0][...] += 0.5 * jnp.sum(jnp.sum(e * e, axis=1, keepdims=True), axis=0, keepdims=True) / d
            dxn_v = e / d
        elif mod_next is not None:
            dxan_ref, nmv_ref = extra_in
            dh = dxn_ref[...]
            dxn_v = dxan_ref[...] + dh * (1.0 + nmv_ref[1:2, :])
            dshift = jnp.sum(dh, axis=0, keepdims=True)
            dscale = jnp.sum(dh * (n * ln_ref[0:1, :] + ln_ref[1:2, :]), axis=0, keepdims=True)
            sn_ref = outs[-1]
            sn_ref[0:1, :] += jnp.where(is_ctx, dshift, 0.0)
            sn_ref[1:2, :] += jnp.where(is_ctx, dscale, 0.0)
            sn_ref[2:3, :] += jnp.where(is_ctx, 0.0, dshift)
            sn_ref[3:4, :] += jnp.where(is_ctx, 0.0, dscale)
        else:
            dxn_v = dxn_ref[...]
        dz = _ln_bwd(dxn_v * ln_ref[0:1, :], n, rstd)
        dy_ref[...] = (gate * dz).astype(BF16)
        dxa_ref[...] = alpha * dz
        s_ref[0:1, :] += jnp.sum(dxn_v * n, axis=0, keepdims=True)
        s_ref[1:2, :] += jnp.sum(dxn_v, axis=0, keepdims=True)
        dgate = jnp.sum(dz * yv, axis=0, keepdims=True)
        s_ref[2:3, :] += jnp.where(is_ctx, dgate, 0.0)
        s_ref[3:4, :] += jnp.where(is_ctx, 0.0, dgate)

    row = pl.BlockSpec((tm, d), lambda i: (i, 0))
    acc = pl.BlockSpec((8, d), lambda i: (0, 0))
    first = pl.BlockSpec((tm, d), lambda i: (jnp.maximum(i - nctx, 0), 0)) if from_loss else row
    loss_spec = [pl.BlockSpec((8, 128), lambda i: (0, 0))] if from_loss else []
    loss_shape = [jax.ShapeDtypeStruct((8, 128), F32)] if from_loss else []
    extra = [] if mod_next is None else list(mod_next)
    return pl.pallas_call(
        body, name=name, grid=(r // tm,),
        in_specs=[first, row, row, _mod_spec(d, nctx), pl.BlockSpec((2, d), lambda i: (0, 0))]
        + ([row, _mod_spec(d, nctx)] if extra else []),
        out_specs=loss_spec + [row, row, acc] + ([acc] if extra else []),
        out_shape=loss_shape + [jax.ShapeDtypeStruct((r, d), BF16), jax.ShapeDtypeStruct((r, d), F32),
                                jax.ShapeDtypeStruct((8, d), F32)]
        + ([jax.ShapeDtypeStruct((8, d), F32)] if extra else []),
        compiler_params=_cp(1),
    )(dxn, xb, y, modv, lnp, *extra)


def _conv_specs(r, w, tm, cblk):
    hb = tm // HALO
    last = r // HALO - 1
    main = pl.BlockSpec((tm, w), lambda i: (i, cblk))
    top = pl.BlockSpec((HALO, w), lambda i: (jnp.maximum(i * hb - 1, 0), cblk))
    bot = pl.BlockSpec((HALO, w), lambda i: (jnp.minimum((i + 1) * hb, last), cblk))
    return [main, top, bot]


def _fill_pad(pad_ref, main, top, bot, top_ok, bot_ok, tm):
    pad_ref[0:HALO, :] = jnp.where(top_ok, top, 0.0)
    pad_ref[HALO:HALO + tm, :] = main
    pad_ref[HALO + tm:2 * HALO + tm, :] = jnp.where(bot_ok, bot, 0.0)


def _edges(i, nctx, nr):
    top_ok = jnp.logical_and(i != 0, i != nctx)
    bot_ok = jnp.logical_and(i != nctx - 1, i != nr - 1)
    return top_ok, bot_ok


def _rot_fill(rot_ref, pad_ref, n):
    rot_ref[0, 0:n, :] = pad_ref[0:n, :]
    for b in range(1, 8):
        rot_ref[b, 0:n, :] = pad_ref[pl.ds(b, n), :]


def _tap(rot_ref, off, tm):
    return rot_ref[off % 8, pl.ds(off - off % 8, tm), :]


def _rows8(x):
    return jnp.sum(x.reshape(x.shape[0] // 8, 8, x.shape[1]), axis=0)


def _conv_fwd(p, caw, cab, nag, nab, cbw, t_ctx, name):
    r = p.shape[0]
    w = p.shape[1] // 7
    ka, kb = caw.shape[0], cbw.shape[0]
    tm = _row_tile(t_ctx, 128)
    nctx, nr = t_ctx // tm, r // tm
    n = tm + 2 * HALO - 8

    def body(av, avt, avb, ag, agt, agb, agate, bx, bxt, bxb, bb, bc, bct, bcb, bgate,
             caw_ref, cab_ref, nag_ref, nab_ref, cbw_ref, cat_ref, u1_ref, pad, rot, v_s):
        top_ok, bot_ok = _edges(pl.program_id(0), nctx, nr)

        def chunk(c, carry):
            cols = pl.ds(pl.multiple_of(c * LANES, LANES), LANES)
            _fill_pad(pad, av[:, cols] * _sigmoid(ag[:, cols]), avt[:, cols] * _sigmoid(agt[:, cols]),
                      avb[:, cols] * _sigmoid(agb[:, cols]), top_ok, bot_ok, tm)
            _rot_fill(rot, pad, n)
            u1 = jnp.zeros((tm, LANES), F32) + cab_ref[:, cols]
            for k in range(ka):
                u1 = u1 + caw_ref[k:k + 1, cols] * _tap(rot, HALO - ka // 2 + k, tm)
            u1_ref[:, cols] = u1
            _fill_pad(pad, bc[:, cols] * bx[:, cols], bct[:, cols] * bxt[:, cols], bcb[:, cols] * bxb[:, cols],
                      top_ok, bot_ok, tm)
            v = jnp.zeros((tm, LANES), F32)
            for k in range(kb):
                v = v + cbw_ref[k:k + 1, cols] * pad[pl.ds(HALO - kb // 2 + k, tm), :]
            v_s[:, cols] = v
            return carry

        lax.fori_loop(0, w // LANES, chunk, 0)
        nrm, _ = _ln_norm(u1_ref[...])
        a_out = _silu(nrm * nag_ref[...] + nab_ref[...]) * _silu(agate[...])
        cat_ref[:, 0:w] = a_out.astype(BF16)
        cat_ref[:, w:2 * w] = (bb[...] * v_s[...] * _silu(bgate[...])).astype(BF16)

    def main(cblk):
        return pl.BlockSpec((tm, w), lambda i: (i, cblk))

    def whole(a):
        return pl.BlockSpec(a.shape, lambda i: (0, 0))

    in_specs = (_conv_specs(r, w, tm, 0) + _conv_specs(r, w, tm, 1) + [main(2)] + _conv_specs(r, w, tm, 3)
                + [main(4)] + _conv_specs(r, w, tm, 5) + [main(6)]
                + [whole(caw), whole(cab), whole(nag), whole(nab), whole(cbw)])
    return pl.pallas_call(
        body, name=name, grid=(nr,), in_specs=in_specs,
        out_specs=[pl.BlockSpec((tm, 2 * w), lambda i: (i, 0)), pl.BlockSpec((tm, w), lambda i: (i, 0))],
        out_shape=[jax.ShapeDtypeStruct((r, 2 * w), BF16), jax.ShapeDtypeStruct((r, w), F32)],
        scratch_shapes=[pltpu.VMEM((tm + 2 * HALO, LANES), F32), pltpu.VMEM((8, n, LANES), F32),
                        pltpu.VMEM((tm, w), F32)],
        compiler_params=_cp(1),
    )(*([p] * 15), caw, cab, nag, nab, cbw)


def _conv_bwd1(dcat, p, u1, nag, nab, t_ctx, name):
    r, w = u1.shape
    tm = _row_tile(t_ctx, 128)

    def body(da_ref, agate_ref, u1_ref, nag_ref, nab_ref, du1_ref, dag_ref, s_ref):
        @pl.when(pl.program_id(0) == 0)
        def _():
            s_ref[...] = jnp.zeros_like(s_ref)

        n, rstd = _ln_norm(u1_ref[...])
        g = nag_ref[...]
        u2 = n * g + nab_ref[...]
        da = da_ref[...]
        ag = agate_ref[...]
        dag_ref[...] = (da * _silu(u2) * _dsilu(ag)).astype(BF16)
        du2 = da * _silu(ag) * _dsilu(u2)
        du1 = _ln_bwd(du2 * g, n, rstd)
        du1_ref[...] = du1
        s_ref[0:1, :] += jnp.sum(du2 * n, axis=0, keepdims=True)
        s_ref[1:2, :] += jnp.sum(du2, axis=0, keepdims=True)
        s_ref[2:3, :] += jnp.sum(du1, axis=0, keepdims=True)

    def win(cblk):
        return pl.BlockSpec((tm, w), lambda i: (i, cblk))

    one = pl.BlockSpec((1, w), lambda i: (0, 0))
    return pl.pallas_call(
        body, name=name, grid=(r // tm,), in_specs=[win(0), win(2), win(0), one, one],
        out_specs=[win(0), win(0), pl.BlockSpec((8, w), lambda i: (0, 0))],
        out_shape=[jax.ShapeDtypeStruct((r, w), F32), jax.ShapeDtypeStruct((r, w), BF16),
                   jax.ShapeDtypeStruct((8, w), F32)],
        compiler_params=_cp(1),
    )(dcat, p, u1, nag, nab)


def _conv_bwd2(du1, dcat, p, dag, caw, cbw, t_ctx, name):
    r, w = du1.shape
    ka, kb = caw.shape[0], cbw.shape[0]
    tm = _row_tile(t_ctx, 128)
    nctx, nr = t_ctx // tm, r // tm
    n = tm + 2 * HALO - 8

    def body(du, dut, dub, av, avt, avb, ag, agt, agb, db, dbt, dbb, bx, bxt, bxb, bb, bbt, bbb,
             bc, bct, bcb, bg, bgt, bgb, dag_ref, caw_ref, cbw_ref, dp_ref, dcaw_ref, dcbw_ref,
             pad, pad2, rot_u, rot_d, acc_a, acc_b):
        i = pl.program_id(0)
        top_ok, bot_ok = _edges(i, nctx, nr)

        @pl.when(i == 0)
        def _():
            acc_a[...] = jnp.zeros_like(acc_a)
            acc_b[...] = jnp.zeros_like(acc_b)

        def chunk(c, carry):
            c0 = pl.multiple_of(c * LANES, LANES)
            cols = pl.ds(c0, LANES)

            def seg(s):
                return pl.ds(pl.multiple_of(s * w + c0, LANES), LANES)

            sg = _sigmoid(ag[:, cols])
            av_m = av[:, cols]
            _fill_pad(pad, av_m * sg, avt[:, cols] * _sigmoid(agt[:, cols]), avb[:, cols] * _sigmoid(agb[:, cols]),
                      top_ok, bot_ok, tm)
            _rot_fill(rot_u, pad, n)
            du_m = du[:, cols]
            _fill_pad(pad, du_m, dut[:, cols], dub[:, cols], top_ok, bot_ok, tm)
            _rot_fill(rot_d, pad, n)
            du0 = jnp.zeros((tm, LANES), F32)
            for k in range(ka):
                du0 = du0 + caw_ref[k:k + 1, cols] * _tap(rot_d, HALO + ka // 2 - k, tm)
                acc_a[k, :, cols] += _rows8(du_m * _tap(rot_u, HALO - ka // 2 + k, tm))
            dp_ref[:, seg(0)] = (du0 * sg).astype(BF16)
            dp_ref[:, seg(1)] = (du0 * av_m * sg * (1.0 - sg)).astype(BF16)
            dp_ref[:, seg(2)] = dag_ref[:, cols]

            bc_m, bx_m = bc[:, cols], bx[:, cols]
            _fill_pad(pad, bc_m * bx_m, bct[:, cols] * bxt[:, cols], bcb[:, cols] * bxb[:, cols],
                      top_ok, bot_ok, tm)
            v = jnp.zeros((tm, LANES), F32)
            for k in range(kb):
                v = v + cbw_ref[k:k + 1, cols] * pad[pl.ds(HALO - kb // 2 + k, tm), :]
            db_m, bb_m, bg_m = db[:, cols], bb[:, cols], bg[:, cols]
            sbg = _silu(bg_m)
            dv_m = db_m * bb_m * sbg
            _fill_pad(pad2, dv_m, dbt[:, cols] * bbt[:, cols] * _silu(bgt[:, cols]),
                      dbb[:, cols] * bbb[:, cols] * _silu(bgb[:, cols]), top_ok, bot_ok, tm)
            dw0 = jnp.zeros((tm, LANES), F32)
            for k in range(kb):
                dw0 = dw0 + cbw_ref[k:k + 1, cols] * pad2[pl.ds(HALO + kb // 2 - k, tm), :]
                acc_b[k, :, cols] += _rows8(dv_m * pad[pl.ds(HALO - kb // 2 + k, tm), :])
            dp_ref[:, seg(3)] = (dw0 * bc_m).astype(BF16)
            dp_ref[:, seg(4)] = (db_m * v * sbg).astype(BF16)
            dp_ref[:, seg(5)] = (dw0 * bx_m).astype(BF16)
            dp_ref[:, seg(6)] = (db_m * bb_m * v * _dsilu(bg_m)).astype(BF16)
            return carry

        lax.fori_loop(0, w // LANES, chunk, 0)

        @pl.when(i == nr - 1)
        def _():
            dcaw_ref[...] = jnp.zeros_like(dcaw_ref)
            dcbw_ref[...] = jnp.zeros_like(dcbw_ref)
            for k in range(ka):
                dcaw_ref[k:k + 1, :] = jnp.sum(acc_a[k], axis=0, keepdims=True)
            for k in range(kb):
                dcbw_ref[k:k + 1, :] = jnp.sum(acc_b[k], axis=0, keepdims=True)

    def whole(a):
        return pl.BlockSpec(a.shape, lambda i: (0, 0))

    in_specs = (_conv_specs(r, w, tm, 0) + _conv_specs(r, w, tm, 0) + _conv_specs(r, w, tm, 1)
                + _conv_specs(r, w, tm, 1) + _conv_specs(r, w, tm, 3) + _conv_specs(r, w, tm, 4)
                + _conv_specs(r, w, tm, 5) + _conv_specs(r, w, tm, 6)
                + [pl.BlockSpec((tm, w), lambda i: (i, 0)), whole(caw), whole(cbw)])
    pad_t = pltpu.VMEM((tm + 2 * HALO, LANES), F32)
    rot_t = pltpu.VMEM((8, n, LANES), F32)
    return pl.pallas_call(
        body, name=name, grid=(nr,), in_specs=in_specs,
        out_specs=[pl.BlockSpec((tm, 7 * w), lambda i: (i, 0)), pl.BlockSpec((32, w), lambda i: (0, 0)),
                   pl.BlockSpec((8, w), lambda i: (0, 0))],
        out_shape=[jax.ShapeDtypeStruct((r, 7 * w), BF16), jax.ShapeDtypeStruct((32, w), F32),
                   jax.ShapeDtypeStruct((8, w), F32)],
        scratch_shapes=[pad_t, pad_t, rot_t, rot_t, pltpu.VMEM((32, 8, w), F32), pltpu.VMEM((8, 8, w), F32)],
        compiler_params=_cp(1),
    )(*([du1] * 3), *([p] * 6), *([dcat] * 3), *([p] * 12), dag, caw, cbw)


def _rms(xh):
    r = lax.rsqrt(jnp.mean(xh * xh, axis=-1, keepdims=True) + RMS_EPS)
    return xh * r, r


def _qk_fwd(p, cos, sin, qg, kg, att, kv, t_ctx, name):
    r = p.shape[0]
    tm = _row_tile(t_ctx)

    def body(q_ref, k_ref, v_ref, cos_ref, sin_ref, qg_ref, kg_ref, qr_ref, kr_ref, vb_ref):
        cs, sn = cos_ref[...], sin_ref[...]
        for src, g_ref, dst, nh, fac in ((q_ref, qg_ref, qr_ref, att // HEAD_DIM, 1.0),
                                         (k_ref, kg_ref, kr_ref, kv // HEAD_DIM, ATTN_SCALE_LOG2)):
            for h in range(nh):
                cols = slice(h * HEAD_DIM, (h + 1) * HEAD_DIM)
                n, _ = _rms(src[:, cols])
                n = n * g_ref[...]
                rot = n * cs + _partner(n) * sn
                dst[:, cols] = (rot if fac == 1.0 else rot * fac).astype(BF16)
        vb_ref[...] = v_ref[...].astype(BF16)

    def rows(width, cblk):
        return pl.BlockSpec((tm, width), lambda i: (i, cblk))

    one = pl.BlockSpec((1, HEAD_DIM), lambda i: (0, 0))
    return pl.pallas_call(
        body, name=name, grid=(r // tm,),
        in_specs=[rows(att, 0), rows(kv, 2 * att // kv), rows(kv, 2 * att // kv + 1),
                  rows(HEAD_DIM, 0), rows(HEAD_DIM, 0), one, one],
        out_specs=[rows(att, 0), rows(kv, 0), rows(kv, 0)],
        out_shape=[jax.ShapeDtypeStruct((r, att), BF16), jax.ShapeDtypeStruct((r, kv), BF16),
                   jax.ShapeDtypeStruct((r, kv), BF16)],
        compiler_params=_cp(1),
    )(p, p, p, cos, sin, qg, kg)


def _qk_bwd(p, dqr, dgate, dkr, dvr, cos, sin, qg, kg, att, kv, t_ctx, name):
    r = p.shape[0]
    tm = _row_tile(t_ctx)

    def body(q_ref, k_ref, dqr_ref, dgate_ref, dkr_ref, dvr_ref, cos_ref, sin_ref, qg_ref, kg_ref, dp_ref, s_ref):
        @pl.when(pl.program_id(0) == 0)
        def _():
            s_ref[...] = jnp.zeros_like(s_ref)

        cs, sn = cos_ref[...], sin_ref[...]
        for row, (src, dsrc, g_ref, off, nh) in enumerate((
                (q_ref, dqr_ref, qg_ref, 0, att // HEAD_DIM), (k_ref, dkr_ref, kg_ref, 2 * att, kv // HEAD_DIM))):
            dg = jnp.zeros((1, HEAD_DIM), F32)
            for h in range(nh):
                cols = slice(h * HEAD_DIM, (h + 1) * HEAD_DIM)
                n0, rr = _rms(src[:, cols])
                d = dsrc[:, cols]
                dng = d * cs + _partner(d * sn)
                dg = dg + jnp.sum(dng * n0, axis=0, keepdims=True)
                dn0 = dng * g_ref[...]
                dp_ref[:, off + h * HEAD_DIM:off + (h + 1) * HEAD_DIM] = (
                    rr * (dn0 - n0 * jnp.mean(dn0 * n0, axis=-1, keepdims=True))).astype(BF16)
            s_ref[row:row + 1, :] += dg
        dp_ref[:, att:2 * att] = dgate_ref[...]
        dp_ref[:, 2 * att + kv:2 * att + 2 * kv] = dvr_ref[...].astype(BF16)

    def rows(width, cblk):
        return pl.BlockSpec((tm, width), lambda i: (i, cblk))

    one = pl.BlockSpec((1, HEAD_DIM), lambda i: (0, 0))
    return pl.pallas_call(
        body, name=name, grid=(r // tm,),
        in_specs=[rows(att, 0), rows(kv, 2 * att // kv), rows(att, 0), rows(att, 0), rows(kv, 0), rows(kv, 0),
                  rows(HEAD_DIM, 0), rows(HEAD_DIM, 0), one, one],
        out_specs=[rows(2 * att + 2 * kv, 0), pl.BlockSpec((8, HEAD_DIM), lambda i: (0, 0))],
        out_shape=[jax.ShapeDtypeStruct((r, 2 * att + 2 * kv), BF16), jax.ShapeDtypeStruct((8, HEAD_DIM), F32)],
        compiler_params=_cp(1),
    )(p, p, dqr, dgate, dkr, dvr, cos, sin, qg, kg)


def _stack_heads(x, tq):
    return jnp.concatenate([x[:, g * HEAD_DIM:(g + 1) * HEAD_DIM] for g in range(GQA_GROUP)], axis=0)


def _attn_tiles(t_ctx, s_lat, tkl_pref):
    return _div(t_ctx, 256, 8), _div(s_lat, tkl_pref, 8)


NT_DIMS = (((1,), (1,)), ((), ()))
TN_DIMS = (((0,), (0,)), ((), ()))
ATTN_SCALE_LOG2 = ATTN_SCALE * math.log2(math.e)


def _flash_fwd(qr, kr, vb, p, att, t_ctx, name, side=()):
    r = qr.shape[0]
    s_lat = r - t_ctx
    gw = GQA_GROUP * HEAD_DIM
    nkv = att // gw
    tq = _div(t_ctx, 256, 8)
    tkl = _div(s_lat // 2, 1024, 8)
    nq, nq_ctx, n_lat = r // tq, t_ctx // tq, s_lat // tkl
    nl = GQA_GROUP * tq
    align = math.gcd(t_ctx, tkl)
    nj = len(side)
    s_ins, s_in_specs, s_out_shape, s_out_specs, s_sems = _side_plan(side)

    def body(q_ref, k_ref, v_ref, g_ref, *rest):
        s_in, (o_ref, og_ref, lse_ref), s_out = rest[:nj], rest[nj:nj + 3], rest[nj + 3:2 * nj + 3]
        (m_s, l_s, acc_s, st_a, st_b), s_sem = rest[2 * nj + 3:2 * nj + 8], rest[2 * nj + 8:]
        start, finish = _side_run(side, s_in, s_out, s_sem, *_grid_ends((nkv, nq)))
        start()
        qi = pl.program_id(1)
        q4 = _stack_heads(q_ref[...], tq)
        m_s[...] = jnp.full_like(m_s, -1e30)
        l_s[...] = jnp.zeros_like(l_s)
        acc_s[...] = jnp.zeros_like(acc_s)

        def scores(off, size):
            return lax.dot_general(k_ref[pl.ds(off, size), :], q4, NT_DIMS, preferred_element_type=F32)

        def update(st, off, size):
            m_old = m_s[...]
            m_new = jnp.maximum(m_old, jnp.max(st, axis=0, keepdims=True))
            pe = jnp.exp2(st - m_new)
            a = jnp.exp2(m_old - m_new)
            l_s[...] = a * l_s[...] + jnp.sum(pe, axis=0, keepdims=True)
            acc_s[...] = a * acc_s[...] + lax.dot_general(v_ref[pl.ds(off, size), :], pe.astype(BF16), TN_DIMS,
                                                          preferred_element_type=F32)
            m_s[...] = m_new

        def lat_off(n):
            return pl.multiple_of(t_ctx + jnp.minimum(n, n_lat - 1) * tkl, align)

        is_lat = qi >= nq_ctx

        @pl.when(is_lat)
        def _():
            st_a[...] = scores(lat_off(0), tkl)

        update(scores(0, t_ctx), 0, t_ctx)

        def pair(j, carry):
            n = 2 * j
            st_b[...] = scores(lat_off(n + 1), tkl)
            update(st_a[...], lat_off(n), tkl)
            st_a[...] = scores(lat_off(n + 2), tkl)
            update(st_b[...], lat_off(n + 1), tkl)
            return carry

        lax.fori_loop(0, jnp.where(is_lat, n_lat // 2, 0), pair, 0)
        o4 = (acc_s[...] / l_s[...]).T
        lse_ref[...] = m_s[...] + jnp.log2(l_s[...])
        sg = _silu(g_ref[...])
        for g in range(GQA_GROUP):
            cols = slice(g * HEAD_DIM, (g + 1) * HEAD_DIM)
            og = o4[g * tq:(g + 1) * tq, :]
            o_ref[:, cols] = og
            og_ref[:, cols] = (og * sg[:, cols]).astype(BF16)
        finish()

    qspec = pl.BlockSpec((tq, gw), lambda h, i: (i, h))
    kspec = pl.BlockSpec((r, HEAD_DIM), lambda h, i: (0, h))
    return pl.pallas_call(
        body, name=name, grid=(nkv, nq),
        in_specs=[qspec, kspec, kspec, pl.BlockSpec((tq, gw), lambda h, i: (i, att // gw + h))] + s_in_specs,
        out_specs=[qspec, qspec, pl.BlockSpec((None, None, 1, nl), lambda h, i: (h, i, 0, 0))] + s_out_specs,
        out_shape=[jax.ShapeDtypeStruct((r, att), F32), jax.ShapeDtypeStruct((r, att), BF16),
                   jax.ShapeDtypeStruct((nkv, nq, 1, nl), F32)] + s_out_shape,
        scratch_shapes=[pltpu.VMEM((1, nl), F32), pltpu.VMEM((1, nl), F32), pltpu.VMEM((HEAD_DIM, nl), F32),
                        pltpu.VMEM((tkl, nl), F32), pltpu.VMEM((tkl, nl), F32)] + s_sems,
        compiler_params=_cp(2),
    )(qr, kr, vb, p, *s_ins)


def _flash_bwd(qr, kr, vb, p, o, dog, lse, att, kv, t_ctx, name, side=()):
    r = qr.shape[0]
    s_lat = r - t_ctx
    gw = GQA_GROUP * HEAD_DIM
    nkv = att // gw
    tq, tkl = _attn_tiles(t_ctx, s_lat, 1024)
    nq, nq_ctx, n_lat = r // tq, t_ctx // tq, s_lat // tkl
    nl = GQA_GROUP * tq
    nj = len(side)
    s_ins, s_in_specs, s_out_shape, s_out_specs, s_sems = _side_plan(side)

    def body(q_ref, k_ref, v_ref, g_ref, o_ref, dog_ref, lse_ref, *rest):
        s_in, (dq_ref, dgate_ref, dk_ref, dv_ref), s_out = rest[:nj], rest[nj:nj + 4], rest[nj + 4:2 * nj + 4]
        dq_s, s_sem = rest[2 * nj + 4], rest[2 * nj + 5:]
        start, finish = _side_run(side, s_in, s_out, s_sem, *_grid_ends((nkv, nq)))
        start()
        qi = pl.program_id(1)

        @pl.when(qi == 0)
        def _():
            dk_ref[...] = jnp.zeros_like(dk_ref)
            dv_ref[...] = jnp.zeros_like(dv_ref)

        gate, ov, dogv = g_ref[...], o_ref[...], dog_ref[...]
        dgate_ref[...] = (dogv * ov * _dsilu(gate)).astype(BF16)
        do = dogv * _silu(gate)
        do4 = _stack_heads(do, tq)
        delta = jnp.sum((do4 * _stack_heads(ov, tq)).T, axis=0, keepdims=True)
        do4 = do4.astype(BF16)
        q4 = _stack_heads(q_ref[...], tq)
        lse_v = lse_ref[...]
        dq_s[...] = jnp.zeros_like(dq_s)

        def step(off, size):
            kb = k_ref[pl.ds(off, size), :]
            vv = v_ref[pl.ds(off, size), :]
            st = lax.dot_general(kb, q4, NT_DIMS, preferred_element_type=F32)
            pe = jnp.exp2(st - lse_v)
            dp = lax.dot_general(vv, do4, NT_DIMS, preferred_element_type=F32)
            ds = (pe * (dp - delta) * ATTN_SCALE).astype(BF16)
            dv_ref[pl.ds(off, size), :] += jnp.dot(pe.astype(BF16), do4, preferred_element_type=F32)
            dk_ref[pl.ds(off, size), :] += jnp.dot(ds, q4, preferred_element_type=F32)
            dq_s[...] += lax.dot_general(kb, ds, TN_DIMS, preferred_element_type=F32)

        step(0, t_ctx)

        def lat(n, carry):
            step(pl.multiple_of(t_ctx + n * tkl, math.gcd(t_ctx, tkl)), tkl)
            return carry

        lax.fori_loop(0, jnp.where(qi < nq_ctx, 0, n_lat), lat, 0)
        dq4 = dq_s[...].T * (1.0 / ATTN_SCALE_LOG2)
        for g in range(GQA_GROUP):
            dq_ref[:, g * HEAD_DIM:(g + 1) * HEAD_DIM] = dq4[g * tq:(g + 1) * tq, :]
        finish()

    qspec = pl.BlockSpec((tq, gw), lambda h, i: (i, h))
    kspec = pl.BlockSpec((r, HEAD_DIM), lambda h, i: (0, h))
    return pl.pallas_call(
        body, name=name, grid=(nkv, nq),
        in_specs=[qspec, kspec, kspec, pl.BlockSpec((tq, gw), lambda h, i: (i, att // gw + h)), qspec, qspec,
                  pl.BlockSpec((None, None, 1, nl), lambda h, i: (h, i, 0, 0))] + s_in_specs,
        out_specs=[qspec, qspec, kspec, kspec] + s_out_specs,
        out_shape=[jax.ShapeDtypeStruct((r, att), F32), jax.ShapeDtypeStruct((r, att), BF16),
                   jax.ShapeDtypeStruct((r, kv), F32), jax.ShapeDtypeStruct((r, kv), F32)] + s_out_shape,
        scratch_shapes=[pltpu.VMEM((HEAD_DIM, nl), F32)] + s_sems,
        compiler_params=_cp(2),
    )(qr, kr, vb, p, o, dog, lse, *s_ins)


def _rope_tables(t_ctx, s_lat):
    rows_n = s_lat // GRID_W
    row = jnp.repeat(jnp.arange(rows_n, dtype=F32), GRID_W)
    col = jnp.tile(jnp.arange(GRID_W, dtype=F32), rows_n)
    axis_dim = HEAD_DIM // 2
    inv_freq = ROPE_THETA ** (-jnp.arange(0, axis_dim, 2, dtype=F32) / axis_dim)
    ang_r = row[:, None] * inv_freq[None, :]
    ang_c = col[:, None] * inv_freq[None, :]
    cos = jnp.concatenate([jnp.cos(ang_r), jnp.cos(ang_r), jnp.cos(ang_c), jnp.cos(ang_c)], axis=1)
    sin = jnp.concatenate([-jnp.sin(ang_r), jnp.sin(ang_r), -jnp.sin(ang_c), jnp.sin(ang_c)], axis=1)
    cos = jnp.concatenate([jnp.ones((t_ctx, HEAD_DIM), F32), cos], axis=0)
    sin = jnp.concatenate([jnp.zeros((t_ctx, HEAD_DIM), F32), sin], axis=0)
    return cos, sin


def _pad_rows(a, rows):
    return jnp.pad(a, ((0, rows - a.shape[0]), (0, 0)))


def kernel(x, c, ctx, c_ctx, w_mod, b_mod, post_ln_g, post_ln_b, w_in_e, conv_a_w, conv_a_b, norm_a_g, norm_a_b, conv_b_w, w_out_e, w_in_o, q_norm_g, k_norm_g, w_out_o, loss_target, m_c_ctx, m_w_mod, m_b_mod, m_post_ln_g, m_post_ln_b, m_w_in_e, m_conv_a_w, m_conv_a_b, m_norm_a_g, m_norm_a_b, m_conv_b_w, m_w_out_e, m_w_in_o, m_q_norm_g, m_k_norm_g, m_w_out_o, v_c_ctx, v_w_mod, v_b_mod, v_post_ln_g, v_post_ln_b, v_w_in_e, v_conv_a_w, v_conv_a_b, v_norm_a_g, v_norm_a_b, v_conv_b_w, v_w_out_e, v_w_in_o, v_q_norm_g, v_k_norm_g, v_w_out_o):
    depth, d, mcols = w_mod.shape
    s_lat, t_ctx = x.shape[1], ctx.shape[1]
    n_even, n_odd = w_in_e.shape[0], w_in_o.shape[0]
    ka, kb = conv_a_w.shape[1], conv_b_w.shape[1]
    wch = conv_a_w.shape[2] * NDEV
    att = w_out_o.shape[1] * NDEV
    kv = (w_in_o.shape[2] * NDEV - 2 * att) // 2
    alpha = (2.0 * depth) ** 0.25
    me = 4 * lax.axis_index("x") + 2 * lax.axis_index("y") + lax.axis_index("c")

    c_all = _all_gather(_pad_rows(c, 8), "ag_c")[:, 0, :]
    c16 = jnp.concatenate([c_all, _pad_rows(c_ctx[None, :], 8)], axis=0)
    sc16 = _ew(_silu, BF16, "silu_c", c16)
    m_part = _mm(sc16, w_mod, "nn", F32, "mod_fwd", tm=16, tn=mcols, tk=d)
    m_all = _all_gather(m_part, "ag_mod")
    m_full = m_all.reshape(NDEV, 16, depth, mcols).transpose(2, 1, 0, 3).reshape(depth * 16, 3 * d)
    b16 = jnp.broadcast_to(b_mod[:, None, :], (depth, 16, 3 * d)).reshape(depth * 16, 3 * d)
    m_full = _ew(lambda a, b: a + b, F32, "mod_bias", m_full, b16).reshape(depth, 16, 3, d)
    modv = [jnp.stack([m_full[l, 8], lax.dynamic_index_in_dim(m_full[l], me, 0, keepdims=False)])
            for l in range(depth)]

    cw = jnp.concatenate([_pad_rows(conv_a_w[i], 32) for i in range(n_even)]
                         + [_pad_rows(conv_b_w[i], 8) for i in range(n_even)], axis=0)
    cw_all = _all_gather(cw, "ag_convw").transpose(1, 0, 2).reshape(cw.shape[0], wch)
    caw = [cw_all[32 * i:32 * i + ka] for i in range(n_even)]
    cbw = [cw_all[32 * n_even + 8 * i:32 * n_even + 8 * i + kb] for i in range(n_even)]

    sh_in = [_ew(lambda t: t, BF16, f"w_in{l}_bf16", (w_in_e if l % 2 == 0 else w_in_o)[l // 2]) for l in range(depth)]
    sh_out = [_ew(lambda t: t, BF16, f"w_out{l}_bf16", (w_out_e if l % 2 == 0 else w_out_o)[l // 2])
              for l in range(depth)]
    g_in = [None] * depth
    g_out = [None] * depth
    g_in[0] = _all_gather(sh_in[0], "ag_w_in0")

    def wanted(keys):
        return [(kind, l) for kind, l in keys if l < depth and (g_in if kind == "in" else g_out)[l] is None]

    def gather_jobs(keys):
        return [("gather", (sh_in if kind == "in" else sh_out)[l]) for kind, l in keys]

    def hosted(res, keys):
        if not keys:
            return res
        n_own = len(res) - len(keys)
        for (kind, l), g in zip(keys, res[n_own:]):
            if kind == "in":
                g_in[l] = g
            else:
                g_out[l] = g.reshape(1, NDEV * g.shape[1], d)
        return res[0] if n_own == 1 else res[:n_own]

    cos, sin = _rope_tables(t_ctx, s_lat)
    lnp = [jnp.stack([post_ln_g[l], post_ln_b[l]]) for l in range(depth)]

    xb = jnp.concatenate([ctx[0], x[0]], axis=0)
    saved = []
    h = _modulate(xb, modv[0], t_ctx, "modulate0")
    for l in range(depth):
        i = l // 2
        if l % 2 == 0:
            keys = wanted([("out", l), ("in", l + 1)])
            p = hosted(_mm(h, g_in[l], "nn", F32, f"in_proj{l}", tm=ROWS_WIDE, side=gather_jobs(keys)), keys)
            cat, u1 = _conv_fwd(p, caw[i], conv_a_b[i][None], norm_a_g[i][None], norm_a_b[i][None], cbw[i],
                                t_ctx, f"conv_fwd{l}")
            keys = wanted([("out", l + 1)])
            y = hosted(_mm(cat, g_out[l], "nn", F32, f"out_proj{l}", side=gather_jobs(keys)), keys)
            saved.append((xb, h, p, cat, u1, y))
        else:
            keys = wanted([("out", l)])
            p = hosted(_mm(h, g_in[l], "nn", F32, f"in_proj{l}", tm=ROWS_WIDE, side=gather_jobs(keys)), keys)
            qr, kr, vb = _qk_fwd(p, cos, sin, q_norm_g[i][None], k_norm_g[i][None], att, kv, t_ctx, f"qk_fwd{l}")
            keys = wanted([("in", l + 1), ("out", l + 1)])
            o, og, lse = hosted(_flash_fwd(qr, kr, vb, p, att, t_ctx, f"flash_fwd{l}", side=gather_jobs(keys)), keys)
            y = _mm(og, g_out[l], "nn", F32, f"out_proj{l}")
            saved.append((xb, h, p, qr, kr, vb, o, og, lse, y))
        if l + 1 < depth:
            xb, h = _postln_fwd(xb, y, modv[l], lnp[l], alpha, modv[l + 1], t_ctx, f"postln_fwd{l}")

    small = {}
    dmod = [None] * depth
    recv_in, recv_out = [None] * depth, [None] * depth
    carried = []
    for l in reversed(range(depth)):
        i = l // 2
        sv = saved[l]
        x_in, h, p, y = sv[0], sv[1], sv[2], sv[-1]
        if l == depth - 1:
            loss_blk, dy, dxa, s_ln = _postln_bwd(loss_target[0], x_in, y, modv[l], lnp[l], alpha, t_ctx,
                                                  f"loss_postln_bwd{l}", from_loss=True)
            loss = lax.psum(loss_blk[0, 0], ("x", "y", "c"))
        else:
            dy, dxa, s_ln, s_mod = _postln_bwd(dh, x_in, y, modv[l], lnp[l], alpha, t_ctx, f"postln_bwd{l}",
                                               mod_next=(dxa, modv[l + 1]))
            dmod[l + 1] = jnp.stack([jnp.stack([s_mod[0], s_mod[1], s_ln_above[2]]),
                                     jnp.stack([s_mod[2], s_mod[3], s_ln_above[3]])])
        s_ln_above = s_ln
        small[f"ln{l}"] = s_ln
        mixed = sv[3] if l % 2 == 0 else sv[7]
        k_out = g_out[l].shape[1]
        part = _mm(mixed, dy, "tn", BF16, f"d_w_out{l}", tm=1024, tk=768).reshape(NDEV, k_out // NDEV, d)
        dmixed, recv_out[l] = _mm(dy, g_out[l], "nt", F32, f"d_mixed{l}", tm=ROWS_WIDE, side=[("exchange", part)])
        if l % 2 == 0:
            u1 = sv[4]
            du1, dag, s_c1 = _conv_bwd1(dmixed, p, u1, norm_a_g[i][None], norm_a_b[i][None], t_ctx,
                                        f"conv_bwd1_{l}")
            dp, dcaw, dcbw = _conv_bwd2(du1, dmixed, p, dag, caw[i], cbw[i], t_ctx, f"conv_bwd2_{l}")
            small[f"c1_{i}"], small[f"caw{i}"], small[f"cbw{i}"] = s_c1, dcaw, dcbw
        else:
            qr, kr, vb, o, _, lse = sv[3:9]
            dqr, dgate, dkr, dvr, *got = _flash_bwd(qr, kr, vb, p, o, dmixed, lse, att, kv, t_ctx, f"flash_bwd{l}",
                                                    side=carried)
            if carried:
                recv_in[l + 1], carried = got[0], []
            dp, s_qk = _qk_bwd(p, dqr, dgate, dkr, dvr, cos, sin, q_norm_g[i][None], k_norm_g[i][None], att, kv,
                               t_ctx, f"qk_bwd{l}")
            small[f"qk{i}"] = jnp.pad(s_qk, ((0, 0), (0, d - HEAD_DIM)))
        part = _mm(h, dp, "tn", BF16, f"d_w_in{l}", tm=d, tk=ROWS_EPILOGUE, out_nd=NDEV, side=carried)
        if carried:
            (part, recv_in[l + 1]), carried = part, []
        if l == 0:
            dxb, s_mod, recv_in[0] = _mm(dp, g_in[0], "nt", F32, "d_h0", tm=ROWS_EPILOGUE, side=[("exchange", part)],
                                         mod_bwd=(x_in, dxa, modv[0], t_ctx))
            dmod[0] = jnp.stack([jnp.stack([s_mod[0], s_mod[1], s_ln[2]]), jnp.stack([s_mod[2], s_mod[3], s_ln[3]])])
        else:
            dh = _mm(dp, g_in[l], "nt", F32, f"d_h{l}", tm=ROWS_WIDE)
            carried = [("exchange", part)]
    grad_x = dxb[t_ctx:][None]

    dm_loc = jnp.stack(dmod).reshape(depth * 2, 3 * d)
    dm_all = _all_gather(dm_loc, "ag_dmod").reshape(NDEV, depth, 2, 3 * d)
    dm_ctx = _sum_lead(dm_all[:, :, 0, :], "sum_dmod_ctx")
    dm16 = jnp.concatenate([dm_all[:, :, 1, :].transpose(1, 0, 2), dm_ctx[:, None, :],
                            jnp.zeros((depth, 7, 3 * d), F32)], axis=1)
    g_b_mod = _sum_lead(dm16.transpose(1, 0, 2), "sum_b_mod")
    dm16_me = lax.dynamic_slice_in_dim(dm16.reshape(depth, 16, NDEV, mcols), me, 1, axis=2)
    dm16_me = dm16_me.reshape(depth, 16, mcols).transpose(1, 0, 2).reshape(16, depth * mcols)
    g_w_mod = _mm(sc16, dm16_me, "tn", F32, "mod_bwd_w", tm=d, tn=mcols, tk=16, out_nd=depth)
    dsc16 = _mm(dm16_me, w_mod, "nt", F32, "mod_bwd_c", tm=16, tn=d, tk=mcols)
    small["c_ctx"] = dsc16[8:16]

    names = sorted(small)
    offs, rows = {}, 0
    for nme in names:
        offs[nme] = rows
        rows += small[nme].shape[0]
    sm_all = _all_gather(jnp.concatenate([small[nme] for nme in names], axis=0), "ag_small")
    sm = _sum_lead(sm_all, "sum_small")

    def part(nme, lo, hi):
        return sm[offs[nme] + lo:offs[nme] + hi]

    g_post_ln_g = jnp.concatenate([part(f"ln{l}", 0, 1) for l in range(depth)], axis=0)
    g_post_ln_b = jnp.concatenate([part(f"ln{l}", 1, 2) for l in range(depth)], axis=0)
    g_norm_a_g = jnp.concatenate([part(f"c1_{i}", 0, 1) for i in range(n_even)], axis=0)
    g_norm_a_b = jnp.concatenate([part(f"c1_{i}", 1, 2) for i in range(n_even)], axis=0)
    g_conv_a_b = jnp.concatenate([part(f"c1_{i}", 2, 3) for i in range(n_even)], axis=0)
    g_q_norm_g = jnp.concatenate([part(f"qk{i}", 0, 1)[:, :HEAD_DIM] for i in range(n_odd)], axis=0)
    g_k_norm_g = jnp.concatenate([part(f"qk{i}", 1, 2)[:, :HEAD_DIM] for i in range(n_odd)], axis=0)
    wsh = wch // NDEV
    g_conv_a_w = jnp.stack([lax.dynamic_slice_in_dim(part(f"caw{i}", 0, ka), me * wsh, wsh, axis=1)
                            for i in range(n_even)])
    g_conv_b_w = jnp.stack([lax.dynamic_slice_in_dim(part(f"cbw{i}", 0, kb), me * wsh, wsh, axis=1)
                            for i in range(n_even)])
    g_c_ctx = _ew(lambda a, b: a * _dsilu(b), F32, "d_c_ctx", part("c_ctx", 0, 8), _pad_rows(c_ctx[None, :], 8))[0]

    def updated(recv, layers, prefix, w, m, v):
        outs = [_reduce_adamw(recv[l], w[j], m[j], v[j], f"adamw_{prefix}{j}") for j, l in enumerate(layers)]
        return [jnp.stack([o[t] for o in outs]) for t in range(4)]

    evens, odds = range(0, depth, 2), range(1, depth, 2)
    r_w_in_e = updated(recv_in, evens, "w_in_e", w_in_e, m_w_in_e, v_w_in_e)
    r_w_out_e = updated(recv_out, evens, "w_out_e", w_out_e, m_w_out_e, v_w_out_e)
    r_w_in_o = updated(recv_in, odds, "w_in_o", w_in_o, m_w_in_o, v_w_in_o)
    r_w_out_o = updated(recv_out, odds, "w_out_o", w_out_o, m_w_out_o, v_w_out_o)

    grads = {
        "c_ctx": g_c_ctx, "w_mod": g_w_mod, "b_mod": g_b_mod, "post_ln_g": g_post_ln_g, "post_ln_b": g_post_ln_b,
        "conv_a_w": g_conv_a_w, "conv_a_b": g_conv_a_b, "norm_a_g": g_norm_a_g, "norm_a_b": g_norm_a_b,
        "conv_b_w": g_conv_b_w, "q_norm_g": g_q_norm_g, "k_norm_g": g_k_norm_g,
    }
    state = {
        "c_ctx": (c_ctx, m_c_ctx, v_c_ctx), "w_mod": (w_mod, m_w_mod, v_w_mod), "b_mod": (b_mod, m_b_mod, v_b_mod),
        "post_ln_g": (post_ln_g, m_post_ln_g, v_post_ln_g), "post_ln_b": (post_ln_b, m_post_ln_b, v_post_ln_b),
        "conv_a_w": (conv_a_w, m_conv_a_w, v_conv_a_w), "conv_a_b": (conv_a_b, m_conv_a_b, v_conv_a_b),
        "norm_a_g": (norm_a_g, m_norm_a_g, v_norm_a_g), "norm_a_b": (norm_a_b, m_norm_a_b, v_norm_a_b),
        "conv_b_w": (conv_b_w, m_conv_b_w, v_conv_b_w), "q_norm_g": (q_norm_g, m_q_norm_g, v_q_norm_g),
        "k_norm_g": (k_norm_g, m_k_norm_g, v_k_norm_g),
    }
    res = {"w_in_e": r_w_in_e, "w_out_e": r_w_out_e, "w_in_o": r_w_in_o, "w_out_o": r_w_out_o}
    for nme, g in grads.items():
        w, m, v = state[nme]
        res[nme] = [g] + _adamw(w, g, m, v, f"adamw_{nme}")
    order = ["c_ctx", "w_mod", "b_mod", "post_ln_g", "post_ln_b", "w_in_e", "conv_a_w", "conv_a_b", "norm_a_g",
             "norm_a_b", "conv_b_w", "w_out_e", "w_in_o", "q_norm_g", "k_norm_g", "w_out_o"]
    return (loss, grad_x, *[res[nme][0] for nme in order], *[res[nme][1] for nme in order],
            *[res[nme][2] for nme in order], *[res[nme][3] for nme in order])
```
